```python
import math
import jax, jax.numpy as jnp
from jax import lax
import numpy as np

D_MODEL = 2048
BATCH = 8
SEQ = 2048
DEPTH = 1

MIX_WIDTH = D_MODEL
DA_WIDTH = MIX_WIDTH // 2
RW_WIDTH = MIX_WIDTH - DA_WIDTH
DA_HEADS = 8
DA_V_DIM = DA_WIDTH // DA_HEADS
DA_QK_DIM = DA_V_DIM // 2
RW_HEAD_SIZE = 64
RW_HEADS = RW_WIDTH // RW_HEAD_SIZE
RW_DECAY_LORA = 64
RW_A_LORA = 64
RW_GATE_LORA = 160
IN_COLS = 3 * DA_WIDTH + 3 * RW_WIDTH
Q_BLOCK = 128
N_GROUPS = 4
EXPERTS_PER_GROUP = 8
N_EXPERTS = N_GROUPS * EXPERTS_PER_GROUP
TOP_K_IN_GROUP = 2
D_EXPERT = 512
MOE_BLOCK = 128
NORM_EPS = 1e-6
LN_X_EPS = 64e-5

kernel_name = 'hybrid_diffattn_rwkv7_hmoe'


def rms_norm(x, g, eps=NORM_EPS):
    xf = x.astype(jnp.float32)
    y = xf * lax.rsqrt(jnp.mean(xf * xf, axis=-1, keepdims=True) + eps)
    return (y * g.astype(jnp.float32)).astype(x.dtype)


def token_shift(z):
    return jnp.pad(z, ((0, 0), (1, 0), (0, 0)))[:, :-1]


def alibi_slopes(n_heads):
    return jnp.asarray(2.0 ** (-8.0 * np.arange(1, n_heads + 1) / n_heads), dtype=jnp.float32)


def lambda_init(layer):
    return 0.8 - 0.6 * math.exp(-0.3 * layer)


def diff_attention(q, k, v, lam, sub_g, lam_init):
    b, s, h, _, dq = q.shape
    scale = dq ** -0.5
    slopes = alibi_slopes(h)
    outs = []
    for i in range(s // Q_BLOCK):
        q0, q1 = i * Q_BLOCK, (i + 1) * Q_BLOCK
        sc = jnp.einsum('bqhcd,bkhcd->bhcqk', q[:, q0:q1], k[:, :q1]).astype(jnp.float32) * scale
        dist = (jnp.arange(q0, q1)[:, None] - jnp.arange(q1)[None, :]).astype(jnp.float32)
        sc = jnp.where(dist >= 0, sc - slopes[:, None, None, None] * dist, -jnp.inf)
        p = jax.nn.softmax(sc, axis=-1)
        a = p[:, :, 0] - lam * p[:, :, 1]
        outs.append(jnp.einsum('bhqk,bkhd->bqhd', a.astype(v.dtype), v[:, :q1]))
    o = jnp.concatenate(outs, axis=1)
    o = rms_norm(o, sub_g) * (1.0 - lam_init)
    return o.reshape(b, s, h * o.shape[-1])


def rwkv7_time_mix(xn, r, k, v, mu_x, mu_rkv, w0, w1, w2, a0, a1, a2, g1, g2, k_k, k_a, r_k, ln_w, ln_b):
    b, s, c = r.shape
    h, n = RW_HEADS, RW_HEAD_SIZE
    f32 = jnp.float32
    xx = token_shift(xn) - xn
    x_w = xn + xx * mu_x[0]
    x_a = xn + xx * mu_x[1]
    x_g = xn + xx * mu_x[2]
    r = r + (token_shift(r) - r) * mu_rkv[0]
    k = k + (token_shift(k) - k) * mu_rkv[1]
    v = v + (token_shift(v) - v) * mu_rkv[2]
    w_log = -jax.nn.softplus(-(w0 + jnp.tanh(x_w @ w1) @ w2).astype(f32)) - 0.5
    decay = jnp.exp(-jnp.exp(w_log))
    a = jax.nn.sigmoid((a0 + (x_a @ a1) @ a2).astype(f32))
    g = jax.nn.sigmoid(x_g @ g1) @ g2
    r_h, k_h, v_h, a_h, w_h = (z.astype(f32).reshape(b, s, h, n) for z in (r, k, v, a, decay))
    kk = k_h * k_k.reshape(h, n).astype(f32)
    kk = kk / jnp.maximum(jnp.linalg.norm(kk, axis=-1, keepdims=True), 1e-12)
    k_h = k_h * (1.0 + (a_h - 1.0) * k_a.reshape(h, n).astype(f32))

    def step(state, inp):
        r_t, w_t, k_t, v_t, kk_t, b_t = inp
        sa = jnp.einsum('bhij,bhj->bhi', state, -kk_t)
        state = (state * w_t[:, :, None, :] + sa[..., None] * b_t[:, :, None, :]
                 + v_t[..., None] * k_t[:, :, None, :])
        return state, jnp.einsum('bhij,bhj->bhi', state, r_t)

    tm = lambda z: jnp.moveaxis(z, 1, 0)
    state0 = jnp.zeros((b, h, n, n), f32)
    _, y = lax.scan(step, state0, (tm(r_h), tm(w_h), tm(k_h), tm(v_h), tm(kk), tm(kk * a_h)))
    y = jnp.moveaxis(y, 0, 1)
    mean = jnp.mean(y, axis=-1, keepdims=True)
    var = jnp.mean(jnp.square(y - mean), axis=-1, keepdims=True)
    y = ((y - mean) * lax.rsqrt(var + LN_X_EPS) * ln_w.reshape(h, n).astype(f32)
         + ln_b.reshape(h, n).astype(f32))
    y = y + jnp.sum(r_h * k_h * r_k.astype(f32), axis=-1, keepdims=True) * v_h
    return (y.reshape(b, s, c) * g.astype(f32)).astype(xn.dtype)


def hier_moe(xn, w_group, b_group, w_expert, b_expert, w_gate, w_up, w_down):
    b, s, d = xn.shape
    t = b * s
    f32 = jnp.float32
    xt = xn.reshape(t, d)
    lg = (xt @ w_group).astype(f32) + b_group.astype(f32)
    grp = jnp.argmax(lg, axis=-1).astype(jnp.int32)
    p_grp = jnp.take_along_axis(jax.nn.softmax(lg, axis=-1), grp[:, None], axis=-1)
    le = (xt @ w_expert).astype(f32).reshape(t, N_GROUPS, EXPERTS_PER_GROUP) + b_expert.astype(f32)
    le = jnp.take_along_axis(le, grp[:, None, None], axis=1)[:, 0]
    top_p, top_i = lax.top_k(jax.nn.softmax(le, axis=-1), TOP_K_IN_GROUP)
    gates = p_grp * top_p / jnp.sum(top_p, axis=-1, keepdims=True)

    tk = t * TOP_K_IN_GROUP
    n_blocks = -(-tk // MOE_BLOCK) + N_EXPERTS
    n_rows = n_blocks * MOE_BLOCK
    flat_e = (grp[:, None] * EXPERTS_PER_GROUP + top_i).reshape(-1).astype(jnp.int32)
    flat_tok = (jnp.arange(tk, dtype=jnp.int32) // TOP_K_IN_GROUP)
    flat_gate = gates.reshape(-1)
    order = jnp.argsort(flat_e)
    se = flat_e[order]
    counts = jnp.zeros((N_EXPERTS,), jnp.int32).at[flat_e].add(1)
    starts = jnp.cumsum(counts) - counts
    pcounts = ((counts + MOE_BLOCK - 1) // MOE_BLOCK) * MOE_BLOCK
    pends = jnp.cumsum(pcounts)
    pstarts = pends - pcounts
    dest = pstarts[se] + (jnp.arange(tk, dtype=jnp.int32) - starts[se])
    row_tok = jnp.full((n_rows,), t, jnp.int32).at[dest].set(flat_tok[order])
    row_gate = jnp.zeros((n_rows,), f32).at[dest].set(flat_gate[order])
    block_e = jnp.minimum(jnp.searchsorted(pends, jnp.arange(n_blocks, dtype=jnp.int32) * MOE_BLOCK,
                                           side='right'), N_EXPERTS - 1).astype(jnp.int32)
    x_pad = jnp.concatenate([xt, jnp.zeros((1, d), xt.dtype)], axis=0)
    xb = x_pad[row_tok].reshape(n_blocks, MOE_BLOCK, d)

    def expert_block(args):
        xblk, e = args
        hdn = jax.nn.silu(xblk @ w_gate[e]) * (xblk @ w_up[e])
        return hdn @ w_down[e]

    yb = lax.map(expert_block, (xb, block_e)).reshape(n_rows, d)
    yb = yb * row_gate[:, None].astype(yb.dtype)
    out = jnp.zeros((t + 1, d), yb.dtype).at[row_tok].add(yb)[:t]
    return out.reshape(b, s, d)


def setup_inputs(seed: int = 0) -> dict:
    key = jax.random.key(seed)
    keys = iter(jax.random.split(key, 48))

    def nrm(shape, scale):
        return scale * jax.random.normal(next(keys), shape, jnp.float32)

    def uni(shape, lo, hi):
        return jax.random.uniform(next(keys), shape, jnp.float32, lo, hi)

    L, D = DEPTH, D_MODEL
    return {
        'x': nrm((BATCH, SEQ, D), 1.0),
        'g_mix': 1.0 + nrm((L, D), 0.02),
        'w_in': nrm((L, D, IN_COLS), D ** -0.5),
        'w_out': nrm((L, MIX_WIDTH, D), MIX_WIDTH ** -0.5),
        'da_lambda_q1': nrm((L, DA_QK_DIM), 0.1),
        'da_lambda_k1': nrm((L, DA_QK_DIM), 0.1),
        'da_lambda_q2': nrm((L, DA_QK_DIM), 0.1),
        'da_lambda_k2': nrm((L, DA_QK_DIM), 0.1),
        'da_subln_g': 1.0 + nrm((L, DA_V_DIM), 0.02),
        'rw_mu_x': uni((L, 3, D), 0.0, 1.0),
        'rw_mu_rkv': uni((L, 3, RW_WIDTH), 0.0, 1.0),
        'rw_w0': uni((L, RW_WIDTH), -2.5, 0.5),
        'rw_w1': nrm((L, D, RW_DECAY_LORA), D ** -0.5),
        'rw_w2': nrm((L, RW_DECAY_LORA, RW_WIDTH), 0.3 * RW_DECAY_LORA ** -0.5),
        'rw_a0': nrm((L, RW_WIDTH), 0.1),
        'rw_a1': nrm((L, D, RW_A_LORA), D ** -0.5),
        'rw_a2': nrm((L, RW_A_LORA, RW_WIDTH), 0.5 * RW_A_LORA ** -0.5),
        'rw_g1': nrm((L, D, RW_GATE_LORA), D ** -0.5),
        'rw_g2': nrm((L, RW_GATE_LORA, RW_WIDTH), RW_GATE_LORA ** -0.5),
        'rw_k_k': 0.85 + nrm((L, RW_WIDTH), 0.02),
        'rw_k_a': 1.0 + nrm((L, RW_WIDTH), 0.02),
        'rw_r_k': nrm((L, RW_HEADS, RW_HEAD_SIZE), 0.1),
        'rw_ln_w': 1.0 + nrm((L, RW_WIDTH), 0.02),
        'rw_ln_b': nrm((L, RW_WIDTH), 0.01),
        'g_ffn': 1.0 + nrm((L, D), 0.02),
        'moe_w_group': nrm((L, D, N_GROUPS), D ** -0.5),
        'moe_b_group': nrm((L, N_GROUPS), 0.01),
        'moe_w_expert': nrm((L, D, N_EXPERTS), D ** -0.5),
        'moe_b_expert': nrm((L, N_GROUPS, EXPERTS_PER_GROUP), 0.01),
        'moe_w_gate': nrm((L, N_EXPERTS, D, D_EXPERT), D ** -0.5),
        'moe_w_up': nrm((L, N_EXPERTS, D, D_EXPERT), D ** -0.5),
        'moe_w_down': nrm((L, N_EXPERTS, D_EXPERT, D), D_EXPERT ** -0.5),
        'g_final': 1.0 + nrm((D,), 0.02),
    }


def reference(x, g_mix, w_in, w_out, da_lambda_q1, da_lambda_k1, da_lambda_q2, da_lambda_k2, da_subln_g,
              rw_mu_x, rw_mu_rkv, rw_w0, rw_w1, rw_w2, rw_a0, rw_a1, rw_a2, rw_g1, rw_g2, rw_k_k, rw_k_a,
              rw_r_k, rw_ln_w, rw_ln_b, g_ffn, moe_w_group, moe_b_group, moe_w_expert, moe_b_expert,
              moe_w_gate, moe_w_up, moe_w_down, g_final):
    b, s, _ = x.shape
    f32 = jnp.float32
    split_at = [DA_WIDTH, 2 * DA_WIDTH, 3 * DA_WIDTH, 3 * DA_WIDTH + RW_WIDTH, 3 * DA_WIDTH + 2 * RW_WIDTH]
    h = x
    for l in range(DEPTH):
        xn = rms_norm(h, g_mix[l])
        proj = xn @ w_in[l]
        dq, dk, dv, rr, rk, rv = jnp.split(proj, split_at, axis=-1)
        lam_init = lambda_init(l)
        lam = (jnp.exp(jnp.sum(da_lambda_q1[l].astype(f32) * da_lambda_k1[l].astype(f32)))
               - jnp.exp(jnp.sum(da_lambda_q2[l].astype(f32) * da_lambda_k2[l].astype(f32))) + lam_init)
        o_da = diff_attention(dq.reshape(b, s, DA_HEADS, 2, DA_QK_DIM),
                              dk.reshape(b, s, DA_HEADS, 2, DA_QK_DIM),
                              dv.reshape(b, s, DA_HEADS, DA_V_DIM),
                              lam, da_subln_g[l], lam_init)
        o_rw = rwkv7_time_mix(xn, rr, rk, rv, rw_mu_x[l], rw_mu_rkv[l], rw_w0[l], rw_w1[l], rw_w2[l],
                              rw_a0[l], rw_a1[l], rw_a2[l], rw_g1[l], rw_g2[l], rw_k_k[l], rw_k_a[l],
                              rw_r_k[l], rw_ln_w[l], rw_ln_b[l])
        h = h + jnp.concatenate([o_da, o_rw], axis=-1) @ w_out[l]
        h = h + hier_moe(rms_norm(h, g_ffn[l]), moe_w_group[l], moe_b_group[l], moe_w_expert[l],
                         moe_b_expert[l], moe_w_gate[l], moe_w_up[l], moe_w_down[l])
    return rms_norm(h, g_final)
```

```python
import functools
import math

import jax
import jax.numpy as jnp
from jax import lax
from jax.experimental import pallas as pl
from jax.experimental.pallas import tpu as pltpu

F32 = jnp.float32
BF16 = jnp.bfloat16

D_MODEL = 2048
DA_WIDTH = 1024
RW_WIDTH = 1024
DA_HEADS = 8
DA_V_DIM = 128
DA_QK_DIM = 64
RW_N = 64
RW_PAIRS = RW_WIDTH // 128
LORA_W, LORA_A, LORA_G = 64, 64, 160
LORA_PAD = 384
IN_COLS = 3 * DA_WIDTH + 3 * RW_WIDTH
NC = IN_COLS + 2 * LORA_PAD
N_GROUPS = 4
EXPERTS_PER_GROUP = 8
N_EXPERTS = 32
D_EXPERT = 512
NORM_EPS = 1e-6
LN_X_EPS = 64e-5
LAM_INIT = 0.8 - 0.6 * math.exp(-0.3 * 0)
LANES = 128
CHUNK = 64
EXPERT_BLOCK = 256
NEG = -1e30
VMEM_LIMIT = 48 * 1024 * 1024


def _dot(a, b):
    return jnp.dot(a, b, preferred_element_type=F32)


def _dot_nt(a, b):
    return lax.dot_general(a, b, (((1,), (1,)), ((), ())), preferred_element_type=F32)


def _dot_tn(a, b):
    return lax.dot_general(a, b, (((0,), (0,)), ((), ())), preferred_element_type=F32)


def _sigmoid(x):
    return 1.0 / (1.0 + jnp.exp(-x))


def _inproj_kernel(x_ref, g_ref, w_ref, o_ref, xn_ref):
    @pl.when(pl.program_id(1) == 0)
    def _():
        x = x_ref[...]
        ms = jnp.mean(x * x, axis=-1, keepdims=True)
        xn_ref[...] = (x * lax.rsqrt(ms + NORM_EPS) * g_ref[...]).astype(BF16)

    o_ref[...] = _dot(xn_ref[...], w_ref[...]).astype(o_ref.dtype)


def _inproj(x2, g, w_all, tm=512, tn=1152):
    t = x2.shape[0]
    return pl.pallas_call(
        _inproj_kernel,
        grid=(t // tm, NC // tn),
        in_specs=[
            pl.BlockSpec((tm, D_MODEL), lambda i, j: (i, 0)),
            pl.BlockSpec((1, D_MODEL), lambda i, j: (0, 0)),
            pl.BlockSpec((D_MODEL, tn), lambda i, j: (0, j)),
        ],
        out_specs=pl.BlockSpec((tm, tn), lambda i, j: (i, j)),
        out_shape=jax.ShapeDtypeStruct((t, NC), BF16),
        scratch_shapes=[pltpu.VMEM((tm, D_MODEL), BF16)],
        compiler_params=pltpu.CompilerParams(
            dimension_semantics=("parallel", "arbitrary"), vmem_limit_bytes=VMEM_LIMIT),
        name="inproj",
    )(x2, g, w_all)


def _da_kernel(slopes_ref, lam_ref, subg_ref, q_ref, k_ref, v_ref, o_ref, *, tq):
    h = pl.program_id(1)
    qi = pl.program_id(2)
    slope = slopes_ref[h]
    q = q_ref[...] * jnp.asarray(DA_QK_DIM ** -0.5, BF16)
    lane = lax.broadcasted_iota(jnp.int32, (tq, LANES), 1)
    zero = jnp.zeros_like(q)
    qc = (jnp.where(lane < DA_QK_DIM, q, zero), jnp.where(lane >= DA_QK_DIM, q, zero))
    q_start = qi * tq
    qpos = q_start + lax.broadcasted_iota(jnp.int32, (tq, 1), 0)
    kiota = lax.broadcasted_iota(jnp.int32, (1, tq), 1)

    def step(j, carry, masked):
        ks = pl.multiple_of(j * tq, tq)
        k = k_ref[pl.ds(ks, tq), :]
        v = v_ref[pl.ds(ks, tq), :]
        kpos = ks + kiota
        bias = slope * (kpos - q_start).astype(F32)
        out = []
        for c in range(2):
            m, l, acc = carry[3 * c:3 * c + 3]
            s = _dot_nt(qc[c], k) + bias
            if masked:
                s = jnp.where(kpos <= qpos, s, NEG)
            m_new = jnp.maximum(m, jnp.max(s, axis=-1, keepdims=True))
            alpha = jnp.exp(m - m_new)
            p = jnp.exp(s - m_new)
            l = alpha * l + jnp.sum(p, axis=-1, keepdims=True)
            acc = alpha * acc + _dot(p.astype(BF16), v)
            out += [m_new, l, acc]
        return tuple(out)

    init = (jnp.full((tq, 1), NEG, F32), jnp.zeros((tq, 1), F32), jnp.zeros((tq, DA_V_DIM), F32)) * 2
    carry = lax.fori_loop(0, qi, lambda j, c: step(j, c, False), init)
    m0, l0, a0, m1, l1, a1 = step(qi, carry, True)

    lv = lam_ref[...]
    lam = (jnp.exp(jnp.sum(lv[0:1] * lv[1:2], axis=-1, keepdims=True))
           - jnp.exp(jnp.sum(lv[2:3] * lv[3:4], axis=-1, keepdims=True)) + LAM_INIT)
    o = a0 / l0 - lam * (a1 / l1)
    ms = jnp.mean(o * o, axis=-1, keepdims=True)
    o = o * lax.rsqrt(ms + NORM_EPS) * subg_ref[...] * (1.0 - LAM_INIT)
    o_ref[...] = o.astype(o_ref.dtype)


def _diffattn(proj, slopes, lam4, subg, b, s, tq=256):
    nq = s // tq
    return pl.pallas_call(
        functools.partial(_da_kernel, tq=tq),
        grid=(b, DA_HEADS, nq),
        in_specs=[
            pl.BlockSpec(memory_space=pltpu.SMEM),
            pl.BlockSpec((4, DA_QK_DIM), lambda bi, h, qi: (0, 0)),
            pl.BlockSpec((1, DA_V_DIM), lambda bi, h, qi: (0, 0)),
            pl.BlockSpec((tq, LANES), lambda bi, h, qi: (bi * nq + qi, h)),
            pl.BlockSpec((s, LANES), lambda bi, h, qi: (bi, DA_HEADS + h)),
            pl.BlockSpec((s, LANES), lambda bi, h, qi: (bi, 2 * DA_HEADS + h)),
        ],
        out_specs=pl.BlockSpec((tq, LANES), lambda bi, h, qi: (bi * nq + qi, h)),
        out_shape=jax.ShapeDtypeStruct((b * s, DA_WIDTH), BF16),
        compiler_params=pltpu.CompilerParams(
            dimension_semantics=("parallel", "parallel", "arbitrary"), vmem_limit_bytes=VMEM_LIMIT),
        name="diffattn",
    )(slopes, lam4, subg, proj, proj, proj)


def _rwkv_kernel(r_ref, k_ref, v_ref, p1_ref, p2_ref, mu_ref, w0_ref, a0_ref, w2_ref, kk_ref, ka_ref,
                 rk_ref, lnw_ref, lnb_ref, o_ref, state_ref, crkv_ref, cp2_ref):
    L = CHUNK
    c = pl.program_id(1)

    @pl.when(c == 0)
    def _():
        state_ref[...] = jnp.zeros_like(state_ref)
        crkv_ref[...] = jnp.zeros_like(crkv_ref)
        cp2_ref[...] = jnp.zeros_like(cp2_ref)

    first = lax.broadcasted_iota(jnp.int32, (L, 1), 0) == 0

    def shift(x, carry):
        return jnp.where(first, carry, pltpu.roll(x, 1, 0))

    r_raw = r_ref[...].astype(F32)
    k_raw = k_ref[...].astype(F32)
    v_raw = v_ref[...].astype(F32)
    p2 = p2_ref[...].astype(F32)
    rs = shift(r_raw, crkv_ref[:, 0:RW_WIDTH])
    ks = shift(k_raw, crkv_ref[:, RW_WIDTH:2 * RW_WIDTH])
    vs = shift(v_raw, crkv_ref[:, 2 * RW_WIDTH:3 * RW_WIDTH])
    lin = p1_ref[...].astype(F32) + shift(p2, cp2_ref[...])
    crkv_ref[:, 0:RW_WIDTH] = r_raw[L - 1:L, :]
    crkv_ref[:, RW_WIDTH:2 * RW_WIDTH] = k_raw[L - 1:L, :]
    crkv_ref[:, 2 * RW_WIDTH:3 * RW_WIDTH] = v_raw[L - 1:L, :]
    cp2_ref[...] = p2[L - 1:L, :]

    r = r_raw + (rs - r_raw) * mu_ref[0:1, :]
    k = k_raw + (ks - k_raw) * mu_ref[1:2, :]
    v = v_raw + (vs - v_raw) * mu_ref[2:3, :]

    ll = lax.broadcasted_iota(jnp.int32, (L, LORA_PAD), 1)
    z = jnp.where(ll < LORA_W, jnp.tanh(lin),
                  jnp.where(ll < LORA_W + LORA_A, lin,
                            jnp.where(ll < LORA_W + LORA_A + LORA_G, _sigmoid(lin), 0.0)))
    d = _dot(z.astype(BF16), w2_ref[...])
    logw = -_sigmoid(w0_ref[...] + d[:, 0:RW_WIDTH]) * math.exp(-0.5)
    a_sig = _sigmoid(a0_ref[...] + d[:, RW_WIDTH:2 * RW_WIDTH])
    g = d[:, 2 * RW_WIDTH:3 * RW_WIDTH]

    li = lax.broadcasted_iota(jnp.int32, (LANES, LANES), 0)
    lj = lax.broadcasted_iota(jnp.int32, (LANES, LANES), 1)
    same_head = jnp.where((li // RW_N) == (lj // RW_N), 1.0, 0.0).astype(BF16)

    def head_sum(x):
        xb = x.astype(BF16)
        return jnp.concatenate(
            [_dot(xb[:, p * LANES:(p + 1) * LANES], same_head) for p in range(RW_PAIRS)], axis=1)

    kk = k * kk_ref[...]
    kk = kk * lax.rsqrt(jnp.maximum(head_sum(kk * kk), 1e-24))
    k = k * (1.0 + (a_sig - 1.0) * ka_ref[...])
    bonus = head_sum(r * k * rk_ref[...])

    ti = lax.broadcasted_iota(jnp.int32, (L, L), 0)
    tj = lax.broadcasted_iota(jnp.int32, (L, L), 1)
    tri = jnp.where(ti >= tj, 1.0, 0.0).astype(BF16)
    w_hi = logw.astype(BF16)
    w_r1 = logw - w_hi.astype(F32)
    w_mid = w_r1.astype(BF16)
    w_lo = (w_r1 - w_mid.astype(F32)).astype(BF16)
    cw = _dot(tri, w_hi) + _dot(tri, w_mid) + _dot(tri, w_lo)
    cw_last = cw[L - 1:L, :]
    e_in = jnp.exp(cw)
    e_out = jnp.exp(-cw)
    e_end = jnp.exp(cw_last - cw)
    w_end = jnp.exp(cw_last)

    at = -kk * jnp.exp(cw - logw)
    rt = r * e_in
    b = kk * a_sig
    bt = b * e_out
    kt = k * e_out
    bd = b * e_end
    kd = k * e_end

    lane = lax.broadcasted_iota(jnp.int32, (L, LANES), 1)
    lo = lane < RW_N

    def stack(x, p):
        xs = x[:, p * LANES:(p + 1) * LANES]
        return jnp.concatenate([jnp.where(lo, xs, 0.0), jnp.where(lo, 0.0, xs)], axis=0).astype(BF16)

    si = lax.broadcasted_iota(jnp.int32, (2 * L, 2 * L), 0)
    sj = lax.broadcasted_iota(jnp.int32, (2 * L, 2 * L), 1)
    strict = (si % L) > (sj % L)
    incl = (si % L) >= (sj % L)
    eye = jnp.where(si == sj, 1.0, 0.0)

    ys = []
    for p in range(RW_PAIRS):
        a_s, r_s, b_s, k_s = stack(at, p), stack(rt, p), stack(bt, p), stack(kt, p)
        bd_s, kd_s, v_s = stack(bd, p), stack(kd, p), stack(v, p)
        bk = jnp.concatenate([b_s, k_s], axis=0)
        aa = _dot_nt(a_s, bk)
        rr = _dot_nt(r_s, bk)
        a_ab = jnp.where(strict, aa[:, 0:2 * L], 0.0)
        a_ak = jnp.where(strict, aa[:, 2 * L:4 * L], 0.0)
        a_rb = jnp.where(incl, rr[:, 0:2 * L], 0.0)
        a_rk = jnp.where(incl, rr[:, 2 * L:4 * L], 0.0)
        inv = eye + a_ab
        apow = a_ab
        for _ in range(int(math.log2(L)) - 1):
            ab = apow.astype(BF16)
            apow = _dot(ab, ab)
            inv = inv + _dot(inv.astype(BF16), apow.astype(BF16))
        s_old = state_ref[p]
        s_b = s_old.astype(BF16)
        x = _dot_nt(a_s, s_b) + _dot(a_ak.astype(BF16), v_s)
        u = _dot(inv.astype(BF16), x.astype(BF16))
        u_b = u.astype(BF16)
        y = _dot_nt(r_s, s_b) + _dot(a_rb.astype(BF16), u_b) + _dot(a_rk.astype(BF16), v_s)
        state_ref[p] = (s_old * w_end[:, p * LANES:(p + 1) * LANES]
                        + _dot_tn(u_b, bd_s) + _dot_tn(v_s, kd_s))
        ys.append(y[0:L] + y[L:2 * L])
    y = jnp.concatenate(ys, axis=1)

    mean = head_sum(y) * (1.0 / RW_N)
    yc = y - mean
    var = head_sum(yc * yc) * (1.0 / RW_N)
    y = yc * lax.rsqrt(var + LN_X_EPS) * lnw_ref[...] + lnb_ref[...]
    y = y + bonus * v
    o_ref[...] = (y * g).astype(o_ref.dtype)


def _rwkv(proj, mu_rkv, w0, a0, w2cat, k_k, k_a, r_k, ln_w, ln_b, b, s):
    L = CHUNK
    nc = s // L
    row = lambda n: pl.BlockSpec((1, n), lambda bi, c: (0, 0))
    rkv_col0 = 3 * DA_WIDTH // RW_WIDTH
    lora_col0 = IN_COLS // LORA_PAD
    return pl.pallas_call(
        _rwkv_kernel,
        grid=(b, nc),
        in_specs=[
            pl.BlockSpec((L, RW_WIDTH), lambda bi, c: (bi * nc + c, rkv_col0)),
            pl.BlockSpec((L, RW_WIDTH), lambda bi, c: (bi * nc + c, rkv_col0 + 1)),
            pl.BlockSpec((L, RW_WIDTH), lambda bi, c: (bi * nc + c, rkv_col0 + 2)),
            pl.BlockSpec((L, LORA_PAD), lambda bi, c: (bi * nc + c, lora_col0)),
            pl.BlockSpec((L, LORA_PAD), lambda bi, c: (bi * nc + c, lora_col0 + 1)),
            pl.BlockSpec((3, RW_WIDTH), lambda bi, c: (0, 0)),
            row(RW_WIDTH), row(RW_WIDTH),
            pl.BlockSpec((LORA_PAD, 3 * RW_WIDTH), lambda bi, c: (0, 0)),
            row(RW_WIDTH), row(RW_WIDTH), row(RW_WIDTH), row(RW_WIDTH), row(RW_WIDTH),
        ],
        out_specs=pl.BlockSpec((L, RW_WIDTH), lambda bi, c: (bi * nc + c, 0)),
        out_shape=jax.ShapeDtypeStruct((b * s, RW_WIDTH), BF16),
        scratch_shapes=[
            pltpu.VMEM((RW_PAIRS, LANES, LANES), F32),
            pltpu.VMEM((1, 3 * RW_WIDTH), F32),
            pltpu.VMEM((1, LORA_PAD), F32),
        ],
        compiler_params=pltpu.CompilerParams(
            dimension_semantics=("parallel", "arbitrary"), vmem_limit_bytes=VMEM_LIMIT),
        name="rwkv",
    )(proj, proj, proj, proj, proj, mu_rkv, w0, a0, w2cat, k_k, k_a, r_k, ln_w, ln_b)


ROUTE_E, ROUTE_G, ROUTE_R = 0, 2, 4
EXPERT_LANE0 = N_GROUPS


def _outproj_kernel(x_ref, oda_ref, orw_ref, wout_ref, g_ref, wrh_ref, wrl_ref, br_ref,
                    h_ref, xn_ref, route_ref, cnt_ref, base_ref, *, tm):
    @pl.when(pl.program_id(0) == 0)
    def _():
        base_ref[...] = jnp.zeros_like(base_ref)

    mix = _dot(oda_ref[...], wout_ref[0:DA_WIDTH, :]) + _dot(orw_ref[...], wout_ref[DA_WIDTH:, :])
    h = x_ref[...] + mix
    h_ref[...] = h
    ms = jnp.mean(h * h, axis=-1, keepdims=True)
    xn = h * lax.rsqrt(ms + NORM_EPS) * g_ref[...]
    xn_ref[...] = xn

    x_hi = xn.astype(BF16)
    x_lo = (xn - x_hi.astype(F32)).astype(BF16)
    lg = (_dot(x_hi, wrh_ref[...]) + _dot(x_lo, wrh_ref[...]) + _dot(x_hi, wrl_ref[...])) + br_ref[...]

    lane = lax.broadcasted_iota(jnp.int32, (tm, LANES), 1)
    big = jnp.int32(1 << 20)
    gl = jnp.where(lane < N_GROUPS, lg, NEG)
    gmax = jnp.max(gl, axis=-1, keepdims=True)
    grp = jnp.min(jnp.where(gl == gmax, lane, big), axis=-1, keepdims=True)
    p_grp = 1.0 / jnp.sum(jnp.exp(gl - gmax), axis=-1, keepdims=True)
    eg = jnp.where(lane >= EXPERT_LANE0, (lane - EXPERT_LANE0) // EXPERTS_PER_GROUP, -1)
    el = jnp.where(eg == grp, lg, NEG)
    emax = jnp.max(el, axis=-1, keepdims=True)
    pe = jnp.exp(el - emax)
    probs = pe / jnp.sum(pe, axis=-1, keepdims=True)
    probs = jnp.where(eg == grp, probs, -1.0)
    p1 = jnp.max(probs, axis=-1, keepdims=True)
    i1 = jnp.min(jnp.where(probs == p1, lane, big), axis=-1, keepdims=True)
    probs2 = jnp.where(lane == i1, -1.0, probs)
    p2 = jnp.max(probs2, axis=-1, keepdims=True)
    i2 = jnp.min(jnp.where(probs2 == p2, lane, big), axis=-1, keepdims=True)
    gate1 = p_grp * p1 / (p1 + p2)
    gate2 = p_grp * p2 / (p1 + p2)

    oh1 = jnp.where(lane == i1, 1.0, 0.0)
    oh2 = jnp.where(lane == i2, 1.0, 0.0)
    ri = lax.broadcasted_iota(jnp.int32, (tm, tm), 0)
    rj = lax.broadcasted_iota(jnp.int32, (tm, tm), 1)
    lower = jnp.where(ri > rj, 1.0, 0.0).astype(BF16)
    base = base_ref[...]
    tot1 = jnp.sum(oh1, axis=0, keepdims=True)
    c1 = base + _dot(lower, oh1.astype(BF16))
    c2 = base + tot1 + _dot(lower, oh2.astype(BF16))
    rank1 = jnp.sum(oh1 * c1, axis=-1, keepdims=True)
    rank2 = jnp.sum(oh2 * c2, axis=-1, keepdims=True)
    base = base + tot1 + jnp.sum(oh2, axis=0, keepdims=True)
    base_ref[...] = base
    cnt_ref[...] = base

    e1 = (i1 - EXPERT_LANE0).astype(F32)
    e2 = (i2 - EXPERT_LANE0).astype(F32)
    rec = jnp.zeros((tm, LANES), F32)
    for ln, val in ((ROUTE_E, e1), (ROUTE_E + 1, e2), (ROUTE_G, gate1), (ROUTE_G + 1, gate2),
                    (ROUTE_R, rank1), (ROUTE_R + 1, rank2)):
        rec = jnp.where(lane == ln, val, rec)
    route_ref[...] = rec


def _outproj(x2, o_da, o_rw, w_out, g_ffn, wr_hi, wr_lo, b_r, tm=256):
    t = x2.shape[0]
    const = lambda shape: pl.BlockSpec(shape, lambda i: (0, 0))
    return pl.pallas_call(
        functools.partial(_outproj_kernel, tm=tm),
        grid=(t // tm,),
        in_specs=[
            pl.BlockSpec((tm, D_MODEL), lambda i: (i, 0)),
            pl.BlockSpec((tm, DA_WIDTH), lambda i: (i, 0)),
            pl.BlockSpec((tm, RW_WIDTH), lambda i: (i, 0)),
            const((D_MODEL, D_MODEL)),
            const((1, D_MODEL)),
            const((D_MODEL, LANES)),
            const((D_MODEL, LANES)),
            const((1, LANES)),
        ],
        out_specs=[
            pl.BlockSpec((tm, D_MODEL), lambda i: (i, 0)),
            pl.BlockSpec((tm, D_MODEL), lambda i: (i, 0)),
            pl.BlockSpec((tm, LANES), lambda i: (i, 0)),
            const((1, LANES)),
        ],
        out_shape=[
            jax.ShapeDtypeStruct((t, D_MODEL), F32),
            jax.ShapeDtypeStruct((t, D_MODEL), F32),
            jax.ShapeDtypeStruct((t, LANES), F32),
            jax.ShapeDtypeStruct((1, LANES), F32),
        ],
        scratch_shapes=[pltpu.VMEM((1, LANES), F32)],
        compiler_params=pltpu.CompilerParams(
            dimension_semantics=("arbitrary",), vmem_limit_bytes=VMEM_LIMIT),
        name="outproj_router",
    )(x2, o_da, o_rw, w_out, g_ffn, wr_hi, wr_lo, b_r)


DMA_WINDOW = 64


def _row_copy(src_hbm, dst_ref, src_row, dst_row, sem):
    return pltpu.make_async_copy(src_hbm.at[pl.ds(src_row, 1)], dst_ref.at[pl.ds(dst_row, 1)], sem)


def _dispatch_kernel(dest_ref, x_hbm, xb_in_hbm, xb_hbm, sem, *, td):
    del xb_in_hbm
    t0 = pl.program_id(0) * td

    def wait_pair():
        _row_copy(x_hbm, xb_hbm, 0, 0, sem).wait()
        _row_copy(x_hbm, xb_hbm, 0, 0, sem).wait()

    def body(i, _):
        tok = t0 + i
        _row_copy(x_hbm, xb_hbm, tok, dest_ref[2 * tok], sem).start()
        _row_copy(x_hbm, xb_hbm, tok, dest_ref[2 * tok + 1], sem).start()

        @pl.when(i >= DMA_WINDOW)
        def _():
            wait_pair()
        return 0

    lax.fori_loop(0, td, body, 0)
    lax.fori_loop(0, DMA_WINDOW, lambda i, _: (wait_pair(), 0)[1], 0)


def _dispatch(dest_flat, xn2, xb_zero, td=1024):
    t = xn2.shape[0]
    td = min(td, t)
    return pl.pallas_call(
        functools.partial(_dispatch_kernel, td=td),
        grid_spec=pltpu.PrefetchScalarGridSpec(
            num_scalar_prefetch=1,
            grid=(t // td,),
            in_specs=[pl.BlockSpec(memory_space=pl.ANY), pl.BlockSpec(memory_space=pl.ANY)],
            out_specs=pl.BlockSpec(memory_space=pl.ANY),
            scratch_shapes=[pltpu.SemaphoreType.DMA],
        ),
        out_shape=jax.ShapeDtypeStruct(xb_zero.shape, xb_zero.dtype),
        input_output_aliases={2: 0},
        compiler_params=pltpu.CompilerParams(dimension_semantics=("arbitrary",)),
        name="dispatch",
    )(dest_flat, xn2, xb_zero)


def _experts_kernel(be_ref, nu_ref, xb_ref, wg_ref, wu_ref, wd_ref, yb_ref):
    del be_ref

    @pl.when(pl.program_id(0) < nu_ref[0])
    def _():
        x = xb_ref[...].astype(BF16)
        hg = _dot(x, wg_ref[...])
        hu = _dot(x, wu_ref[...])
        hid = hg * _sigmoid(hg) * hu
        yb_ref[...] = _dot(hid.astype(BF16), wd_ref[...])

    @pl.when(pl.program_id(0) >= nu_ref[0])
    def _():
        yb_ref[...] = jnp.zeros_like(yb_ref)


def _experts(block_e, n_used, xb, w_gate, w_up, w_down):
    nb = xb.shape[0] // EXPERT_BLOCK
    rows = lambda i, be, nu: (jnp.minimum(i, nu[0] - 1), 0)
    return pl.pallas_call(
        _experts_kernel,
        grid_spec=pltpu.PrefetchScalarGridSpec(
            num_scalar_prefetch=2,
            grid=(nb,),
            in_specs=[
                pl.BlockSpec((EXPERT_BLOCK, D_MODEL), rows),
                pl.BlockSpec((None, D_MODEL, D_EXPERT), lambda i, be, nu: (be[i], 0, 0)),
                pl.BlockSpec((None, D_MODEL, D_EXPERT), lambda i, be, nu: (be[i], 0, 0)),
                pl.BlockSpec((None, D_EXPERT, D_MODEL), lambda i, be, nu: (be[i], 0, 0)),
            ],
            out_specs=pl.BlockSpec((EXPERT_BLOCK, D_MODEL), lambda i, be, nu: (i, 0)),
        ),
        out_shape=jax.ShapeDtypeStruct(xb.shape, F32),
        compiler_params=pltpu.CompilerParams(
            dimension_semantics=("arbitrary",), vmem_limit_bytes=VMEM_LIMIT),
        name="experts",
    )(block_e, n_used, xb, w_gate, w_up, w_down)


def _combine_kernel(dest_ref, route_ref, h_ref, g_ref, yb_hbm, o_ref, y1_ref, y2_ref, sem, *, tc):
    t0 = pl.program_id(0) * tc

    def start(i, _):
        tok = t0 + i
        _row_copy(yb_hbm, y1_ref, dest_ref[2 * tok], i, sem).start()
        _row_copy(yb_hbm, y2_ref, dest_ref[2 * tok + 1], i, sem).start()
        return 0

    def wait(i, _):
        _row_copy(yb_hbm, y1_ref, 0, 0, sem).wait()
        _row_copy(yb_hbm, y2_ref, 0, 0, sem).wait()
        return 0

    lax.fori_loop(0, tc, start, 0)
    lax.fori_loop(0, tc, wait, 0)

    rec = route_ref[...]
    lane = lax.broadcasted_iota(jnp.int32, (tc, LANES), 1)
    g1 = jnp.sum(jnp.where(lane == ROUTE_G, rec, 0.0), axis=-1, keepdims=True)
    g2 = jnp.sum(jnp.where(lane == ROUTE_G + 1, rec, 0.0), axis=-1, keepdims=True)
    h = h_ref[...] + (y1_ref[...] * g1 + y2_ref[...] * g2)
    ms = jnp.mean(h * h, axis=-1, keepdims=True)
    o_ref[...] = h * lax.rsqrt(ms + NORM_EPS) * g_ref[...]


def _combine(dest_flat, route, h1, g_final, yb, tc=256):
    t = h1.shape[0]
    return pl.pallas_call(
        functools.partial(_combine_kernel, tc=tc),
        grid_spec=pltpu.PrefetchScalarGridSpec(
            num_scalar_prefetch=1,
            grid=(t // tc,),
            in_specs=[
                pl.BlockSpec((tc, LANES), lambda i, d: (i, 0)),
                pl.BlockSpec((tc, D_MODEL), lambda i, d: (i, 0)),
                pl.BlockSpec((1, D_MODEL), lambda i, d: (0, 0)),
                pl.BlockSpec(memory_space=pl.ANY),
            ],
            out_specs=pl.BlockSpec((tc, D_MODEL), lambda i, d: (i, 0)),
            scratch_shapes=[
                pltpu.VMEM((tc, D_MODEL), F32),
                pltpu.VMEM((tc, D_MODEL), F32),
                pltpu.SemaphoreType.DMA,
            ],
        ),
        out_shape=jax.ShapeDtypeStruct((t, D_MODEL), F32),
        compiler_params=pltpu.CompilerParams(
            dimension_semantics=("arbitrary",), vmem_limit_bytes=VMEM_LIMIT),
        name="combine",
    )(dest_flat, route, h1, g_final, yb)


def _pad_cols(w, n):
    return jnp.pad(w, ((0, 0), (0, n - w.shape[1])))


def kernel(x, g_mix, w_in, w_out, da_lambda_q1, da_lambda_k1, da_lambda_q2, da_lambda_k2, da_subln_g,
           rw_mu_x, rw_mu_rkv, rw_w0, rw_w1, rw_w2, rw_a0, rw_a1, rw_a2, rw_g1, rw_g2, rw_k_k, rw_k_a,
           rw_r_k, rw_ln_w, rw_ln_b, g_ffn, moe_w_group, moe_b_group, moe_w_expert, moe_b_expert,
           moe_w_gate, moe_w_up, moe_w_down, g_final):
    b, s, d = x.shape
    t = b * s
    x2 = x.reshape(t, d)

    mu = rw_mu_x[0]
    lora = (rw_w1[0], rw_a1[0], rw_g1[0])
    keep = _pad_cols(jnp.concatenate([(1.0 - mu[i])[:, None] * w for i, w in enumerate(lora)], axis=1), LORA_PAD)
    prev = _pad_cols(jnp.concatenate([mu[i][:, None] * w for i, w in enumerate(lora)], axis=1), LORA_PAD)
    w_all = jnp.concatenate([w_in[0], keep, prev], axis=1).astype(BF16)
    proj = _inproj(x2, g_mix, w_all)

    slopes = jnp.asarray([2.0 ** (-8.0 * (i + 1) / DA_HEADS) for i in range(DA_HEADS)], F32)
    lam4 = jnp.concatenate([da_lambda_q1, da_lambda_k1, da_lambda_q2, da_lambda_k2], axis=0)
    o_da = _diffattn(proj, slopes, lam4, da_subln_g, b, s)

    w2cat = jnp.zeros((LORA_PAD, 3 * RW_WIDTH), F32)
    w2cat = w2cat.at[0:LORA_W, 0:RW_WIDTH].set(rw_w2[0])
    w2cat = w2cat.at[LORA_W:LORA_W + LORA_A, RW_WIDTH:2 * RW_WIDTH].set(rw_a2[0])
    w2cat = w2cat.at[LORA_W + LORA_A:LORA_W + LORA_A + LORA_G, 2 * RW_WIDTH:].set(rw_g2[0])
    o_rw = _rwkv(proj, rw_mu_rkv[0], rw_w0, rw_a0, w2cat.astype(BF16), rw_k_k, rw_k_a,
                 rw_r_k.reshape(1, RW_WIDTH), rw_ln_w, rw_ln_b, b, s)

    w_r = _pad_cols(jnp.concatenate([moe_w_group[0], moe_w_expert[0]], axis=1), LANES)
    wr_hi = w_r.astype(BF16)
    wr_lo = (w_r - wr_hi.astype(F32)).astype(BF16)
    b_r = _pad_cols(jnp.concatenate([moe_b_group[0], moe_b_expert[0].reshape(-1)])[None, :], LANES)
    h1, xn2, route, cnt = _outproj(x2, o_da, o_rw, w_out[0].astype(BF16), g_ffn, wr_hi, wr_lo, b_r)

    counts = cnt[0, EXPERT_LANE0:EXPERT_LANE0 + N_EXPERTS].astype(jnp.int32)
    pcounts = ((counts + EXPERT_BLOCK - 1) // EXPERT_BLOCK) * EXPERT_BLOCK
    pends = jnp.cumsum(pcounts)
    pstarts = pends - pcounts
    e_idx = route[:, ROUTE_E:ROUTE_E + 2].astype(jnp.int32)
    rank = route[:, ROUTE_R:ROUTE_R + 2].astype(jnp.int32)
    dest = (pstarts[e_idx] + rank).reshape(-1)
    n_blocks = (2 * t) // EXPERT_BLOCK + N_EXPERTS
    block_e = jnp.minimum(
        jnp.searchsorted(pends, jnp.arange(n_blocks, dtype=jnp.int32) * EXPERT_BLOCK, side='right'),
        N_EXPERTS - 1).astype(jnp.int32)
    n_used = (pends[-1:] // EXPERT_BLOCK).astype(jnp.int32)

    xb = _dispatch(dest, xn2, jnp.zeros((n_blocks * EXPERT_BLOCK, d), F32))
    yb = _experts(block_e, n_used, xb, moe_w_gate[0].astype(BF16), moe_w_up[0].astype(BF16),
                  moe_w_down[0].astype(BF16))
    out = _combine(dest, route, h1, g_final[None, :], yb)
    return out.reshape(b, s, d)
```

```python
import functools
import math

import jax
import jax.numpy as jnp
from jax import lax
from jax.experimental import pallas as pl
from jax.experimental.pallas import tpu as pltpu

F32 = jnp.float32
BF16 = jnp.bfloat16

D_MODEL = 2048
DA_WIDTH = 1024
RW_WIDTH = 1024
DA_HEADS = 8
DA_V_DIM = 128
DA_QK_DIM = 64
RW_N = 64
RW_PAIRS = RW_WIDTH // 128
LORA_W, LORA_A, LORA_G = 64, 64, 160
LORA_PAD = 384
IN_COLS = 3 * DA_WIDTH + 3 * RW_WIDTH
NC = IN_COLS + 2 * LORA_PAD
N_GROUPS = 4
EXPERTS_PER_GROUP = 8
N_EXPERTS = 32
D_EXPERT = 512
NORM_EPS = 1e-6
LN_X_EPS = 64e-5
LAM_INIT = 0.8 - 0.6 * math.exp(-0.3 * 0)
LANES = 128
CHUNK = 64
EXPERT_BLOCK = 256
NEG = -1e30
VMEM_LIMIT = 48 * 1024 * 1024


def _dot(a, b):
    return jnp.dot(a, b, preferred_element_type=F32)


def _dot_nt(a, b):
    return lax.dot_general(a, b, (((1,), (1,)), ((), ())), preferred_element_type=F32)


def _dot_tn(a, b):
    return lax.dot_general(a, b, (((0,), (0,)), ((), ())), preferred_element_type=F32)


def _sigmoid(x):
    return 1.0 / (1.0 + jnp.exp(-x))


def _inproj_kernel(x_ref, g_ref, w_ref, o_ref, xn_ref):
    @pl.when(pl.program_id(1) == 0)
    def _():
        x = x_ref[...]
        ms = jnp.mean(x * x, axis=-1, keepdims=True)
        xn_ref[...] = (x * lax.rsqrt(ms + NORM_EPS) * g_ref[...]).astype(BF16)

    o_ref[...] = _dot(xn_ref[...], w_ref[...]).astype(o_ref.dtype)


def _inproj(x2, g, w_all, tm=512, tn=1152):
    t = x2.shape[0]
    return pl.pallas_call(
        _inproj_kernel,
        grid=(t // tm, NC // tn),
        in_specs=[
            pl.BlockSpec((tm, D_MODEL), lambda i, j: (i, 0)),
            pl.BlockSpec((1, D_MODEL), lambda i, j: (0, 0)),
            pl.BlockSpec((D_MODEL, tn), lambda i, j: (0, j)),
        ],
        out_specs=pl.BlockSpec((tm, tn), lambda i, j: (i, j)),
        out_shape=jax.ShapeDtypeStruct((t, NC), BF16),
        scratch_shapes=[pltpu.VMEM((tm, D_MODEL), BF16)],
        compiler_params=pltpu.CompilerParams(
            dimension_semantics=("parallel", "arbitrary"), vmem_limit_bytes=VMEM_LIMIT),
        name="inproj",
    )(x2, g, w_all)


def _da_kernel(slopes_ref, lam_ref, subg_ref, q_ref, k_ref, v_ref, o_ref, *, tq):
    h = pl.program_id(1)
    qi = pl.program_id(2)
    slope = slopes_ref[h]
    q = q_ref[...] * jnp.asarray(DA_QK_DIM ** -0.5, BF16)
    lane = lax.broadcasted_iota(jnp.int32, (tq, LANES), 1)
    zero = jnp.zeros_like(q)
    qc = (jnp.where(lane < DA_QK_DIM, q, zero), jnp.where(lane >= DA_QK_DIM, q, zero))
    q_start = qi * tq
    qpos = q_start + lax.broadcasted_iota(jnp.int32, (tq, 1), 0)
    kiota = lax.broadcasted_iota(jnp.int32, (1, tq), 1)

    def step(j, carry, masked):
        ks = pl.multiple_of(j * tq, tq)
        k = k_ref[pl.ds(ks, tq), :]
        v = v_ref[pl.ds(ks, tq), :]
        kpos = ks + kiota
        bias = slope * (kpos - q_start).astype(F32)
        out = []
        for c in range(2):
            m, l, acc = carry[3 * c:3 * c + 3]
            s = _dot_nt(qc[c], k) + bias
            if masked:
                s = jnp.where(kpos <= qpos, s, NEG)
            m_new = jnp.maximum(m, jnp.max(s, axis=-1, keepdims=True))
            alpha = jnp.exp(m - m_new)
            p = jnp.exp(s - m_new)
            l = alpha * l + jnp.sum(p, axis=-1, keepdims=True)
            acc = alpha * acc + _dot(p.astype(BF16), v)
            out += [m_new, l, acc]
        return tuple(out)

    init = (jnp.full((tq, 1), NEG, F32), jnp.zeros((tq, 1), F32), jnp.zeros((tq, DA_V_DIM), F32)) * 2
    carry = lax.fori_loop(0, qi, lambda j, c: step(j, c, False), init)
    m0, l0, a0, m1, l1, a1 = step(qi, carry, True)

    lv = lam_ref[...]
    lam = (jnp.exp(jnp.sum(lv[0:1] * lv[1:2], axis=-1, keepdims=True))
           - jnp.exp(jnp.sum(lv[2:3] * lv[3:4], axis=-1, keepdims=True)) + LAM_INIT)
    o = a0 / l0 - lam * (a1 / l1)
    ms = jnp.mean(o * o, axis=-1, keepdims=True)
    o = o * lax.rsqrt(ms + NORM_EPS) * subg_ref[...] * (1.0 - LAM_INIT)
    o_ref[...] = o.astype(o_ref.dtype)


def _diffattn(proj, slopes, lam4, subg, b, s, tq=256):
    nq = s // tq
    return pl.pallas_call(
        functools.partial(_da_kernel, tq=tq),
        grid=(b, DA_HEADS, nq),
        in_specs=[
            pl.BlockSpec(memory_space=pltpu.SMEM),
            pl.BlockSpec((4, DA_QK_DIM), lambda bi, h, qi: (0, 0)),
            pl.BlockSpec((1, DA_V_DIM), lambda bi, h, qi: (0, 0)),
            pl.BlockSpec((tq, LANES), lambda bi, h, qi: (bi * nq + qi, h)),
            pl.BlockSpec((s, LANES), lambda bi, h, qi: (bi, DA_HEADS + h)),
            pl.BlockSpec((s, LANES), lambda bi, h, qi: (bi, 2 * DA_HEADS + h)),
        ],
        out_specs=pl.BlockSpec((tq, LANES), lambda bi, h, qi: (bi * nq + qi, h)),
        out_shape=jax.ShapeDtypeStruct((b * s, DA_WIDTH), BF16),
        compiler_params=pltpu.CompilerParams(
            dimension_semantics=("parallel", "parallel", "arbitrary"), vmem_limit_bytes=VMEM_LIMIT),
        name="diffattn",
    )(slopes, lam4, subg, proj, proj, proj)


def _rwkv_kernel(r_ref, k_ref, v_ref, p1_ref, p2_ref, mu_ref, w0_ref, a0_ref, w2_ref, kk_ref, ka_ref,
                 rk_ref, lnw_ref, lnb_ref, o_ref, state_ref, crkv_ref, cp2_ref):
    L = CHUNK
    c = pl.program_id(1)

    @pl.when(c == 0)
    def _():
        state_ref[...] = jnp.zeros_like(state_ref)
        crkv_ref[...] = jnp.zeros_like(crkv_ref)
        cp2_ref[...] = jnp.zeros_like(cp2_ref)

    first = lax.broadcasted_iota(jnp.int32, (L, 1), 0) == 0

    def shift(x, carry):
        return jnp.where(first, carry, pltpu.roll(x, 1, 0))

    r_raw = r_ref[...].astype(F32)
    k_raw = k_ref[...].astype(F32)
    v_raw = v_ref[...].astype(F32)
    p2 = p2_ref[...].astype(F32)
    rs = shift(r_raw, crkv_ref[:, 0:RW_WIDTH])
    ks = shift(k_raw, crkv_ref[:, RW_WIDTH:2 * RW_WIDTH])
    vs = shift(v_raw, crkv_ref[:, 2 * RW_WIDTH:3 * RW_WIDTH])
    lin = p1_ref[...].astype(F32) + shift(p2, cp2_ref[...])
    crkv_ref[:, 0:RW_WIDTH] = r_raw[L - 1:L, :]
    crkv_ref[:, RW_WIDTH:2 * RW_WIDTH] = k_raw[L - 1:L, :]
    crkv_ref[:, 2 * RW_WIDTH:3 * RW_WIDTH] = v_raw[L - 1:L, :]
    cp2_ref[...] = p2[L - 1:L, :]

    r = r_raw + (rs - r_raw) * mu_ref[0:1, :]
    k = k_raw + (ks - k_raw) * mu_ref[1:2, :]
    v = v_raw + (vs - v_raw) * mu_ref[2:3, :]

    ll = lax.broadcasted_iota(jnp.int32, (L, LORA_PAD), 1)
    z = jnp.where(ll < LORA_W, jnp.tanh(lin),
                  jnp.where(ll < LORA_W + LORA_A, lin,
                            jnp.where(ll < LORA_W + LORA_A + LORA_G, _sigmoid(lin), 0.0)))
    d = _dot(z.astype(BF16), w2_ref[...])
    logw = -_sigmoid(w0_ref[...] + d[:, 0:RW_WIDTH]) * math.exp(-0.5)
    a_sig = _sigmoid(a0_ref[...] + d[:, RW_WIDTH:2 * RW_WIDTH])
    g = d[:, 2 * RW_WIDTH:3 * RW_WIDTH]

    li = lax.broadcasted_iota(jnp.int32, (LANES, LANES), 0)
    lj = lax.broadcasted_iota(jnp.int32, (LANES, LANES), 1)
    same_head = jnp.where((li // RW_N) == (lj // RW_N), 1.0, 0.0).astype(BF16)

    def head_sum(x):
        xb = x.astype(BF16)
        return jnp.concatenate(
            [_dot(xb[:, p * LANES:(p + 1) * LANES], same_head) for p in range(RW_PAIRS)], axis=1)

    kk = k * kk_ref[...]
    kk = kk * lax.rsqrt(jnp.maximum(head_sum(kk * kk), 1e-24))
    k = k * (1.0 + (a_sig - 1.0) * ka_ref[...])
    bonus = head_sum(r * k * rk_ref[...])

    ti = lax.broadcasted_iota(jnp.int32, (L, L), 0)
    tj = lax.broadcasted_iota(jnp.int32, (L, L), 1)
    tri = jnp.where(ti >= tj, 1.0, 0.0).astype(BF16)
    w_hi = logw.astype(BF16)
    w_r1 = logw - w_hi.astype(F32)
    w_mid = w_r1.astype(BF16)
    w_lo = (w_r1 - w_mid.astype(F32)).astype(BF16)
    cw = _dot(tri, w_hi) + _dot(tri, w_mid) + _dot(tri, w_lo)
    cw_last = cw[L - 1:L, :]
    e_in = jnp.exp(cw)
    e_out = jnp.exp(-cw)
    e_end = jnp.exp(cw_last - cw)
    w_end = jnp.exp(cw_last)

    at = -kk * jnp.exp(cw - logw)
    rt = r * e_in
    b = kk * a_sig
    bt = b * e_out
    kt = k * e_out
    bd = b * e_end
    kd = k * e_end

    lane = lax.broadcasted_iota(jnp.int32, (L, LANES), 1)
    lo = lane < RW_N

    def stack(x, p):
        xs = x[:, p * LANES:(p + 1) * LANES]
        return jnp.concatenate([jnp.where(lo, xs, 0.0), jnp.where(lo, 0.0, xs)], axis=0).astype(BF16)

    si = lax.broadcasted_iota(jnp.int32, (2 * L, 2 * L), 0)
    sj = lax.broadcasted_iota(jnp.int32, (2 * L, 2 * L), 1)
    strict = (si % L) > (sj % L)
    incl = (si % L) >= (sj % L)
    eye = jnp.where(si == sj, 1.0, 0.0)

    ys = []
    for p in range(RW_PAIRS):
        a_s, r_s, b_s, k_s = stack(at, p), stack(rt, p), stack(bt, p), stack(kt, p)
        bd_s, kd_s, v_s = stack(bd, p), stack(kd, p), stack(v, p)
        bk = jnp.concatenate([b_s, k_s], axis=0)
        aa = _dot_nt(a_s, bk)
        rr = _dot_nt(r_s, bk)
        a_ab = jnp.where(strict, aa[:, 0:2 * L], 0.0)
        a_ak = jnp.where(strict, aa[:, 2 * L:4 * L], 0.0)
        a_rb = jnp.where(incl, rr[:, 0:2 * L], 0.0)
        a_rk = jnp.where(incl, rr[:, 2 * L:4 * L], 0.0)
        inv = eye + a_ab
        apow = a_ab
        for _ in range(int(math.log2(L)) - 1):
            ab = apow.astype(BF16)
            apow = _dot(ab, ab)
            inv = inv + _dot(inv.astype(BF16), apow.astype(BF16))
        s_old = state_ref[p]
        s_b = s_old.astype(BF16)
        x = _dot_nt(a_s, s_b) + _dot(a_ak.astype(BF16), v_s)
        u = _dot(inv.astype(BF16), x.astype(BF16))
        u_b = u.astype(BF16)
        y = _dot_nt(r_s, s_b) + _dot(a_rb.astype(BF16), u_b) + _dot(a_rk.astype(BF16), v_s)
        state_ref[p] = (s_old * w_end[:, p * LANES:(p + 1) * LANES]
                        + _dot_tn(u_b, bd_s) + _dot_tn(v_s, kd_s))
        ys.append(y[0:L] + y[L:2 * L])
    y = jnp.concatenate(ys, axis=1)

    mean = head_sum(y) * (1.0 / RW_N)
    yc = y - mean
    var = head_sum(yc * yc) * (1.0 / RW_N)
    y = yc * lax.rsqrt(var + LN_X_EPS) * lnw_ref[...] + lnb_ref[...]
    y = y + bonus * v
    o_ref[...] = (y * g).astype(o_ref.dtype)


def _rwkv(proj, mu_rkv, w0, a0, w2cat, k_k, k_a, r_k, ln_w, ln_b, b, s):
    L = CHUNK
    nc = s // L
    row = lambda n: pl.BlockSpec((1, n), lambda bi, c: (0, 0))
    rkv_col0 = 3 * DA_WIDTH // RW_WIDTH
    lora_col0 = IN_COLS // LORA_PAD
    return pl.pallas_call(
        _rwkv_kernel,
        grid=(b, nc),
        in_specs=[
            pl.BlockSpec((L, RW_WIDTH), lambda bi, c: (bi * nc + c, rkv_col0)),
            pl.BlockSpec((L, RW_WIDTH), lambda bi, c: (bi * nc + c, rkv_col0 + 1)),
            pl.BlockSpec((L, RW_WIDTH), lambda bi, c: (bi * nc + c, rkv_col0 + 2)),
            pl.BlockSpec((L, LORA_PAD), lambda bi, c: (bi * nc + c, lora_col0)),
            pl.BlockSpec((L, LORA_PAD), lambda bi, c: (bi * nc + c, lora_col0 + 1)),
            pl.BlockSpec((3, RW_WIDTH), lambda bi, c: (0, 0)),
            row(RW_WIDTH), row(RW_WIDTH),
            pl.BlockSpec((LORA_PAD, 3 * RW_WIDTH), lambda bi, c: (0, 0)),
            row(RW_WIDTH), row(RW_WIDTH), row(RW_WIDTH), row(RW_WIDTH), row(RW_WIDTH),
        ],
        out_specs=pl.BlockSpec((L, RW_WIDTH), lambda bi, c: (bi * nc + c, 0)),
        out_shape=jax.ShapeDtypeStruct((b * s, RW_WIDTH), BF16),
        scratch_shapes=[
            pltpu.VMEM((RW_PAIRS, LANES, LANES), F32),
            pltpu.VMEM((1, 3 * RW_WIDTH), F32),
            pltpu.VMEM((1, LORA_PAD), F32),
        ],
        compiler_params=pltpu.CompilerParams(
            dimension_semantics=("parallel", "arbitrary"), vmem_limit_bytes=VMEM_LIMIT),
        name="rwkv",
    )(proj, proj, proj, proj, proj, mu_rkv, w0, a0, w2cat, k_k, k_a, r_k, ln_w, ln_b)


ROUTE_E, ROUTE_G, ROUTE_R = 0, 2, 4
EXPERT_LANE0 = N_GROUPS


def _outproj_kernel(x_ref, oda_ref, orw_ref, wout_ref, g_ref, wrh_ref, wrl_ref, br_ref,
                    h_ref, xn_ref, route_ref, cnt_ref, base_ref, *, tm):
    @pl.when(pl.program_id(0) == 0)
    def _():
        base_ref[...] = jnp.zeros_like(base_ref)

    mix = _dot(oda_ref[...], wout_ref[0:DA_WIDTH, :]) + _dot(orw_ref[...], wout_ref[DA_WIDTH:, :])
    h = x_ref[...] + mix
    h_ref[...] = h
    ms = jnp.mean(h * h, axis=-1, keepdims=True)
    xn = h * lax.rsqrt(ms + NORM_EPS) * g_ref[...]
    xn_ref[...] = xn

    x_hi = xn.astype(BF16)
    x_lo = (xn - x_hi.astype(F32)).astype(BF16)
    lg = (_dot(x_hi, wrh_ref[...]) + _dot(x_lo, wrh_ref[...]) + _dot(x_hi, wrl_ref[...])) + br_ref[...]

    lane = lax.broadcasted_iota(jnp.int32, (tm, LANES), 1)
    big = jnp.int32(1 << 20)
    gl = jnp.where(lane < N_GROUPS, lg, NEG)
    gmax = jnp.max(gl, axis=-1, keepdims=True)
    grp = jnp.min(jnp.where(gl == gmax, lane, big), axis=-1, keepdims=True)
    p_grp = 1.0 / jnp.sum(jnp.exp(gl - gmax), axis=-1, keepdims=True)
    eg = jnp.where(lane >= EXPERT_LANE0, (lane - EXPERT_LANE0) // EXPERTS_PER_GROUP, -1)
    el = jnp.where(eg == grp, lg, NEG)
    emax = jnp.max(el, axis=-1, keepdims=True)
    pe = jnp.exp(el - emax)
    probs = pe / jnp.sum(pe, axis=-1, keepdims=True)
    probs = jnp.where(eg == grp, probs, -1.0)
    p1 = jnp.max(probs, axis=-1, keepdims=True)
    i1 = jnp.min(jnp.where(probs == p1, lane, big), axis=-1, keepdims=True)
    probs2 = jnp.where(lane == i1, -1.0, probs)
    p2 = jnp.max(probs2, axis=-1, keepdims=True)
    i2 = jnp.min(jnp.where(probs2 == p2, lane, big), axis=-1, keepdims=True)
    gate1 = p_grp * p1 / (p1 + p2)
    gate2 = p_grp * p2 / (p1 + p2)

    oh1 = jnp.where(lane == i1, 1.0, 0.0)
    oh2 = jnp.where(lane == i2, 1.0, 0.0)
    ri = lax.broadcasted_iota(jnp.int32, (tm, tm), 0)
    rj = lax.broadcasted_iota(jnp.int32, (tm, tm), 1)
    lower = jnp.where(ri > rj, 1.0, 0.0).astype(BF16)
    base = base_ref[...]
    tot1 = jnp.sum(oh1, axis=0, keepdims=True)
    c1 = base + _dot(lower, oh1.astype(BF16))
    c2 = base + tot1 + _dot(lower, oh2.astype(BF16))
    rank1 = jnp.sum(oh1 * c1, axis=-1, keepdims=True)
    rank2 = jnp.sum(oh2 * c2, axis=-1, keepdims=True)
    base = base + tot1 + jnp.sum(oh2, axis=0, keepdims=True)
    base_ref[...] = base
    cnt_ref[...] = base

    e1 = (i1 - EXPERT_LANE0).astype(F32)
    e2 = (i2 - EXPERT_LANE0).astype(F32)
    rec = jnp.zeros((tm, LANES), F32)
    for ln, val in ((ROUTE_E, e1), (ROUTE_E + 1, e2), (ROUTE_G, gate1), (ROUTE_G + 1, gate2),
                    (ROUTE_R, rank1), (ROUTE_R + 1, rank2)):
        rec = jnp.where(lane == ln, val, rec)
    route_ref[...] = rec


def _outproj(x2, o_da, o_rw, w_out, g_ffn, wr_hi, wr_lo, b_r, tm=256):
    t = x2.shape[0]
    const = lambda shape: pl.BlockSpec(shape, lambda i: (0, 0))
    return pl.pallas_call(
        functools.partial(_outproj_kernel, tm=tm),
        grid=(t // tm,),
        in_specs=[
            pl.BlockSpec((tm, D_MODEL), lambda i: (i, 0)),
            pl.BlockSpec((tm, DA_WIDTH), lambda i: (i, 0)),
            pl.BlockSpec((tm, RW_WIDTH), lambda i: (i, 0)),
            const((D_MODEL, D_MODEL)),
            const((1, D_MODEL)),
            const((D_MODEL, LANES)),
            const((D_MODEL, LANES)),
            const((1, LANES)),
        ],
        out_specs=[
            pl.BlockSpec((tm, D_MODEL), lambda i: (i, 0)),
            pl.BlockSpec((tm, D_MODEL), lambda i: (i, 0)),
            pl.BlockSpec((tm, LANES), lambda i: (i, 0)),
            const((1, LANES)),
        ],
        out_shape=[
            jax.ShapeDtypeStruct((t, D_MODEL), F32),
            jax.ShapeDtypeStruct((t, D_MODEL), F32),
            jax.ShapeDtypeStruct((t, LANES), F32),
            jax.ShapeDtypeStruct((1, LANES), F32),
        ],
        scratch_shapes=[pltpu.VMEM((1, LANES), F32)],
        compiler_params=pltpu.CompilerParams(
            dimension_semantics=("arbitrary",), vmem_limit_bytes=VMEM_LIMIT),
        name="outproj_router",
    )(x2, o_da, o_rw, w_out, g_ffn, wr_hi, wr_lo, b_r)


def _row_copy(src_hbm, dst_ref, src_row, dst_row, sem):
    return pltpu.make_async_copy(src_hbm.at[pl.ds(src_row, 1)], dst_ref.at[pl.ds(dst_row, 1)], sem)


def _dispatch_kernel(dest_ref, x_ref, xb_in_hbm, xb_hbm, sem, *, td):
    del xb_in_hbm
    t0 = pl.program_id(0) * td

    def start(i, _):
        tok = t0 + i
        _row_copy(x_ref, xb_hbm, i, dest_ref[2 * tok], sem).start()
        _row_copy(x_ref, xb_hbm, i, dest_ref[2 * tok + 1], sem).start()
        return 0

    def wait(i, _):
        _row_copy(x_ref, xb_hbm, 0, 0, sem).wait()
        _row_copy(x_ref, xb_hbm, 0, 0, sem).wait()
        return 0

    lax.fori_loop(0, td, start, 0)
    lax.fori_loop(0, td, wait, 0)


def _dispatch(dest_flat, xn2, xb_zero, td=256):
    t = xn2.shape[0]
    return pl.pallas_call(
        functools.partial(_dispatch_kernel, td=td),
        grid_spec=pltpu.PrefetchScalarGridSpec(
            num_scalar_prefetch=1,
            grid=(t // td,),
            in_specs=[pl.BlockSpec((td, D_MODEL), lambda i, d: (i, 0)), pl.BlockSpec(memory_space=pl.ANY)],
            out_specs=pl.BlockSpec(memory_space=pl.ANY),
            scratch_shapes=[pltpu.SemaphoreType.DMA],
        ),
        out_shape=jax.ShapeDtypeStruct(xb_zero.shape, xb_zero.dtype),
        input_output_aliases={2: 0},
        compiler_params=pltpu.CompilerParams(dimension_semantics=("arbitrary",)),
        name="dispatch",
    )(dest_flat, xn2, xb_zero)


def _experts_kernel(be_ref, nu_ref, xb_ref, wg_ref, wu_ref, wd_ref, yb_ref):
    del be_ref

    @pl.when(pl.program_id(0) < nu_ref[0])
    def _():
        x = xb_ref[...].astype(BF16)
        hg = _dot(x, wg_ref[...])
        hu = _dot(x, wu_ref[...])
        hid = hg * _sigmoid(hg) * hu
        yb_ref[...] = _dot(hid.astype(BF16), wd_ref[...])

    @pl.when(pl.program_id(0) >= nu_ref[0])
    def _():
        yb_ref[...] = jnp.zeros_like(yb_ref)


def _experts(block_e, n_used, xb, w_gate, w_up, w_down):
    nb = xb.shape[0] // EXPERT_BLOCK
    rows = lambda i, be, nu: (jnp.minimum(i, nu[0] - 1), 0)
    return pl.pallas_call(
        _experts_kernel,
        grid_spec=pltpu.PrefetchScalarGridSpec(
            num_scalar_prefetch=2,
            grid=(nb,),
            in_specs=[
                pl.BlockSpec((EXPERT_BLOCK, D_MODEL), rows),
                pl.BlockSpec((None, D_MODEL, D_EXPERT), lambda i, be, nu: (be[i], 0, 0)),
                pl.BlockSpec((None, D_MODEL, D_EXPERT), lambda i, be, nu: (be[i], 0, 0)),
                pl.BlockSpec((None, D_EXPERT, D_MODEL), lambda i, be, nu: (be[i], 0, 0)),
            ],
            out_specs=pl.BlockSpec((EXPERT_BLOCK, D_MODEL), lambda i, be, nu: (i, 0)),
        ),
        out_shape=jax.ShapeDtypeStruct(xb.shape, F32),
        compiler_params=pltpu.CompilerParams(
            dimension_semantics=("arbitrary",), vmem_limit_bytes=VMEM_LIMIT),
        name="experts",
    )(block_e, n_used, xb, w_gate, w_up, w_down)


def _combine_kernel(dest_ref, route_ref, h_ref, g_ref, yb_hbm, o_ref, y1_ref, y2_ref, sem, *, tc):
    t0 = pl.program_id(0) * tc

    def start(i, _):
        tok = t0 + i
        _row_copy(yb_hbm, y1_ref, dest_ref[2 * tok], i, sem).start()
        _row_copy(yb_hbm, y2_ref, dest_ref[2 * tok + 1], i, sem).start()
        return 0

    def wait(i, _):
        _row_copy(yb_hbm, y1_ref, 0, 0, sem).wait()
        _row_copy(yb_hbm, y2_ref, 0, 0, sem).wait()
        return 0

    lax.fori_loop(0, tc, start, 0)
    lax.fori_loop(0, tc, wait, 0)

    rec = route_ref[...]
    lane = lax.broadcasted_iota(jnp.int32, (tc, LANES), 1)
    g1 = jnp.sum(jnp.where(lane == ROUTE_G, rec, 0.0), axis=-1, keepdims=True)
    g2 = jnp.sum(jnp.where(lane == ROUTE_G + 1, rec, 0.0), axis=-1, keepdims=True)
    h = h_ref[...] + (y1_ref[...] * g1 + y2_ref[...] * g2)
    ms = jnp.mean(h * h, axis=-1, keepdims=True)
    o_ref[...] = h * lax.rsqrt(ms + NORM_EPS) * g_ref[...]


def _combine(dest_flat, route, h1, g_final, yb, tc=256):
    t = h1.shape[0]
    return pl.pallas_call(
        functools.partial(_combine_kernel, tc=tc),
        grid_spec=pltpu.PrefetchScalarGridSpec(
            num_scalar_prefetch=1,
            grid=(t // tc,),
            in_specs=[
                pl.BlockSpec((tc, LANES), lambda i, d: (i, 0)),
                pl.BlockSpec((tc, D_MODEL), lambda i, d: (i, 0)),
                pl.BlockSpec((1, D_MODEL), lambda i, d: (0, 0)),
                pl.BlockSpec(memory_space=pl.ANY),
            ],
            out_specs=pl.BlockSpec((tc, D_MODEL), lambda i, d: (i, 0)),
            scratch_shapes=[
                pltpu.VMEM((tc, D_MODEL), F32),
                pltpu.VMEM((tc, D_MODEL), F32),
                pltpu.SemaphoreType.DMA,
            ],
        ),
        out_shape=jax.ShapeDtypeStruct((t, D_MODEL), F32),
        compiler_params=pltpu.CompilerParams(
            dimension_semantics=("arbitrary",), vmem_limit_bytes=VMEM_LIMIT),
        name="combine",
    )(dest_flat, route, h1, g_final, yb)


def _pad_cols(w, n):
    return jnp.pad(w, ((0, 0), (0, n - w.shape[1])))


def kernel(x, g_mix, w_in, w_out, da_lambda_q1, da_lambda_k1, da_lambda_q2, da_lambda_k2, da_subln_g,
           rw_mu_x, rw_mu_rkv, rw_w0, rw_w1, rw_w2, rw_a0, rw_a1, rw_a2, rw_g1, rw_g2, rw_k_k, rw_k_a,
           rw_r_k, rw_ln_w, rw_ln_b, g_ffn, moe_w_group, moe_b_group, moe_w_expert, moe_b_expert,
           moe_w_gate, moe_w_up, moe_w_down, g_final):
    b, s, d = x.shape
    t = b * s
    x2 = x.reshape(t, d)

    mu = rw_mu_x[0]
    lora = (rw_w1[0], rw_a1[0], rw_g1[0])
    keep = _pad_cols(jnp.concatenate([(1.0 - mu[i])[:, None] * w for i, w in enumerate(lora)], axis=1), LORA_PAD)
    prev = _pad_cols(jnp.concatenate([mu[i][:, None] * w for i, w in enumerate(lora)], axis=1), LORA_PAD)
    w_all = jnp.concatenate([w_in[0], keep, prev], axis=1).astype(BF16)
    proj = _inproj(x2, g_mix, w_all)

    slopes = jnp.asarray([2.0 ** (-8.0 * (i + 1) / DA_HEADS) for i in range(DA_HEADS)], F32)
    lam4 = jnp.concatenate([da_lambda_q1, da_lambda_k1, da_lambda_q2, da_lambda_k2], axis=0)
    o_da = _diffattn(proj, slopes, lam4, da_subln_g, b, s)

    w2cat = jnp.zeros((LORA_PAD, 3 * RW_WIDTH), F32)
    w2cat = w2cat.at[0:LORA_W, 0:RW_WIDTH].set(rw_w2[0])
    w2cat = w2cat.at[LORA_W:LORA_W + LORA_A, RW_WIDTH:2 * RW_WIDTH].set(rw_a2[0])
    w2cat = w2cat.at[LORA_W + LORA_A:LORA_W + LORA_A + LORA_G, 2 * RW_WIDTH:].set(rw_g2[0])
    o_rw = _rwkv(proj, rw_mu_rkv[0], rw_w0, rw_a0, w2cat.astype(BF16), rw_k_k, rw_k_a,
                 rw_r_k.reshape(1, RW_WIDTH), rw_ln_w, rw_ln_b, b, s)

    w_r = _pad_cols(jnp.concatenate([moe_w_group[0], moe_w_expert[0]], axis=1), LANES)
    wr_hi = w_r.astype(BF16)
    wr_lo = (w_r - wr_hi.astype(F32)).astype(BF16)
    b_r = _pad_cols(jnp.concatenate([moe_b_group[0], moe_b_expert[0].reshape(-1)])[None, :], LANES)
    h1, xn2, route, cnt = _outproj(x2, o_da, o_rw, w_out[0].astype(BF16), g_ffn, wr_hi, wr_lo, b_r)

    counts = cnt[0, EXPERT_LANE0:EXPERT_LANE0 + N_EXPERTS].astype(jnp.int32)
    pcounts = ((counts + EXPERT_BLOCK - 1) // EXPERT_BLOCK) * EXPERT_BLOCK
    pends = jnp.cumsum(pcounts)
    pstarts = pends - pcounts
    e_idx = route[:, ROUTE_E:ROUTE_E + 2].astype(jnp.int32)
    rank = route[:, ROUTE_R:ROUTE_R + 2].astype(jnp.int32)
    dest = (pstarts[e_idx] + rank).reshape(-1)
    n_blocks = (2 * t) // EXPERT_BLOCK + N_EXPERTS
    block_e = jnp.minimum(
        jnp.searchsorted(pends, jnp.arange(n_blocks, dtype=jnp.int32) * EXPERT_BLOCK, side='right'),
        N_EXPERTS - 1).astype(jnp.int32)
    n_used = (pends[-1:] // EXPERT_BLOCK).astype(jnp.int32)

    xb = _dispatch(dest, xn2, jnp.zeros((n_blocks * EXPERT_BLOCK, d), F32))
    yb = _experts(block_e, n_used, xb, moe_w_gate[0].astype(BF16), moe_w_up[0].astype(BF16),
                  moe_w_down[0].astype(BF16))
    out = _combine(dest, route, h1, g_final[None, :], yb)
    return out.reshape(b, s, d)
```

```python
import functools
import math

import jax
import jax.numpy as jnp
from jax import lax
from jax.experimental import pallas as pl
from jax.experimental.pallas import tpu as pltpu

F32 = jnp.float32
BF16 = jnp.bfloat16

D_MODEL = 2048
DA_WIDTH = 1024
RW_WIDTH = 1024
DA_HEADS = 8
DA_V_DIM = 128
DA_QK_DIM = 64
RW_N = 64
RW_PAIRS = RW_WIDTH // 128
LORA_W, LORA_A, LORA_G = 64, 64, 160
LORA_PAD = 384
IN_COLS = 3 * DA_WIDTH + 3 * RW_WIDTH
NC = IN_COLS + 2 * LORA_PAD
N_GROUPS = 4
EXPERTS_PER_GROUP = 8
N_EXPERTS = 32
D_EXPERT = 512
NORM_EPS = 1e-6
LN_X_EPS = 64e-5
LAM_INIT = 0.8 - 0.6 * math.exp(-0.3 * 0)
LANES = 128
CHUNK = 64
EXPERT_BLOCK = 256
NEG = -1e30
VMEM_LIMIT = 48 * 1024 * 1024


def _dot(a, b):
    return jnp.dot(a, b, preferred_element_type=F32)


def _dot_nt(a, b):
    return lax.dot_general(a, b, (((1,), (1,)), ((), ())), preferred_element_type=F32)


def _dot_tn(a, b):
    return lax.dot_general(a, b, (((0,), (0,)), ((), ())), preferred_element_type=F32)


def _sigmoid(x):
    return 1.0 / (1.0 + jnp.exp(-x))


def _inproj_kernel(x_ref, g_ref, w_ref, o_ref, xn_ref):
    @pl.when(pl.program_id(1) == 0)
    def _():
        x = x_ref[...]
        ms = jnp.mean(x * x, axis=-1, keepdims=True)
        xn_ref[...] = (x * lax.rsqrt(ms + NORM_EPS) * g_ref[...]).astype(BF16)

    o_ref[...] = _dot(xn_ref[...], w_ref[...]).astype(o_ref.dtype)


def _inproj(x2, g, w_all, tm=512, tn=1152):
    t = x2.shape[0]
    return pl.pallas_call(
        _inproj_kernel,
        grid=(t // tm, NC // tn),
        in_specs=[
            pl.BlockSpec((tm, D_MODEL), lambda i, j: (i, 0)),
            pl.BlockSpec((1, D_MODEL), lambda i, j: (0, 0)),
            pl.BlockSpec((D_MODEL, tn), lambda i, j: (0, j)),
        ],
        out_specs=pl.BlockSpec((tm, tn), lambda i, j: (i, j)),
        out_shape=jax.ShapeDtypeStruct((t, NC), BF16),
        scratch_shapes=[pltpu.VMEM((tm, D_MODEL), BF16)],
        compiler_params=pltpu.CompilerParams(
            dimension_semantics=("parallel", "arbitrary"), vmem_limit_bytes=VMEM_LIMIT),
        name="inproj",
    )(x2, g, w_all)


def _da_kernel(slopes_ref, lam_ref, subg_ref, q_ref, k_ref, v_ref, o_ref, vt_ref, m_ref, l_ref, acc_ref,
               *, tq, tk, nsub, seq):
    h = pl.program_id(1)
    qi = pl.program_id(2)
    slope = slopes_ref[h]

    @pl.when(qi == 0)
    def _():
        for blk in range(seq // LANES):
            rows = slice(blk * LANES, (blk + 1) * LANES)
            vt_ref[:, rows] = v_ref[rows, :].astype(F32).T.astype(BF16)

    q = q_ref[...] * jnp.asarray(DA_QK_DIM ** -0.5, BF16)
    lane = lax.broadcasted_iota(jnp.int32, (tq, LANES), 1)
    zero = jnp.zeros_like(q)
    qc = (jnp.where(lane < DA_QK_DIM, q, zero), jnp.where(lane >= DA_QK_DIM, q, zero))
    sub = tk // nsub
    krow = lax.broadcasted_iota(jnp.int32, (sub, tq), 0)
    qcol = lax.broadcasted_iota(jnp.int32, (sub, tq), 1)
    bias = [slope * (krow + u * sub).astype(F32) for u in range(nsub)]

    m_ref[...] = jnp.full_like(m_ref, NEG)
    l_ref[...] = jnp.zeros_like(l_ref)
    acc_ref[...] = jnp.zeros_like(acc_ref)

    def step(j, masked):
        ks = pl.multiple_of(j * tk, tk)
        rel = ks - qi * tq
        off = slope * rel.astype(F32)
        cu = [(c, u) for c in range(2) for u in range(nsub)]
        k = [k_ref[pl.ds(ks + u * sub, sub), :] for u in range(nsub)]
        s = {(c, u): _dot_nt(k[u], qc[c]) + bias[u] for c, u in cu}
        if masked:
            s = {(c, u): jnp.where(krow + (rel + u * sub) <= qcol, s[c, u], NEG) for c, u in cu}
        m = [m_ref[c:c + 1, :] for c in range(2)]
        smax = [functools.reduce(jnp.maximum, [s[c, u] for u in range(nsub)]) for c in range(2)]
        m_new = [jnp.maximum(m[c], jnp.max(smax[c], axis=0, keepdims=True) + off) for c in range(2)]
        alpha = [jnp.exp(m[c] - m_new[c]) for c in range(2)]
        p = {(c, u): jnp.exp(s[c, u] - (m_new[c] - off)) for c, u in cu}
        psum = [functools.reduce(jnp.add, [p[c, u] for u in range(nsub)]) for c in range(2)]
        pb = [jnp.concatenate([p[c, u].astype(BF16) for u in range(nsub)], axis=0) for c in range(2)]
        vt = vt_ref[:, pl.ds(ks, tk)]
        pv = [_dot(vt, pb[c]) for c in range(2)]
        for c in range(2):
            m_ref[c:c + 1, :] = m_new[c]
            l_ref[c:c + 1, :] = alpha[c] * l_ref[c:c + 1, :] + jnp.sum(psum[c], axis=0, keepdims=True)
            acc_ref[c] = alpha[c] * acc_ref[c] + pv[c]

    def body(j, _):
        step(j, False)
        return 0

    n_diag = tq // tk
    lax.fori_loop(0, qi * n_diag, body, 0)
    for dj in range(n_diag):
        step(qi * n_diag + dj, True)

    lv = lam_ref[...]
    lam = (jnp.exp(jnp.sum(lv[0:1] * lv[1:2], axis=-1, keepdims=True))
           - jnp.exp(jnp.sum(lv[2:3] * lv[3:4], axis=-1, keepdims=True)) + LAM_INIT)
    o = acc_ref[0] / l_ref[0:1, :] - lam * (acc_ref[1] / l_ref[1:2, :])
    ms = jnp.mean(o * o, axis=0, keepdims=True)
    o = o * lax.rsqrt(ms + NORM_EPS) * subg_ref[...] * (1.0 - LAM_INIT)
    o_ref[...] = o.T.astype(o_ref.dtype)


def _diffattn(proj, slopes, lam4, subg_col, b, s, tq=256, tk=256, nsub=2):
    nq = s // tq
    return pl.pallas_call(
        functools.partial(_da_kernel, tq=tq, tk=tk, nsub=nsub, seq=s),
        grid=(b, DA_HEADS, nq),
        in_specs=[
            pl.BlockSpec(memory_space=pltpu.SMEM),
            pl.BlockSpec((4, DA_QK_DIM), lambda bi, h, qi: (0, 0)),
            pl.BlockSpec((DA_V_DIM, 1), lambda bi, h, qi: (0, 0)),
            pl.BlockSpec((tq, LANES), lambda bi, h, qi: (bi * nq + qi, h)),
            pl.BlockSpec((s, LANES), lambda bi, h, qi: (bi, DA_HEADS + h)),
            pl.BlockSpec((s, LANES), lambda bi, h, qi: (bi, 2 * DA_HEADS + h)),
        ],
        out_specs=pl.BlockSpec((tq, LANES), lambda bi, h, qi: (bi * nq + qi, h)),
        out_shape=jax.ShapeDtypeStruct((b * s, DA_WIDTH), BF16),
        scratch_shapes=[
            pltpu.VMEM((DA_V_DIM, s), BF16),
            pltpu.VMEM((2, tq), F32),
            pltpu.VMEM((2, tq), F32),
            pltpu.VMEM((2, DA_V_DIM, tq), F32),
        ],
        compiler_params=pltpu.CompilerParams(
            dimension_semantics=("parallel", "parallel", "arbitrary"), vmem_limit_bytes=VMEM_LIMIT),
        name="diffattn",
    )(slopes, lam4, subg_col, proj, proj, proj)


def _rwkv_kernel(r_ref, k_ref, v_ref, p1_ref, p2_ref, mu_ref, w0_ref, a0_ref, w2_ref, kk_ref, ka_ref,
                 rk_ref, lnw_ref, lnb_ref, o_ref, state_ref, crkv_ref, cp2_ref):
    L = CHUNK
    c = pl.program_id(1)

    @pl.when(c == 0)
    def _():
        state_ref[...] = jnp.zeros_like(state_ref)
        crkv_ref[...] = jnp.zeros_like(crkv_ref)
        cp2_ref[...] = jnp.zeros_like(cp2_ref)

    first = lax.broadcasted_iota(jnp.int32, (L, 1), 0) == 0

    def shift(x, carry):
        return jnp.where(first, carry, pltpu.roll(x, 1, 0))

    r_raw = r_ref[...].astype(F32)
    k_raw = k_ref[...].astype(F32)
    v_raw = v_ref[...].astype(F32)
    p2 = p2_ref[...].astype(F32)
    rs = shift(r_raw, crkv_ref[:, 0:RW_WIDTH])
    ks = shift(k_raw, crkv_ref[:, RW_WIDTH:2 * RW_WIDTH])
    vs = shift(v_raw, crkv_ref[:, 2 * RW_WIDTH:3 * RW_WIDTH])
    lin = p1_ref[...].astype(F32) + shift(p2, cp2_ref[...])
    crkv_ref[:, 0:RW_WIDTH] = r_raw[L - 1:L, :]
    crkv_ref[:, RW_WIDTH:2 * RW_WIDTH] = k_raw[L - 1:L, :]
    crkv_ref[:, 2 * RW_WIDTH:3 * RW_WIDTH] = v_raw[L - 1:L, :]
    cp2_ref[...] = p2[L - 1:L, :]

    r = r_raw + (rs - r_raw) * mu_ref[0:1, :]
    k = k_raw + (ks - k_raw) * mu_ref[1:2, :]
    v = v_raw + (vs - v_raw) * mu_ref[2:3, :]

    ll = lax.broadcasted_iota(jnp.int32, (L, LORA_PAD), 1)
    z = jnp.where(ll < LORA_W, jnp.tanh(lin),
                  jnp.where(ll < LORA_W + LORA_A, lin,
                            jnp.where(ll < LORA_W + LORA_A + LORA_G, _sigmoid(lin), 0.0)))
    d = _dot(z.astype(BF16), w2_ref[...])
    logw = -_sigmoid(w0_ref[...] + d[:, 0:RW_WIDTH]) * math.exp(-0.5)
    a_sig = _sigmoid(a0_ref[...] + d[:, RW_WIDTH:2 * RW_WIDTH])
    g = d[:, 2 * RW_WIDTH:3 * RW_WIDTH]

    li = lax.broadcasted_iota(jnp.int32, (LANES, LANES), 0)
    lj = lax.broadcasted_iota(jnp.int32, (LANES, LANES), 1)
    same_head = jnp.where((li // RW_N) == (lj // RW_N), 1.0, 0.0).astype(BF16)

    def head_sum(x):
        xb = x.astype(BF16)
        return jnp.concatenate(
            [_dot(xb[:, p * LANES:(p + 1) * LANES], same_head) for p in range(RW_PAIRS)], axis=1)

    kk = k * kk_ref[...]
    kk = kk * lax.rsqrt(jnp.maximum(head_sum(kk * kk), 1e-24))
    k = k * (1.0 + (a_sig - 1.0) * ka_ref[...])
    bonus = head_sum(r * k * rk_ref[...])

    ti = lax.broadcasted_iota(jnp.int32, (L, L), 0)
    tj = lax.broadcasted_iota(jnp.int32, (L, L), 1)
    tri = jnp.where(ti >= tj, 1.0, 0.0).astype(BF16)
    w_hi = logw.astype(BF16)
    w_r1 = logw - w_hi.astype(F32)
    w_mid = w_r1.astype(BF16)
    w_lo = (w_r1 - w_mid.astype(F32)).astype(BF16)
    cw = _dot(tri, w_hi) + _dot(tri, w_mid) + _dot(tri, w_lo)
    cw_last = cw[L - 1:L, :]
    e_in = jnp.exp(cw)
    e_out = jnp.exp(-cw)
    e_end = jnp.exp(cw_last - cw)
    w_end = jnp.exp(cw_last)

    at = -kk * jnp.exp(cw - logw)
    rt = r * e_in
    b = kk * a_sig
    bt = b * e_out
    kt = k * e_out
    bd = b * e_end
    kd = k * e_end

    lane = lax.broadcasted_iota(jnp.int32, (L, LANES), 1)
    lo = lane < RW_N

    def stack(x, p):
        xs = x[:, p * LANES:(p + 1) * LANES]
        return jnp.concatenate([jnp.where(lo, xs, 0.0), jnp.where(lo, 0.0, xs)], axis=0).astype(BF16)

    si = lax.broadcasted_iota(jnp.int32, (2 * L, 2 * L), 0)
    sj = lax.broadcasted_iota(jnp.int32, (2 * L, 2 * L), 1)
    strict = (si % L) > (sj % L)
    incl = (si % L) >= (sj % L)
    eye = jnp.where(si == sj, 1.0, 0.0)

    pairs = range(RW_PAIRS)
    cat0 = lambda *xs: jnp.concatenate(xs, axis=0)
    cat1 = lambda *xs: jnp.concatenate(xs, axis=1)
    bf = lambda x: x.astype(BF16)
    a_s = [stack(at, p) for p in pairs]
    r_s = [stack(rt, p) for p in pairs]
    v_s = [stack(v, p) for p in pairs]
    bk = [cat0(stack(bt, p), stack(kt, p)) for p in pairs]
    bkd = [cat0(stack(bd, p), stack(kd, p)) for p in pairs]
    aa = [_dot_nt(a_s[p], bk[p]) for p in pairs]
    rr = [_dot_nt(r_s[p], bk[p]) for p in pairs]
    a_ab = [jnp.where(strict, aa[p][:, 0:2 * L], 0.0) for p in pairs]
    a_ak = [bf(jnp.where(strict, aa[p][:, 2 * L:4 * L], 0.0)) for p in pairs]
    a_rb = [bf(jnp.where(incl, rr[p][:, 0:2 * L], 0.0)) for p in pairs]
    a_rk = [bf(jnp.where(incl, rr[p][:, 2 * L:4 * L], 0.0)) for p in pairs]
    ab = [bf(a) for a in a_ab]
    qpow = [_dot(ab[p], ab[p]) for p in pairs]
    inv = [eye + a_ab[p] for p in pairs]
    for _ in range(int(math.log2(L)) - 2):
        qb = [bf(q) for q in qpow]
        res = [_dot(qb[p], cat1(qb[p], bf(inv[p]))) for p in pairs]
        qpow = [res[p][:, 0:2 * L] for p in pairs]
        inv = [inv[p] + res[p][:, 2 * L:4 * L] for p in pairs]
    inv = [bf(inv[p] + _dot(bf(qpow[p]), bf(inv[p]))) for p in pairs]
    h_old = [state_ref[p] for p in pairs]
    h_b = [bf(h) for h in h_old]
    x = [_dot(cat1(a_s[p], a_ak[p]), cat0(h_b[p], v_s[p])) for p in pairs]
    u_b = [bf(_dot(inv[p], bf(x[p]))) for p in pairs]
    y2 = [_dot(cat1(r_s[p], a_rb[p], a_rk[p]), cat0(h_b[p], u_b[p], v_s[p])) for p in pairs]
    for p in pairs:
        w_col = jnp.broadcast_to(w_end[:, p * LANES:(p + 1) * LANES], (LANES, LANES)).T
        state_ref[p] = w_col * h_old[p] + _dot_tn(bkd[p], cat0(u_b[p], v_s[p]))
    y = cat1(*[y2[p][0:L] + y2[p][L:2 * L] for p in pairs])

    mean = head_sum(y) * (1.0 / RW_N)
    yc = y - mean
    var = head_sum(yc * yc) * (1.0 / RW_N)
    y = yc * lax.rsqrt(var + LN_X_EPS) * lnw_ref[...] + lnb_ref[...]
    y = y + bonus * v
    o_ref[...] = (y * g).astype(o_ref.dtype)


def _rwkv(proj, mu_rkv, w0, a0, w2cat, k_k, k_a, r_k, ln_w, ln_b, b, s):
    L = CHUNK
    nc = s // L
    row = lambda n: pl.BlockSpec((1, n), lambda bi, c: (0, 0))
    rkv_col0 = 3 * DA_WIDTH // RW_WIDTH
    lora_col0 = IN_COLS // LORA_PAD
    return pl.pallas_call(
        _rwkv_kernel,
        grid=(b, nc),
        in_specs=[
            pl.BlockSpec((L, RW_WIDTH), lambda bi, c: (bi * nc + c, rkv_col0)),
            pl.BlockSpec((L, RW_WIDTH), lambda bi, c: (bi * nc + c, rkv_col0 + 1)),
            pl.BlockSpec((L, RW_WIDTH), lambda bi, c: (bi * nc + c, rkv_col0 + 2)),
            pl.BlockSpec((L, LORA_PAD), lambda bi, c: (bi * nc + c, lora_col0)),
            pl.BlockSpec((L, LORA_PAD), lambda bi, c: (bi * nc + c, lora_col0 + 1)),
            pl.BlockSpec((3, RW_WIDTH), lambda bi, c: (0, 0)),
            row(RW_WIDTH), row(RW_WIDTH),
            pl.BlockSpec((LORA_PAD, 3 * RW_WIDTH), lambda bi, c: (0, 0)),
            row(RW_WIDTH), row(RW_WIDTH), row(RW_WIDTH), row(RW_WIDTH), row(RW_WIDTH),
        ],
        out_specs=pl.BlockSpec((L, RW_WIDTH), lambda bi, c: (bi * nc + c, 0)),
        out_shape=jax.ShapeDtypeStruct((b * s, RW_WIDTH), BF16),
        scratch_shapes=[
            pltpu.VMEM((RW_PAIRS, LANES, LANES), F32),
            pltpu.VMEM((1, 3 * RW_WIDTH), F32),
            pltpu.VMEM((1, LORA_PAD), F32),
        ],
        compiler_params=pltpu.CompilerParams(
            dimension_semantics=("parallel", "arbitrary"), vmem_limit_bytes=VMEM_LIMIT),
        name="rwkv",
    )(proj, proj, proj, proj, proj, mu_rkv, w0, a0, w2cat, k_k, k_a, r_k, ln_w, ln_b)


ROUTE_E, ROUTE_G, ROUTE_R = 0, 2, 4
EXPERT_LANE0 = N_GROUPS


def _outproj_kernel(x_ref, oda_ref, orw_ref, wout_ref, g_ref, wrh_ref, wrl_ref, br_ref,
                    h_ref, xn_ref, route_ref, cnt_ref, base_ref, *, tm):
    @pl.when(pl.program_id(0) == 0)
    def _():
        base_ref[...] = jnp.zeros_like(base_ref)

    mix = _dot(oda_ref[...], wout_ref[0:DA_WIDTH, :]) + _dot(orw_ref[...], wout_ref[DA_WIDTH:, :])
    h = x_ref[...] + mix
    h_ref[...] = h
    ms = jnp.mean(h * h, axis=-1, keepdims=True)
    xn = h * lax.rsqrt(ms + NORM_EPS) * g_ref[...]
    xn_ref[...] = xn

    x_hi = xn.astype(BF16)
    x_lo = (xn - x_hi.astype(F32)).astype(BF16)
    lg = (_dot(x_hi, wrh_ref[...]) + _dot(x_lo, wrh_ref[...]) + _dot(x_hi, wrl_ref[...])) + br_ref[...]

    lane = lax.broadcasted_iota(jnp.int32, (tm, LANES), 1)
    big = jnp.int32(1 << 20)
    gl = jnp.where(lane < N_GROUPS, lg, NEG)
    gmax = jnp.max(gl, axis=-1, keepdims=True)
    grp = jnp.min(jnp.where(gl == gmax, lane, big), axis=-1, keepdims=True)
    p_grp = 1.0 / jnp.sum(jnp.exp(gl - gmax), axis=-1, keepdims=True)
    eg = jnp.where(lane >= EXPERT_LANE0, (lane - EXPERT_LANE0) // EXPERTS_PER_GROUP, -1)
    el = jnp.where(eg == grp, lg, NEG)
    emax = jnp.max(el, axis=-1, keepdims=True)
    pe = jnp.exp(el - emax)
    probs = pe / jnp.sum(pe, axis=-1, keepdims=True)
    probs = jnp.where(eg == grp, probs, -1.0)
    p1 = jnp.max(probs, axis=-1, keepdims=True)
    i1 = jnp.min(jnp.where(probs == p1, lane, big), axis=-1, keepdims=True)
    probs2 = jnp.where(lane == i1, -1.0, probs)
    p2 = jnp.max(probs2, axis=-1, keepdims=True)
    i2 = jnp.min(jnp.where(probs2 == p2, lane, big), axis=-1, keepdims=True)
    gate1 = p_grp * p1 / (p1 + p2)
    gate2 = p_grp * p2 / (p1 + p2)

    oh1 = jnp.where(lane == i1, 1.0, 0.0)
    oh2 = jnp.where(lane == i2, 1.0, 0.0)
    ri = lax.broadcasted_iota(jnp.int32, (tm, tm), 0)
    rj = lax.broadcasted_iota(jnp.int32, (tm, tm), 1)
    lower = jnp.where(ri > rj, 1.0, 0.0).astype(BF16)
    base = base_ref[...]
    tot1 = jnp.sum(oh1, axis=0, keepdims=True)
    c1 = base + _dot(lower, oh1.astype(BF16))
    c2 = base + tot1 + _dot(lower, oh2.astype(BF16))
    rank1 = jnp.sum(oh1 * c1, axis=-1, keepdims=True)
    rank2 = jnp.sum(oh2 * c2, axis=-1, keepdims=True)
    base = base + tot1 + jnp.sum(oh2, axis=0, keepdims=True)
    base_ref[...] = base
    cnt_ref[...] = base

    e1 = (i1 - EXPERT_LANE0).astype(F32)
    e2 = (i2 - EXPERT_LANE0).astype(F32)
    rec = jnp.zeros((tm, LANES), F32)
    for ln, val in ((ROUTE_E, e1), (ROUTE_E + 1, e2), (ROUTE_G, gate1), (ROUTE_G + 1, gate2),
                    (ROUTE_R, rank1), (ROUTE_R + 1, rank2)):
        rec = jnp.where(lane == ln, val, rec)
    route_ref[...] = rec


def _outproj(x2, o_da, o_rw, w_out, g_ffn, wr_hi, wr_lo, b_r, tm=256):
    t = x2.shape[0]
    const = lambda shape: pl.BlockSpec(shape, lambda i: (0, 0))
    return pl.pallas_call(
        functools.partial(_outproj_kernel, tm=tm),
        grid=(t // tm,),
        in_specs=[
            pl.BlockSpec((tm, D_MODEL), lambda i: (i, 0)),
            pl.BlockSpec((tm, DA_WIDTH), lambda i: (i, 0)),
            pl.BlockSpec((tm, RW_WIDTH), lambda i: (i, 0)),
            const((D_MODEL, D_MODEL)),
            const((1, D_MODEL)),
            const((D_MODEL, LANES)),
            const((D_MODEL, LANES)),
            const((1, LANES)),
        ],
        out_specs=[
            pl.BlockSpec((tm, D_MODEL), lambda i: (i, 0)),
            pl.BlockSpec((tm, D_MODEL), lambda i: (i, 0)),
            pl.BlockSpec((tm, LANES), lambda i: (i, 0)),
            const((1, LANES)),
        ],
        out_shape=[
            jax.ShapeDtypeStruct((t, D_MODEL), F32),
            jax.ShapeDtypeStruct((t, D_MODEL), F32),
            jax.ShapeDtypeStruct((t, LANES), F32),
            jax.ShapeDtypeStruct((1, LANES), F32),
        ],
        scratch_shapes=[pltpu.VMEM((1, LANES), F32)],
        compiler_params=pltpu.CompilerParams(
            dimension_semantics=("arbitrary",), vmem_limit_bytes=VMEM_LIMIT),
        name="outproj_router",
    )(x2, o_da, o_rw, w_out, g_ffn, wr_hi, wr_lo, b_r)


def _row_copy(src_hbm, dst_ref, src_row, dst_row, sem):
    return pltpu.make_async_copy(src_hbm.at[pl.ds(src_row, 1)], dst_ref.at[pl.ds(dst_row, 1)], sem)


def _dispatch_kernel(dest_ref, x_ref, xb_in_hbm, xb_hbm, sem, *, td):
    del xb_in_hbm
    t0 = pl.program_id(0) * td

    def start(i, _):
        tok = t0 + i
        _row_copy(x_ref, xb_hbm, i, dest_ref[2 * tok], sem).start()
        _row_copy(x_ref, xb_hbm, i, dest_ref[2 * tok + 1], sem).start()
        return 0

    def wait(i, _):
        _row_copy(x_ref, xb_hbm, 0, 0, sem).wait()
        _row_copy(x_ref, xb_hbm, 0, 0, sem).wait()
        return 0

    lax.fori_loop(0, td, start, 0)
    lax.fori_loop(0, td, wait, 0)


def _dispatch(dest_flat, xn2, xb_zero, td=256):
    t = xn2.shape[0]
    return pl.pallas_call(
        functools.partial(_dispatch_kernel, td=td),
        grid_spec=pltpu.PrefetchScalarGridSpec(
            num_scalar_prefetch=1,
            grid=(t // td,),
            in_specs=[pl.BlockSpec((td, D_MODEL), lambda i, d: (i, 0)), pl.BlockSpec(memory_space=pl.ANY)],
            out_specs=pl.BlockSpec(memory_space=pl.ANY),
            scratch_shapes=[pltpu.SemaphoreType.DMA],
        ),
        out_shape=jax.ShapeDtypeStruct(xb_zero.shape, xb_zero.dtype),
        input_output_aliases={2: 0},
        compiler_params=pltpu.CompilerParams(dimension_semantics=("arbitrary",)),
        name="dispatch",
    )(dest_flat, xn2, xb_zero)


def _experts_kernel(be_ref, nu_ref, xb_ref, wg_ref, wu_ref, wd_ref, yb_ref):
    del be_ref

    @pl.when(pl.program_id(0) < nu_ref[0])
    def _():
        x = xb_ref[...].astype(BF16)
        hg = _dot(x, wg_ref[...])
        hu = _dot(x, wu_ref[...])
        hid = hg * _sigmoid(hg) * hu
        yb_ref[...] = _dot(hid.astype(BF16), wd_ref[...])

    @pl.when(pl.program_id(0) >= nu_ref[0])
    def _():
        yb_ref[...] = jnp.zeros_like(yb_ref)


def _experts(block_e, n_used, xb, w_gate, w_up, w_down):
    nb = xb.shape[0] // EXPERT_BLOCK
    rows = lambda i, be, nu: (jnp.minimum(i, nu[0] - 1), 0)
    return pl.pallas_call(
        _experts_kernel,
        grid_spec=pltpu.PrefetchScalarGridSpec(
            num_scalar_prefetch=2,
            grid=(nb,),
            in_specs=[
                pl.BlockSpec((EXPERT_BLOCK, D_MODEL), rows),
                pl.BlockSpec((None, D_MODEL, D_EXPERT), lambda i, be, nu: (be[i], 0, 0)),
                pl.BlockSpec((None, D_MODEL, D_EXPERT), lambda i, be, nu: (be[i], 0, 0)),
                pl.BlockSpec((None, D_EXPERT, D_MODEL), lambda i, be, nu: (be[i], 0, 0)),
            ],
            out_specs=pl.BlockSpec((EXPERT_BLOCK, D_MODEL), lambda i, be, nu: (i, 0)),
        ),
        out_shape=jax.ShapeDtypeStruct(xb.shape, F32),
        compiler_params=pltpu.CompilerParams(
            dimension_semantics=("arbitrary",), vmem_limit_bytes=VMEM_LIMIT),
        name="experts",
    )(block_e, n_used, xb, w_gate, w_up, w_down)


def _combine_kernel(dest_ref, route_ref, h_ref, g_ref, yb_hbm, o_ref, y1_ref, y2_ref, sem, *, tc):
    t0 = pl.program_id(0) * tc

    def start(i, _):
        tok = t0 + i
        _row_copy(yb_hbm, y1_ref, dest_ref[2 * tok], i, sem).start()
        _row_copy(yb_hbm, y2_ref, dest_ref[2 * tok + 1], i, sem).start()
        return 0

    def wait(i, _):
        _row_copy(yb_hbm, y1_ref, 0, 0, sem).wait()
        _row_copy(yb_hbm, y2_ref, 0, 0, sem).wait()
        return 0

    lax.fori_loop(0, tc, start, 0)
    lax.fori_loop(0, tc, wait, 0)

    rec = route_ref[...]
    lane = lax.broadcasted_iota(jnp.int32, (tc, LANES), 1)
    g1 = jnp.sum(jnp.where(lane == ROUTE_G, rec, 0.0), axis=-1, keepdims=True)
    g2 = jnp.sum(jnp.where(lane == ROUTE_G + 1, rec, 0.0), axis=-1, keepdims=True)
    h = h_ref[...] + (y1_ref[...] * g1 + y2_ref[...] * g2)
    ms = jnp.mean(h * h, axis=-1, keepdims=True)
    o_ref[...] = h * lax.rsqrt(ms + NORM_EPS) * g_ref[...]


def _combine(dest_flat, route, h1, g_final, yb, tc=256):
    t = h1.shape[0]
    return pl.pallas_call(
        functools.partial(_combine_kernel, tc=tc),
        grid_spec=pltpu.PrefetchScalarGridSpec(
            num_scalar_prefetch=1,
            grid=(t // tc,),
            in_specs=[
                pl.BlockSpec((tc, LANES), lambda i, d: (i, 0)),
                pl.BlockSpec((tc, D_MODEL), lambda i, d: (i, 0)),
                pl.BlockSpec((1, D_MODEL), lambda i, d: (0, 0)),
                pl.BlockSpec(memory_space=pl.ANY),
            ],
            out_specs=pl.BlockSpec((tc, D_MODEL), lambda i, d: (i, 0)),
            scratch_shapes=[
                pltpu.VMEM((tc, D_MODEL), F32),
                pltpu.VMEM((tc, D_MODEL), F32),
                pltpu.SemaphoreType.DMA,
            ],
        ),
        out_shape=jax.ShapeDtypeStruct((t, D_MODEL), F32),
        compiler_params=pltpu.CompilerParams(
            dimension_semantics=("arbitrary",), vmem_limit_bytes=VMEM_LIMIT),
        name="combine",
    )(dest_flat, route, h1, g_final, yb)


def _pad_cols(w, n):
    return jnp.pad(w, ((0, 0), (0, n - w.shape[1])))


def kernel(x, g_mix, w_in, w_out, da_lambda_q1, da_lambda_k1, da_lambda_q2, da_lambda_k2, da_subln_g,
           rw_mu_x, rw_mu_rkv, rw_w0, rw_w1, rw_w2, rw_a0, rw_a1, rw_a2, rw_g1, rw_g2, rw_k_k, rw_k_a,
           rw_r_k, rw_ln_w, rw_ln_b, g_ffn, moe_w_group, moe_b_group, moe_w_expert, moe_b_expert,
           moe_w_gate, moe_w_up, moe_w_down, g_final):
    b, s, d = x.shape
    t = b * s
    x2 = x.reshape(t, d)

    mu = rw_mu_x[0]
    lora = (rw_w1[0], rw_a1[0], rw_g1[0])
    keep = _pad_cols(jnp.concatenate([(1.0 - mu[i])[:, None] * w for i, w in enumerate(lora)], axis=1), LORA_PAD)
    prev = _pad_cols(jnp.concatenate([mu[i][:, None] * w for i, w in enumerate(lora)], axis=1), LORA_PAD)
    w_all = jnp.concatenate([w_in[0], keep, prev], axis=1).astype(BF16)
    proj = _inproj(x2, g_mix, w_all)

    slopes = jnp.asarray([2.0 ** (-8.0 * (i + 1) / DA_HEADS) for i in range(DA_HEADS)], F32)
    lam4 = jnp.concatenate([da_lambda_q1, da_lambda_k1, da_lambda_q2, da_lambda_k2], axis=0)
    o_da = _diffattn(proj, slopes, lam4, da_subln_g.reshape(DA_V_DIM, 1), b, s)

    w2cat = jnp.zeros((LORA_PAD, 3 * RW_WIDTH), F32)
    w2cat = w2cat.at[0:LORA_W, 0:RW_WIDTH].set(rw_w2[0])
    w2cat = w2cat.at[LORA_W:LORA_W + LORA_A, RW_WIDTH:2 * RW_WIDTH].set(rw_a2[0])
    w2cat = w2cat.at[LORA_W + LORA_A:LORA_W + LORA_A + LORA_G, 2 * RW_WIDTH:].set(rw_g2[0])
    o_rw = _rwkv(proj, rw_mu_rkv[0], rw_w0, rw_a0, w2cat.astype(BF16), rw_k_k, rw_k_a,
                 rw_r_k.reshape(1, RW_WIDTH), rw_ln_w, rw_ln_b, b, s)

    w_r = _pad_cols(jnp.concatenate([moe_w_group[0], moe_w_expert[0]], axis=1), LANES)
    wr_hi = w_r.astype(BF16)
    wr_lo = (w_r - wr_hi.astype(F32)).astype(BF16)
    b_r = _pad_cols(jnp.concatenate([moe_b_group[0], moe_b_expert[0].reshape(-1)])[None, :], LANES)
    h1, xn2, route, cnt = _outproj(x2, o_da, o_rw, w_out[0].astype(BF16), g_ffn, wr_hi, wr_lo, b_r)

    counts = cnt[0, EXPERT_LANE0:EXPERT_LANE0 + N_EXPERTS].astype(jnp.int32)
    pcounts = ((counts + EXPERT_BLOCK - 1) // EXPERT_BLOCK) * EXPERT_BLOCK
    pends = jnp.cumsum(pcounts)
    pstarts = pends - pcounts
    e_idx = route[:, ROUTE_E:ROUTE_E + 2].astype(jnp.int32)
    rank = route[:, ROUTE_R:ROUTE_R + 2].astype(jnp.int32)
    dest = (pstarts[e_idx] + rank).reshape(-1)
    n_blocks = (2 * t) // EXPERT_BLOCK + N_EXPERTS
    block_e = jnp.minimum(
        jnp.searchsorted(pends, jnp.arange(n_blocks, dtype=jnp.int32) * EXPERT_BLOCK, side='right'),
        N_EXPERTS - 1).astype(jnp.int32)
    n_used = (pends[-1:] // EXPERT_BLOCK).astype(jnp.int32)

    xb = _dispatch(dest, xn2, jnp.zeros((n_blocks * EXPERT_BLOCK, d), F32))
    yb = _experts(block_e, n_used, xb, moe_w_gate[0].astype(BF16), moe_w_up[0].astype(BF16),
                  moe_w_down[0].astype(BF16))
    out = _combine(dest, route, h1, g_final[None, :], yb)
    return out.reshape(b, s, d)
```

```python
import functools
import math

import jax
import jax.numpy as jnp
from jax import lax
from jax.experimental import pallas as pl
from jax.experimental.pallas import tpu as pltpu

F32 = jnp.float32
BF16 = jnp.bfloat16

D_MODEL = 2048
DA_WIDTH = 1024
RW_WIDTH = 1024
DA_HEADS = 8
DA_V_DIM = 128
DA_QK_DIM = 64
RW_N = 64
RW_PAIRS = RW_WIDTH // 128
LORA_W, LORA_A, LORA_G = 64, 64, 160
LORA_PAD = 384
IN_COLS = 3 * DA_WIDTH + 3 * RW_WIDTH
NC = IN_COLS + 2 * LORA_PAD
N_GROUPS = 4
EXPERTS_PER_GROUP = 8
N_EXPERTS = 32
D_EXPERT = 512
NORM_EPS = 1e-6
LN_X_EPS = 64e-5
LAM_INIT = 0.8 - 0.6 * math.exp(-0.3 * 0)
LANES = 128
CHUNK = 64
EXPERT_BLOCK = 256
NEG = -1e30
VMEM_LIMIT = 48 * 1024 * 1024
EXPERTS_VMEM_LIMIT = 56 * 1024 * 1024


def _dot(a, b):
    return jnp.dot(a, b, preferred_element_type=F32)


def _dot_nt(a, b):
    return lax.dot_general(a, b, (((1,), (1,)), ((), ())), preferred_element_type=F32)


def _dot_tn(a, b):
    return lax.dot_general(a, b, (((0,), (0,)), ((), ())), preferred_element_type=F32)


def _sigmoid(x):
    return 1.0 / (1.0 + jnp.exp(-x))


def _inproj_kernel(x_ref, g_ref, w_ref, o_ref, xn_ref):
    @pl.when(pl.program_id(1) == 0)
    def _():
        x = x_ref[...]
        ms = jnp.mean(x * x, axis=-1, keepdims=True)
        xn_ref[...] = (x * lax.rsqrt(ms + NORM_EPS) * g_ref[...]).astype(BF16)

    o_ref[...] = _dot(xn_ref[...], w_ref[...]).astype(o_ref.dtype)


def _inproj(x2, g, w_all, tm=512, tn=1152):
    t = x2.shape[0]
    return pl.pallas_call(
        _inproj_kernel,
        grid=(t // tm, NC // tn),
        in_specs=[
            pl.BlockSpec((tm, D_MODEL), lambda i, j: (i, 0)),
            pl.BlockSpec((1, D_MODEL), lambda i, j: (0, 0)),
            pl.BlockSpec((D_MODEL, tn), lambda i, j: (0, j)),
        ],
        out_specs=pl.BlockSpec((tm, tn), lambda i, j: (i, j)),
        out_shape=jax.ShapeDtypeStruct((t, NC), BF16),
        scratch_shapes=[pltpu.VMEM((tm, D_MODEL), BF16)],
        compiler_params=pltpu.CompilerParams(
            dimension_semantics=("parallel", "arbitrary"), vmem_limit_bytes=VMEM_LIMIT),
        name="inproj",
    )(x2, g, w_all)


def _da_kernel(slopes_ref, lam_ref, subg_ref, q_ref, k_ref, v_ref, o_ref, vt_ref, m_ref, l_ref, acc_ref,
               s_ref, p_ref, alpha_ref, *, tq, nsub, seq):
    tk = tq
    h = pl.program_id(1)
    qi = pl.program_id(2)
    slope = slopes_ref[h]

    @pl.when(qi == 0)
    def _():
        for blk in range(seq // LANES):
            rows = slice(blk * LANES, (blk + 1) * LANES)
            vt_ref[:, rows] = v_ref[rows, :].astype(F32).T.astype(BF16)

    q = q_ref[...] * jnp.asarray(DA_QK_DIM ** -0.5, BF16)
    lane = lax.broadcasted_iota(jnp.int32, (tq, LANES), 1)
    zero = jnp.zeros_like(q)
    qc = (jnp.where(lane < DA_QK_DIM, q, zero), jnp.where(lane >= DA_QK_DIM, q, zero))
    sub = tk // nsub
    krow = lax.broadcasted_iota(jnp.int32, (sub, tq), 0)
    qcol = lax.broadcasted_iota(jnp.int32, (sub, tq), 1)
    bias = [slope * (krow + u * sub).astype(F32) for u in range(nsub)]

    m_ref[...] = jnp.full_like(m_ref, NEG)
    l_ref[...] = jnp.zeros_like(l_ref)
    acc_ref[...] = jnp.zeros_like(acc_ref)
    p_ref[...] = jnp.zeros_like(p_ref)
    alpha_ref[...] = jnp.ones_like(alpha_ref)
    cu = [(c, u) for c in range(2) for u in range(nsub)]
    rows = lambda u: slice(u * sub, (u + 1) * sub)

    def scores_to_scratch(j):
        ks = pl.multiple_of(j * tk, tk)
        for c, u in cu:
            s_ref[c, rows(u), :] = _dot_nt(k_ref[pl.ds(ks + u * sub, sub), :], qc[c]) + bias[u]

    def values_from_scratch(j):
        ks = pl.multiple_of(j * tk, tk)
        vt = vt_ref[:, pl.ds(ks, tk)]
        for c in range(2):
            acc_ref[c] = alpha_ref[c:c + 1, :] * acc_ref[c] + _dot(vt, p_ref[c])

    def softmax(j, masked):
        rel = (j - qi) * tk
        off = slope * rel.astype(F32)
        s = {(c, u): s_ref[c, rows(u), :] for c, u in cu}
        if masked:
            s = {(c, u): jnp.where(krow + (rel + u * sub) <= qcol, s[c, u], NEG) for c, u in cu}
        m = [m_ref[c:c + 1, :] for c in range(2)]
        smax = [functools.reduce(jnp.maximum, [s[c, u] for u in range(nsub)]) for c in range(2)]
        m_new = [jnp.maximum(m[c], jnp.max(smax[c], axis=0, keepdims=True) + off) for c in range(2)]
        p = {(c, u): jnp.exp(s[c, u] - (m_new[c] - off)) for c, u in cu}
        psum = [functools.reduce(jnp.add, [p[c, u] for u in range(nsub)]) for c in range(2)]
        alpha = [jnp.exp(m[c] - m_new[c]) for c in range(2)]
        for c in range(2):
            m_ref[c:c + 1, :] = m_new[c]
            l_ref[c:c + 1, :] = alpha[c] * l_ref[c:c + 1, :] + jnp.sum(psum[c], axis=0, keepdims=True)
        return p, alpha

    def publish(p, alpha):
        for c in range(2):
            alpha_ref[c:c + 1, :] = alpha[c]
        for c, u in cu:
            p_ref[c, rows(u), :] = p[c, u].astype(BF16)

    scores_to_scratch(0)

    def body(j, _):
        p, alpha = softmax(j, False)
        values_from_scratch(jnp.maximum(j - 1, 0))
        scores_to_scratch(j + 1)
        publish(p, alpha)
        return 0

    lax.fori_loop(0, qi, body, 0)
    p, alpha = softmax(qi, True)
    values_from_scratch(jnp.maximum(qi - 1, 0))
    publish(p, alpha)
    values_from_scratch(qi)

    lv = lam_ref[...]
    lam = (jnp.exp(jnp.sum(lv[0:1] * lv[1:2], axis=-1, keepdims=True))
           - jnp.exp(jnp.sum(lv[2:3] * lv[3:4], axis=-1, keepdims=True)) + LAM_INIT)
    o = acc_ref[0] / l_ref[0:1, :] - lam * (acc_ref[1] / l_ref[1:2, :])
    ms = jnp.mean(o * o, axis=0, keepdims=True)
    o = o * lax.rsqrt(ms + NORM_EPS) * subg_ref[...] * (1.0 - LAM_INIT)
    o_ref[...] = o.T.astype(o_ref.dtype)


def _diffattn(proj, slopes, lam4, subg_col, b, s, tq=256, nsub=2):
    nq = s // tq
    return pl.pallas_call(
        functools.partial(_da_kernel, tq=tq, nsub=nsub, seq=s),
        grid=(b, DA_HEADS, nq),
        in_specs=[
            pl.BlockSpec(memory_space=pltpu.SMEM),
            pl.BlockSpec((4, DA_QK_DIM), lambda bi, h, qi: (0, 0)),
            pl.BlockSpec((DA_V_DIM, 1), lambda bi, h, qi: (0, 0)),
            pl.BlockSpec((tq, LANES), lambda bi, h, qi: (bi * nq + qi, h)),
            pl.BlockSpec((s, LANES), lambda bi, h, qi: (bi, DA_HEADS + h)),
            pl.BlockSpec((s, LANES), lambda bi, h, qi: (bi, 2 * DA_HEADS + h)),
        ],
        out_specs=pl.BlockSpec((tq, LANES), lambda bi, h, qi: (bi * nq + qi, h)),
        out_shape=jax.ShapeDtypeStruct((b * s, DA_WIDTH), BF16),
        scratch_shapes=[
            pltpu.VMEM((DA_V_DIM, s), BF16),
            pltpu.VMEM((2, tq), F32),
            pltpu.VMEM((2, tq), F32),
            pltpu.VMEM((2, DA_V_DIM, tq), F32),
            pltpu.VMEM((2, tq, tq), F32),
            pltpu.VMEM((2, tq, tq), BF16),
            pltpu.VMEM((2, tq), F32),
        ],
        compiler_params=pltpu.CompilerParams(
            dimension_semantics=("parallel", "parallel", "arbitrary"), vmem_limit_bytes=VMEM_LIMIT),
        name="diffattn",
    )(slopes, lam4, subg_col, proj, proj, proj)


def _rwkv_kernel(r_ref, k_ref, v_ref, p1_ref, p2_ref, mu_ref, w0_ref, a0_ref, w2_ref, kk_ref, ka_ref,
                 rk_ref, lnw_ref, lnb_ref, o_ref, state_ref, crkv_ref, cp2_ref):
    L = CHUNK
    c = pl.program_id(1)

    @pl.when(c == 0)
    def _():
        state_ref[...] = jnp.zeros_like(state_ref)
        crkv_ref[...] = jnp.zeros_like(crkv_ref)
        cp2_ref[...] = jnp.zeros_like(cp2_ref)

    first = lax.broadcasted_iota(jnp.int32, (L, 1), 0) == 0

    def shift(x, carry):
        return jnp.where(first, carry, pltpu.roll(x, 1, 0))

    r_raw = r_ref[...].astype(F32)
    k_raw = k_ref[...].astype(F32)
    v_raw = v_ref[...].astype(F32)
    p2 = p2_ref[...].astype(F32)
    rs = shift(r_raw, crkv_ref[:, 0:RW_WIDTH])
    ks = shift(k_raw, crkv_ref[:, RW_WIDTH:2 * RW_WIDTH])
    vs = shift(v_raw, crkv_ref[:, 2 * RW_WIDTH:3 * RW_WIDTH])
    lin = p1_ref[...].astype(F32) + shift(p2, cp2_ref[...])
    crkv_ref[:, 0:RW_WIDTH] = r_raw[L - 1:L, :]
    crkv_ref[:, RW_WIDTH:2 * RW_WIDTH] = k_raw[L - 1:L, :]
    crkv_ref[:, 2 * RW_WIDTH:3 * RW_WIDTH] = v_raw[L - 1:L, :]
    cp2_ref[...] = p2[L - 1:L, :]

    r = r_raw + (rs - r_raw) * mu_ref[0:1, :]
    k = k_raw + (ks - k_raw) * mu_ref[1:2, :]
    v = v_raw + (vs - v_raw) * mu_ref[2:3, :]

    ll = lax.broadcasted_iota(jnp.int32, (L, LORA_PAD), 1)
    z = jnp.where(ll < LORA_W, jnp.tanh(lin),
                  jnp.where(ll < LORA_W + LORA_A, lin,
                            jnp.where(ll < LORA_W + LORA_A + LORA_G, _sigmoid(lin), 0.0)))
    d = _dot(z.astype(BF16), w2_ref[...])
    logw = -_sigmoid(w0_ref[...] + d[:, 0:RW_WIDTH]) * math.exp(-0.5)
    a_sig = _sigmoid(a0_ref[...] + d[:, RW_WIDTH:2 * RW_WIDTH])
    g = d[:, 2 * RW_WIDTH:3 * RW_WIDTH]

    li = lax.broadcasted_iota(jnp.int32, (LANES, LANES), 0)
    lj = lax.broadcasted_iota(jnp.int32, (LANES, LANES), 1)
    same_head = jnp.where((li // RW_N) == (lj // RW_N), 1.0, 0.0).astype(BF16)

    def head_sum(x):
        xb = x.astype(BF16)
        return jnp.concatenate(
            [_dot(xb[:, p * LANES:(p + 1) * LANES], same_head) for p in range(RW_PAIRS)], axis=1)

    kk = k * kk_ref[...]
    kk = kk * lax.rsqrt(jnp.maximum(head_sum(kk * kk), 1e-24))
    k = k * (1.0 + (a_sig - 1.0) * ka_ref[...])
    bonus = head_sum(r * k * rk_ref[...])

    ti = lax.broadcasted_iota(jnp.int32, (L, L), 0)
    tj = lax.broadcasted_iota(jnp.int32, (L, L), 1)
    tri = jnp.where(ti >= tj, 1.0, 0.0).astype(BF16)
    w_hi = logw.astype(BF16)
    w_r1 = logw - w_hi.astype(F32)
    w_mid = w_r1.astype(BF16)
    w_lo = (w_r1 - w_mid.astype(F32)).astype(BF16)
    cw = _dot(tri, w_hi) + _dot(tri, w_mid) + _dot(tri, w_lo)
    cw_last = cw[L - 1:L, :]
    e_in = jnp.exp(cw)
    e_out = jnp.exp(-cw)
    e_end = jnp.exp(cw_last - cw)
    w_end = jnp.exp(cw_last)

    at = -kk * jnp.exp(cw - logw)
    rt = r * e_in
    b = kk * a_sig
    bt = b * e_out
    kt = k * e_out
    bd = b * e_end
    kd = k * e_end

    lane = lax.broadcasted_iota(jnp.int32, (L, LANES), 1)
    lo = lane < RW_N

    def stack(x, p):
        xs = x[:, p * LANES:(p + 1) * LANES]
        return jnp.concatenate([jnp.where(lo, xs, 0.0), jnp.where(lo, 0.0, xs)], axis=0).astype(BF16)

    si = lax.broadcasted_iota(jnp.int32, (2 * L, 2 * L), 0)
    sj = lax.broadcasted_iota(jnp.int32, (2 * L, 2 * L), 1)
    strict = (si % L) > (sj % L)
    incl = (si % L) >= (sj % L)
    eye = jnp.where(si == sj, 1.0, 0.0)

    pairs = range(RW_PAIRS)
    cat0 = lambda *xs: jnp.concatenate(xs, axis=0)
    cat1 = lambda *xs: jnp.concatenate(xs, axis=1)
    bf = lambda x: x.astype(BF16)
    a_s = [stack(at, p) for p in pairs]
    r_s = [stack(rt, p) for p in pairs]
    v_s = [stack(v, p) for p in pairs]
    bk = [cat0(stack(bt, p), stack(kt, p)) for p in pairs]
    bkd = [cat0(stack(bd, p), stack(kd, p)) for p in pairs]
    aa = [_dot_nt(a_s[p], bk[p]) for p in pairs]
    rr = [_dot_nt(r_s[p], bk[p]) for p in pairs]
    a_ab = [jnp.where(strict, aa[p][:, 0:2 * L], 0.0) for p in pairs]
    a_ak = [bf(jnp.where(strict, aa[p][:, 2 * L:4 * L], 0.0)) for p in pairs]
    a_rb = [bf(jnp.where(incl, rr[p][:, 0:2 * L], 0.0)) for p in pairs]
    a_rk = [bf(jnp.where(incl, rr[p][:, 2 * L:4 * L], 0.0)) for p in pairs]
    ab = [bf(a) for a in a_ab]
    qpow = [_dot(ab[p], ab[p]) for p in pairs]
    inv = [eye + a_ab[p] for p in pairs]
    for _ in range(int(math.log2(L)) - 2):
        qb = [bf(q) for q in qpow]
        res = [_dot(qb[p], cat1(qb[p], bf(inv[p]))) for p in pairs]
        qpow = [res[p][:, 0:2 * L] for p in pairs]
        inv = [inv[p] + res[p][:, 2 * L:4 * L] for p in pairs]
    inv = [bf(inv[p] + _dot(bf(qpow[p]), bf(inv[p]))) for p in pairs]
    h_old = [state_ref[p] for p in pairs]
    h_b = [bf(h) for h in h_old]
    x = [_dot(cat1(a_s[p], a_ak[p]), cat0(h_b[p], v_s[p])) for p in pairs]
    u_b = [bf(_dot(inv[p], bf(x[p]))) for p in pairs]
    y2 = [_dot(cat1(r_s[p], a_rb[p], a_rk[p]), cat0(h_b[p], u_b[p], v_s[p])) for p in pairs]
    for p in pairs:
        w_col = jnp.broadcast_to(w_end[:, p * LANES:(p + 1) * LANES], (LANES, LANES)).T
        state_ref[p] = w_col * h_old[p] + _dot_tn(bkd[p], cat0(u_b[p], v_s[p]))
    y = cat1(*[y2[p][0:L] + y2[p][L:2 * L] for p in pairs])

    mean = head_sum(y) * (1.0 / RW_N)
    yc = y - mean
    var = head_sum(yc * yc) * (1.0 / RW_N)
    y = yc * lax.rsqrt(var + LN_X_EPS) * lnw_ref[...] + lnb_ref[...]
    y = y + bonus * v
    o_ref[...] = (y * g).astype(o_ref.dtype)


def _rwkv(proj, mu_rkv, w0, a0, w2cat, k_k, k_a, r_k, ln_w, ln_b, b, s):
    L = CHUNK
    nc = s // L
    row = lambda n: pl.BlockSpec((1, n), lambda bi, c: (0, 0))
    rkv_col0 = 3 * DA_WIDTH // RW_WIDTH
    lora_col0 = IN_COLS // LORA_PAD
    return pl.pallas_call(
        _rwkv_kernel,
        grid=(b, nc),
        in_specs=[
            pl.BlockSpec((L, RW_WIDTH), lambda bi, c: (bi * nc + c, rkv_col0)),
            pl.BlockSpec((L, RW_WIDTH), lambda bi, c: (bi * nc + c, rkv_col0 + 1)),
            pl.BlockSpec((L, RW_WIDTH), lambda bi, c: (bi * nc + c, rkv_col0 + 2)),
            pl.BlockSpec((L, LORA_PAD), lambda bi, c: (bi * nc + c, lora_col0)),
            pl.BlockSpec((L, LORA_PAD), lambda bi, c: (bi * nc + c, lora_col0 + 1)),
            pl.BlockSpec((3, RW_WIDTH), lambda bi, c: (0, 0)),
            row(RW_WIDTH), row(RW_WIDTH),
            pl.BlockSpec((LORA_PAD, 3 * RW_WIDTH), lambda bi, c: (0, 0)),
            row(RW_WIDTH), row(RW_WIDTH), row(RW_WIDTH), row(RW_WIDTH), row(RW_WIDTH),
        ],
        out_specs=pl.BlockSpec((L, RW_WIDTH), lambda bi, c: (bi * nc + c, 0)),
        out_shape=jax.ShapeDtypeStruct((b * s, RW_WIDTH), BF16),
        scratch_shapes=[
            pltpu.VMEM((RW_PAIRS, LANES, LANES), F32),
            pltpu.VMEM((1, 3 * RW_WIDTH), F32),
            pltpu.VMEM((1, LORA_PAD), F32),
        ],
        compiler_params=pltpu.CompilerParams(
            dimension_semantics=("parallel", "arbitrary"), vmem_limit_bytes=VMEM_LIMIT),
        name="rwkv",
    )(proj, proj, proj, proj, proj, mu_rkv, w0, a0, w2cat, k_k, k_a, r_k, ln_w, ln_b)


ROUTE_E, ROUTE_G, ROUTE_R = 0, 2, 4
EXPERT_LANE0 = N_GROUPS


def _outproj_kernel(x_ref, oda_ref, orw_ref, wout_ref, g_ref, wrh_ref, wrl_ref, br_ref,
                    h_ref, xn_ref, route_ref, cnt_ref, base_ref, *, tm):
    @pl.when(pl.program_id(0) == 0)
    def _():
        base_ref[...] = jnp.zeros_like(base_ref)

    mix = _dot(oda_ref[...], wout_ref[0:DA_WIDTH, :]) + _dot(orw_ref[...], wout_ref[DA_WIDTH:, :])
    h = x_ref[...] + mix
    h_ref[...] = h
    ms = jnp.mean(h * h, axis=-1, keepdims=True)
    xn = h * lax.rsqrt(ms + NORM_EPS) * g_ref[...]
    xn_ref[...] = xn

    x_hi = xn.astype(BF16)
    x_lo = (xn - x_hi.astype(F32)).astype(BF16)
    lg = (_dot(x_hi, wrh_ref[...]) + _dot(x_lo, wrh_ref[...]) + _dot(x_hi, wrl_ref[...])) + br_ref[...]

    lane = lax.broadcasted_iota(jnp.int32, (tm, LANES), 1)
    big = jnp.int32(1 << 20)
    gl = jnp.where(lane < N_GROUPS, lg, NEG)
    gmax = jnp.max(gl, axis=-1, keepdims=True)
    grp = jnp.min(jnp.where(gl == gmax, lane, big), axis=-1, keepdims=True)
    p_grp = 1.0 / jnp.sum(jnp.exp(gl - gmax), axis=-1, keepdims=True)
    eg = jnp.where(lane >= EXPERT_LANE0, (lane - EXPERT_LANE0) // EXPERTS_PER_GROUP, -1)
    el = jnp.where(eg == grp, lg, NEG)
    emax = jnp.max(el, axis=-1, keepdims=True)
    pe = jnp.exp(el - emax)
    probs = pe / jnp.sum(pe, axis=-1, keepdims=True)
    probs = jnp.where(eg == grp, probs, -1.0)
    p1 = jnp.max(probs, axis=-1, keepdims=True)
    i1 = jnp.min(jnp.where(probs == p1, lane, big), axis=-1, keepdims=True)
    probs2 = jnp.where(lane == i1, -1.0, probs)
    p2 = jnp.max(probs2, axis=-1, keepdims=True)
    i2 = jnp.min(jnp.where(probs2 == p2, lane, big), axis=-1, keepdims=True)
    gate1 = p_grp * p1 / (p1 + p2)
    gate2 = p_grp * p2 / (p1 + p2)

    oh1 = jnp.where(lane == i1, 1.0, 0.0)
    oh2 = jnp.where(lane == i2, 1.0, 0.0)
    ri = lax.broadcasted_iota(jnp.int32, (tm, tm), 0)
    rj = lax.broadcasted_iota(jnp.int32, (tm, tm), 1)
    lower = jnp.where(ri > rj, 1.0, 0.0).astype(BF16)
    base = base_ref[...]
    tot1 = jnp.sum(oh1, axis=0, keepdims=True)
    c1 = base + _dot(lower, oh1.astype(BF16))
    c2 = base + tot1 + _dot(lower, oh2.astype(BF16))
    rank1 = jnp.sum(oh1 * c1, axis=-1, keepdims=True)
    rank2 = jnp.sum(oh2 * c2, axis=-1, keepdims=True)
    base = base + tot1 + jnp.sum(oh2, axis=0, keepdims=True)
    base_ref[...] = base
    cnt_ref[...] = base

    e1 = (i1 - EXPERT_LANE0).astype(F32)
    e2 = (i2 - EXPERT_LANE0).astype(F32)
    rec = jnp.zeros((tm, LANES), F32)
    for ln, val in ((ROUTE_E, e1), (ROUTE_E + 1, e2), (ROUTE_G, gate1), (ROUTE_G + 1, gate2),
                    (ROUTE_R, rank1), (ROUTE_R + 1, rank2)):
        rec = jnp.where(lane == ln, val, rec)
    route_ref[...] = rec


def _outproj(x2, o_da, o_rw, w_out, g_ffn, wr_hi, wr_lo, b_r, tm=256):
    t = x2.shape[0]
    const = lambda shape: pl.BlockSpec(shape, lambda i: (0, 0))
    return pl.pallas_call(
        functools.partial(_outproj_kernel, tm=tm),
        grid=(t // tm,),
        in_specs=[
            pl.BlockSpec((tm, D_MODEL), lambda i: (i, 0)),
            pl.BlockSpec((tm, DA_WIDTH), lambda i: (i, 0)),
            pl.BlockSpec((tm, RW_WIDTH), lambda i: (i, 0)),
            const((D_MODEL, D_MODEL)),
            const((1, D_MODEL)),
            const((D_MODEL, LANES)),
            const((D_MODEL, LANES)),
            const((1, LANES)),
        ],
        out_specs=[
            pl.BlockSpec((tm, D_MODEL), lambda i: (i, 0)),
            pl.BlockSpec((tm, D_MODEL), lambda i: (i, 0)),
            pl.BlockSpec((tm, LANES), lambda i: (i, 0)),
            const((1, LANES)),
        ],
        out_shape=[
            jax.ShapeDtypeStruct((t, D_MODEL), F32),
            jax.ShapeDtypeStruct((t, D_MODEL), F32),
            jax.ShapeDtypeStruct((t, LANES), F32),
            jax.ShapeDtypeStruct((1, LANES), F32),
        ],
        scratch_shapes=[pltpu.VMEM((1, LANES), F32)],
        compiler_params=pltpu.CompilerParams(
            dimension_semantics=("arbitrary",), vmem_limit_bytes=VMEM_LIMIT),
        name="outproj_router",
    )(x2, o_da, o_rw, w_out, g_ffn, wr_hi, wr_lo, b_r)


def _row_copy(src_hbm, dst_ref, src_row, dst_row, sem):
    return pltpu.make_async_copy(src_hbm.at[pl.ds(src_row, 1)], dst_ref.at[pl.ds(dst_row, 1)], sem)


def _dispatch_kernel(dest_ref, x_ref, xb_in_hbm, xb_hbm, sem, *, td):
    del xb_in_hbm
    t0 = pl.program_id(0) * td

    def start(i, _):
        tok = t0 + i
        _row_copy(x_ref, xb_hbm, i, dest_ref[2 * tok], sem).start()
        _row_copy(x_ref, xb_hbm, i, dest_ref[2 * tok + 1], sem).start()
        return 0

    def wait(i, _):
        _row_copy(x_ref, xb_hbm, 0, 0, sem).wait()
        _row_copy(x_ref, xb_hbm, 0, 0, sem).wait()
        return 0

    lax.fori_loop(0, td, start, 0)
    lax.fori_loop(0, td, wait, 0)


def _dispatch(dest_flat, xn2, xb_zero, td=256):
    t = xn2.shape[0]
    return pl.pallas_call(
        functools.partial(_dispatch_kernel, td=td),
        grid_spec=pltpu.PrefetchScalarGridSpec(
            num_scalar_prefetch=1,
            grid=(t // td,),
            in_specs=[pl.BlockSpec((td, D_MODEL), lambda i, d: (i, 0)), pl.BlockSpec(memory_space=pl.ANY)],
            out_specs=pl.BlockSpec(memory_space=pl.ANY),
            scratch_shapes=[pltpu.SemaphoreType.DMA],
        ),
        out_shape=jax.ShapeDtypeStruct(xb_zero.shape, xb_zero.dtype),
        input_output_aliases={2: 0},
        compiler_params=pltpu.CompilerParams(dimension_semantics=("arbitrary",)),
        name="dispatch",
    )(dest_flat, xn2, xb_zero)


def _experts_kernel(be_ref, nu_ref, xb_ref, wg_ref, wu_ref, wd_ref, yb_ref, wgb_ref, wub_ref, wdb_ref):
    i = pl.program_id(0)
    used = i < nu_ref[0]
    new_expert = jnp.logical_or(i == 0, be_ref[i] != be_ref[jnp.maximum(i - 1, 0)])

    @pl.when(jnp.logical_and(used, new_expert))
    def _():
        wgb_ref[...] = wg_ref[...].astype(BF16)
        wub_ref[...] = wu_ref[...].astype(BF16)
        wdb_ref[...] = wd_ref[...].astype(BF16)

    @pl.when(used)
    def _():
        x = xb_ref[...].astype(BF16)
        hg = _dot(x, wgb_ref[...])
        hu = _dot(x, wub_ref[...])
        hid = hg * _sigmoid(hg) * hu
        yb_ref[...] = _dot(hid.astype(BF16), wdb_ref[...])

    @pl.when(jnp.logical_not(used))
    def _():
        yb_ref[...] = jnp.zeros_like(yb_ref)


def _experts(block_e, n_used, xb, w_gate, w_up, w_down):
    nb = xb.shape[0] // EXPERT_BLOCK
    rows = lambda i, be, nu: (jnp.minimum(i, nu[0] - 1), 0)
    return pl.pallas_call(
        _experts_kernel,
        grid_spec=pltpu.PrefetchScalarGridSpec(
            num_scalar_prefetch=2,
            grid=(nb,),
            in_specs=[
                pl.BlockSpec((EXPERT_BLOCK, D_MODEL), rows),
                pl.BlockSpec((None, D_MODEL, D_EXPERT), lambda i, be, nu: (be[i], 0, 0)),
                pl.BlockSpec((None, D_MODEL, D_EXPERT), lambda i, be, nu: (be[i], 0, 0)),
                pl.BlockSpec((None, D_EXPERT, D_MODEL), lambda i, be, nu: (be[i], 0, 0)),
            ],
            out_specs=pl.BlockSpec((EXPERT_BLOCK, D_MODEL), lambda i, be, nu: (i, 0)),
            scratch_shapes=[
                pltpu.VMEM((D_MODEL, D_EXPERT), BF16),
                pltpu.VMEM((D_MODEL, D_EXPERT), BF16),
                pltpu.VMEM((D_EXPERT, D_MODEL), BF16),
            ],
        ),
        out_shape=jax.ShapeDtypeStruct(xb.shape, F32),
        compiler_params=pltpu.CompilerParams(
            dimension_semantics=("arbitrary",), vmem_limit_bytes=EXPERTS_VMEM_LIMIT),
        name="experts",
    )(block_e, n_used, xb, w_gate, w_up, w_down)


def _combine_kernel(dest_ref, route_ref, h_ref, g_ref, yb_hbm, o_ref, y1_ref, y2_ref, sem, *, tc):
    t0 = pl.program_id(0) * tc

    def start(i, _):
        tok = t0 + i
        _row_copy(yb_hbm, y1_ref, dest_ref[2 * tok], i, sem).start()
        _row_copy(yb_hbm, y2_ref, dest_ref[2 * tok + 1], i, sem).start()
        return 0

    def wait(i, _):
        _row_copy(yb_hbm, y1_ref, 0, 0, sem).wait()
        _row_copy(yb_hbm, y2_ref, 0, 0, sem).wait()
        return 0

    lax.fori_loop(0, tc, start, 0)
    lax.fori_loop(0, tc, wait, 0)

    rec = route_ref[...]
    lane = lax.broadcasted_iota(jnp.int32, (tc, LANES), 1)
    g1 = jnp.sum(jnp.where(lane == ROUTE_G, rec, 0.0), axis=-1, keepdims=True)
    g2 = jnp.sum(jnp.where(lane == ROUTE_G + 1, rec, 0.0), axis=-1, keepdims=True)
    h = h_ref[...] + (y1_ref[...] * g1 + y2_ref[...] * g2)
    ms = jnp.mean(h * h, axis=-1, keepdims=True)
    o_ref[...] = h * lax.rsqrt(ms + NORM_EPS) * g_ref[...]


def _combine(dest_flat, route, h1, g_final, yb, tc=256):
    t = h1.shape[0]
    return pl.pallas_call(
        functools.partial(_combine_kernel, tc=tc),
        grid_spec=pltpu.PrefetchScalarGridSpec(
            num_scalar_prefetch=1,
            grid=(t // tc,),
            in_specs=[
                pl.BlockSpec((tc, LANES), lambda i, d: (i, 0)),
                pl.BlockSpec((tc, D_MODEL), lambda i, d: (i, 0)),
                pl.BlockSpec((1, D_MODEL), lambda i, d: (0, 0)),
                pl.BlockSpec(memory_space=pl.ANY),
            ],
            out_specs=pl.BlockSpec((tc, D_MODEL), lambda i, d: (i, 0)),
            scratch_shapes=[
                pltpu.VMEM((tc, D_MODEL), F32),
                pltpu.VMEM((tc, D_MODEL), F32),
                pltpu.SemaphoreType.DMA,
            ],
        ),
        out_shape=jax.ShapeDtypeStruct((t, D_MODEL), F32),
        compiler_params=pltpu.CompilerParams(
            dimension_semantics=("arbitrary",), vmem_limit_bytes=VMEM_LIMIT),
        name="combine",
    )(dest_flat, route, h1, g_final, yb)


def _pad_cols(w, n):
    return jnp.pad(w, ((0, 0), (0, n - w.shape[1])))


def kernel(x, g_mix, w_in, w_out, da_lambda_q1, da_lambda_k1, da_lambda_q2, da_lambda_k2, da_subln_g,
           rw_mu_x, rw_mu_rkv, rw_w0, rw_w1, rw_w2, rw_a0, rw_a1, rw_a2, rw_g1, rw_g2, rw_k_k, rw_k_a,
           rw_r_k, rw_ln_w, rw_ln_b, g_ffn, moe_w_group, moe_b_group, moe_w_expert, moe_b_expert,
           moe_w_gate, moe_w_up, moe_w_down, g_final):
    b, s, d = x.shape
    t = b * s
    x2 = x.reshape(t, d)

    mu = rw_mu_x[0]
    lora = (rw_w1[0], rw_a1[0], rw_g1[0])
    keep = _pad_cols(jnp.concatenate([(1.0 - mu[i])[:, None] * w for i, w in enumerate(lora)], axis=1), LORA_PAD)
    prev = _pad_cols(jnp.concatenate([mu[i][:, None] * w for i, w in enumerate(lora)], axis=1), LORA_PAD)
    w_all = jnp.concatenate([w_in[0], keep, prev], axis=1).astype(BF16)
    proj = _inproj(x2, g_mix, w_all)

    slopes = jnp.asarray([2.0 ** (-8.0 * (i + 1) / DA_HEADS) for i in range(DA_HEADS)], F32)
    lam4 = jnp.concatenate([da_lambda_q1, da_lambda_k1, da_lambda_q2, da_lambda_k2], axis=0)
    o_da = _diffattn(proj, slopes, lam4, da_subln_g.reshape(DA_V_DIM, 1), b, s)

    w2cat = jnp.zeros((LORA_PAD, 3 * RW_WIDTH), F32)
    w2cat = w2cat.at[0:LORA_W, 0:RW_WIDTH].set(rw_w2[0])
    w2cat = w2cat.at[LORA_W:LORA_W + LORA_A, RW_WIDTH:2 * RW_WIDTH].set(rw_a2[0])
    w2cat = w2cat.at[LORA_W + LORA_A:LORA_W + LORA_A + LORA_G, 2 * RW_WIDTH:].set(rw_g2[0])
    o_rw = _rwkv(proj, rw_mu_rkv[0], rw_w0, rw_a0, w2cat.astype(BF16), rw_k_k, rw_k_a,
                 rw_r_k.reshape(1, RW_WIDTH), rw_ln_w, rw_ln_b, b, s)

    w_r = _pad_cols(jnp.concatenate([moe_w_group[0], moe_w_expert[0]], axis=1), LANES)
    wr_hi = w_r.astype(BF16)
    wr_lo = (w_r - wr_hi.astype(F32)).astype(BF16)
    b_r = _pad_cols(jnp.concatenate([moe_b_group[0], moe_b_expert[0].reshape(-1)])[None, :], LANES)
    h1, xn2, route, cnt = _outproj(x2, o_da, o_rw, w_out[0].astype(BF16), g_ffn, wr_hi, wr_lo, b_r)

    counts = cnt[0, EXPERT_LANE0:EXPERT_LANE0 + N_EXPERTS].astype(jnp.int32)
    pcounts = ((counts + EXPERT_BLOCK - 1) // EXPERT_BLOCK) * EXPERT_BLOCK
    pends = jnp.cumsum(pcounts)
    pstarts = pends - pcounts
    e_idx = route[:, ROUTE_E:ROUTE_E + 2].astype(jnp.int32)
    rank = route[:, ROUTE_R:ROUTE_R + 2].astype(jnp.int32)
    dest = (pstarts[e_idx] + rank).reshape(-1)
    n_blocks = (2 * t) // EXPERT_BLOCK + N_EXPERTS
    block_row0 = jnp.arange(n_blocks, dtype=jnp.int32) * EXPERT_BLOCK
    block_e = jnp.minimum(jnp.sum(pends[None, :] <= block_row0[:, None], axis=1), N_EXPERTS - 1).astype(jnp.int32)
    n_used = (pends[-1:] // EXPERT_BLOCK).astype(jnp.int32)

    xb = _dispatch(dest, xn2, jnp.zeros((n_blocks * EXPERT_BLOCK, d), F32))
    yb = _experts(block_e, n_used, xb, moe_w_gate[0], moe_w_up[0], moe_w_down[0])
    out = _combine(dest, route, h1, g_final[None, :], yb)
    return out.reshape(b, s, d)
```

```python
import functools
import math

import jax
import jax.numpy as jnp
from jax import lax
from jax.experimental import pallas as pl
from jax.experimental.pallas import tpu as pltpu

F32 = jnp.float32
BF16 = jnp.bfloat16

D_MODEL = 2048
DA_WIDTH = 1024
RW_WIDTH = 1024
DA_HEADS = 8
DA_V_DIM = 128
DA_QK_DIM = 64
RW_N = 64
RW_PAIRS = RW_WIDTH // 128
LORA_W, LORA_A, LORA_G = 64, 64, 160
LORA_PAD = 384
IN_COLS = 3 * DA_WIDTH + 3 * RW_WIDTH
NC = IN_COLS + 2 * LORA_PAD
N_GROUPS = 4
EXPERTS_PER_GROUP = 8
N_EXPERTS = 32
D_EXPERT = 512
NORM_EPS = 1e-6
LN_X_EPS = 64e-5
LAM_INIT = 0.8 - 0.6 * math.exp(-0.3 * 0)
LANES = 128
CHUNK = 64
EXPERT_BLOCK = 256
NEG = -1e30
VMEM_LIMIT = 48 * 1024 * 1024
EXPERTS_VMEM_LIMIT = 56 * 1024 * 1024


def _dot(a, b):
    return jnp.dot(a, b, preferred_element_type=F32)


def _dot_nt(a, b):
    return lax.dot_general(a, b, (((1,), (1,)), ((), ())), preferred_element_type=F32)


def _dot_tn(a, b):
    return lax.dot_general(a, b, (((0,), (0,)), ((), ())), preferred_element_type=F32)


def _sigmoid(x):
    return 1.0 / (1.0 + jnp.exp(-x))


HALF = D_MODEL // 2


def _pack_halves(xb):
    lo = lax.bitcast_convert_type(xb[:, :HALF].astype(F32), jnp.uint32) >> 16
    hi = lax.bitcast_convert_type(xb[:, HALF:].astype(F32), jnp.uint32) & jnp.uint32(0xFFFF0000)
    return hi | lo


def _unpack_halves(u):
    lo = lax.bitcast_convert_type(u << 16, F32)
    hi = lax.bitcast_convert_type(u & jnp.uint32(0xFFFF0000), F32)
    return lo, hi


def _inproj_kernel(x_ref, g_ref, w_ref, o_ref, xn_ref):
    @pl.when(pl.program_id(1) == 0)
    def _():
        x = x_ref[...]
        ms = jnp.mean(x * x, axis=-1, keepdims=True)
        xn_ref[...] = (x * lax.rsqrt(ms + NORM_EPS) * g_ref[...]).astype(BF16)

    o_ref[...] = _dot(xn_ref[...], w_ref[...]).astype(o_ref.dtype)


def _inproj(x2, g, w_all, tm=512, tn=1152):
    t = x2.shape[0]
    return pl.pallas_call(
        _inproj_kernel,
        grid=(t // tm, NC // tn),
        in_specs=[
            pl.BlockSpec((tm, D_MODEL), lambda i, j: (i, 0)),
            pl.BlockSpec((1, D_MODEL), lambda i, j: (0, 0)),
            pl.BlockSpec((D_MODEL, tn), lambda i, j: (0, j)),
        ],
        out_specs=pl.BlockSpec((tm, tn), lambda i, j: (i, j)),
        out_shape=jax.ShapeDtypeStruct((t, NC), BF16),
        scratch_shapes=[pltpu.VMEM((tm, D_MODEL), BF16)],
        compiler_params=pltpu.CompilerParams(
            dimension_semantics=("parallel", "arbitrary"), vmem_limit_bytes=VMEM_LIMIT),
        name="inproj",
    )(x2, g, w_all)


def _da_kernel(slopes_ref, lam_ref, subg_ref, q_ref, k_ref, v_ref, o_ref, vt_ref, m_ref, l_ref, acc_ref,
               s_ref, p_ref, alpha_ref, *, tq, nsub, seq):
    tk = tq
    h = pl.program_id(1)
    qi = pl.program_id(2)
    slope = slopes_ref[h]

    @pl.when(qi == 0)
    def _():
        for blk in range(seq // LANES):
            rows = slice(blk * LANES, (blk + 1) * LANES)
            vt_ref[:, rows] = v_ref[rows, :].astype(F32).T.astype(BF16)

    q = q_ref[...] * jnp.asarray(DA_QK_DIM ** -0.5, BF16)
    lane = lax.broadcasted_iota(jnp.int32, (tq, LANES), 1)
    zero = jnp.zeros_like(q)
    qc = (jnp.where(lane < DA_QK_DIM, q, zero), jnp.where(lane >= DA_QK_DIM, q, zero))
    sub = tk // nsub
    krow = lax.broadcasted_iota(jnp.int32, (sub, tq), 0)
    qcol = lax.broadcasted_iota(jnp.int32, (sub, tq), 1)
    bias = [slope * (krow + u * sub).astype(F32) for u in range(nsub)]

    m_ref[...] = jnp.full_like(m_ref, NEG)
    l_ref[...] = jnp.zeros_like(l_ref)
    acc_ref[...] = jnp.zeros_like(acc_ref)
    p_ref[...] = jnp.zeros_like(p_ref)
    alpha_ref[...] = jnp.ones_like(alpha_ref)
    cu = [(c, u) for c in range(2) for u in range(nsub)]
    rows = lambda u: slice(u * sub, (u + 1) * sub)

    def scores_to_scratch(j):
        ks = pl.multiple_of(j * tk, tk)
        for c, u in cu:
            s_ref[c, rows(u), :] = _dot_nt(k_ref[pl.ds(ks + u * sub, sub), :], qc[c]) + bias[u]

    def values_from_scratch(j):
        ks = pl.multiple_of(j * tk, tk)
        vt = vt_ref[:, pl.ds(ks, tk)]
        for c in range(2):
            acc_ref[c] = alpha_ref[c:c + 1, :] * acc_ref[c] + _dot(vt, p_ref[c])

    def softmax(j, masked):
        rel = (j - qi) * tk
        off = slope * rel.astype(F32)
        s = {(c, u): s_ref[c, rows(u), :] for c, u in cu}
        if masked:
            s = {(c, u): jnp.where(krow + (rel + u * sub) <= qcol, s[c, u], NEG) for c, u in cu}
        m = [m_ref[c:c + 1, :] for c in range(2)]
        smax = [functools.reduce(jnp.maximum, [s[c, u] for u in range(nsub)]) for c in range(2)]
        m_new = [jnp.maximum(m[c], jnp.max(smax[c], axis=0, keepdims=True) + off) for c in range(2)]
        p = {(c, u): jnp.exp(s[c, u] - (m_new[c] - off)) for c, u in cu}
        psum = [functools.reduce(jnp.add, [p[c, u] for u in range(nsub)]) for c in range(2)]
        alpha = [jnp.exp(m[c] - m_new[c]) for c in range(2)]
        for c in range(2):
            m_ref[c:c + 1, :] = m_new[c]
            l_ref[c:c + 1, :] = alpha[c] * l_ref[c:c + 1, :] + jnp.sum(psum[c], axis=0, keepdims=True)
        return p, alpha

    def publish(p, alpha):
        for c in range(2):
            alpha_ref[c:c + 1, :] = alpha[c]
        for c, u in cu:
            p_ref[c, rows(u), :] = p[c, u].astype(BF16)

    scores_to_scratch(0)

    def body(j, _):
        p, alpha = softmax(j, False)
        values_from_scratch(jnp.maximum(j - 1, 0))
        scores_to_scratch(j + 1)
        publish(p, alpha)
        return 0

    lax.fori_loop(0, qi, body, 0)
    p, alpha = softmax(qi, True)
    values_from_scratch(jnp.maximum(qi - 1, 0))
    publish(p, alpha)
    values_from_scratch(qi)

    lv = lam_ref[...]
    lam = (jnp.exp(jnp.sum(lv[0:1] * lv[1:2], axis=-1, keepdims=True))
           - jnp.exp(jnp.sum(lv[2:3] * lv[3:4], axis=-1, keepdims=True)) + LAM_INIT)
    o = acc_ref[0] / l_ref[0:1, :] - lam * (acc_ref[1] / l_ref[1:2, :])
    ms = jnp.mean(o * o, axis=0, keepdims=True)
    o = o * lax.rsqrt(ms + NORM_EPS) * subg_ref[...] * (1.0 - LAM_INIT)
    o_ref[...] = o.T.astype(o_ref.dtype)


def _diffattn(proj, slopes, lam4, subg_col, b, s, tq=256, nsub=2):
    nq = s // tq
    return pl.pallas_call(
        functools.partial(_da_kernel, tq=tq, nsub=nsub, seq=s),
        grid=(b, DA_HEADS, nq),
        in_specs=[
            pl.BlockSpec(memory_space=pltpu.SMEM),
            pl.BlockSpec((4, DA_QK_DIM), lambda bi, h, qi: (0, 0)),
            pl.BlockSpec((DA_V_DIM, 1), lambda bi, h, qi: (0, 0)),
            pl.BlockSpec((tq, LANES), lambda bi, h, qi: (bi * nq + qi, h)),
            pl.BlockSpec((s, LANES), lambda bi, h, qi: (bi, DA_HEADS + h)),
            pl.BlockSpec((s, LANES), lambda bi, h, qi: (bi, 2 * DA_HEADS + h)),
        ],
        out_specs=pl.BlockSpec((tq, LANES), lambda bi, h, qi: (bi * nq + qi, h)),
        out_shape=jax.ShapeDtypeStruct((b * s, DA_WIDTH), BF16),
        scratch_shapes=[
            pltpu.VMEM((DA_V_DIM, s), BF16),
            pltpu.VMEM((2, tq), F32),
            pltpu.VMEM((2, tq), F32),
            pltpu.VMEM((2, DA_V_DIM, tq), F32),
            pltpu.VMEM((2, tq, tq), F32),
            pltpu.VMEM((2, tq, tq), BF16),
            pltpu.VMEM((2, tq), F32),
        ],
        compiler_params=pltpu.CompilerParams(
            dimension_semantics=("parallel", "parallel", "arbitrary"), vmem_limit_bytes=VMEM_LIMIT),
        name="diffattn",
    )(slopes, lam4, subg_col, proj, proj, proj)


def _rwkv_kernel(r_ref, k_ref, v_ref, p1_ref, p2_ref, mu_ref, w0_ref, a0_ref, w2_ref, kk_ref, ka_ref,
                 rk_ref, lnw_ref, lnb_ref, o_ref, state_ref, crkv_ref, cp2_ref):
    L = CHUNK
    c = pl.program_id(1)

    @pl.when(c == 0)
    def _():
        state_ref[...] = jnp.zeros_like(state_ref)
        crkv_ref[...] = jnp.zeros_like(crkv_ref)
        cp2_ref[...] = jnp.zeros_like(cp2_ref)

    first = lax.broadcasted_iota(jnp.int32, (L, 1), 0) == 0

    def shift(x, carry):
        return jnp.where(first, carry, pltpu.roll(x, 1, 0))

    r_raw = r_ref[...].astype(F32)
    k_raw = k_ref[...].astype(F32)
    v_raw = v_ref[...].astype(F32)
    p2 = p2_ref[...].astype(F32)
    rs = shift(r_raw, crkv_ref[:, 0:RW_WIDTH])
    ks = shift(k_raw, crkv_ref[:, RW_WIDTH:2 * RW_WIDTH])
    vs = shift(v_raw, crkv_ref[:, 2 * RW_WIDTH:3 * RW_WIDTH])
    lin = p1_ref[...].astype(F32) + shift(p2, cp2_ref[...])
    crkv_ref[:, 0:RW_WIDTH] = r_raw[L - 1:L, :]
    crkv_ref[:, RW_WIDTH:2 * RW_WIDTH] = k_raw[L - 1:L, :]
    crkv_ref[:, 2 * RW_WIDTH:3 * RW_WIDTH] = v_raw[L - 1:L, :]
    cp2_ref[...] = p2[L - 1:L, :]

    r = r_raw + (rs - r_raw) * mu_ref[0:1, :]
    k = k_raw + (ks - k_raw) * mu_ref[1:2, :]
    v = v_raw + (vs - v_raw) * mu_ref[2:3, :]

    ll = lax.broadcasted_iota(jnp.int32, (L, LORA_PAD), 1)
    z = jnp.where(ll < LORA_W, jnp.tanh(lin),
                  jnp.where(ll < LORA_W + LORA_A, lin,
                            jnp.where(ll < LORA_W + LORA_A + LORA_G, _sigmoid(lin), 0.0)))
    d = _dot(z.astype(BF16), w2_ref[...])
    logw = -_sigmoid(w0_ref[...] + d[:, 0:RW_WIDTH]) * math.exp(-0.5)
    a_sig = _sigmoid(a0_ref[...] + d[:, RW_WIDTH:2 * RW_WIDTH])
    g = d[:, 2 * RW_WIDTH:3 * RW_WIDTH]

    li = lax.broadcasted_iota(jnp.int32, (LANES, LANES), 0)
    lj = lax.broadcasted_iota(jnp.int32, (LANES, LANES), 1)
    same_head = jnp.where((li // RW_N) == (lj // RW_N), 1.0, 0.0).astype(BF16)

    def head_sum(x):
        xb = x.astype(BF16)
        return jnp.concatenate(
            [_dot(xb[:, p * LANES:(p + 1) * LANES], same_head) for p in range(RW_PAIRS)], axis=1)

    kk = k * kk_ref[...]
    kk = kk * lax.rsqrt(jnp.maximum(head_sum(kk * kk), 1e-24))
    k = k * (1.0 + (a_sig - 1.0) * ka_ref[...])
    bonus = head_sum(r * k * rk_ref[...])

    ti = lax.broadcasted_iota(jnp.int32, (L, L), 0)
    tj = lax.broadcasted_iota(jnp.int32, (L, L), 1)
    tri = jnp.where(ti >= tj, 1.0, 0.0).astype(BF16)
    w_hi = logw.astype(BF16)
    w_r1 = logw - w_hi.astype(F32)
    w_mid = w_r1.astype(BF16)
    w_lo = (w_r1 - w_mid.astype(F32)).astype(BF16)
    cw = _dot(tri, w_hi) + _dot(tri, w_mid) + _dot(tri, w_lo)
    cw_last = cw[L - 1:L, :]
    e_in = jnp.exp(cw)
    e_out = jnp.exp(-cw)
    e_end = jnp.exp(cw_last - cw)
    w_end = jnp.exp(cw_last)

    at = -kk * jnp.exp(cw - logw)
    rt = r * e_in
    b = kk * a_sig
    bt = b * e_out
    kt = k * e_out
    bd = b * e_end
    kd = k * e_end

    lane = lax.broadcasted_iota(jnp.int32, (L, LANES), 1)
    lo = lane < RW_N

    def stack(x, p):
        xs = x[:, p * LANES:(p + 1) * LANES]
        return jnp.concatenate([jnp.where(lo, xs, 0.0), jnp.where(lo, 0.0, xs)], axis=0).astype(BF16)

    si = lax.broadcasted_iota(jnp.int32, (2 * L, 2 * L), 0)
    sj = lax.broadcasted_iota(jnp.int32, (2 * L, 2 * L), 1)
    strict = (si % L) > (sj % L)
    incl = (si % L) >= (sj % L)
    eye = jnp.where(si == sj, 1.0, 0.0)

    pairs = range(RW_PAIRS)
    cat0 = lambda *xs: jnp.concatenate(xs, axis=0)
    cat1 = lambda *xs: jnp.concatenate(xs, axis=1)
    bf = lambda x: x.astype(BF16)
    a_s = [stack(at, p) for p in pairs]
    r_s = [stack(rt, p) for p in pairs]
    v_s = [stack(v, p) for p in pairs]
    bk = [cat0(stack(bt, p), stack(kt, p)) for p in pairs]
    bkd = [cat0(stack(bd, p), stack(kd, p)) for p in pairs]
    aa = [_dot_nt(a_s[p], bk[p]) for p in pairs]
    rr = [_dot_nt(r_s[p], bk[p]) for p in pairs]
    a_ab = [jnp.where(strict, aa[p][:, 0:2 * L], 0.0) for p in pairs]
    a_ak = [bf(jnp.where(strict, aa[p][:, 2 * L:4 * L], 0.0)) for p in pairs]
    a_rb = [bf(jnp.where(incl, rr[p][:, 0:2 * L], 0.0)) for p in pairs]
    a_rk = [bf(jnp.where(incl, rr[p][:, 2 * L:4 * L], 0.0)) for p in pairs]
    ab = [bf(a) for a in a_ab]
    qpow = [_dot(ab[p], ab[p]) for p in pairs]
    inv = [eye + a_ab[p] for p in pairs]
    for _ in range(int(math.log2(L)) - 2):
        qb = [bf(q) for q in qpow]
        res = [_dot(qb[p], cat1(qb[p], bf(inv[p]))) for p in pairs]
        qpow = [res[p][:, 0:2 * L] for p in pairs]
        inv = [inv[p] + res[p][:, 2 * L:4 * L] for p in pairs]
    inv = [bf(inv[p] + _dot(bf(qpow[p]), bf(inv[p]))) for p in pairs]
    h_old = [state_ref[p] for p in pairs]
    h_b = [bf(h) for h in h_old]
    x = [_dot(cat1(a_s[p], a_ak[p]), cat0(h_b[p], v_s[p])) for p in pairs]
    u_b = [bf(_dot(inv[p], bf(x[p]))) for p in pairs]
    y2 = [_dot(cat1(r_s[p], a_rb[p], a_rk[p]), cat0(h_b[p], u_b[p], v_s[p])) for p in pairs]
    for p in pairs:
        w_col = jnp.broadcast_to(w_end[:, p * LANES:(p + 1) * LANES], (LANES, LANES)).T
        state_ref[p] = w_col * h_old[p] + _dot_tn(bkd[p], cat0(u_b[p], v_s[p]))
    y = cat1(*[y2[p][0:L] + y2[p][L:2 * L] for p in pairs])

    mean = head_sum(y) * (1.0 / RW_N)
    yc = y - mean
    var = head_sum(yc * yc) * (1.0 / RW_N)
    y = yc * lax.rsqrt(var + LN_X_EPS) * lnw_ref[...] + lnb_ref[...]
    y = y + bonus * v
    o_ref[...] = (y * g).astype(o_ref.dtype)


def _rwkv(proj, mu_rkv, w0, a0, w2cat, k_k, k_a, r_k, ln_w, ln_b, b, s):
    L = CHUNK
    nc = s // L
    row = lambda n: pl.BlockSpec((1, n), lambda bi, c: (0, 0))
    rkv_col0 = 3 * DA_WIDTH // RW_WIDTH
    lora_col0 = IN_COLS // LORA_PAD
    return pl.pallas_call(
        _rwkv_kernel,
        grid=(b, nc),
        in_specs=[
            pl.BlockSpec((L, RW_WIDTH), lambda bi, c: (bi * nc + c, rkv_col0)),
            pl.BlockSpec((L, RW_WIDTH), lambda bi, c: (bi * nc + c, rkv_col0 + 1)),
            pl.BlockSpec((L, RW_WIDTH), lambda bi, c: (bi * nc + c, rkv_col0 + 2)),
            pl.BlockSpec((L, LORA_PAD), lambda bi, c: (bi * nc + c, lora_col0)),
            pl.BlockSpec((L, LORA_PAD), lambda bi, c: (bi * nc + c, lora_col0 + 1)),
            pl.BlockSpec((3, RW_WIDTH), lambda bi, c: (0, 0)),
            row(RW_WIDTH), row(RW_WIDTH),
            pl.BlockSpec((LORA_PAD, 3 * RW_WIDTH), lambda bi, c: (0, 0)),
            row(RW_WIDTH), row(RW_WIDTH), row(RW_WIDTH), row(RW_WIDTH), row(RW_WIDTH),
        ],
        out_specs=pl.BlockSpec((L, RW_WIDTH), lambda bi, c: (bi * nc + c, 0)),
        out_shape=jax.ShapeDtypeStruct((b * s, RW_WIDTH), BF16),
        scratch_shapes=[
            pltpu.VMEM((RW_PAIRS, LANES, LANES), F32),
            pltpu.VMEM((1, 3 * RW_WIDTH), F32),
            pltpu.VMEM((1, LORA_PAD), F32),
        ],
        compiler_params=pltpu.CompilerParams(
            dimension_semantics=("parallel", "arbitrary"), vmem_limit_bytes=VMEM_LIMIT),
        name="rwkv",
    )(proj, proj, proj, proj, proj, mu_rkv, w0, a0, w2cat, k_k, k_a, r_k, ln_w, ln_b)


ROUTE_E, ROUTE_G, ROUTE_R = 0, 2, 4
EXPERT_LANE0 = N_GROUPS


def _outproj_kernel(x_ref, oda_ref, orw_ref, wout_ref, g_ref, wrh_ref, wrl_ref, br_ref,
                    h_ref, xn_ref, route_ref, cnt_ref, base_ref, *, tm):
    @pl.when(pl.program_id(0) == 0)
    def _():
        base_ref[...] = jnp.zeros_like(base_ref)

    mix = _dot(oda_ref[...], wout_ref[0:DA_WIDTH, :]) + _dot(orw_ref[...], wout_ref[DA_WIDTH:, :])
    h = x_ref[...] + mix
    h_ref[...] = h
    ms = jnp.mean(h * h, axis=-1, keepdims=True)
    xn = h * lax.rsqrt(ms + NORM_EPS) * g_ref[...]
    x_hi = xn.astype(BF16)
    xn_ref[...] = _pack_halves(x_hi)

    x_lo = (xn - x_hi.astype(F32)).astype(BF16)
    lg = (_dot(x_hi, wrh_ref[...]) + _dot(x_lo, wrh_ref[...]) + _dot(x_hi, wrl_ref[...])) + br_ref[...]

    lane = lax.broadcasted_iota(jnp.int32, (tm, LANES), 1)
    big = jnp.int32(1 << 20)
    gl = jnp.where(lane < N_GROUPS, lg, NEG)
    gmax = jnp.max(gl, axis=-1, keepdims=True)
    grp = jnp.min(jnp.where(gl == gmax, lane, big), axis=-1, keepdims=True)
    p_grp = 1.0 / jnp.sum(jnp.exp(gl - gmax), axis=-1, keepdims=True)
    eg = jnp.where(lane >= EXPERT_LANE0, (lane - EXPERT_LANE0) // EXPERTS_PER_GROUP, -1)
    el = jnp.where(eg == grp, lg, NEG)
    emax = jnp.max(el, axis=-1, keepdims=True)
    pe = jnp.exp(el - emax)
    probs = pe / jnp.sum(pe, axis=-1, keepdims=True)
    probs = jnp.where(eg == grp, probs, -1.0)
    p1 = jnp.max(probs, axis=-1, keepdims=True)
    i1 = jnp.min(jnp.where(probs == p1, lane, big), axis=-1, keepdims=True)
    probs2 = jnp.where(lane == i1, -1.0, probs)
    p2 = jnp.max(probs2, axis=-1, keepdims=True)
    i2 = jnp.min(jnp.where(probs2 == p2, lane, big), axis=-1, keepdims=True)
    gate1 = p_grp * p1 / (p1 + p2)
    gate2 = p_grp * p2 / (p1 + p2)

    oh1 = jnp.where(lane == i1, 1.0, 0.0)
    oh2 = jnp.where(lane == i2, 1.0, 0.0)
    ri = lax.broadcasted_iota(jnp.int32, (tm, tm), 0)
    rj = lax.broadcasted_iota(jnp.int32, (tm, tm), 1)
    lower = jnp.where(ri > rj, 1.0, 0.0).astype(BF16)
    base = base_ref[...]
    tot1 = jnp.sum(oh1, axis=0, keepdims=True)
    c1 = base + _dot(lower, oh1.astype(BF16))
    c2 = base + tot1 + _dot(lower, oh2.astype(BF16))
    rank1 = jnp.sum(oh1 * c1, axis=-1, keepdims=True)
    rank2 = jnp.sum(oh2 * c2, axis=-1, keepdims=True)
    base = base + tot1 + jnp.sum(oh2, axis=0, keepdims=True)
    base_ref[...] = base
    cnt_ref[...] = base

    e1 = (i1 - EXPERT_LANE0).astype(F32)
    e2 = (i2 - EXPERT_LANE0).astype(F32)
    rec = jnp.zeros((tm, LANES), F32)
    for ln, val in ((ROUTE_E, e1), (ROUTE_E + 1, e2), (ROUTE_G, gate1), (ROUTE_G + 1, gate2),
                    (ROUTE_R, rank1), (ROUTE_R + 1, rank2)):
        rec = jnp.where(lane == ln, val, rec)
    route_ref[...] = rec


def _outproj(x2, o_da, o_rw, w_out, g_ffn, wr_hi, wr_lo, b_r, tm=256):
    t = x2.shape[0]
    const = lambda shape: pl.BlockSpec(shape, lambda i: (0, 0))
    return pl.pallas_call(
        functools.partial(_outproj_kernel, tm=tm),
        grid=(t // tm,),
        in_specs=[
            pl.BlockSpec((tm, D_MODEL), lambda i: (i, 0)),
            pl.BlockSpec((tm, DA_WIDTH), lambda i: (i, 0)),
            pl.BlockSpec((tm, RW_WIDTH), lambda i: (i, 0)),
            const((D_MODEL, D_MODEL)),
            const((1, D_MODEL)),
            const((D_MODEL, LANES)),
            const((D_MODEL, LANES)),
            const((1, LANES)),
        ],
        out_specs=[
            pl.BlockSpec((tm, D_MODEL), lambda i: (i, 0)),
            pl.BlockSpec((tm, HALF), lambda i: (i, 0)),
            pl.BlockSpec((tm, LANES), lambda i: (i, 0)),
            const((1, LANES)),
        ],
        out_shape=[
            jax.ShapeDtypeStruct((t, D_MODEL), F32),
            jax.ShapeDtypeStruct((t, HALF), jnp.uint32),
            jax.ShapeDtypeStruct((t, LANES), F32),
            jax.ShapeDtypeStruct((1, LANES), F32),
        ],
        scratch_shapes=[pltpu.VMEM((1, LANES), F32)],
        compiler_params=pltpu.CompilerParams(
            dimension_semantics=("arbitrary",), vmem_limit_bytes=VMEM_LIMIT),
        name="outproj_router",
    )(x2, o_da, o_rw, w_out, g_ffn, wr_hi, wr_lo, b_r)


def _row_copy(src_hbm, dst_ref, src_row, dst_row, sem):
    return pltpu.make_async_copy(src_hbm.at[pl.ds(src_row, 1)], dst_ref.at[pl.ds(dst_row, 1)], sem)


def _dispatch_kernel(dest_ref, x_ref, xb_in_hbm, xb_hbm, sem, *, td):
    del xb_in_hbm
    t0 = pl.program_id(0) * td

    def start(i, _):
        tok = t0 + i
        _row_copy(x_ref, xb_hbm, i, dest_ref[2 * tok], sem).start()
        _row_copy(x_ref, xb_hbm, i, dest_ref[2 * tok + 1], sem).start()
        return 0

    lax.fori_loop(0, td, start, 0, unroll=8)
    for _ in range(2):
        pltpu.make_async_copy(x_ref, xb_hbm.at[pl.ds(0, td)], sem).wait()


def _dispatch(dest_flat, xn2, xb_zero, td=256):
    t = xn2.shape[0]
    return pl.pallas_call(
        functools.partial(_dispatch_kernel, td=td),
        grid_spec=pltpu.PrefetchScalarGridSpec(
            num_scalar_prefetch=1,
            grid=(t // td,),
            in_specs=[pl.BlockSpec((td, HALF), lambda i, d: (i, 0)), pl.BlockSpec(memory_space=pl.ANY)],
            out_specs=pl.BlockSpec(memory_space=pl.ANY),
            scratch_shapes=[pltpu.SemaphoreType.DMA],
        ),
        out_shape=jax.ShapeDtypeStruct(xb_zero.shape, xb_zero.dtype),
        input_output_aliases={2: 0},
        compiler_params=pltpu.CompilerParams(dimension_semantics=("arbitrary",)),
        name="dispatch",
    )(dest_flat, xn2, xb_zero)


def _experts_kernel(be_ref, nu_ref, xb_ref, wg_ref, wu_ref, wd_ref, yb_ref, wgb_ref, wub_ref, wdb_ref):
    i = pl.program_id(0)
    used = i < nu_ref[0]
    new_expert = jnp.logical_or(i == 0, be_ref[i] != be_ref[jnp.maximum(i - 1, 0)])

    @pl.when(jnp.logical_and(used, new_expert))
    def _():
        wgb_ref[...] = wg_ref[...].astype(BF16)
        wub_ref[...] = wu_ref[...].astype(BF16)
        wdb_ref[...] = wd_ref[...].astype(BF16)

    @pl.when(used)
    def _():
        x_lo, x_hi = (half.astype(BF16) for half in _unpack_halves(xb_ref[...]))
        hg = _dot(x_lo, wgb_ref[0:HALF, :]) + _dot(x_hi, wgb_ref[HALF:, :])
        hu = _dot(x_lo, wub_ref[0:HALF, :]) + _dot(x_hi, wub_ref[HALF:, :])
        hid = hg * _sigmoid(hg) * hu
        yb_ref[...] = _pack_halves(_dot(hid.astype(BF16), wdb_ref[...]).astype(BF16))

    @pl.when(jnp.logical_not(used))
    def _():
        yb_ref[...] = jnp.zeros_like(yb_ref)


def _experts(block_e, n_used, xb, w_gate, w_up, w_down):
    nb = xb.shape[0] // EXPERT_BLOCK
    rows = lambda i, be, nu: (jnp.minimum(i, nu[0] - 1), 0)
    return pl.pallas_call(
        _experts_kernel,
        grid_spec=pltpu.PrefetchScalarGridSpec(
            num_scalar_prefetch=2,
            grid=(nb,),
            in_specs=[
                pl.BlockSpec((EXPERT_BLOCK, HALF), rows),
                pl.BlockSpec((None, D_MODEL, D_EXPERT), lambda i, be, nu: (be[i], 0, 0)),
                pl.BlockSpec((None, D_MODEL, D_EXPERT), lambda i, be, nu: (be[i], 0, 0)),
                pl.BlockSpec((None, D_EXPERT, D_MODEL), lambda i, be, nu: (be[i], 0, 0)),
            ],
            out_specs=pl.BlockSpec((EXPERT_BLOCK, HALF), lambda i, be, nu: (i, 0)),
            scratch_shapes=[
                pltpu.VMEM((D_MODEL, D_EXPERT), BF16),
                pltpu.VMEM((D_MODEL, D_EXPERT), BF16),
                pltpu.VMEM((D_EXPERT, D_MODEL), BF16),
            ],
        ),
        out_shape=jax.ShapeDtypeStruct(xb.shape, jnp.uint32),
        compiler_params=pltpu.CompilerParams(
            dimension_semantics=("arbitrary",), vmem_limit_bytes=EXPERTS_VMEM_LIMIT),
        name="experts",
    )(block_e, n_used, xb, w_gate, w_up, w_down)


def _combine_kernel(dest_ref, route_ref, h_ref, g_ref, yb_hbm, o_ref, y1_ref, y2_ref, sem, *, tc):
    t0 = pl.program_id(0) * tc

    def start(i, _):
        tok = t0 + i
        _row_copy(yb_hbm, y1_ref, dest_ref[2 * tok], i, sem).start()
        _row_copy(yb_hbm, y2_ref, dest_ref[2 * tok + 1], i, sem).start()
        return 0

    lax.fori_loop(0, tc, start, 0, unroll=8)
    pltpu.make_async_copy(yb_hbm.at[pl.ds(0, tc)], y1_ref, sem).wait()
    pltpu.make_async_copy(yb_hbm.at[pl.ds(0, tc)], y2_ref, sem).wait()

    rec = route_ref[...]
    lane = lax.broadcasted_iota(jnp.int32, (tc, LANES), 1)
    g1 = jnp.sum(jnp.where(lane == ROUTE_G, rec, 0.0), axis=-1, keepdims=True)
    g2 = jnp.sum(jnp.where(lane == ROUTE_G + 1, rec, 0.0), axis=-1, keepdims=True)
    y1_lo, y1_hi = _unpack_halves(y1_ref[...])
    y2_lo, y2_hi = _unpack_halves(y2_ref[...])
    h_lo = h_ref[:, 0:HALF] + (y1_lo * g1 + y2_lo * g2)
    h_hi = h_ref[:, HALF:] + (y1_hi * g1 + y2_hi * g2)
    ms = (jnp.sum(h_lo * h_lo, axis=-1, keepdims=True)
          + jnp.sum(h_hi * h_hi, axis=-1, keepdims=True)) * (1.0 / D_MODEL)
    scale = lax.rsqrt(ms + NORM_EPS)
    o_ref[:, 0:HALF] = h_lo * scale * g_ref[:, 0:HALF]
    o_ref[:, HALF:] = h_hi * scale * g_ref[:, HALF:]


def _combine(dest_flat, route, h1, g_final, yb, tc=256):
    t = h1.shape[0]
    return pl.pallas_call(
        functools.partial(_combine_kernel, tc=tc),
        grid_spec=pltpu.PrefetchScalarGridSpec(
            num_scalar_prefetch=1,
            grid=(t // tc,),
            in_specs=[
                pl.BlockSpec((tc, LANES), lambda i, d: (i, 0)),
                pl.BlockSpec((tc, D_MODEL), lambda i, d: (i, 0)),
                pl.BlockSpec((1, D_MODEL), lambda i, d: (0, 0)),
                pl.BlockSpec(memory_space=pl.ANY),
            ],
            out_specs=pl.BlockSpec((tc, D_MODEL), lambda i, d: (i, 0)),
            scratch_shapes=[
                pltpu.VMEM((tc, HALF), jnp.uint32),
                pltpu.VMEM((tc, HALF), jnp.uint32),
                pltpu.SemaphoreType.DMA,
            ],
        ),
        out_shape=jax.ShapeDtypeStruct((t, D_MODEL), F32),
        compiler_params=pltpu.CompilerParams(
            dimension_semantics=("arbitrary",), vmem_limit_bytes=VMEM_LIMIT),
        name="combine",
    )(dest_flat, route, h1, g_final, yb)


def _pad_cols(w, n):
    return jnp.pad(w, ((0, 0), (0, n - w.shape[1])))


def kernel(x, g_mix, w_in, w_out, da_lambda_q1, da_lambda_k1, da_lambda_q2, da_lambda_k2, da_subln_g,
           rw_mu_x, rw_mu_rkv, rw_w0, rw_w1, rw_w2, rw_a0, rw_a1, rw_a2, rw_g1, rw_g2, rw_k_k, rw_k_a,
           rw_r_k, rw_ln_w, rw_ln_b, g_ffn, moe_w_group, moe_b_group, moe_w_expert, moe_b_expert,
           moe_w_gate, moe_w_up, moe_w_down, g_final):
    b, s, d = x.shape
    t = b * s
    x2 = x.reshape(t, d)

    mu = rw_mu_x[0]
    lora = (rw_w1[0], rw_a1[0], rw_g1[0])
    keep = _pad_cols(jnp.concatenate([(1.0 - mu[i])[:, None] * w for i, w in enumerate(lora)], axis=1), LORA_PAD)
    prev = _pad_cols(jnp.concatenate([mu[i][:, None] * w for i, w in enumerate(lora)], axis=1), LORA_PAD)
    w_all = jnp.concatenate([w_in[0], keep, prev], axis=1).astype(BF16)
    proj = _inproj(x2, g_mix, w_all)

    slopes = jnp.asarray([2.0 ** (-8.0 * (i + 1) / DA_HEADS) for i in range(DA_HEADS)], F32)
    lam4 = jnp.concatenate([da_lambda_q1, da_lambda_k1, da_lambda_q2, da_lambda_k2], axis=0)
    o_da = _diffattn(proj, slopes, lam4, da_subln_g.reshape(DA_V_DIM, 1), b, s)

    w2cat = jnp.zeros((LORA_PAD, 3 * RW_WIDTH), F32)
    w2cat = w2cat.at[0:LORA_W, 0:RW_WIDTH].set(rw_w2[0])
    w2cat = w2cat.at[LORA_W:LORA_W + LORA_A, RW_WIDTH:2 * RW_WIDTH].set(rw_a2[0])
    w2cat = w2cat.at[LORA_W + LORA_A:LORA_W + LORA_A + LORA_G, 2 * RW_WIDTH:].set(rw_g2[0])
    o_rw = _rwkv(proj, rw_mu_rkv[0], rw_w0, rw_a0, w2cat.astype(BF16), rw_k_k, rw_k_a,
                 rw_r_k.reshape(1, RW_WIDTH), rw_ln_w, rw_ln_b, b, s)

    w_r = _pad_cols(jnp.concatenate([moe_w_group[0], moe_w_expert[0]], axis=1), LANES)
    wr_hi = w_r.astype(BF16)
    wr_lo = (w_r - wr_hi.astype(F32)).astype(BF16)
    b_r = _pad_cols(jnp.concatenate([moe_b_group[0], moe_b_expert[0].reshape(-1)])[None, :], LANES)
    h1, xn2, route, cnt = _outproj(x2, o_da, o_rw, w_out[0].astype(BF16), g_ffn, wr_hi, wr_lo, b_r)

    counts = cnt[0, EXPERT_LANE0:EXPERT_LANE0 + N_EXPERTS].astype(jnp.int32)
    pcounts = ((counts + EXPERT_BLOCK - 1) // EXPERT_BLOCK) * EXPERT_BLOCK
    pends = jnp.cumsum(pcounts)
    pstarts = pends - pcounts
    e_idx = route[:, ROUTE_E:ROUTE_E + 2].astype(jnp.int32)
    rank = route[:, ROUTE_R:ROUTE_R + 2].astype(jnp.int32)
    dest = (pstarts[e_idx] + rank).reshape(-1)
    n_blocks = (2 * t) // EXPERT_BLOCK + N_EXPERTS
    block_row0 = jnp.arange(n_blocks, dtype=jnp.int32) * EXPERT_BLOCK
    block_e = jnp.minimum(jnp.sum(pends[None, :] <= block_row0[:, None], axis=1), N_EXPERTS - 1).astype(jnp.int32)
    n_used = (pends[-1:] // EXPERT_BLOCK).astype(jnp.int32)

    xb = _dispatch(dest, xn2, jnp.zeros((n_blocks * EXPERT_BLOCK, HALF), jnp.uint32))
    yb = _experts(block_e, n_used, xb, moe_w_gate[0], moe_w_up[0], moe_w_down[0])
    out = _combine(dest, route, h1, g_final[None, :], yb)
    return out.reshape(b, s, d)
```

```python
import functools
import math

import jax
import jax.numpy as jnp
from jax import lax
from jax.experimental import pallas as pl
from jax.experimental.pallas import tpu as pltpu

F32 = jnp.float32
BF16 = jnp.bfloat16

D_MODEL = 2048
DA_WIDTH = 1024
RW_WIDTH = 1024
DA_HEADS = 8
DA_V_DIM = 128
DA_QK_DIM = 64
RW_N = 64
RW_PAIRS = RW_WIDTH // 128
LORA_W, LORA_A, LORA_G = 64, 64, 160
LORA_PAD = 384
IN_COLS = 3 * DA_WIDTH + 3 * RW_WIDTH
NC = IN_COLS + 2 * LORA_PAD
N_GROUPS = 4
EXPERTS_PER_GROUP = 8
N_EXPERTS = 32
D_EXPERT = 512
NORM_EPS = 1e-6
LN_X_EPS = 64e-5
LAM_INIT = 0.8 - 0.6 * math.exp(-0.3 * 0)
LANES = 128
CHUNK = 64
EXPERT_BLOCK = 256
NEG = -1e30
VMEM_LIMIT = 48 * 1024 * 1024
EXPERTS_VMEM_LIMIT = 56 * 1024 * 1024


def _dot(a, b):
    return jnp.dot(a, b, preferred_element_type=F32)


def _dot_nt(a, b):
    return lax.dot_general(a, b, (((1,), (1,)), ((), ())), preferred_element_type=F32)


def _dot_tn(a, b):
    return lax.dot_general(a, b, (((0,), (0,)), ((), ())), preferred_element_type=F32)


def _sigmoid(x):
    return 1.0 / (1.0 + jnp.exp(-x))


HALF = D_MODEL // 2


def _pack_halves(xb):
    lo = lax.bitcast_convert_type(xb[:, :HALF].astype(F32), jnp.uint32) >> 16
    hi = lax.bitcast_convert_type(xb[:, HALF:].astype(F32), jnp.uint32) & jnp.uint32(0xFFFF0000)
    return hi | lo


def _unpack_halves(u):
    lo = lax.bitcast_convert_type(u << 16, F32)
    hi = lax.bitcast_convert_type(u & jnp.uint32(0xFFFF0000), F32)
    return lo, hi


def _inproj_kernel(x_ref, g_ref, w_ref, o_ref, xn_ref):
    @pl.when(pl.program_id(1) == 0)
    def _():
        x = x_ref[...]
        ms = jnp.mean(x * x, axis=-1, keepdims=True)
        xn_ref[...] = (x * lax.rsqrt(ms + NORM_EPS) * g_ref[...]).astype(BF16)

    o_ref[...] = _dot(xn_ref[...], w_ref[...]).astype(o_ref.dtype)


def _inproj(x2, g, w_all, tm=1024, tn=768):
    t = x2.shape[0]
    tm = min(tm, t)
    return pl.pallas_call(
        _inproj_kernel,
        grid=(t // tm, NC // tn),
        in_specs=[
            pl.BlockSpec((tm, D_MODEL), lambda i, j: (i, 0)),
            pl.BlockSpec((1, D_MODEL), lambda i, j: (0, 0)),
            pl.BlockSpec((D_MODEL, tn), lambda i, j: (0, j)),
        ],
        out_specs=pl.BlockSpec((tm, tn), lambda i, j: (i, j)),
        out_shape=jax.ShapeDtypeStruct((t, NC), BF16),
        scratch_shapes=[pltpu.VMEM((tm, D_MODEL), BF16)],
        compiler_params=pltpu.CompilerParams(
            dimension_semantics=("parallel", "arbitrary"), vmem_limit_bytes=VMEM_LIMIT),
        name="inproj",
    )(x2, g, w_all)


def _da_kernel(slopes_ref, lam_ref, subg_ref, q_ref, k_ref, v_ref, o_ref, vt_ref, m_ref, l_ref, acc_ref,
               s_ref, p_ref, alpha_ref, *, tq, nsub, seq):
    tk = tq
    h = pl.program_id(1)
    qi = pl.program_id(2)
    slope = slopes_ref[h]

    @pl.when(qi == 0)
    def _():
        for blk in range(seq // LANES):
            rows = slice(blk * LANES, (blk + 1) * LANES)
            vt_ref[:, rows] = v_ref[rows, :].astype(F32).T.astype(BF16)

    q = q_ref[...] * jnp.asarray(DA_QK_DIM ** -0.5, BF16)
    lane = lax.broadcasted_iota(jnp.int32, (tq, LANES), 1)
    zero = jnp.zeros_like(q)
    qc = (jnp.where(lane < DA_QK_DIM, q, zero), jnp.where(lane >= DA_QK_DIM, q, zero))
    sub = tk // nsub
    krow = lax.broadcasted_iota(jnp.int32, (sub, tq), 0)
    qcol = lax.broadcasted_iota(jnp.int32, (sub, tq), 1)
    bias = [slope * (krow + u * sub).astype(F32) for u in range(nsub)]

    m_ref[...] = jnp.full_like(m_ref, NEG)
    l_ref[...] = jnp.zeros_like(l_ref)
    acc_ref[...] = jnp.zeros_like(acc_ref)
    p_ref[...] = jnp.zeros_like(p_ref)
    alpha_ref[...] = jnp.ones_like(alpha_ref)
    cu = [(c, u) for c in range(2) for u in range(nsub)]
    rows = lambda u: slice(u * sub, (u + 1) * sub)

    def scores_to_scratch(j):
        ks = pl.multiple_of(j * tk, tk)
        for c, u in cu:
            s_ref[c, rows(u), :] = _dot_nt(k_ref[pl.ds(ks + u * sub, sub), :], qc[c]) + bias[u]

    def values_from_scratch(j):
        ks = pl.multiple_of(j * tk, tk)
        vt = vt_ref[:, pl.ds(ks, tk)]
        for c in range(2):
            acc_ref[c] = alpha_ref[c:c + 1, :] * acc_ref[c] + _dot(vt, p_ref[c])

    def softmax(j, masked):
        rel = (j - qi) * tk
        off = slope * rel.astype(F32)
        s = {(c, u): s_ref[c, rows(u), :] for c, u in cu}
        if masked:
            s = {(c, u): jnp.where(krow + (rel + u * sub) <= qcol, s[c, u], NEG) for c, u in cu}
        m = [m_ref[c:c + 1, :] for c in range(2)]
        smax = [functools.reduce(jnp.maximum, [s[c, u] for u in range(nsub)]) for c in range(2)]
        m_new = [jnp.maximum(m[c], jnp.max(smax[c], axis=0, keepdims=True) + off) for c in range(2)]
        p = {(c, u): jnp.exp(s[c, u] - (m_new[c] - off)) for c, u in cu}
        psum = [functools.reduce(jnp.add, [p[c, u] for u in range(nsub)]) for c in range(2)]
        alpha = [jnp.exp(m[c] - m_new[c]) for c in range(2)]
        for c in range(2):
            m_ref[c:c + 1, :] = m_new[c]
            l_ref[c:c + 1, :] = alpha[c] * l_ref[c:c + 1, :] + jnp.sum(psum[c], axis=0, keepdims=True)
        return p, alpha

    def publish(p, alpha):
        for c in range(2):
            alpha_ref[c:c + 1, :] = alpha[c]
        for c, u in cu:
            p_ref[c, rows(u), :] = p[c, u].astype(BF16)

    scores_to_scratch(0)

    def body(j, _):
        p, alpha = softmax(j, False)
        values_from_scratch(jnp.maximum(j - 1, 0))
        scores_to_scratch(j + 1)
        publish(p, alpha)
        return 0

    lax.fori_loop(0, qi, body, 0)
    p, alpha = softmax(qi, True)
    values_from_scratch(jnp.maximum(qi - 1, 0))
    publish(p, alpha)
    values_from_scratch(qi)

    lv = lam_ref[...]
    lam = (jnp.exp(jnp.sum(lv[0:1] * lv[1:2], axis=-1, keepdims=True))
           - jnp.exp(jnp.sum(lv[2:3] * lv[3:4], axis=-1, keepdims=True)) + LAM_INIT)
    o = acc_ref[0] / l_ref[0:1, :] - lam * (acc_ref[1] / l_ref[1:2, :])
    ms = jnp.mean(o * o, axis=0, keepdims=True)
    o = o * lax.rsqrt(ms + NORM_EPS) * subg_ref[...] * (1.0 - LAM_INIT)
    o_ref[...] = o.T.astype(o_ref.dtype)


def _diffattn(proj, slopes, lam4, subg_col, b, s, tq=256, nsub=2):
    nq = s // tq
    return pl.pallas_call(
        functools.partial(_da_kernel, tq=tq, nsub=nsub, seq=s),
        grid=(b, DA_HEADS, nq),
        in_specs=[
            pl.BlockSpec(memory_space=pltpu.SMEM),
            pl.BlockSpec((4, DA_QK_DIM), lambda bi, h, qi: (0, 0)),
            pl.BlockSpec((DA_V_DIM, 1), lambda bi, h, qi: (0, 0)),
            pl.BlockSpec((tq, LANES), lambda bi, h, qi: (bi * nq + qi, h)),
            pl.BlockSpec((s, LANES), lambda bi, h, qi: (bi, DA_HEADS + h)),
            pl.BlockSpec((s, LANES), lambda bi, h, qi: (bi, 2 * DA_HEADS + h)),
        ],
        out_specs=pl.BlockSpec((tq, LANES), lambda bi, h, qi: (bi * nq + qi, h)),
        out_shape=jax.ShapeDtypeStruct((b * s, DA_WIDTH), BF16),
        scratch_shapes=[
            pltpu.VMEM((DA_V_DIM, s), BF16),
            pltpu.VMEM((2, tq), F32),
            pltpu.VMEM((2, tq), F32),
            pltpu.VMEM((2, DA_V_DIM, tq), F32),
            pltpu.VMEM((2, tq, tq), F32),
            pltpu.VMEM((2, tq, tq), BF16),
            pltpu.VMEM((2, tq), F32),
        ],
        compiler_params=pltpu.CompilerParams(
            dimension_semantics=("parallel", "parallel", "arbitrary"), vmem_limit_bytes=VMEM_LIMIT),
        name="diffattn",
    )(slopes, lam4, subg_col, proj, proj, proj)


def _rwkv_kernel(r_ref, k_ref, v_ref, p1_ref, p2_ref, mu_ref, w0_ref, a0_ref, w2_ref, kk_ref, ka_ref,
                 rk_ref, lnw_ref, lnb_ref, o_ref, state_ref, crkv_ref, cp2_ref):
    L = CHUNK
    c = pl.program_id(1)

    @pl.when(c == 0)
    def _():
        state_ref[...] = jnp.zeros_like(state_ref)
        crkv_ref[...] = jnp.zeros_like(crkv_ref)
        cp2_ref[...] = jnp.zeros_like(cp2_ref)

    first = lax.broadcasted_iota(jnp.int32, (L, 1), 0) == 0

    def shift(x, carry):
        return jnp.where(first, carry, pltpu.roll(x, 1, 0))

    r_raw = r_ref[...].astype(F32)
    k_raw = k_ref[...].astype(F32)
    v_raw = v_ref[...].astype(F32)
    p2 = p2_ref[...].astype(F32)
    rs = shift(r_raw, crkv_ref[:, 0:RW_WIDTH])
    ks = shift(k_raw, crkv_ref[:, RW_WIDTH:2 * RW_WIDTH])
    vs = shift(v_raw, crkv_ref[:, 2 * RW_WIDTH:3 * RW_WIDTH])
    lin = p1_ref[...].astype(F32) + shift(p2, cp2_ref[...])
    crkv_ref[:, 0:RW_WIDTH] = r_raw[L - 1:L, :]
    crkv_ref[:, RW_WIDTH:2 * RW_WIDTH] = k_raw[L - 1:L, :]
    crkv_ref[:, 2 * RW_WIDTH:3 * RW_WIDTH] = v_raw[L - 1:L, :]
    cp2_ref[...] = p2[L - 1:L, :]

    r = r_raw + (rs - r_raw) * mu_ref[0:1, :]
    k = k_raw + (ks - k_raw) * mu_ref[1:2, :]
    v = v_raw + (vs - v_raw) * mu_ref[2:3, :]

    ll = lax.broadcasted_iota(jnp.int32, (L, LORA_PAD), 1)
    z = jnp.where(ll < LORA_W, jnp.tanh(lin),
                  jnp.where(ll < LORA_W + LORA_A, lin,
                            jnp.where(ll < LORA_W + LORA_A + LORA_G, _sigmoid(lin), 0.0)))
    d = _dot(z.astype(BF16), w2_ref[...])
    logw = -_sigmoid(w0_ref[...] + d[:, 0:RW_WIDTH]) * math.exp(-0.5)
    a_sig = _sigmoid(a0_ref[...] + d[:, RW_WIDTH:2 * RW_WIDTH])
    g = d[:, 2 * RW_WIDTH:3 * RW_WIDTH]

    li = lax.broadcasted_iota(jnp.int32, (LANES, LANES), 0)
    lj = lax.broadcasted_iota(jnp.int32, (LANES, LANES), 1)
    same_head = jnp.where((li // RW_N) == (lj // RW_N), 1.0, 0.0).astype(BF16)

    def head_sum(x):
        xb = x.astype(BF16)
        return jnp.concatenate(
            [_dot(xb[:, p * LANES:(p + 1) * LANES], same_head) for p in range(RW_PAIRS)], axis=1)

    kk = k * kk_ref[...]
    kk = kk * lax.rsqrt(jnp.maximum(head_sum(kk * kk), 1e-24))
    k = k * (1.0 + (a_sig - 1.0) * ka_ref[...])
    bonus = head_sum(r * k * rk_ref[...])

    ti = lax.broadcasted_iota(jnp.int32, (L, L), 0)
    tj = lax.broadcasted_iota(jnp.int32, (L, L), 1)
    tri = jnp.where(ti >= tj, 1.0, 0.0).astype(BF16)
    w_hi = logw.astype(BF16)
    w_r1 = logw - w_hi.astype(F32)
    w_mid = w_r1.astype(BF16)
    w_lo = (w_r1 - w_mid.astype(F32)).astype(BF16)
    cw = _dot(tri, w_hi) + _dot(tri, w_mid) + _dot(tri, w_lo)
    cw_last = cw[L - 1:L, :]
    e_in = jnp.exp(cw)
    e_out = jnp.exp(-cw)
    e_end = jnp.exp(cw_last - cw)
    w_end = jnp.exp(cw_last)

    at = -kk * jnp.exp(cw - logw)
    rt = r * e_in
    b = kk * a_sig
    bt = b * e_out
    kt = k * e_out
    bd = b * e_end
    kd = k * e_end

    lane = lax.broadcasted_iota(jnp.int32, (L, LANES), 1)
    lo = lane < RW_N

    def stack(x, p):
        xs = x[:, p * LANES:(p + 1) * LANES]
        return jnp.concatenate([jnp.where(lo, xs, 0.0), jnp.where(lo, 0.0, xs)], axis=0).astype(BF16)

    si = lax.broadcasted_iota(jnp.int32, (2 * L, 2 * L), 0)
    sj = lax.broadcasted_iota(jnp.int32, (2 * L, 2 * L), 1)
    strict = (si % L) > (sj % L)
    incl = (si % L) >= (sj % L)
    eye = jnp.where(si == sj, 1.0, 0.0)

    pairs = range(RW_PAIRS)
    cat0 = lambda *xs: jnp.concatenate(xs, axis=0)
    cat1 = lambda *xs: jnp.concatenate(xs, axis=1)
    bf = lambda x: x.astype(BF16)
    a_s = [stack(at, p) for p in pairs]
    r_s = [stack(rt, p) for p in pairs]
    v_s = [stack(v, p) for p in pairs]
    bk = [cat0(stack(bt, p), stack(kt, p)) for p in pairs]
    bkd = [cat0(stack(bd, p), stack(kd, p)) for p in pairs]
    aa = [_dot_nt(a_s[p], bk[p]) for p in pairs]
    rr = [_dot_nt(r_s[p], bk[p]) for p in pairs]
    a_ab = [jnp.where(strict, aa[p][:, 0:2 * L], 0.0) for p in pairs]
    a_ak = [bf(jnp.where(strict, aa[p][:, 2 * L:4 * L], 0.0)) for p in pairs]
    a_rb = [bf(jnp.where(incl, rr[p][:, 0:2 * L], 0.0)) for p in pairs]
    a_rk = [bf(jnp.where(incl, rr[p][:, 2 * L:4 * L], 0.0)) for p in pairs]
    ab = [bf(a) for a in a_ab]
    qpow = [_dot(ab[p], ab[p]) for p in pairs]
    inv = [eye + a_ab[p] for p in pairs]
    for _ in range(int(math.log2(L)) - 2):
        qb = [bf(q) for q in qpow]
        res = [_dot(qb[p], cat1(qb[p], bf(inv[p]))) for p in pairs]
        qpow = [res[p][:, 0:2 * L] for p in pairs]
        inv = [inv[p] + res[p][:, 2 * L:4 * L] for p in pairs]
    inv = [bf(inv[p] + _dot(bf(qpow[p]), bf(inv[p]))) for p in pairs]
    h_old = [state_ref[p] for p in pairs]
    h_b = [bf(h) for h in h_old]
    x = [_dot(cat1(a_s[p], a_ak[p]), cat0(h_b[p], v_s[p])) for p in pairs]
    u_b = [bf(_dot(inv[p], bf(x[p]))) for p in pairs]
    y2 = [_dot(cat1(r_s[p], a_rb[p], a_rk[p]), cat0(h_b[p], u_b[p], v_s[p])) for p in pairs]
    for p in pairs:
        w_col = jnp.broadcast_to(w_end[:, p * LANES:(p + 1) * LANES], (LANES, LANES)).T
        state_ref[p] = w_col * h_old[p] + _dot_tn(bkd[p], cat0(u_b[p], v_s[p]))
    y = cat1(*[y2[p][0:L] + y2[p][L:2 * L] for p in pairs])

    mean = head_sum(y) * (1.0 / RW_N)
    yc = y - mean
    var = head_sum(yc * yc) * (1.0 / RW_N)
    y = yc * lax.rsqrt(var + LN_X_EPS) * lnw_ref[...] + lnb_ref[...]
    y = y + bonus * v
    o_ref[...] = (y * g).astype(o_ref.dtype)


def _rwkv(proj, mu_rkv, w0, a0, w2cat, k_k, k_a, r_k, ln_w, ln_b, b, s):
    L = CHUNK
    nc = s // L
    row = lambda n: pl.BlockSpec((1, n), lambda bi, c: (0, 0))
    rkv_col0 = 3 * DA_WIDTH // RW_WIDTH
    lora_col0 = IN_COLS // LORA_PAD
    return pl.pallas_call(
        _rwkv_kernel,
        grid=(b, nc),
        in_specs=[
            pl.BlockSpec((L, RW_WIDTH), lambda bi, c: (bi * nc + c, rkv_col0)),
            pl.BlockSpec((L, RW_WIDTH), lambda bi, c: (bi * nc + c, rkv_col0 + 1)),
            pl.BlockSpec((L, RW_WIDTH), lambda bi, c: (bi * nc + c, rkv_col0 + 2)),
            pl.BlockSpec((L, LORA_PAD), lambda bi, c: (bi * nc + c, lora_col0)),
            pl.BlockSpec((L, LORA_PAD), lambda bi, c: (bi * nc + c, lora_col0 + 1)),
            pl.BlockSpec((3, RW_WIDTH), lambda bi, c: (0, 0)),
            row(RW_WIDTH), row(RW_WIDTH),
            pl.BlockSpec((LORA_PAD, 3 * RW_WIDTH), lambda bi, c: (0, 0)),
            row(RW_WIDTH), row(RW_WIDTH), row(RW_WIDTH), row(RW_WIDTH), row(RW_WIDTH),
        ],
        out_specs=pl.BlockSpec((L, RW_WIDTH), lambda bi, c: (bi * nc + c, 0)),
        out_shape=jax.ShapeDtypeStruct((b * s, RW_WIDTH), BF16),
        scratch_shapes=[
            pltpu.VMEM((RW_PAIRS, LANES, LANES), F32),
            pltpu.VMEM((1, 3 * RW_WIDTH), F32),
            pltpu.VMEM((1, LORA_PAD), F32),
        ],
        compiler_params=pltpu.CompilerParams(
            dimension_semantics=("parallel", "arbitrary"), vmem_limit_bytes=VMEM_LIMIT),
        name="rwkv",
    )(proj, proj, proj, proj, proj, mu_rkv, w0, a0, w2cat, k_k, k_a, r_k, ln_w, ln_b)


ROUTE_E, ROUTE_G, ROUTE_R = 0, 2, 4
EXPERT_LANE0 = N_GROUPS


def _outproj_kernel(x_ref, oda_ref, orw_ref, wout_ref, g_ref, wrh_ref, wrl_ref, br_ref,
                    h_ref, xn_ref, route_ref, cnt_ref, base_ref, *, tm):
    @pl.when(pl.program_id(0) == 0)
    def _():
        base_ref[...] = jnp.zeros_like(base_ref)

    mix = _dot(oda_ref[...], wout_ref[0:DA_WIDTH, :]) + _dot(orw_ref[...], wout_ref[DA_WIDTH:, :])
    h = x_ref[...] + mix
    h_ref[...] = h
    ms = jnp.mean(h * h, axis=-1, keepdims=True)
    xn = h * lax.rsqrt(ms + NORM_EPS) * g_ref[...]
    x_hi = xn.astype(BF16)
    xn_ref[...] = _pack_halves(x_hi)

    x_lo = (xn - x_hi.astype(F32)).astype(BF16)
    lg = (_dot(x_hi, wrh_ref[...]) + _dot(x_lo, wrh_ref[...]) + _dot(x_hi, wrl_ref[...])) + br_ref[...]

    lane = lax.broadcasted_iota(jnp.int32, (tm, LANES), 1)
    big = jnp.int32(1 << 20)
    gl = jnp.where(lane < N_GROUPS, lg, NEG)
    gmax = jnp.max(gl, axis=-1, keepdims=True)
    grp = jnp.min(jnp.where(gl == gmax, lane, big), axis=-1, keepdims=True)
    p_grp = 1.0 / jnp.sum(jnp.exp(gl - gmax), axis=-1, keepdims=True)
    eg = jnp.where(lane >= EXPERT_LANE0, (lane - EXPERT_LANE0) // EXPERTS_PER_GROUP, -1)
    el = jnp.where(eg == grp, lg, NEG)
    emax = jnp.max(el, axis=-1, keepdims=True)
    pe = jnp.exp(el - emax)
    probs = pe / jnp.sum(pe, axis=-1, keepdims=True)
    probs = jnp.where(eg == grp, probs, -1.0)
    p1 = jnp.max(probs, axis=-1, keepdims=True)
    i1 = jnp.min(jnp.where(probs == p1, lane, big), axis=-1, keepdims=True)
    probs2 = jnp.where(lane == i1, -1.0, probs)
    p2 = jnp.max(probs2, axis=-1, keepdims=True)
    i2 = jnp.min(jnp.where(probs2 == p2, lane, big), axis=-1, keepdims=True)
    gate1 = p_grp * p1 / (p1 + p2)
    gate2 = p_grp * p2 / (p1 + p2)

    oh1 = jnp.where(lane == i1, 1.0, 0.0)
    oh2 = jnp.where(lane == i2, 1.0, 0.0)
    ri = lax.broadcasted_iota(jnp.int32, (tm, tm), 0)
    rj = lax.broadcasted_iota(jnp.int32, (tm, tm), 1)
    lower = jnp.where(ri > rj, 1.0, 0.0).astype(BF16)
    base = base_ref[...]
    tot1 = jnp.sum(oh1, axis=0, keepdims=True)
    c1 = base + _dot(lower, oh1.astype(BF16))
    c2 = base + tot1 + _dot(lower, oh2.astype(BF16))
    rank1 = jnp.sum(oh1 * c1, axis=-1, keepdims=True)
    rank2 = jnp.sum(oh2 * c2, axis=-1, keepdims=True)
    base = base + tot1 + jnp.sum(oh2, axis=0, keepdims=True)
    base_ref[...] = base
    cnt_ref[...] = base

    e1 = (i1 - EXPERT_LANE0).astype(F32)
    e2 = (i2 - EXPERT_LANE0).astype(F32)
    rec = jnp.zeros((tm, LANES), F32)
    for ln, val in ((ROUTE_E, e1), (ROUTE_E + 1, e2), (ROUTE_G, gate1), (ROUTE_G + 1, gate2),
                    (ROUTE_R, rank1), (ROUTE_R + 1, rank2)):
        rec = jnp.where(lane == ln, val, rec)
    route_ref[...] = rec


def _outproj(x2, o_da, o_rw, w_out, g_ffn, wr_hi, wr_lo, b_r, tm=512):
    t = x2.shape[0]
    const = lambda shape: pl.BlockSpec(shape, lambda i: (0, 0))
    return pl.pallas_call(
        functools.partial(_outproj_kernel, tm=tm),
        grid=(t // tm,),
        in_specs=[
            pl.BlockSpec((tm, D_MODEL), lambda i: (i, 0)),
            pl.BlockSpec((tm, DA_WIDTH), lambda i: (i, 0)),
            pl.BlockSpec((tm, RW_WIDTH), lambda i: (i, 0)),
            const((D_MODEL, D_MODEL)),
            const((1, D_MODEL)),
            const((D_MODEL, LANES)),
            const((D_MODEL, LANES)),
            const((1, LANES)),
        ],
        out_specs=[
            pl.BlockSpec((tm, D_MODEL), lambda i: (i, 0)),
            pl.BlockSpec((tm, HALF), lambda i: (i, 0)),
            pl.BlockSpec((tm, LANES), lambda i: (i, 0)),
            const((1, LANES)),
        ],
        out_shape=[
            jax.ShapeDtypeStruct((t, D_MODEL), F32),
            jax.ShapeDtypeStruct((t, HALF), jnp.uint32),
            jax.ShapeDtypeStruct((t, LANES), F32),
            jax.ShapeDtypeStruct((1, LANES), F32),
        ],
        scratch_shapes=[pltpu.VMEM((1, LANES), F32)],
        compiler_params=pltpu.CompilerParams(
            dimension_semantics=("arbitrary",), vmem_limit_bytes=VMEM_LIMIT),
        name="outproj_router",
    )(x2, o_da, o_rw, w_out, g_ffn, wr_hi, wr_lo, b_r)


def _row_copy(src_hbm, dst_ref, src_row, dst_row, sem):
    return pltpu.make_async_copy(src_hbm.at[pl.ds(src_row, 1)], dst_ref.at[pl.ds(dst_row, 1)], sem)


def _dispatch_kernel(dest_ref, x_ref, xb_in_hbm, xb_hbm, sem, *, td):
    del xb_in_hbm
    t0 = pl.program_id(0) * td

    def start(i, _):
        tok = t0 + i
        _row_copy(x_ref, xb_hbm, i, dest_ref[2 * tok], sem).start(priority=0)
        _row_copy(x_ref, xb_hbm, i, dest_ref[2 * tok + 1], sem).start(priority=1)
        return 0

    lax.fori_loop(0, td, start, 0, unroll=8)
    for _ in range(2):
        pltpu.make_async_copy(x_ref, xb_hbm.at[pl.ds(0, td)], sem).wait()


def _dispatch(dest_flat, xn2, xb_zero, td=256):
    t = xn2.shape[0]
    return pl.pallas_call(
        functools.partial(_dispatch_kernel, td=td),
        grid_spec=pltpu.PrefetchScalarGridSpec(
            num_scalar_prefetch=1,
            grid=(t // td,),
            in_specs=[pl.BlockSpec((td, HALF), lambda i, d: (i, 0)), pl.BlockSpec(memory_space=pl.ANY)],
            out_specs=pl.BlockSpec(memory_space=pl.ANY),
            scratch_shapes=[pltpu.SemaphoreType.DMA],
        ),
        out_shape=jax.ShapeDtypeStruct(xb_zero.shape, xb_zero.dtype),
        input_output_aliases={2: 0},
        compiler_params=pltpu.CompilerParams(dimension_semantics=("arbitrary",)),
        name="dispatch",
    )(dest_flat, xn2, xb_zero)


def _experts_kernel(be_ref, nu_ref, xb_ref, wg_ref, wu_ref, wd_ref, yb_ref, wgb_ref, wub_ref, wdb_ref):
    i = pl.program_id(0)
    used = i < nu_ref[0]
    new_expert = jnp.logical_or(i == 0, be_ref[i] != be_ref[jnp.maximum(i - 1, 0)])

    @pl.when(jnp.logical_and(used, new_expert))
    def _():
        wgb_ref[...] = wg_ref[...].astype(BF16)
        wub_ref[...] = wu_ref[...].astype(BF16)
        wdb_ref[...] = wd_ref[...].astype(BF16)

    @pl.when(used)
    def _():
        x_lo, x_hi = (half.astype(BF16) for half in _unpack_halves(xb_ref[...]))
        hg = _dot(x_lo, wgb_ref[0:HALF, :]) + _dot(x_hi, wgb_ref[HALF:, :])
        hu = _dot(x_lo, wub_ref[0:HALF, :]) + _dot(x_hi, wub_ref[HALF:, :])
        hid = hg * _sigmoid(hg) * hu
        yb_ref[...] = _pack_halves(_dot(hid.astype(BF16), wdb_ref[...]).astype(BF16))

    @pl.when(jnp.logical_not(used))
    def _():
        yb_ref[...] = jnp.zeros_like(yb_ref)


def _experts(block_e, n_used, xb, w_gate, w_up, w_down):
    nb = xb.shape[0] // EXPERT_BLOCK
    rows = lambda i, be, nu: (jnp.minimum(i, nu[0] - 1), 0)
    return pl.pallas_call(
        _experts_kernel,
        grid_spec=pltpu.PrefetchScalarGridSpec(
            num_scalar_prefetch=2,
            grid=(nb,),
            in_specs=[
                pl.BlockSpec((EXPERT_BLOCK, HALF), rows),
                pl.BlockSpec((None, D_MODEL, D_EXPERT), lambda i, be, nu: (be[i], 0, 0)),
                pl.BlockSpec((None, D_MODEL, D_EXPERT), lambda i, be, nu: (be[i], 0, 0)),
                pl.BlockSpec((None, D_EXPERT, D_MODEL), lambda i, be, nu: (be[i], 0, 0)),
            ],
            out_specs=pl.BlockSpec((EXPERT_BLOCK, HALF), lambda i, be, nu: (i, 0)),
            scratch_shapes=[
                pltpu.VMEM((D_MODEL, D_EXPERT), BF16),
                pltpu.VMEM((D_MODEL, D_EXPERT), BF16),
                pltpu.VMEM((D_EXPERT, D_MODEL), BF16),
            ],
        ),
        out_shape=jax.ShapeDtypeStruct(xb.shape, jnp.uint32),
        compiler_params=pltpu.CompilerParams(
            dimension_semantics=("arbitrary",), vmem_limit_bytes=EXPERTS_VMEM_LIMIT),
        name="experts",
    )(block_e, n_used, xb, w_gate, w_up, w_down)


def _combine_kernel(dest_ref, route_ref, h_ref, g_ref, yb_hbm, o_ref, y1_ref, y2_ref, sem, *, tc):
    t0 = pl.program_id(0) * tc

    def start(i, _):
        tok = t0 + i
        _row_copy(yb_hbm, y1_ref, dest_ref[2 * tok], i, sem).start(priority=0)
        _row_copy(yb_hbm, y2_ref, dest_ref[2 * tok + 1], i, sem).start(priority=1)
        return 0

    lax.fori_loop(0, tc, start, 0, unroll=8)
    pltpu.make_async_copy(yb_hbm.at[pl.ds(0, tc)], y1_ref, sem).wait()
    pltpu.make_async_copy(yb_hbm.at[pl.ds(0, tc)], y2_ref, sem).wait()

    rec = route_ref[...]
    lane = lax.broadcasted_iota(jnp.int32, (tc, LANES), 1)
    g1 = jnp.sum(jnp.where(lane == ROUTE_G, rec, 0.0), axis=-1, keepdims=True)
    g2 = jnp.sum(jnp.where(lane == ROUTE_G + 1, rec, 0.0), axis=-1, keepdims=True)
    y1_lo, y1_hi = _unpack_halves(y1_ref[...])
    y2_lo, y2_hi = _unpack_halves(y2_ref[...])
    h_lo = h_ref[:, 0:HALF] + (y1_lo * g1 + y2_lo * g2)
    h_hi = h_ref[:, HALF:] + (y1_hi * g1 + y2_hi * g2)
    ms = (jnp.sum(h_lo * h_lo, axis=-1, keepdims=True)
          + jnp.sum(h_hi * h_hi, axis=-1, keepdims=True)) * (1.0 / D_MODEL)
    scale = lax.rsqrt(ms + NORM_EPS)
    o_ref[:, 0:HALF] = h_lo * scale * g_ref[:, 0:HALF]
    o_ref[:, HALF:] = h_hi * scale * g_ref[:, HALF:]


def _combine(dest_flat, route, h1, g_final, yb, tc=256):
    t = h1.shape[0]
    return pl.pallas_call(
        functools.partial(_combine_kernel, tc=tc),
        grid_spec=pltpu.PrefetchScalarGridSpec(
            num_scalar_prefetch=1,
            grid=(t // tc,),
            in_specs=[
                pl.BlockSpec((tc, LANES), lambda i, d: (i, 0)),
                pl.BlockSpec((tc, D_MODEL), lambda i, d: (i, 0)),
                pl.BlockSpec((1, D_MODEL), lambda i, d: (0, 0)),
                pl.BlockSpec(memory_space=pl.ANY),
            ],
            out_specs=pl.BlockSpec((tc, D_MODEL), lambda i, d: (i, 0)),
            scratch_shapes=[
                pltpu.VMEM((tc, HALF), jnp.uint32),
                pltpu.VMEM((tc, HALF), jnp.uint32),
                pltpu.SemaphoreType.DMA,
            ],
        ),
        out_shape=jax.ShapeDtypeStruct((t, D_MODEL), F32),
        compiler_params=pltpu.CompilerParams(
            dimension_semantics=("arbitrary",), vmem_limit_bytes=VMEM_LIMIT),
        name="combine",
    )(dest_flat, route, h1, g_final, yb)


def _pad_cols(w, n):
    return jnp.pad(w, ((0, 0), (0, n - w.shape[1])))


def kernel(x, g_mix, w_in, w_out, da_lambda_q1, da_lambda_k1, da_lambda_q2, da_lambda_k2, da_subln_g,
           rw_mu_x, rw_mu_rkv, rw_w0, rw_w1, rw_w2, rw_a0, rw_a1, rw_a2, rw_g1, rw_g2, rw_k_k, rw_k_a,
           rw_r_k, rw_ln_w, rw_ln_b, g_ffn, moe_w_group, moe_b_group, moe_w_expert, moe_b_expert,
           moe_w_gate, moe_w_up, moe_w_down, g_final):
    b, s, d = x.shape
    t = b * s
    x2 = x.reshape(t, d)

    mu = rw_mu_x[0]
    lora = (rw_w1[0], rw_a1[0], rw_g1[0])
    keep = _pad_cols(jnp.concatenate([(1.0 - mu[i])[:, None] * w for i, w in enumerate(lora)], axis=1), LORA_PAD)
    prev = _pad_cols(jnp.concatenate([mu[i][:, None] * w for i, w in enumerate(lora)], axis=1), LORA_PAD)
    w_all = jnp.concatenate([w_in[0], keep, prev], axis=1).astype(BF16)
    proj = _inproj(x2, g_mix, w_all)

    slopes = jnp.asarray([2.0 ** (-8.0 * (i + 1) / DA_HEADS) for i in range(DA_HEADS)], F32)
    lam4 = jnp.concatenate([da_lambda_q1, da_lambda_k1, da_lambda_q2, da_lambda_k2], axis=0)
    o_da = _diffattn(proj, slopes, lam4, da_subln_g.reshape(DA_V_DIM, 1), b, s)

    w2cat = jnp.zeros((LORA_PAD, 3 * RW_WIDTH), F32)
    w2cat = w2cat.at[0:LORA_W, 0:RW_WIDTH].set(rw_w2[0])
    w2cat = w2cat.at[LORA_W:LORA_W + LORA_A, RW_WIDTH:2 * RW_WIDTH].set(rw_a2[0])
    w2cat = w2cat.at[LORA_W + LORA_A:LORA_W + LORA_A + LORA_G, 2 * RW_WIDTH:].set(rw_g2[0])
    o_rw = _rwkv(proj, rw_mu_rkv[0], rw_w0, rw_a0, w2cat.astype(BF16), rw_k_k, rw_k_a,
                 rw_r_k.reshape(1, RW_WIDTH), rw_ln_w, rw_ln_b, b, s)

    w_r = _pad_cols(jnp.concatenate([moe_w_group[0], moe_w_expert[0]], axis=1), LANES)
    wr_hi = w_r.astype(BF16)
    wr_lo = (w_r - wr_hi.astype(F32)).astype(BF16)
    b_r = _pad_cols(jnp.concatenate([moe_b_group[0], moe_b_expert[0].reshape(-1)])[None, :], LANES)
    h1, xn2, route, cnt = _outproj(x2, o_da, o_rw, w_out[0].astype(BF16), g_ffn, wr_hi, wr_lo, b_r)

    counts = cnt[0, EXPERT_LANE0:EXPERT_LANE0 + N_EXPERTS].astype(jnp.int32)
    pcounts = ((counts + EXPERT_BLOCK - 1) // EXPERT_BLOCK) * EXPERT_BLOCK
    pends = jnp.cumsum(pcounts)
    pstarts = pends - pcounts
    e_idx = route[:, ROUTE_E:ROUTE_E + 2].astype(jnp.int32)
    rank = route[:, ROUTE_R:ROUTE_R + 2].astype(jnp.int32)
    dest = (pstarts[e_idx] + rank).reshape(-1)
    n_blocks = (2 * t) // EXPERT_BLOCK + N_EXPERTS
    block_row0 = jnp.arange(n_blocks, dtype=jnp.int32) * EXPERT_BLOCK
    block_e = jnp.minimum(jnp.sum(pends[None, :] <= block_row0[:, None], axis=1), N_EXPERTS - 1).astype(jnp.int32)
    n_used = (pends[-1:] // EXPERT_BLOCK).astype(jnp.int32)

    xb = _dispatch(dest, xn2, jnp.zeros((n_blocks * EXPERT_BLOCK, HALF), jnp.uint32))
    yb = _experts(block_e, n_used, xb, moe_w_gate[0], moe_w_up[0], moe_w_down[0])
    out = _combine(dest, route, h1, g_final[None, :], yb)
    return out.reshape(b, s, d)
```

```python
import functools
import math

import jax
import jax.numpy as jnp
from jax import lax
from jax.experimental import pallas as pl
from jax.experimental.pallas import tpu as pltpu

F32 = jnp.float32
BF16 = jnp.bfloat16

D_MODEL = 2048
DA_WIDTH = 1024
RW_WIDTH = 1024
DA_HEADS = 8
DA_V_DIM = 128
DA_QK_DIM = 64
RW_N = 64
RW_PAIRS = RW_WIDTH // 128
LORA_W, LORA_A, LORA_G = 64, 64, 160
LORA_PAD = 384
IN_COLS = 3 * DA_WIDTH + 3 * RW_WIDTH
NC = IN_COLS + 2 * LORA_PAD
N_GROUPS = 4
EXPERTS_PER_GROUP = 8
N_EXPERTS = 32
D_EXPERT = 512
NORM_EPS = 1e-6
LN_X_EPS = 64e-5
LAM_INIT = 0.8 - 0.6 * math.exp(-0.3 * 0)
LANES = 128
CHUNK = 64
RW_TILE = 256
NCH = RW_TILE // CHUNK
EXPERT_BLOCK = 256
NEG = -1e30
VMEM_LIMIT = 48 * 1024 * 1024
EXPERTS_VMEM_LIMIT = 56 * 1024 * 1024


def _dot(a, b):
    return jnp.dot(a, b, preferred_element_type=F32)


def _dot_nt(a, b):
    return lax.dot_general(a, b, (((1,), (1,)), ((), ())), preferred_element_type=F32)


def _dot_tn(a, b):
    return lax.dot_general(a, b, (((0,), (0,)), ((), ())), preferred_element_type=F32)


def _sigmoid(x):
    return 1.0 / (1.0 + jnp.exp(-x))


HALF = D_MODEL // 2


def _pack_halves(xb):
    lo = lax.bitcast_convert_type(xb[:, :HALF].astype(F32), jnp.uint32) >> 16
    hi = lax.bitcast_convert_type(xb[:, HALF:].astype(F32), jnp.uint32) & jnp.uint32(0xFFFF0000)
    return hi | lo


def _unpack_halves(u):
    lo = lax.bitcast_convert_type(u << 16, F32)
    hi = lax.bitcast_convert_type(u & jnp.uint32(0xFFFF0000), F32)
    return lo, hi


def _inproj_kernel(x_ref, g_ref, w_ref, o_ref, xn_ref):
    @pl.when(pl.program_id(1) == 0)
    def _():
        x = x_ref[...]
        ms = jnp.mean(x * x, axis=-1, keepdims=True)
        xn_ref[...] = (x * lax.rsqrt(ms + NORM_EPS) * g_ref[...]).astype(BF16)

    o_ref[...] = _dot(xn_ref[...], w_ref[...]).astype(o_ref.dtype)


def _inproj(x2, g, w_all, tm=1024, tn=768):
    t = x2.shape[0]
    tm = min(tm, t)
    return pl.pallas_call(
        _inproj_kernel,
        grid=(t // tm, NC // tn),
        in_specs=[
            pl.BlockSpec((tm, D_MODEL), lambda i, j: (i, 0)),
            pl.BlockSpec((1, D_MODEL), lambda i, j: (0, 0)),
            pl.BlockSpec((D_MODEL, tn), lambda i, j: (0, j)),
        ],
        out_specs=pl.BlockSpec((tm, tn), lambda i, j: (i, j)),
        out_shape=jax.ShapeDtypeStruct((t, NC), BF16),
        scratch_shapes=[pltpu.VMEM((tm, D_MODEL), BF16)],
        compiler_params=pltpu.CompilerParams(
            dimension_semantics=("parallel", "arbitrary"), vmem_limit_bytes=VMEM_LIMIT),
        name="inproj",
    )(x2, g, w_all)


def _da_kernel(slopes_ref, lam_ref, subg_ref, q_ref, k_ref, v_ref, o_ref, vt_ref, m_ref, l_ref, acc_ref,
               s_ref, p_ref, alpha_ref, *, tq, nsub, seq):
    tk = tq
    h = pl.program_id(1)
    qi = pl.program_id(2)
    slope = slopes_ref[h]

    @pl.when(qi == 0)
    def _():
        for blk in range(seq // LANES):
            rows = slice(blk * LANES, (blk + 1) * LANES)
            vt_ref[:, rows] = v_ref[rows, :].astype(F32).T.astype(BF16)

    q = q_ref[...] * jnp.asarray(DA_QK_DIM ** -0.5, BF16)
    lane = lax.broadcasted_iota(jnp.int32, (tq, LANES), 1)
    zero = jnp.zeros_like(q)
    qc = (jnp.where(lane < DA_QK_DIM, q, zero), jnp.where(lane >= DA_QK_DIM, q, zero))
    sub = tk // nsub
    krow = lax.broadcasted_iota(jnp.int32, (sub, tq), 0)
    qcol = lax.broadcasted_iota(jnp.int32, (sub, tq), 1)
    bias = [slope * (krow + u * sub).astype(F32) for u in range(nsub)]

    m_ref[...] = jnp.full_like(m_ref, NEG)
    l_ref[...] = jnp.zeros_like(l_ref)
    acc_ref[...] = jnp.zeros_like(acc_ref)
    p_ref[...] = jnp.zeros_like(p_ref)
    alpha_ref[...] = jnp.ones_like(alpha_ref)
    cu = [(c, u) for c in range(2) for u in range(nsub)]
    rows = lambda u: slice(u * sub, (u + 1) * sub)

    def scores_to_scratch(j):
        ks = pl.multiple_of(j * tk, tk)
        for c, u in cu:
            s_ref[c, rows(u), :] = _dot_nt(k_ref[pl.ds(ks + u * sub, sub), :], qc[c]) + bias[u]

    def values_from_scratch(j):
        ks = pl.multiple_of(j * tk, tk)
        vt = vt_ref[:, pl.ds(ks, tk)]
        for c in range(2):
            acc_ref[c] = alpha_ref[c:c + 1, :] * acc_ref[c] + _dot(vt, p_ref[c])

    def softmax(j, masked):
        rel = (j - qi) * tk
        off = slope * rel.astype(F32)
        s = {(c, u): s_ref[c, rows(u), :] for c, u in cu}
        if masked:
            s = {(c, u): jnp.where(krow + (rel + u * sub) <= qcol, s[c, u], NEG) for c, u in cu}
        m = [m_ref[c:c + 1, :] for c in range(2)]
        smax = [functools.reduce(jnp.maximum, [s[c, u] for u in range(nsub)]) for c in range(2)]
        m_new = [jnp.maximum(m[c], jnp.max(smax[c], axis=0, keepdims=True) + off) for c in range(2)]
        p = {(c, u): jnp.exp(s[c, u] - (m_new[c] - off)) for c, u in cu}
        psum = [functools.reduce(jnp.add, [p[c, u] for u in range(nsub)]) for c in range(2)]
        alpha = [jnp.exp(m[c] - m_new[c]) for c in range(2)]
        for c in range(2):
            m_ref[c:c + 1, :] = m_new[c]
            l_ref[c:c + 1, :] = alpha[c] * l_ref[c:c + 1, :] + jnp.sum(psum[c], axis=0, keepdims=True)
        return p, alpha

    def publish(p, alpha):
        for c in range(2):
            alpha_ref[c:c + 1, :] = alpha[c]
        for c, u in cu:
            p_ref[c, rows(u), :] = p[c, u].astype(BF16)

    scores_to_scratch(0)

    def body(j, _):
        p, alpha = softmax(j, False)
        values_from_scratch(jnp.maximum(j - 1, 0))
        scores_to_scratch(j + 1)
        publish(p, alpha)
        return 0

    lax.fori_loop(0, qi, body, 0)
    p, alpha = softmax(qi, True)
    values_from_scratch(jnp.maximum(qi - 1, 0))
    publish(p, alpha)
    values_from_scratch(qi)

    lv = lam_ref[...]
    lam = (jnp.exp(jnp.sum(lv[0:1] * lv[1:2], axis=-1, keepdims=True))
           - jnp.exp(jnp.sum(lv[2:3] * lv[3:4], axis=-1, keepdims=True)) + LAM_INIT)
    o = acc_ref[0] / l_ref[0:1, :] - lam * (acc_ref[1] / l_ref[1:2, :])
    ms = jnp.mean(o * o, axis=0, keepdims=True)
    o = o * lax.rsqrt(ms + NORM_EPS) * subg_ref[...] * (1.0 - LAM_INIT)
    o_ref[...] = o.T.astype(o_ref.dtype)


def _diffattn(proj, slopes, lam4, subg_col, b, s, tq=256, nsub=2):
    nq = s // tq
    return pl.pallas_call(
        functools.partial(_da_kernel, tq=tq, nsub=nsub, seq=s),
        grid=(b, DA_HEADS, nq),
        in_specs=[
            pl.BlockSpec(memory_space=pltpu.SMEM),
            pl.BlockSpec((4, DA_QK_DIM), lambda bi, h, qi: (0, 0)),
            pl.BlockSpec((DA_V_DIM, 1), lambda bi, h, qi: (0, 0)),
            pl.BlockSpec((tq, LANES), lambda bi, h, qi: (bi * nq + qi, h)),
            pl.BlockSpec((s, LANES), lambda bi, h, qi: (bi, DA_HEADS + h)),
            pl.BlockSpec((s, LANES), lambda bi, h, qi: (bi, 2 * DA_HEADS + h)),
        ],
        out_specs=pl.BlockSpec((tq, LANES), lambda bi, h, qi: (bi * nq + qi, h)),
        out_shape=jax.ShapeDtypeStruct((b * s, DA_WIDTH), BF16),
        scratch_shapes=[
            pltpu.VMEM((DA_V_DIM, s), BF16),
            pltpu.VMEM((2, tq), F32),
            pltpu.VMEM((2, tq), F32),
            pltpu.VMEM((2, DA_V_DIM, tq), F32),
            pltpu.VMEM((2, tq, tq), F32),
            pltpu.VMEM((2, tq, tq), BF16),
            pltpu.VMEM((2, tq), F32),
        ],
        compiler_params=pltpu.CompilerParams(
            dimension_semantics=("parallel", "parallel", "arbitrary"), vmem_limit_bytes=VMEM_LIMIT),
        name="diffattn",
    )(slopes, lam4, subg_col, proj, proj, proj)


def _rwkv_kernel(r_ref, k_ref, v_ref, p1_ref, p2_ref, mu_ref, w0_ref, a0_ref, w2_ref, kk_ref, ka_ref,
                 rk_ref, lnw_ref, lnb_ref, o_ref, state_ref, crkv_ref, cp2_ref):
    L, TT = CHUNK, RW_TILE
    c = pl.program_id(1)

    @pl.when(c == 0)
    def _():
        state_ref[...] = jnp.zeros_like(state_ref)
        crkv_ref[...] = jnp.zeros_like(crkv_ref)
        cp2_ref[...] = jnp.zeros_like(cp2_ref)

    first = lax.broadcasted_iota(jnp.int32, (TT, 1), 0) == 0

    def shift(x, carry):
        return jnp.where(first, carry, pltpu.roll(x, 1, 0))

    r_raw = r_ref[...].astype(F32)
    k_raw = k_ref[...].astype(F32)
    v_raw = v_ref[...].astype(F32)
    p2 = p2_ref[...].astype(F32)
    rs = shift(r_raw, crkv_ref[:, 0:RW_WIDTH])
    ks = shift(k_raw, crkv_ref[:, RW_WIDTH:2 * RW_WIDTH])
    vs = shift(v_raw, crkv_ref[:, 2 * RW_WIDTH:3 * RW_WIDTH])
    lin = p1_ref[...].astype(F32) + shift(p2, cp2_ref[...])
    crkv_ref[:, 0:RW_WIDTH] = r_raw[TT - 1:TT, :]
    crkv_ref[:, RW_WIDTH:2 * RW_WIDTH] = k_raw[TT - 1:TT, :]
    crkv_ref[:, 2 * RW_WIDTH:3 * RW_WIDTH] = v_raw[TT - 1:TT, :]
    cp2_ref[...] = p2[TT - 1:TT, :]

    r = r_raw + (rs - r_raw) * mu_ref[0:1, :]
    k = k_raw + (ks - k_raw) * mu_ref[1:2, :]
    v = v_raw + (vs - v_raw) * mu_ref[2:3, :]

    ll = lax.broadcasted_iota(jnp.int32, (TT, LORA_PAD), 1)
    z = jnp.where(ll < LORA_W, jnp.tanh(lin),
                  jnp.where(ll < LORA_W + LORA_A, lin,
                            jnp.where(ll < LORA_W + LORA_A + LORA_G, _sigmoid(lin), 0.0)))
    d = _dot(z.astype(BF16), w2_ref[...])
    logw = -_sigmoid(w0_ref[...] + d[:, 0:RW_WIDTH]) * math.exp(-0.5)
    a_sig = _sigmoid(a0_ref[...] + d[:, RW_WIDTH:2 * RW_WIDTH])
    g = d[:, 2 * RW_WIDTH:3 * RW_WIDTH]

    li = lax.broadcasted_iota(jnp.int32, (LANES, LANES), 0)
    lj = lax.broadcasted_iota(jnp.int32, (LANES, LANES), 1)
    same_head = jnp.where((li // RW_N) == (lj // RW_N), 1.0, 0.0).astype(BF16)

    def head_sum(x):
        xb = x.astype(BF16)
        return jnp.concatenate(
            [_dot(xb[:, p * LANES:(p + 1) * LANES], same_head) for p in range(RW_PAIRS)], axis=1)

    kk = k * kk_ref[...]
    kk = kk * lax.rsqrt(jnp.maximum(head_sum(kk * kk), 1e-24))
    k = k * (1.0 + (a_sig - 1.0) * ka_ref[...])
    bonus = head_sum(r * k * rk_ref[...])

    ti = lax.broadcasted_iota(jnp.int32, (TT, TT), 0)
    tj = lax.broadcasted_iota(jnp.int32, (TT, TT), 1)
    tri = jnp.where(ti >= tj, jnp.where((ti // L) == (tj // L), 1.0, 0.0), 0.0).astype(BF16)
    w_hi = logw.astype(BF16)
    w_r1 = logw - w_hi.astype(F32)
    w_mid = w_r1.astype(BF16)
    w_lo = (w_r1 - w_mid.astype(F32)).astype(BF16)
    cw = _dot(tri, w_hi) + _dot(tri, w_mid) + _dot(tri, w_lo)
    cw_last = [cw[(ci + 1) * L - 1:(ci + 1) * L, :] for ci in range(NCH)]
    e_in = jnp.exp(cw)
    e_out = jnp.exp(-cw)
    e_end = jnp.exp(jnp.concatenate([jnp.broadcast_to(cl, (L, RW_WIDTH)) for cl in cw_last], axis=0) - cw)
    w_end = [jnp.exp(cl) for cl in cw_last]

    at = -kk * jnp.exp(cw - logw)
    rt = r * e_in
    b = kk * a_sig
    bt = b * e_out
    kt = k * e_out
    bd = b * e_end
    kd = k * e_end

    lane = lax.broadcasted_iota(jnp.int32, (L, LANES), 1)
    lo = lane < RW_N

    def stack(x, ci, p):
        xs = x[ci * L:(ci + 1) * L, p * LANES:(p + 1) * LANES]
        return jnp.concatenate([jnp.where(lo, xs, 0.0), jnp.where(lo, 0.0, xs)], axis=0).astype(BF16)

    si = lax.broadcasted_iota(jnp.int32, (2 * L, 2 * L), 0)
    sj = lax.broadcasted_iota(jnp.int32, (2 * L, 2 * L), 1)
    strict = (si % L) > (sj % L)
    incl = (si % L) >= (sj % L)
    eye = jnp.where(si == sj, 1.0, 0.0)

    pairs = range(RW_PAIRS)
    cat0 = lambda *xs: jnp.concatenate(xs, axis=0)
    cat1 = lambda *xs: jnp.concatenate(xs, axis=1)
    bf = lambda x: x.astype(BF16)
    h = [state_ref[p] for p in pairs]
    ys = []
    for ci in range(NCH):
        a_s = [stack(at, ci, p) for p in pairs]
        r_s = [stack(rt, ci, p) for p in pairs]
        v_s = [stack(v, ci, p) for p in pairs]
        bk = [cat0(stack(bt, ci, p), stack(kt, ci, p)) for p in pairs]
        bkd = [cat0(stack(bd, ci, p), stack(kd, ci, p)) for p in pairs]
        aa = [_dot_nt(a_s[p], bk[p]) for p in pairs]
        rr = [_dot_nt(r_s[p], bk[p]) for p in pairs]
        a_ab = [jnp.where(strict, aa[p][:, 0:2 * L], 0.0) for p in pairs]
        a_ak = [bf(jnp.where(strict, aa[p][:, 2 * L:4 * L], 0.0)) for p in pairs]
        a_rb = [bf(jnp.where(incl, rr[p][:, 0:2 * L], 0.0)) for p in pairs]
        a_rk = [bf(jnp.where(incl, rr[p][:, 2 * L:4 * L], 0.0)) for p in pairs]
        ab = [bf(a) for a in a_ab]
        qpow = [_dot(ab[p], ab[p]) for p in pairs]
        inv = [eye + a_ab[p] for p in pairs]
        for _ in range(int(math.log2(L)) - 2):
            qb = [bf(q) for q in qpow]
            res = [_dot(qb[p], cat1(qb[p], bf(inv[p]))) for p in pairs]
            qpow = [res[p][:, 0:2 * L] for p in pairs]
            inv = [inv[p] + res[p][:, 2 * L:4 * L] for p in pairs]
        inv = [bf(inv[p] + _dot(bf(qpow[p]), bf(inv[p]))) for p in pairs]
        h_b = [bf(hp) for hp in h]
        x = [_dot(cat1(a_s[p], a_ak[p]), cat0(h_b[p], v_s[p])) for p in pairs]
        u_b = [bf(_dot(inv[p], bf(x[p]))) for p in pairs]
        y2 = [_dot(cat1(r_s[p], a_rb[p], a_rk[p]), cat0(h_b[p], u_b[p], v_s[p])) for p in pairs]
        w_col = [jnp.broadcast_to(w_end[ci][:, p * LANES:(p + 1) * LANES], (LANES, LANES)).T for p in pairs]
        h = [w_col[p] * h[p] + _dot_tn(bkd[p], cat0(u_b[p], v_s[p])) for p in pairs]
        ys.append(cat1(*[y2[p][0:L] + y2[p][L:2 * L] for p in pairs]))
    for p in pairs:
        state_ref[p] = h[p]
    y = cat0(*ys)

    mean = head_sum(y) * (1.0 / RW_N)
    yc = y - mean
    var = head_sum(yc * yc) * (1.0 / RW_N)
    y = yc * lax.rsqrt(var + LN_X_EPS) * lnw_ref[...] + lnb_ref[...]
    y = y + bonus * v
    o_ref[...] = (y * g).astype(o_ref.dtype)


def _rwkv(proj, mu_rkv, w0, a0, w2cat, k_k, k_a, r_k, ln_w, ln_b, b, s):
    tt = RW_TILE
    nc = s // tt
    row = lambda n: pl.BlockSpec((1, n), lambda bi, c: (0, 0))
    rkv_col0 = 3 * DA_WIDTH // RW_WIDTH
    lora_col0 = IN_COLS // LORA_PAD
    return pl.pallas_call(
        _rwkv_kernel,
        grid=(b, nc),
        in_specs=[
            pl.BlockSpec((tt, RW_WIDTH), lambda bi, c: (bi * nc + c, rkv_col0)),
            pl.BlockSpec((tt, RW_WIDTH), lambda bi, c: (bi * nc + c, rkv_col0 + 1)),
            pl.BlockSpec((tt, RW_WIDTH), lambda bi, c: (bi * nc + c, rkv_col0 + 2)),
            pl.BlockSpec((tt, LORA_PAD), lambda bi, c: (bi * nc + c, lora_col0)),
            pl.BlockSpec((tt, LORA_PAD), lambda bi, c: (bi * nc + c, lora_col0 + 1)),
            pl.BlockSpec((3, RW_WIDTH), lambda bi, c: (0, 0)),
            row(RW_WIDTH), row(RW_WIDTH),
            pl.BlockSpec((LORA_PAD, 3 * RW_WIDTH), lambda bi, c: (0, 0)),
            row(RW_WIDTH), row(RW_WIDTH), row(RW_WIDTH), row(RW_WIDTH), row(RW_WIDTH),
        ],
        out_specs=pl.BlockSpec((tt, RW_WIDTH), lambda bi, c: (bi * nc + c, 0)),
        out_shape=jax.ShapeDtypeStruct((b * s, RW_WIDTH), BF16),
        scratch_shapes=[
            pltpu.VMEM((RW_PAIRS, LANES, LANES), F32),
            pltpu.VMEM((1, 3 * RW_WIDTH), F32),
            pltpu.VMEM((1, LORA_PAD), F32),
        ],
        compiler_params=pltpu.CompilerParams(
            dimension_semantics=("parallel", "arbitrary"), vmem_limit_bytes=VMEM_LIMIT),
        name="rwkv",
    )(proj, proj, proj, proj, proj, mu_rkv, w0, a0, w2cat, k_k, k_a, r_k, ln_w, ln_b)


ROUTE_E, ROUTE_G, ROUTE_R = 0, 2, 4
EXPERT_LANE0 = N_GROUPS


def _outproj_kernel(x_ref, oda_ref, orw_ref, wout_ref, g_ref, wrh_ref, wrl_ref, br_ref,
                    h_ref, xn_ref, route_ref, cnt_ref, base_ref, *, tm):
    @pl.when(pl.program_id(0) == 0)
    def _():
        base_ref[...] = jnp.zeros_like(base_ref)

    mix = _dot(oda_ref[...], wout_ref[0:DA_WIDTH, :]) + _dot(orw_ref[...], wout_ref[DA_WIDTH:, :])
    h = x_ref[...] + mix
    h_ref[...] = h
    ms = jnp.mean(h * h, axis=-1, keepdims=True)
    xn = h * lax.rsqrt(ms + NORM_EPS) * g_ref[...]
    x_hi = xn.astype(BF16)
    xn_ref[...] = _pack_halves(x_hi)

    x_lo = (xn - x_hi.astype(F32)).astype(BF16)
    lg = (_dot(x_hi, wrh_ref[...]) + _dot(x_lo, wrh_ref[...]) + _dot(x_hi, wrl_ref[...])) + br_ref[...]

    lane = lax.broadcasted_iota(jnp.int32, (tm, LANES), 1)
    big = jnp.int32(1 << 20)
    gl = jnp.where(lane < N_GROUPS, lg, NEG)
    gmax = jnp.max(gl, axis=-1, keepdims=True)
    grp = jnp.min(jnp.where(gl == gmax, lane, big), axis=-1, keepdims=True)
    p_grp = 1.0 / jnp.sum(jnp.exp(gl - gmax), axis=-1, keepdims=True)
    eg = jnp.where(lane >= EXPERT_LANE0, (lane - EXPERT_LANE0) // EXPERTS_PER_GROUP, -1)
    el = jnp.where(eg == grp, lg, NEG)
    emax = jnp.max(el, axis=-1, keepdims=True)
    pe = jnp.exp(el - emax)
    probs = pe / jnp.sum(pe, axis=-1, keepdims=True)
    probs = jnp.where(eg == grp, probs, -1.0)
    p1 = jnp.max(probs, axis=-1, keepdims=True)
    i1 = jnp.min(jnp.where(probs == p1, lane, big), axis=-1, keepdims=True)
    probs2 = jnp.where(lane == i1, -1.0, probs)
    p2 = jnp.max(probs2, axis=-1, keepdims=True)
    i2 = jnp.min(jnp.where(probs2 == p2, lane, big), axis=-1, keepdims=True)
    gate1 = p_grp * p1 / (p1 + p2)
    gate2 = p_grp * p2 / (p1 + p2)

    oh1 = jnp.where(lane == i1, 1.0, 0.0)
    oh2 = jnp.where(lane == i2, 1.0, 0.0)
    ri = lax.broadcasted_iota(jnp.int32, (tm, tm), 0)
    rj = lax.broadcasted_iota(jnp.int32, (tm, tm), 1)
    lower = jnp.where(ri > rj, 1.0, 0.0).astype(BF16)
    base = base_ref[...]
    tot1 = jnp.sum(oh1, axis=0, keepdims=True)
    c1 = base + _dot(lower, oh1.astype(BF16))
    c2 = base + tot1 + _dot(lower, oh2.astype(BF16))
    rank1 = jnp.sum(oh1 * c1, axis=-1, keepdims=True)
    rank2 = jnp.sum(oh2 * c2, axis=-1, keepdims=True)
    base = base + tot1 + jnp.sum(oh2, axis=0, keepdims=True)
    base_ref[...] = base
    cnt_ref[...] = base

    e1 = (i1 - EXPERT_LANE0).astype(F32)
    e2 = (i2 - EXPERT_LANE0).astype(F32)
    rec = jnp.zeros((tm, LANES), F32)
    for ln, val in ((ROUTE_E, e1), (ROUTE_E + 1, e2), (ROUTE_G, gate1), (ROUTE_G + 1, gate2),
                    (ROUTE_R, rank1), (ROUTE_R + 1, rank2)):
        rec = jnp.where(lane == ln, val, rec)
    route_ref[...] = rec


def _outproj(x2, o_da, o_rw, w_out, g_ffn, wr_hi, wr_lo, b_r, tm=512):
    t = x2.shape[0]
    const = lambda shape: pl.BlockSpec(shape, lambda i: (0, 0))
    return pl.pallas_call(
        functools.partial(_outproj_kernel, tm=tm),
        grid=(t // tm,),
        in_specs=[
            pl.BlockSpec((tm, D_MODEL), lambda i: (i, 0)),
            pl.BlockSpec((tm, DA_WIDTH), lambda i: (i, 0)),
            pl.BlockSpec((tm, RW_WIDTH), lambda i: (i, 0)),
            const((D_MODEL, D_MODEL)),
            const((1, D_MODEL)),
            const((D_MODEL, LANES)),
            const((D_MODEL, LANES)),
            const((1, LANES)),
        ],
        out_specs=[
            pl.BlockSpec((tm, D_MODEL), lambda i: (i, 0)),
            pl.BlockSpec((tm, HALF), lambda i: (i, 0)),
            pl.BlockSpec((tm, LANES), lambda i: (i, 0)),
            const((1, LANES)),
        ],
        out_shape=[
            jax.ShapeDtypeStruct((t, D_MODEL), F32),
            jax.ShapeDtypeStruct((t, HALF), jnp.uint32),
            jax.ShapeDtypeStruct((t, LANES), F32),
            jax.ShapeDtypeStruct((1, LANES), F32),
        ],
        scratch_shapes=[pltpu.VMEM((1, LANES), F32)],
        compiler_params=pltpu.CompilerParams(
            dimension_semantics=("arbitrary",), vmem_limit_bytes=VMEM_LIMIT),
        name="outproj_router",
    )(x2, o_da, o_rw, w_out, g_ffn, wr_hi, wr_lo, b_r)


def _row_copy(src_hbm, dst_ref, src_row, dst_row, sem):
    return pltpu.make_async_copy(src_hbm.at[pl.ds(src_row, 1)], dst_ref.at[pl.ds(dst_row, 1)], sem)


def _dispatch_kernel(dest_ref, x_ref, xb_in_hbm, xb_hbm, sem, *, td):
    del xb_in_hbm
    t0 = pl.program_id(0) * td

    def start(i, _):
        tok = t0 + i
        _row_copy(x_ref, xb_hbm, i, dest_ref[2 * tok], sem).start(priority=0)
        _row_copy(x_ref, xb_hbm, i, dest_ref[2 * tok + 1], sem).start(priority=1)
        return 0

    lax.fori_loop(0, td, start, 0, unroll=8)
    for _ in range(2):
        pltpu.make_async_copy(x_ref, xb_hbm.at[pl.ds(0, td)], sem).wait()


def _dispatch(dest_flat, xn2, xb_zero, td=256):
    t = xn2.shape[0]
    return pl.pallas_call(
        functools.partial(_dispatch_kernel, td=td),
        grid_spec=pltpu.PrefetchScalarGridSpec(
            num_scalar_prefetch=1,
            grid=(t // td,),
            in_specs=[pl.BlockSpec((td, HALF), lambda i, d: (i, 0)), pl.BlockSpec(memory_space=pl.ANY)],
            out_specs=pl.BlockSpec(memory_space=pl.ANY),
            scratch_shapes=[pltpu.SemaphoreType.DMA],
        ),
        out_shape=jax.ShapeDtypeStruct(xb_zero.shape, xb_zero.dtype),
        input_output_aliases={2: 0},
        compiler_params=pltpu.CompilerParams(dimension_semantics=("arbitrary",)),
        name="dispatch",
    )(dest_flat, xn2, xb_zero)


def _experts_kernel(be_ref, nu_ref, xb_ref, wg_ref, wu_ref, wd_ref, yb_ref, wgb_ref, wub_ref, wdb_ref):
    i = pl.program_id(0)
    used = i < nu_ref[0]
    new_expert = jnp.logical_or(i == 0, be_ref[i] != be_ref[jnp.maximum(i - 1, 0)])

    @pl.when(jnp.logical_and(used, new_expert))
    def _():
        wgb_ref[...] = wg_ref[...].astype(BF16)
        wub_ref[...] = wu_ref[...].astype(BF16)
        wdb_ref[...] = wd_ref[...].astype(BF16)

    @pl.when(used)
    def _():
        x_lo, x_hi = (half.astype(BF16) for half in _unpack_halves(xb_ref[...]))
        hg = _dot(x_lo, wgb_ref[0:HALF, :]) + _dot(x_hi, wgb_ref[HALF:, :])
        hu = _dot(x_lo, wub_ref[0:HALF, :]) + _dot(x_hi, wub_ref[HALF:, :])
        hid = hg * _sigmoid(hg) * hu
        yb_ref[...] = _pack_halves(_dot(hid.astype(BF16), wdb_ref[...]).astype(BF16))

    @pl.when(jnp.logical_not(used))
    def _():
        yb_ref[...] = jnp.zeros_like(yb_ref)


def _experts(block_e, n_used, xb, w_gate, w_up, w_down):
    nb = xb.shape[0] // EXPERT_BLOCK
    rows = lambda i, be, nu: (jnp.minimum(i, nu[0] - 1), 0)
    return pl.pallas_call(
        _experts_kernel,
        grid_spec=pltpu.PrefetchScalarGridSpec(
            num_scalar_prefetch=2,
            grid=(nb,),
            in_specs=[
                pl.BlockSpec((EXPERT_BLOCK, HALF), rows),
                pl.BlockSpec((None, D_MODEL, D_EXPERT), lambda i, be, nu: (be[i], 0, 0)),
                pl.BlockSpec((None, D_MODEL, D_EXPERT), lambda i, be, nu: (be[i], 0, 0)),
                pl.BlockSpec((None, D_EXPERT, D_MODEL), lambda i, be, nu: (be[i], 0, 0)),
            ],
            out_specs=pl.BlockSpec((EXPERT_BLOCK, HALF), lambda i, be, nu: (i, 0)),
            scratch_shapes=[
                pltpu.VMEM((D_MODEL, D_EXPERT), BF16),
                pltpu.VMEM((D_MODEL, D_EXPERT), BF16),
                pltpu.VMEM((D_EXPERT, D_MODEL), BF16),
            ],
        ),
        out_shape=jax.ShapeDtypeStruct(xb.shape, jnp.uint32),
        compiler_params=pltpu.CompilerParams(
            dimension_semantics=("arbitrary",), vmem_limit_bytes=EXPERTS_VMEM_LIMIT),
        name="experts",
    )(block_e, n_used, xb, w_gate, w_up, w_down)


def _combine_kernel(dest_ref, route_ref, h_ref, g_ref, yb_hbm, o_ref, y1_ref, y2_ref, sem, *, tc):
    t0 = pl.program_id(0) * tc

    def start(i, _):
        tok = t0 + i
        _row_copy(yb_hbm, y1_ref, dest_ref[2 * tok], i, sem).start(priority=0)
        _row_copy(yb_hbm, y2_ref, dest_ref[2 * tok + 1], i, sem).start(priority=1)
        return 0

    lax.fori_loop(0, tc, start, 0, unroll=8)
    pltpu.make_async_copy(yb_hbm.at[pl.ds(0, tc)], y1_ref, sem).wait()
    pltpu.make_async_copy(yb_hbm.at[pl.ds(0, tc)], y2_ref, sem).wait()

    rec = route_ref[...]
    lane = lax.broadcasted_iota(jnp.int32, (tc, LANES), 1)
    g1 = jnp.sum(jnp.where(lane == ROUTE_G, rec, 0.0), axis=-1, keepdims=True)
    g2 = jnp.sum(jnp.where(lane == ROUTE_G + 1, rec, 0.0), axis=-1, keepdims=True)
    y1_lo, y1_hi = _unpack_halves(y1_ref[...])
    y2_lo, y2_hi = _unpack_halves(y2_ref[...])
    h_lo = h_ref[:, 0:HALF] + (y1_lo * g1 + y2_lo * g2)
    h_hi = h_ref[:, HALF:] + (y1_hi * g1 + y2_hi * g2)
    ms = (jnp.sum(h_lo * h_lo, axis=-1, keepdims=True)
          + jnp.sum(h_hi * h_hi, axis=-1, keepdims=True)) * (1.0 / D_MODEL)
    scale = lax.rsqrt(ms + NORM_EPS)
    o_ref[:, 0:HALF] = h_lo * scale * g_ref[:, 0:HALF]
    o_ref[:, HALF:] = h_hi * scale * g_ref[:, HALF:]


def _combine(dest_flat, route, h1, g_final, yb, tc=256):
    t = h1.shape[0]
    return pl.pallas_call(
        functools.partial(_combine_kernel, tc=tc),
        grid_spec=pltpu.PrefetchScalarGridSpec(
            num_scalar_prefetch=1,
            grid=(t // tc,),
            in_specs=[
                pl.BlockSpec((tc, LANES), lambda i, d: (i, 0)),
                pl.BlockSpec((tc, D_MODEL), lambda i, d: (i, 0)),
                pl.BlockSpec((1, D_MODEL), lambda i, d: (0, 0)),
                pl.BlockSpec(memory_space=pl.ANY),
            ],
            out_specs=pl.BlockSpec((tc, D_MODEL), lambda i, d: (i, 0)),
            scratch_shapes=[
                pltpu.VMEM((tc, HALF), jnp.uint32),
                pltpu.VMEM((tc, HALF), jnp.uint32),
                pltpu.SemaphoreType.DMA,
            ],
        ),
        out_shape=jax.ShapeDtypeStruct((t, D_MODEL), F32),
        compiler_params=pltpu.CompilerParams(
            dimension_semantics=("arbitrary",), vmem_limit_bytes=VMEM_LIMIT),
        name="combine",
    )(dest_flat, route, h1, g_final, yb)


def _pad_cols(w, n):
    return jnp.pad(w, ((0, 0), (0, n - w.shape[1])))


def kernel(x, g_mix, w_in, w_out, da_lambda_q1, da_lambda_k1, da_lambda_q2, da_lambda_k2, da_subln_g,
           rw_mu_x, rw_mu_rkv, rw_w0, rw_w1, rw_w2, rw_a0, rw_a1, rw_a2, rw_g1, rw_g2, rw_k_k, rw_k_a,
           rw_r_k, rw_ln_w, rw_ln_b, g_ffn, moe_w_group, moe_b_group, moe_w_expert, moe_b_expert,
           moe_w_gate, moe_w_up, moe_w_down, g_final):
    b, s, d = x.shape
    t = b * s
    x2 = x.reshape(t, d)

    mu = rw_mu_x[0]
    lora = (rw_w1[0], rw_a1[0], rw_g1[0])
    keep = _pad_cols(jnp.concatenate([(1.0 - mu[i])[:, None] * w for i, w in enumerate(lora)], axis=1), LORA_PAD)
    prev = _pad_cols(jnp.concatenate([mu[i][:, None] * w for i, w in enumerate(lora)], axis=1), LORA_PAD)
    w_all = jnp.concatenate([w_in[0], keep, prev], axis=1).astype(BF16)
    proj = _inproj(x2, g_mix, w_all)

    slopes = jnp.asarray([2.0 ** (-8.0 * (i + 1) / DA_HEADS) for i in range(DA_HEADS)], F32)
    lam4 = jnp.concatenate([da_lambda_q1, da_lambda_k1, da_lambda_q2, da_lambda_k2], axis=0)
    o_da = _diffattn(proj, slopes, lam4, da_subln_g.reshape(DA_V_DIM, 1), b, s)

    w2cat = jnp.zeros((LORA_PAD, 3 * RW_WIDTH), F32)
    w2cat = w2cat.at[0:LORA_W, 0:RW_WIDTH].set(rw_w2[0])
    w2cat = w2cat.at[LORA_W:LORA_W + LORA_A, RW_WIDTH:2 * RW_WIDTH].set(rw_a2[0])
    w2cat = w2cat.at[LORA_W + LORA_A:LORA_W + LORA_A + LORA_G, 2 * RW_WIDTH:].set(rw_g2[0])
    o_rw = _rwkv(proj, rw_mu_rkv[0], rw_w0, rw_a0, w2cat.astype(BF16), rw_k_k, rw_k_a,
                 rw_r_k.reshape(1, RW_WIDTH), rw_ln_w, rw_ln_b, b, s)

    w_r = _pad_cols(jnp.concatenate([moe_w_group[0], moe_w_expert[0]], axis=1), LANES)
    wr_hi = w_r.astype(BF16)
    wr_lo = (w_r - wr_hi.astype(F32)).astype(BF16)
    b_r = _pad_cols(jnp.concatenate([moe_b_group[0], moe_b_expert[0].reshape(-1)])[None, :], LANES)
    h1, xn2, route, cnt = _outproj(x2, o_da, o_rw, w_out[0].astype(BF16), g_ffn, wr_hi, wr_lo, b_r)

    counts = cnt[0, EXPERT_LANE0:EXPERT_LANE0 + N_EXPERTS].astype(jnp.int32)
    pcounts = ((counts + EXPERT_BLOCK - 1) // EXPERT_BLOCK) * EXPERT_BLOCK
    pends = jnp.cumsum(pcounts)
    pstarts = pends - pcounts
    e_idx = route[:, ROUTE_E:ROUTE_E + 2].astype(jnp.int32)
    rank = route[:, ROUTE_R:ROUTE_R + 2].astype(jnp.int32)
    dest = (pstarts[e_idx] + rank).reshape(-1)
    n_blocks = (2 * t) // EXPERT_BLOCK + N_EXPERTS
    block_row0 = jnp.arange(n_blocks, dtype=jnp.int32) * EXPERT_BLOCK
    block_e = jnp.minimum(jnp.sum(pends[None, :] <= block_row0[:, None], axis=1), N_EXPERTS - 1).astype(jnp.int32)
    n_used = (pends[-1:] // EXPERT_BLOCK).astype(jnp.int32)

    xb = _dispatch(dest, xn2, jnp.zeros((n_blocks * EXPERT_BLOCK, HALF), jnp.uint32))
    yb = _experts(block_e, n_used, xb, moe_w_gate[0], moe_w_up[0], moe_w_down[0])
    out = _combine(dest, route, h1, g_final[None, :], yb)
    return out.reshape(b, s, d)
```

```python
import functools
import math

import jax
import jax.numpy as jnp
from jax import lax
from jax.experimental import pallas as pl
from jax.experimental.pallas import tpu as pltpu

F32 = jnp.float32
BF16 = jnp.bfloat16

D_MODEL = 2048
DA_WIDTH = 1024
RW_WIDTH = 1024
DA_HEADS = 8
DA_V_DIM = 128
DA_QK_DIM = 64
RW_N = 64
RW_PAIRS = RW_WIDTH // 128
LORA_W, LORA_A, LORA_G = 64, 64, 160
LORA_PAD = 384
IN_COLS = 3 * DA_WIDTH + 3 * RW_WIDTH
NC = IN_COLS + 2 * LORA_PAD
N_GROUPS = 4
EXPERTS_PER_GROUP = 8
N_EXPERTS = 32
D_EXPERT = 512
NORM_EPS = 1e-6
LN_X_EPS = 64e-5
LAM_INIT = 0.8 - 0.6 * math.exp(-0.3 * 0)
LANES = 128
CHUNK = 64
RW_TILE = 256
NCH = RW_TILE // CHUNK
EXPERT_BLOCK = 256
NEG = -1e30
VMEM_LIMIT = 48 * 1024 * 1024
EXPERTS_VMEM_LIMIT = 56 * 1024 * 1024


def _dot(a, b):
    return jnp.dot(a, b, preferred_element_type=F32)


def _dot_nt(a, b):
    return lax.dot_general(a, b, (((1,), (1,)), ((), ())), preferred_element_type=F32)


def _dot_tn(a, b):
    return lax.dot_general(a, b, (((0,), (0,)), ((), ())), preferred_element_type=F32)


def _sigmoid(x):
    return 1.0 / (1.0 + jnp.exp(-x))


HALF = D_MODEL // 2


def _pack_halves(xb):
    lo = lax.bitcast_convert_type(xb[:, :HALF].astype(F32), jnp.uint32) >> 16
    hi = lax.bitcast_convert_type(xb[:, HALF:].astype(F32), jnp.uint32) & jnp.uint32(0xFFFF0000)
    return hi | lo


def _unpack_halves(u):
    lo = lax.bitcast_convert_type(u << 16, F32)
    hi = lax.bitcast_convert_type(u & jnp.uint32(0xFFFF0000), F32)
    return lo, hi


def _inproj_kernel(x_ref, g_ref, w_ref, o_ref, xn_ref):
    @pl.when(pl.program_id(1) == 0)
    def _():
        x = x_ref[...]
        ms = jnp.mean(x * x, axis=-1, keepdims=True)
        xn_ref[...] = (x * lax.rsqrt(ms + NORM_EPS) * g_ref[...]).astype(BF16)

    o_ref[...] = _dot(xn_ref[...], w_ref[...]).astype(o_ref.dtype)


def _inproj(x2, g, w_all, tm=1024, tn=768):
    t = x2.shape[0]
    tm = min(tm, t)
    return pl.pallas_call(
        _inproj_kernel,
        grid=(t // tm, NC // tn),
        in_specs=[
            pl.BlockSpec((tm, D_MODEL), lambda i, j: (i, 0)),
            pl.BlockSpec((1, D_MODEL), lambda i, j: (0, 0)),
            pl.BlockSpec((D_MODEL, tn), lambda i, j: (0, j)),
        ],
        out_specs=pl.BlockSpec((tm, tn), lambda i, j: (i, j)),
        out_shape=jax.ShapeDtypeStruct((t, NC), BF16),
        scratch_shapes=[pltpu.VMEM((tm, D_MODEL), BF16)],
        compiler_params=pltpu.CompilerParams(
            dimension_semantics=("parallel", "arbitrary"), vmem_limit_bytes=VMEM_LIMIT),
        name="inproj",
    )(x2, g, w_all)


def _da_kernel(qi_tab, kj_tab, slopes_ref, lam_ref, subg_ref, q_ref, k_ref, v_ref, o_ref,
               vt_ref, qz_ref, bias_ref, m_ref, l_ref, lfin_ref, acc_ref, s_ref, p_ref, alpha_ref,
               *, tq, nsub, seq):
    tk = tq
    nq = seq // tq
    n_trips = nq * (nq + 1) // 2
    sub = tk // nsub
    slope = slopes_ref[pl.program_id(1)]
    cu = [(c, u) for c in range(2) for u in range(nsub)]
    rows = lambda u: slice(u * sub, (u + 1) * sub)

    for blk in range(seq // LANES):
        r = slice(blk * LANES, (blk + 1) * LANES)
        vt_ref[:, r] = v_ref[r, :].astype(F32).T.astype(BF16)
    lane = lax.broadcasted_iota(jnp.int32, (tq, LANES), 1)
    for blk in range(nq):
        r = slice(blk * tq, (blk + 1) * tq)
        q = q_ref[r, :] * jnp.asarray(DA_QK_DIM ** -0.5, BF16)
        qz_ref[0, r, :] = jnp.where(lane < DA_QK_DIM, q, jnp.zeros_like(q))
        qz_ref[1, r, :] = jnp.where(lane >= DA_QK_DIM, q, jnp.zeros_like(q))
    krow = lax.broadcasted_iota(jnp.int32, (sub, tq), 0)
    qcol = lax.broadcasted_iota(jnp.int32, (sub, tq), 1)
    for u in range(nsub):
        alibi = slope * (krow + u * sub).astype(F32)
        bias_ref[0, rows(u), :] = alibi
        bias_ref[1, rows(u), :] = jnp.where(krow + u * sub <= qcol, alibi, NEG)

    m_ref[...] = jnp.full_like(m_ref, NEG)
    l_ref[...] = jnp.zeros_like(l_ref)
    lfin_ref[...] = jnp.ones_like(lfin_ref)
    acc_ref[...] = jnp.zeros_like(acc_ref)
    p_ref[...] = jnp.zeros_like(p_ref)
    alpha_ref[...] = jnp.zeros_like(alpha_ref)

    lv = lam_ref[...]
    lam = (jnp.exp(jnp.sum(lv[0:1] * lv[1:2], axis=-1, keepdims=True))
           - jnp.exp(jnp.sum(lv[2:3] * lv[3:4], axis=-1, keepdims=True)) + LAM_INIT)

    def scores_to_scratch(t):
        qs = pl.multiple_of(qi_tab[t] * tq, tq)
        ks = pl.multiple_of(kj_tab[t] * tk, tk)
        diag = (qi_tab[t] == kj_tab[t]).astype(jnp.int32)
        for c, u in cu:
            s_ref[c, rows(u), :] = (_dot_nt(k_ref[pl.ds(ks + u * sub, sub), :], qz_ref[c, pl.ds(qs, tq), :])
                                    + bias_ref[diag, rows(u), :])

    def values_from_scratch(t):
        ks = pl.multiple_of(kj_tab[t] * tk, tk)
        vt = vt_ref[:, pl.ds(ks, tk)]
        for c in range(2):
            acc_ref[c] = alpha_ref[c:c + 1, :] * acc_ref[c] + _dot(vt, p_ref[c])

    def softmax(t):
        first = kj_tab[t] == 0
        diag = kj_tab[t] == qi_tab[t]
        off = slope * ((kj_tab[t] - qi_tab[t]) * tk).astype(F32)
        s = {(c, u): s_ref[c, rows(u), :] for c, u in cu}
        m = [jnp.where(first, NEG, m_ref[c:c + 1, :]) for c in range(2)]
        l = [jnp.where(first, 0.0, l_ref[c:c + 1, :]) for c in range(2)]
        smax = [functools.reduce(jnp.maximum, [s[c, u] for u in range(nsub)]) for c in range(2)]
        m_new = [jnp.maximum(m[c], jnp.max(smax[c], axis=0, keepdims=True) + off) for c in range(2)]
        p = {(c, u): jnp.exp(s[c, u] - (m_new[c] - off)) for c, u in cu}
        psum = [functools.reduce(jnp.add, [p[c, u] for u in range(nsub)]) for c in range(2)]
        alpha = [jnp.exp(m[c] - m_new[c]) for c in range(2)]
        for c in range(2):
            l_new = alpha[c] * l[c] + jnp.sum(psum[c], axis=0, keepdims=True)
            m_ref[c:c + 1, :] = m_new[c]
            l_ref[c:c + 1, :] = l_new
            lfin_ref[c:c + 1, :] = jnp.where(diag, l_new, lfin_ref[c:c + 1, :])
        return p, alpha

    def publish(p, alpha):
        for c in range(2):
            alpha_ref[c:c + 1, :] = alpha[c]
        for c, u in cu:
            p_ref[c, rows(u), :] = p[c, u].astype(BF16)

    def finalize(t):
        qs = pl.multiple_of(qi_tab[t] * tq, tq)
        o = acc_ref[0] / lfin_ref[0:1, :] - lam * (acc_ref[1] / lfin_ref[1:2, :])
        ms = jnp.mean(o * o, axis=0, keepdims=True)
        o = o * lax.rsqrt(ms + NORM_EPS) * subg_ref[...] * (1.0 - LAM_INIT)
        o_ref[pl.ds(qs, tq), :] = o.T.astype(o_ref.dtype)

    scores_to_scratch(0)

    def body(t, _):
        prev = jnp.maximum(t - 1, 0)
        p, alpha = softmax(t)
        values_from_scratch(prev)
        scores_to_scratch(jnp.minimum(t + 1, n_trips - 1))
        publish(p, alpha)

        @pl.when(jnp.logical_and(t > 0, qi_tab[prev] == kj_tab[prev]))
        def _():
            finalize(prev)
        return 0

    lax.fori_loop(0, n_trips, body, 0)
    values_from_scratch(n_trips - 1)
    finalize(n_trips - 1)


def _diffattn(proj, slopes, lam4, subg_col, b, s, tq=256, nsub=2):
    nq = s // tq
    pairs = [(qi, kj) for qi in range(nq) for kj in range(qi + 1)]
    qi_tab = jnp.asarray([p[0] for p in pairs], jnp.int32)
    kj_tab = jnp.asarray([p[1] for p in pairs], jnp.int32)
    return pl.pallas_call(
        functools.partial(_da_kernel, tq=tq, nsub=nsub, seq=s),
        grid_spec=pltpu.PrefetchScalarGridSpec(
            num_scalar_prefetch=2,
            grid=(b, DA_HEADS),
            in_specs=[
                pl.BlockSpec(memory_space=pltpu.SMEM),
                pl.BlockSpec((4, DA_QK_DIM), lambda bi, h, *_: (0, 0)),
                pl.BlockSpec((DA_V_DIM, 1), lambda bi, h, *_: (0, 0)),
                pl.BlockSpec((s, LANES), lambda bi, h, *_: (bi, h)),
                pl.BlockSpec((s, LANES), lambda bi, h, *_: (bi, DA_HEADS + h)),
                pl.BlockSpec((s, LANES), lambda bi, h, *_: (bi, 2 * DA_HEADS + h)),
            ],
            out_specs=pl.BlockSpec((s, LANES), lambda bi, h, *_: (bi, h)),
            scratch_shapes=[
                pltpu.VMEM((DA_V_DIM, s), BF16),
                pltpu.VMEM((2, s, LANES), BF16),
                pltpu.VMEM((2, tq, tq), F32),
                pltpu.VMEM((2, tq), F32),
                pltpu.VMEM((2, tq), F32),
                pltpu.VMEM((2, tq), F32),
                pltpu.VMEM((2, DA_V_DIM, tq), F32),
                pltpu.VMEM((2, tq, tq), F32),
                pltpu.VMEM((2, tq, tq), BF16),
                pltpu.VMEM((2, tq), F32),
            ],
        ),
        out_shape=jax.ShapeDtypeStruct((b * s, DA_WIDTH), BF16),
        compiler_params=pltpu.CompilerParams(
            dimension_semantics=("parallel", "arbitrary"), vmem_limit_bytes=VMEM_LIMIT),
        name="diffattn",
    )(qi_tab, kj_tab, slopes, lam4, subg_col, proj, proj, proj)


def _rwkv_kernel(r_ref, k_ref, v_ref, p1_ref, p2_ref, mu_ref, w0_ref, a0_ref, w2_ref, kk_ref, ka_ref,
                 rk_ref, lnw_ref, lnb_ref, o_ref, state_ref, crkv_ref, cp2_ref):
    L, TT = CHUNK, RW_TILE
    c = pl.program_id(1)

    @pl.when(c == 0)
    def _():
        state_ref[...] = jnp.zeros_like(state_ref)
        crkv_ref[...] = jnp.zeros_like(crkv_ref)
        cp2_ref[...] = jnp.zeros_like(cp2_ref)

    first = lax.broadcasted_iota(jnp.int32, (TT, 1), 0) == 0

    def shift(x, carry):
        return jnp.where(first, carry, pltpu.roll(x, 1, 0))

    r_raw = r_ref[...].astype(F32)
    k_raw = k_ref[...].astype(F32)
    v_raw = v_ref[...].astype(F32)
    p2 = p2_ref[...].astype(F32)
    rs = shift(r_raw, crkv_ref[:, 0:RW_WIDTH])
    ks = shift(k_raw, crkv_ref[:, RW_WIDTH:2 * RW_WIDTH])
    vs = shift(v_raw, crkv_ref[:, 2 * RW_WIDTH:3 * RW_WIDTH])
    lin = p1_ref[...].astype(F32) + shift(p2, cp2_ref[...])
    crkv_ref[:, 0:RW_WIDTH] = r_raw[TT - 1:TT, :]
    crkv_ref[:, RW_WIDTH:2 * RW_WIDTH] = k_raw[TT - 1:TT, :]
    crkv_ref[:, 2 * RW_WIDTH:3 * RW_WIDTH] = v_raw[TT - 1:TT, :]
    cp2_ref[...] = p2[TT - 1:TT, :]

    r = r_raw + (rs - r_raw) * mu_ref[0:1, :]
    k = k_raw + (ks - k_raw) * mu_ref[1:2, :]
    v = v_raw + (vs - v_raw) * mu_ref[2:3, :]

    ll = lax.broadcasted_iota(jnp.int32, (TT, LORA_PAD), 1)
    z = jnp.where(ll < LORA_W, jnp.tanh(lin),
                  jnp.where(ll < LORA_W + LORA_A, lin,
                            jnp.where(ll < LORA_W + LORA_A + LORA_G, _sigmoid(lin), 0.0)))
    d = _dot(z.astype(BF16), w2_ref[...])
    logw = -_sigmoid(w0_ref[...] + d[:, 0:RW_WIDTH]) * math.exp(-0.5)
    a_sig = _sigmoid(a0_ref[...] + d[:, RW_WIDTH:2 * RW_WIDTH])
    g = d[:, 2 * RW_WIDTH:3 * RW_WIDTH]

    li = lax.broadcasted_iota(jnp.int32, (LANES, LANES), 0)
    lj = lax.broadcasted_iota(jnp.int32, (LANES, LANES), 1)
    same_head = jnp.where((li // RW_N) == (lj // RW_N), 1.0, 0.0).astype(BF16)

    def head_sum(x):
        xb = x.astype(BF16)
        return jnp.concatenate(
            [_dot(xb[:, p * LANES:(p + 1) * LANES], same_head) for p in range(RW_PAIRS)], axis=1)

    kk = k * kk_ref[...]
    kk = kk * lax.rsqrt(jnp.maximum(head_sum(kk * kk), 1e-24))
    k = k * (1.0 + (a_sig - 1.0) * ka_ref[...])
    bonus = head_sum(r * k * rk_ref[...])

    ti = lax.broadcasted_iota(jnp.int32, (TT, TT), 0)
    tj = lax.broadcasted_iota(jnp.int32, (TT, TT), 1)
    tri = jnp.where(ti >= tj, jnp.where((ti // L) == (tj // L), 1.0, 0.0), 0.0).astype(BF16)
    w_hi = logw.astype(BF16)
    w_r1 = logw - w_hi.astype(F32)
    w_mid = w_r1.astype(BF16)
    w_lo = (w_r1 - w_mid.astype(F32)).astype(BF16)
    cw = _dot(tri, w_hi) + _dot(tri, w_mid) + _dot(tri, w_lo)
    cw_last = [cw[(ci + 1) * L - 1:(ci + 1) * L, :] for ci in range(NCH)]
    e_in = jnp.exp(cw)
    e_out = jnp.exp(-cw)
    e_end = jnp.exp(jnp.concatenate([jnp.broadcast_to(cl, (L, RW_WIDTH)) for cl in cw_last], axis=0) - cw)
    w_end = [jnp.exp(cl) for cl in cw_last]

    at = -kk * jnp.exp(cw - logw)
    rt = r * e_in
    b = kk * a_sig
    bt = b * e_out
    kt = k * e_out
    bd = b * e_end
    kd = k * e_end

    lane = lax.broadcasted_iota(jnp.int32, (L, LANES), 1)
    lo = lane < RW_N

    def stack(x, ci, p):
        xs = x[ci * L:(ci + 1) * L, p * LANES:(p + 1) * LANES]
        return jnp.concatenate([jnp.where(lo, xs, 0.0), jnp.where(lo, 0.0, xs)], axis=0).astype(BF16)

    si = lax.broadcasted_iota(jnp.int32, (2 * L, 2 * L), 0)
    sj = lax.broadcasted_iota(jnp.int32, (2 * L, 2 * L), 1)
    strict = (si % L) > (sj % L)
    incl = (si % L) >= (sj % L)
    eye = jnp.where(si == sj, 1.0, 0.0)

    pairs = range(RW_PAIRS)
    cat0 = lambda *xs: jnp.concatenate(xs, axis=0)
    cat1 = lambda *xs: jnp.concatenate(xs, axis=1)
    bf = lambda x: x.astype(BF16)
    h = [state_ref[p] for p in pairs]
    ys = []
    for ci in range(NCH):
        a_s = [stack(at, ci, p) for p in pairs]
        r_s = [stack(rt, ci, p) for p in pairs]
        v_s = [stack(v, ci, p) for p in pairs]
        bk = [cat0(stack(bt, ci, p), stack(kt, ci, p)) for p in pairs]
        bkd = [cat0(stack(bd, ci, p), stack(kd, ci, p)) for p in pairs]
        aa = [_dot_nt(a_s[p], bk[p]) for p in pairs]
        rr = [_dot_nt(r_s[p], bk[p]) for p in pairs]
        a_ab = [jnp.where(strict, aa[p][:, 0:2 * L], 0.0) for p in pairs]
        a_ak = [bf(jnp.where(strict, aa[p][:, 2 * L:4 * L], 0.0)) for p in pairs]
        a_rb = [bf(jnp.where(incl, rr[p][:, 0:2 * L], 0.0)) for p in pairs]
        a_rk = [bf(jnp.where(incl, rr[p][:, 2 * L:4 * L], 0.0)) for p in pairs]
        ab = [bf(a) for a in a_ab]
        qpow = [_dot(ab[p], ab[p]) for p in pairs]
        inv = [eye + a_ab[p] for p in pairs]
        for _ in range(int(math.log2(L)) - 2):
            qb = [bf(q) for q in qpow]
            res = [_dot(qb[p], cat1(qb[p], bf(inv[p]))) for p in pairs]
            qpow = [res[p][:, 0:2 * L] for p in pairs]
            inv = [inv[p] + res[p][:, 2 * L:4 * L] for p in pairs]
        inv = [bf(inv[p] + _dot(bf(qpow[p]), bf(inv[p]))) for p in pairs]
        h_b = [bf(hp) for hp in h]
        x = [_dot(cat1(a_s[p], a_ak[p]), cat0(h_b[p], v_s[p])) for p in pairs]
        u_b = [bf(_dot(inv[p], bf(x[p]))) for p in pairs]
        y2 = [_dot(cat1(r_s[p], a_rb[p], a_rk[p]), cat0(h_b[p], u_b[p], v_s[p])) for p in pairs]
        w_col = [jnp.broadcast_to(w_end[ci][:, p * LANES:(p + 1) * LANES], (LANES, LANES)).T for p in pairs]
        h = [w_col[p] * h[p] + _dot_tn(bkd[p], cat0(u_b[p], v_s[p])) for p in pairs]
        ys.append(cat1(*[y2[p][0:L] + y2[p][L:2 * L] for p in pairs]))
    for p in pairs:
        state_ref[p] = h[p]
    y = cat0(*ys)

    mean = head_sum(y) * (1.0 / RW_N)
    yc = y - mean
    var = head_sum(yc * yc) * (1.0 / RW_N)
    y = yc * lax.rsqrt(var + LN_X_EPS) * lnw_ref[...] + lnb_ref[...]
    y = y + bonus * v
    o_ref[...] = (y * g).astype(o_ref.dtype)


def _rwkv(proj, mu_rkv, w0, a0, w2cat, k_k, k_a, r_k, ln_w, ln_b, b, s):
    tt = RW_TILE
    nc = s // tt
    row = lambda n: pl.BlockSpec((1, n), lambda bi, c: (0, 0))
    rkv_col0 = 3 * DA_WIDTH // RW_WIDTH
    lora_col0 = IN_COLS // LORA_PAD
    return pl.pallas_call(
        _rwkv_kernel,
        grid=(b, nc),
        in_specs=[
            pl.BlockSpec((tt, RW_WIDTH), lambda bi, c: (bi * nc + c, rkv_col0)),
            pl.BlockSpec((tt, RW_WIDTH), lambda bi, c: (bi * nc + c, rkv_col0 + 1)),
            pl.BlockSpec((tt, RW_WIDTH), lambda bi, c: (bi * nc + c, rkv_col0 + 2)),
            pl.BlockSpec((tt, LORA_PAD), lambda bi, c: (bi * nc + c, lora_col0)),
            pl.BlockSpec((tt, LORA_PAD), lambda bi, c: (bi * nc + c, lora_col0 + 1)),
            pl.BlockSpec((3, RW_WIDTH), lambda bi, c: (0, 0)),
            row(RW_WIDTH), row(RW_WIDTH),
            pl.BlockSpec((LORA_PAD, 3 * RW_WIDTH), lambda bi, c: (0, 0)),
            row(RW_WIDTH), row(RW_WIDTH), row(RW_WIDTH), row(RW_WIDTH), row(RW_WIDTH),
        ],
        out_specs=pl.BlockSpec((tt, RW_WIDTH), lambda bi, c: (bi * nc + c, 0)),
        out_shape=jax.ShapeDtypeStruct((b * s, RW_WIDTH), BF16),
        scratch_shapes=[
            pltpu.VMEM((RW_PAIRS, LANES, LANES), F32),
            pltpu.VMEM((1, 3 * RW_WIDTH), F32),
            pltpu.VMEM((1, LORA_PAD), F32),
        ],
        compiler_params=pltpu.CompilerParams(
            dimension_semantics=("parallel", "arbitrary"), vmem_limit_bytes=VMEM_LIMIT),
        name="rwkv",
    )(proj, proj, proj, proj, proj, mu_rkv, w0, a0, w2cat, k_k, k_a, r_k, ln_w, ln_b)


ROUTE_E, ROUTE_G, ROUTE_R = 0, 2, 4
EXPERT_LANE0 = N_GROUPS


def _outproj_kernel(x_ref, oda_ref, orw_ref, wout_ref, g_ref, wrh_ref, wrl_ref, br_ref,
                    h_ref, xn_ref, route_ref, cnt_ref, base_ref, *, tm):
    @pl.when(pl.program_id(0) == 0)
    def _():
        base_ref[...] = jnp.zeros_like(base_ref)

    mix = _dot(oda_ref[...], wout_ref[0:DA_WIDTH, :]) + _dot(orw_ref[...], wout_ref[DA_WIDTH:, :])
    h = x_ref[...] + mix
    h_ref[...] = h
    ms = jnp.mean(h * h, axis=-1, keepdims=True)
    xn = h * lax.rsqrt(ms + NORM_EPS) * g_ref[...]
    x_hi = xn.astype(BF16)
    xn_ref[...] = _pack_halves(x_hi)

    x_lo = (xn - x_hi.astype(F32)).astype(BF16)
    lg = (_dot(x_hi, wrh_ref[...]) + _dot(x_lo, wrh_ref[...]) + _dot(x_hi, wrl_ref[...])) + br_ref[...]

    lane = lax.broadcasted_iota(jnp.int32, (tm, LANES), 1)
    big = jnp.int32(1 << 20)
    gl = jnp.where(lane < N_GROUPS, lg, NEG)
    gmax = jnp.max(gl, axis=-1, keepdims=True)
    grp = jnp.min(jnp.where(gl == gmax, lane, big), axis=-1, keepdims=True)
    p_grp = 1.0 / jnp.sum(jnp.exp(gl - gmax), axis=-1, keepdims=True)
    eg = jnp.where(lane >= EXPERT_LANE0, (lane - EXPERT_LANE0) // EXPERTS_PER_GROUP, -1)
    el = jnp.where(eg == grp, lg, NEG)
    emax = jnp.max(el, axis=-1, keepdims=True)
    pe = jnp.exp(el - emax)
    probs = pe / jnp.sum(pe, axis=-1, keepdims=True)
    probs = jnp.where(eg == grp, probs, -1.0)
    p1 = jnp.max(probs, axis=-1, keepdims=True)
    i1 = jnp.min(jnp.where(probs == p1, lane, big), axis=-1, keepdims=True)
    probs2 = jnp.where(lane == i1, -1.0, probs)
    p2 = jnp.max(probs2, axis=-1, keepdims=True)
    i2 = jnp.min(jnp.where(probs2 == p2, lane, big), axis=-1, keepdims=True)
    gate1 = p_grp * p1 / (p1 + p2)
    gate2 = p_grp * p2 / (p1 + p2)

    oh1 = jnp.where(lane == i1, 1.0, 0.0)
    oh2 = jnp.where(lane == i2, 1.0, 0.0)
    ri = lax.broadcasted_iota(jnp.int32, (tm, tm), 0)
    rj = lax.broadcasted_iota(jnp.int32, (tm, tm), 1)
    lower = jnp.where(ri > rj, 1.0, 0.0).astype(BF16)
    base = base_ref[...]
    tot1 = jnp.sum(oh1, axis=0, keepdims=True)
    c1 = base + _dot(lower, oh1.astype(BF16))
    c2 = base + tot1 + _dot(lower, oh2.astype(BF16))
    rank1 = jnp.sum(oh1 * c1, axis=-1, keepdims=True)
    rank2 = jnp.sum(oh2 * c2, axis=-1, keepdims=True)
    base = base + tot1 + jnp.sum(oh2, axis=0, keepdims=True)
    base_ref[...] = base
    cnt_ref[...] = base

    e1 = (i1 - EXPERT_LANE0).astype(F32)
    e2 = (i2 - EXPERT_LANE0).astype(F32)
    rec = jnp.zeros((tm, LANES), F32)
    for ln, val in ((ROUTE_E, e1), (ROUTE_E + 1, e2), (ROUTE_G, gate1), (ROUTE_G + 1, gate2),
                    (ROUTE_R, rank1), (ROUTE_R + 1, rank2)):
        rec = jnp.where(lane == ln, val, rec)
    route_ref[...] = rec


def _outproj(x2, o_da, o_rw, w_out, g_ffn, wr_hi, wr_lo, b_r, tm=512):
    t = x2.shape[0]
    const = lambda shape: pl.BlockSpec(shape, lambda i: (0, 0))
    return pl.pallas_call(
        functools.partial(_outproj_kernel, tm=tm),
        grid=(t // tm,),
        in_specs=[
            pl.BlockSpec((tm, D_MODEL), lambda i: (i, 0)),
            pl.BlockSpec((tm, DA_WIDTH), lambda i: (i, 0)),
            pl.BlockSpec((tm, RW_WIDTH), lambda i: (i, 0)),
            const((D_MODEL, D_MODEL)),
            const((1, D_MODEL)),
            const((D_MODEL, LANES)),
            const((D_MODEL, LANES)),
            const((1, LANES)),
        ],
        out_specs=[
            pl.BlockSpec((tm, D_MODEL), lambda i: (i, 0)),
            pl.BlockSpec((tm, HALF), lambda i: (i, 0)),
            pl.BlockSpec((tm, LANES), lambda i: (i, 0)),
            const((1, LANES)),
        ],
        out_shape=[
            jax.ShapeDtypeStruct((t, D_MODEL), F32),
            jax.ShapeDtypeStruct((t, HALF), jnp.uint32),
            jax.ShapeDtypeStruct((t, LANES), F32),
            jax.ShapeDtypeStruct((1, LANES), F32),
        ],
        scratch_shapes=[pltpu.VMEM((1, LANES), F32)],
        compiler_params=pltpu.CompilerParams(
            dimension_semantics=("arbitrary",), vmem_limit_bytes=VMEM_LIMIT),
        name="outproj_router",
    )(x2, o_da, o_rw, w_out, g_ffn, wr_hi, wr_lo, b_r)


def _row_copy(src_hbm, dst_ref, src_row, dst_row, sem):
    return pltpu.make_async_copy(src_hbm.at[pl.ds(src_row, 1)], dst_ref.at[pl.ds(dst_row, 1)], sem)


def _dispatch_kernel(dest_ref, x_ref, xb_in_hbm, xb_hbm, sem, *, td):
    del xb_in_hbm
    t0 = pl.program_id(0) * td

    def start(i, _):
        tok = t0 + i
        _row_copy(x_ref, xb_hbm, i, dest_ref[2 * tok], sem).start(priority=0)
        _row_copy(x_ref, xb_hbm, i, dest_ref[2 * tok + 1], sem).start(priority=1)
        return 0

    lax.fori_loop(0, td, start, 0, unroll=8)
    for _ in range(2):
        pltpu.make_async_copy(x_ref, xb_hbm.at[pl.ds(0, td)], sem).wait()


def _dispatch(dest_flat, xn2, xb_zero, td=256):
    t = xn2.shape[0]
    return pl.pallas_call(
        functools.partial(_dispatch_kernel, td=td),
        grid_spec=pltpu.PrefetchScalarGridSpec(
            num_scalar_prefetch=1,
            grid=(t // td,),
            in_specs=[pl.BlockSpec((td, HALF), lambda i, d: (i, 0)), pl.BlockSpec(memory_space=pl.ANY)],
            out_specs=pl.BlockSpec(memory_space=pl.ANY),
            scratch_shapes=[pltpu.SemaphoreType.DMA],
        ),
        out_shape=jax.ShapeDtypeStruct(xb_zero.shape, xb_zero.dtype),
        input_output_aliases={2: 0},
        compiler_params=pltpu.CompilerParams(dimension_semantics=("arbitrary",)),
        name="dispatch",
    )(dest_flat, xn2, xb_zero)


def _experts_kernel(be_ref, nu_ref, xb_ref, wg_ref, wu_ref, wd_ref, yb_ref, wgb_ref, wub_ref, wdb_ref):
    i = pl.program_id(0)
    used = i < nu_ref[0]
    new_expert = jnp.logical_or(i == 0, be_ref[i] != be_ref[jnp.maximum(i - 1, 0)])

    def swiglu(wg_lo, wg_hi, wu_lo, wu_hi, wd):
        x_lo, x_hi = (half.astype(BF16) for half in _unpack_halves(xb_ref[...]))
        hg = _dot(x_lo, wg_lo) + _dot(x_hi, wg_hi)
        hu = _dot(x_lo, wu_lo) + _dot(x_hi, wu_hi)
        hid = hg * _sigmoid(hg) * hu
        yb_ref[...] = _pack_halves(_dot(hid.astype(BF16), wd).astype(BF16))

    @pl.when(jnp.logical_and(used, new_expert))
    def _():
        wg, wu, wd = (w[...].astype(BF16) for w in (wg_ref, wu_ref, wd_ref))
        wgb_ref[...] = wg
        wub_ref[...] = wu
        wdb_ref[...] = wd
        swiglu(wg[0:HALF, :], wg[HALF:, :], wu[0:HALF, :], wu[HALF:, :], wd)

    @pl.when(jnp.logical_and(used, jnp.logical_not(new_expert)))
    def _():
        swiglu(wgb_ref[0:HALF, :], wgb_ref[HALF:, :], wub_ref[0:HALF, :], wub_ref[HALF:, :], wdb_ref[...])

    @pl.when(jnp.logical_not(used))
    def _():
        yb_ref[...] = jnp.zeros_like(yb_ref)


def _experts(block_e, n_used, xb, w_gate, w_up, w_down):
    nb = xb.shape[0] // EXPERT_BLOCK
    rows = lambda i, be, nu: (jnp.minimum(i, nu[0] - 1), 0)
    return pl.pallas_call(
        _experts_kernel,
        grid_spec=pltpu.PrefetchScalarGridSpec(
            num_scalar_prefetch=2,
            grid=(nb,),
            in_specs=[
                pl.BlockSpec((EXPERT_BLOCK, HALF), rows),
                pl.BlockSpec((None, D_MODEL, D_EXPERT), lambda i, be, nu: (be[i], 0, 0)),
                pl.BlockSpec((None, D_MODEL, D_EXPERT), lambda i, be, nu: (be[i], 0, 0)),
                pl.BlockSpec((None, D_EXPERT, D_MODEL), lambda i, be, nu: (be[i], 0, 0)),
            ],
            out_specs=pl.BlockSpec((EXPERT_BLOCK, HALF), lambda i, be, nu: (i, 0)),
            scratch_shapes=[
                pltpu.VMEM((D_MODEL, D_EXPERT), BF16),
                pltpu.VMEM((D_MODEL, D_EXPERT), BF16),
                pltpu.VMEM((D_EXPERT, D_MODEL), BF16),
            ],
        ),
        out_shape=jax.ShapeDtypeStruct(xb.shape, jnp.uint32),
        compiler_params=pltpu.CompilerParams(
            dimension_semantics=("arbitrary",), vmem_limit_bytes=EXPERTS_VMEM_LIMIT),
        name="experts",
    )(block_e, n_used, xb, w_gate, w_up, w_down)


def _combine_kernel(dest_ref, route_ref, h_ref, g_ref, yb_hbm, o_ref, y1_ref, y2_ref, sem, *, tc):
    t0 = pl.program_id(0) * tc

    def start(i, _):
        tok = t0 + i
        _row_copy(yb_hbm, y1_ref, dest_ref[2 * tok], i, sem).start(priority=0)
        _row_copy(yb_hbm, y2_ref, dest_ref[2 * tok + 1], i, sem).start(priority=1)
        return 0

    lax.fori_loop(0, tc, start, 0, unroll=8)
    pltpu.make_async_copy(yb_hbm.at[pl.ds(0, tc)], y1_ref, sem).wait()
    pltpu.make_async_copy(yb_hbm.at[pl.ds(0, tc)], y2_ref, sem).wait()

    rec = route_ref[...]
    lane = lax.broadcasted_iota(jnp.int32, (tc, LANES), 1)
    g1 = jnp.sum(jnp.where(lane == ROUTE_G, rec, 0.0), axis=-1, keepdims=True)
    g2 = jnp.sum(jnp.where(lane == ROUTE_G + 1, rec, 0.0), axis=-1, keepdims=True)
    y1_lo, y1_hi = _unpack_halves(y1_ref[...])
    y2_lo, y2_hi = _unpack_halves(y2_ref[...])
    h_lo = h_ref[:, 0:HALF] + (y1_lo * g1 + y2_lo * g2)
    h_hi = h_ref[:, HALF:] + (y1_hi * g1 + y2_hi * g2)
    ms = (jnp.sum(h_lo * h_lo, axis=-1, keepdims=True)
          + jnp.sum(h_hi * h_hi, axis=-1, keepdims=True)) * (1.0 / D_MODEL)
    scale = lax.rsqrt(ms + NORM_EPS)
    o_ref[:, 0:HALF] = h_lo * scale * g_ref[:, 0:HALF]
    o_ref[:, HALF:] = h_hi * scale * g_ref[:, HALF:]


def _combine(dest_flat, route, h1, g_final, yb, tc=256):
    t = h1.shape[0]
    return pl.pallas_call(
        functools.partial(_combine_kernel, tc=tc),
        grid_spec=pltpu.PrefetchScalarGridSpec(
            num_scalar_prefetch=1,
            grid=(t // tc,),
            in_specs=[
                pl.BlockSpec((tc, LANES), lambda i, d: (i, 0)),
                pl.BlockSpec((tc, D_MODEL), lambda i, d: (i, 0)),
                pl.BlockSpec((1, D_MODEL), lambda i, d: (0, 0)),
                pl.BlockSpec(memory_space=pl.ANY),
            ],
            out_specs=pl.BlockSpec((tc, D_MODEL), lambda i, d: (i, 0)),
            scratch_shapes=[
                pltpu.VMEM((tc, HALF), jnp.uint32),
                pltpu.VMEM((tc, HALF), jnp.uint32),
                pltpu.SemaphoreType.DMA,
            ],
        ),
        out_shape=jax.ShapeDtypeStruct((t, D_MODEL), F32),
        compiler_params=pltpu.CompilerParams(
            dimension_semantics=("arbitrary",), vmem_limit_bytes=VMEM_LIMIT),
        name="combine",
    )(dest_flat, route, h1, g_final, yb)


def _pad_cols(w, n):
    return jnp.pad(w, ((0, 0), (0, n - w.shape[1])))


def kernel(x, g_mix, w_in, w_out, da_lambda_q1, da_lambda_k1, da_lambda_q2, da_lambda_k2, da_subln_g,
           rw_mu_x, rw_mu_rkv, rw_w0, rw_w1, rw_w2, rw_a0, rw_a1, rw_a2, rw_g1, rw_g2, rw_k_k, rw_k_a,
           rw_r_k, rw_ln_w, rw_ln_b, g_ffn, moe_w_group, moe_b_group, moe_w_expert, moe_b_expert,
           moe_w_gate, moe_w_up, moe_w_down, g_final):
    b, s, d = x.shape
    t = b * s
    x2 = x.reshape(t, d)

    mu = rw_mu_x[0]
    lora = (rw_w1[0], rw_a1[0], rw_g1[0])
    keep = _pad_cols(jnp.concatenate([(1.0 - mu[i])[:, None] * w for i, w in enumerate(lora)], axis=1), LORA_PAD)
    prev = _pad_cols(jnp.concatenate([mu[i][:, None] * w for i, w in enumerate(lora)], axis=1), LORA_PAD)
    w_all = jnp.concatenate([w_in[0], keep, prev], axis=1).astype(BF16)
    proj = _inproj(x2, g_mix, w_all)

    slopes = jnp.asarray([2.0 ** (-8.0 * (i + 1) / DA_HEADS) for i in range(DA_HEADS)], F32)
    lam4 = jnp.concatenate([da_lambda_q1, da_lambda_k1, da_lambda_q2, da_lambda_k2], axis=0)
    o_da = _diffattn(proj, slopes, lam4, da_subln_g.reshape(DA_V_DIM, 1), b, s)

    w2cat = jnp.zeros((LORA_PAD, 3 * RW_WIDTH), F32)
    w2cat = w2cat.at[0:LORA_W, 0:RW_WIDTH].set(rw_w2[0])
    w2cat = w2cat.at[LORA_W:LORA_W + LORA_A, RW_WIDTH:2 * RW_WIDTH].set(rw_a2[0])
    w2cat = w2cat.at[LORA_W + LORA_A:LORA_W + LORA_A + LORA_G, 2 * RW_WIDTH:].set(rw_g2[0])
    o_rw = _rwkv(proj, rw_mu_rkv[0], rw_w0, rw_a0, w2cat.astype(BF16), rw_k_k, rw_k_a,
                 rw_r_k.reshape(1, RW_WIDTH), rw_ln_w, rw_ln_b, b, s)

    w_r = _pad_cols(jnp.concatenate([moe_w_group[0], moe_w_expert[0]], axis=1), LANES)
    wr_hi = w_r.astype(BF16)
    wr_lo = (w_r - wr_hi.astype(F32)).astype(BF16)
    b_r = _pad_cols(jnp.concatenate([moe_b_group[0], moe_b_expert[0].reshape(-1)])[None, :], LANES)
    h1, xn2, route, cnt = _outproj(x2, o_da, o_rw, w_out[0].astype(BF16), g_ffn, wr_hi, wr_lo, b_r)

    counts = cnt[0, EXPERT_LANE0:EXPERT_LANE0 + N_EXPERTS].astype(jnp.int32)
    pcounts = ((counts + EXPERT_BLOCK - 1) // EXPERT_BLOCK) * EXPERT_BLOCK
    pends = jnp.cumsum(pcounts)
    pstarts = pends - pcounts
    e_idx = route[:, ROUTE_E:ROUTE_E + 2].astype(jnp.int32)
    rank = route[:, ROUTE_R:ROUTE_R + 2].astype(jnp.int32)
    expert_ids = jnp.arange(N_EXPERTS, dtype=jnp.int32)
    row0 = jnp.sum(jnp.where(e_idx[..., None] == expert_ids, pstarts, 0), axis=-1)
    dest = (row0 + rank).reshape(-1)
    n_blocks = (2 * t) // EXPERT_BLOCK + N_EXPERTS
    block_row0 = jnp.arange(n_blocks, dtype=jnp.int32) * EXPERT_BLOCK
    block_e = jnp.minimum(jnp.sum(pends[None, :] <= block_row0[:, None], axis=1), N_EXPERTS - 1).astype(jnp.int32)
    n_used = (pends[-1:] // EXPERT_BLOCK).astype(jnp.int32)

    xb = _dispatch(dest, xn2, jnp.zeros((n_blocks * EXPERT_BLOCK, HALF), jnp.uint32))
    yb = _experts(block_e, n_used, xb, moe_w_gate[0], moe_w_up[0], moe_w_down[0])
    out = _combine(dest, route, h1, g_final[None, :], yb)
    return out.reshape(b, s, d)
```

```python
import functools
import math

import jax
import jax.numpy as jnp
from jax import lax
from jax.experimental import pallas as pl
from jax.experimental.pallas import tpu as pltpu

F32 = jnp.float32
BF16 = jnp.bfloat16

D_MODEL = 2048
DA_WIDTH = 1024
RW_WIDTH = 1024
DA_HEADS = 8
DA_V_DIM = 128
DA_QK_DIM = 64
RW_N = 64
RW_PAIRS = RW_WIDTH // 128
LORA_W, LORA_A, LORA_G = 64, 64, 160
LORA_PAD = 384
IN_COLS = 3 * DA_WIDTH + 3 * RW_WIDTH
NC = IN_COLS + 2 * LORA_PAD
N_GROUPS = 4
EXPERTS_PER_GROUP = 8
N_EXPERTS = 32
D_EXPERT = 512
NORM_EPS = 1e-6
LN_X_EPS = 64e-5
LAM_INIT = 0.8 - 0.6 * math.exp(-0.3 * 0)
LANES = 128
CHUNK = 64
RW_TILE = 256
NCH = RW_TILE // CHUNK
EXPERT_BLOCK = 256
NEG = -1e30
VMEM_LIMIT = 48 * 1024 * 1024
EXPERTS_VMEM_LIMIT = 56 * 1024 * 1024


def _dot(a, b):
    return jnp.dot(a, b, preferred_element_type=F32)


def _dot_nt(a, b):
    return lax.dot_general(a, b, (((1,), (1,)), ((), ())), preferred_element_type=F32)


def _dot_tn(a, b):
    return lax.dot_general(a, b, (((0,), (0,)), ((), ())), preferred_element_type=F32)


def _sigmoid(x):
    return 1.0 / (1.0 + jnp.exp(-x))


HALF = D_MODEL // 2


def _pack_halves(xb):
    lo = lax.bitcast_convert_type(xb[:, :HALF].astype(F32), jnp.uint32) >> 16
    hi = lax.bitcast_convert_type(xb[:, HALF:].astype(F32), jnp.uint32) & jnp.uint32(0xFFFF0000)
    return hi | lo


def _unpack_halves(u):
    lo = lax.bitcast_convert_type(u << 16, F32)
    hi = lax.bitcast_convert_type(u & jnp.uint32(0xFFFF0000), F32)
    return lo, hi


SUBLANES = 8
assert HALF == SUBLANES * LANES


def _store_row_tiles(ref, u):
    n = u.shape[0]
    for k in range(SUBLANES):
        ref[pl.ds(k, n, stride=SUBLANES), :] = u[:, k * LANES:(k + 1) * LANES]


def _load_row_tiles(ref):
    n = ref.shape[0] // SUBLANES
    return jnp.concatenate([ref[pl.ds(k, n, stride=SUBLANES), :] for k in range(SUBLANES)], axis=1)


def _inproj_kernel(x_ref, g_ref, w_ref, o_ref, xn_ref):
    @pl.when(pl.program_id(1) == 0)
    def _():
        x = x_ref[...]
        ms = jnp.mean(x * x, axis=-1, keepdims=True)
        xn_ref[...] = (x * lax.rsqrt(ms + NORM_EPS) * g_ref[...]).astype(BF16)

    o_ref[...] = _dot(xn_ref[...], w_ref[...]).astype(o_ref.dtype)


def _inproj(x2, g, w_all, tm=1024, tn=768):
    t = x2.shape[0]
    tm = min(tm, t)
    return pl.pallas_call(
        _inproj_kernel,
        grid=(t // tm, NC // tn),
        in_specs=[
            pl.BlockSpec((tm, D_MODEL), lambda i, j: (i, 0)),
            pl.BlockSpec((1, D_MODEL), lambda i, j: (0, 0)),
            pl.BlockSpec((D_MODEL, tn), lambda i, j: (0, j)),
        ],
        out_specs=pl.BlockSpec((tm, tn), lambda i, j: (i, j)),
        out_shape=jax.ShapeDtypeStruct((t, NC), BF16),
        scratch_shapes=[pltpu.VMEM((tm, D_MODEL), BF16)],
        compiler_params=pltpu.CompilerParams(
            dimension_semantics=("parallel", "arbitrary"), vmem_limit_bytes=VMEM_LIMIT),
        name="inproj",
    )(x2, g, w_all)


def _da_kernel(qi_tab, kj_tab, slopes_ref, lam_ref, subg_ref, q_ref, k_ref, v_ref, o_ref,
               vt_ref, qz_ref, bias_ref, m_ref, l_ref, lfin_ref, acc_ref, s_ref, p_ref, alpha_ref,
               *, tq, nsub, seq):
    tk = tq
    nq = seq // tq
    n_trips = nq * (nq + 1) // 2
    sub = tk // nsub
    slope = slopes_ref[pl.program_id(1)]
    cu = [(c, u) for c in range(2) for u in range(nsub)]
    rows = lambda u: slice(u * sub, (u + 1) * sub)

    for blk in range(seq // LANES):
        r = slice(blk * LANES, (blk + 1) * LANES)
        vt_ref[:, r] = v_ref[r, :].astype(F32).T.astype(BF16)
    lane = lax.broadcasted_iota(jnp.int32, (tq, LANES), 1)
    for blk in range(nq):
        r = slice(blk * tq, (blk + 1) * tq)
        q = q_ref[r, :] * jnp.asarray(DA_QK_DIM ** -0.5, BF16)
        qz_ref[0, r, :] = jnp.where(lane < DA_QK_DIM, q, jnp.zeros_like(q))
        qz_ref[1, r, :] = jnp.where(lane >= DA_QK_DIM, q, jnp.zeros_like(q))
    krow = lax.broadcasted_iota(jnp.int32, (sub, tq), 0)
    qcol = lax.broadcasted_iota(jnp.int32, (sub, tq), 1)
    for u in range(nsub):
        alibi = slope * (krow + u * sub).astype(F32)
        bias_ref[0, rows(u), :] = alibi
        bias_ref[1, rows(u), :] = jnp.where(krow + u * sub <= qcol, alibi, NEG)

    m_ref[...] = jnp.full_like(m_ref, NEG)
    l_ref[...] = jnp.zeros_like(l_ref)
    lfin_ref[...] = jnp.ones_like(lfin_ref)
    acc_ref[...] = jnp.zeros_like(acc_ref)
    p_ref[...] = jnp.zeros_like(p_ref)
    alpha_ref[...] = jnp.zeros_like(alpha_ref)

    lv = lam_ref[...]
    lam = (jnp.exp(jnp.sum(lv[0:1] * lv[1:2], axis=-1, keepdims=True))
           - jnp.exp(jnp.sum(lv[2:3] * lv[3:4], axis=-1, keepdims=True)) + LAM_INIT)

    def scores_to_scratch(t):
        qs = pl.multiple_of(qi_tab[t] * tq, tq)
        ks = pl.multiple_of(kj_tab[t] * tk, tk)
        diag = (qi_tab[t] == kj_tab[t]).astype(jnp.int32)
        for c, u in cu:
            s_ref[c, rows(u), :] = (_dot_nt(k_ref[pl.ds(ks + u * sub, sub), :], qz_ref[c, pl.ds(qs, tq), :])
                                    + bias_ref[diag, rows(u), :])

    def values_from_scratch(t):
        ks = pl.multiple_of(kj_tab[t] * tk, tk)
        vt = vt_ref[:, pl.ds(ks, tk)]
        for c in range(2):
            acc_ref[c] = alpha_ref[c:c + 1, :] * acc_ref[c] + _dot(vt, p_ref[c])

    def softmax(t):
        first = kj_tab[t] == 0
        diag = kj_tab[t] == qi_tab[t]
        off = slope * ((kj_tab[t] - qi_tab[t]) * tk).astype(F32)
        s = {(c, u): s_ref[c, rows(u), :] for c, u in cu}
        m = [jnp.where(first, NEG, m_ref[c:c + 1, :]) for c in range(2)]
        l = [jnp.where(first, 0.0, l_ref[c:c + 1, :]) for c in range(2)]
        smax = [functools.reduce(jnp.maximum, [s[c, u] for u in range(nsub)]) for c in range(2)]
        m_new = [jnp.maximum(m[c], jnp.max(smax[c], axis=0, keepdims=True) + off) for c in range(2)]
        p = {(c, u): jnp.exp(s[c, u] - (m_new[c] - off)) for c, u in cu}
        psum = [functools.reduce(jnp.add, [p[c, u] for u in range(nsub)]) for c in range(2)]
        alpha = [jnp.exp(m[c] - m_new[c]) for c in range(2)]
        for c in range(2):
            l_new = alpha[c] * l[c] + jnp.sum(psum[c], axis=0, keepdims=True)
            m_ref[c:c + 1, :] = m_new[c]
            l_ref[c:c + 1, :] = l_new
            lfin_ref[c:c + 1, :] = jnp.where(diag, l_new, lfin_ref[c:c + 1, :])
        return p, alpha

    def publish(p, alpha):
        for c in range(2):
            alpha_ref[c:c + 1, :] = alpha[c]
        for c, u in cu:
            p_ref[c, rows(u), :] = p[c, u].astype(BF16)

    def finalize(t):
        qs = pl.multiple_of(qi_tab[t] * tq, tq)
        o = acc_ref[0] / lfin_ref[0:1, :] - lam * (acc_ref[1] / lfin_ref[1:2, :])
        ms = jnp.mean(o * o, axis=0, keepdims=True)
        o = o * lax.rsqrt(ms + NORM_EPS) * subg_ref[...] * (1.0 - LAM_INIT)
        o_ref[pl.ds(qs, tq), :] = o.T.astype(o_ref.dtype)

    scores_to_scratch(0)

    def body(t, _):
        prev = jnp.maximum(t - 1, 0)
        p, alpha = softmax(t)
        values_from_scratch(prev)
        scores_to_scratch(jnp.minimum(t + 1, n_trips - 1))
        publish(p, alpha)

        @pl.when(jnp.logical_and(t > 0, qi_tab[prev] == kj_tab[prev]))
        def _():
            finalize(prev)
        return 0

    lax.fori_loop(0, n_trips, body, 0)
    values_from_scratch(n_trips - 1)
    finalize(n_trips - 1)


def _diffattn(proj, slopes, lam4, subg_col, b, s, tq=256, nsub=2):
    nq = s // tq
    pairs = [(qi, kj) for qi in range(nq) for kj in range(qi + 1)]
    qi_tab = jnp.asarray([p[0] for p in pairs], jnp.int32)
    kj_tab = jnp.asarray([p[1] for p in pairs], jnp.int32)
    return pl.pallas_call(
        functools.partial(_da_kernel, tq=tq, nsub=nsub, seq=s),
        grid_spec=pltpu.PrefetchScalarGridSpec(
            num_scalar_prefetch=2,
            grid=(b, DA_HEADS),
            in_specs=[
                pl.BlockSpec(memory_space=pltpu.SMEM),
                pl.BlockSpec((4, DA_QK_DIM), lambda bi, h, *_: (0, 0)),
                pl.BlockSpec((DA_V_DIM, 1), lambda bi, h, *_: (0, 0)),
                pl.BlockSpec((s, LANES), lambda bi, h, *_: (bi, h)),
                pl.BlockSpec((s, LANES), lambda bi, h, *_: (bi, DA_HEADS + h)),
                pl.BlockSpec((s, LANES), lambda bi, h, *_: (bi, 2 * DA_HEADS + h)),
            ],
            out_specs=pl.BlockSpec((s, LANES), lambda bi, h, *_: (bi, h)),
            scratch_shapes=[
                pltpu.VMEM((DA_V_DIM, s), BF16),
                pltpu.VMEM((2, s, LANES), BF16),
                pltpu.VMEM((2, tq, tq), F32),
                pltpu.VMEM((2, tq), F32),
                pltpu.VMEM((2, tq), F32),
                pltpu.VMEM((2, tq), F32),
                pltpu.VMEM((2, DA_V_DIM, tq), F32),
                pltpu.VMEM((2, tq, tq), F32),
                pltpu.VMEM((2, tq, tq), BF16),
                pltpu.VMEM((2, tq), F32),
            ],
        ),
        out_shape=jax.ShapeDtypeStruct((b * s, DA_WIDTH), BF16),
        compiler_params=pltpu.CompilerParams(
            dimension_semantics=("parallel", "arbitrary"), vmem_limit_bytes=VMEM_LIMIT),
        name="diffattn",
    )(qi_tab, kj_tab, slopes, lam4, subg_col, proj, proj, proj)


def _rwkv_kernel(r_ref, k_ref, v_ref, p1_ref, p2_ref, mu_ref, w0_ref, a0_ref, w2_ref, kk_ref, ka_ref,
                 rk_ref, lnw_ref, lnb_ref, o_ref, state_ref, crkv_ref, cp2_ref):
    L, TT = CHUNK, RW_TILE
    c = pl.program_id(1)

    @pl.when(c == 0)
    def _():
        state_ref[...] = jnp.zeros_like(state_ref)
        crkv_ref[...] = jnp.zeros_like(crkv_ref)
        cp2_ref[...] = jnp.zeros_like(cp2_ref)

    first = lax.broadcasted_iota(jnp.int32, (TT, 1), 0) == 0

    def shift(x, carry):
        return jnp.where(first, carry, pltpu.roll(x, 1, 0))

    r_raw = r_ref[...].astype(F32)
    k_raw = k_ref[...].astype(F32)
    v_raw = v_ref[...].astype(F32)
    p2 = p2_ref[...].astype(F32)
    rs = shift(r_raw, crkv_ref[:, 0:RW_WIDTH])
    ks = shift(k_raw, crkv_ref[:, RW_WIDTH:2 * RW_WIDTH])
    vs = shift(v_raw, crkv_ref[:, 2 * RW_WIDTH:3 * RW_WIDTH])
    lin = p1_ref[...].astype(F32) + shift(p2, cp2_ref[...])
    crkv_ref[:, 0:RW_WIDTH] = r_raw[TT - 1:TT, :]
    crkv_ref[:, RW_WIDTH:2 * RW_WIDTH] = k_raw[TT - 1:TT, :]
    crkv_ref[:, 2 * RW_WIDTH:3 * RW_WIDTH] = v_raw[TT - 1:TT, :]
    cp2_ref[...] = p2[TT - 1:TT, :]

    r = r_raw + (rs - r_raw) * mu_ref[0:1, :]
    k = k_raw + (ks - k_raw) * mu_ref[1:2, :]
    v = v_raw + (vs - v_raw) * mu_ref[2:3, :]

    ll = lax.broadcasted_iota(jnp.int32, (TT, LORA_PAD), 1)
    z = jnp.where(ll < LORA_W, jnp.tanh(lin),
                  jnp.where(ll < LORA_W + LORA_A, lin,
                            jnp.where(ll < LORA_W + LORA_A + LORA_G, _sigmoid(lin), 0.0)))
    d = _dot(z.astype(BF16), w2_ref[...])
    logw = -_sigmoid(w0_ref[...] + d[:, 0:RW_WIDTH]) * math.exp(-0.5)
    a_sig = _sigmoid(a0_ref[...] + d[:, RW_WIDTH:2 * RW_WIDTH])
    g = d[:, 2 * RW_WIDTH:3 * RW_WIDTH]

    li = lax.broadcasted_iota(jnp.int32, (LANES, LANES), 0)
    lj = lax.broadcasted_iota(jnp.int32, (LANES, LANES), 1)
    same_head = jnp.where((li // RW_N) == (lj // RW_N), 1.0, 0.0).astype(BF16)

    def head_sum(x):
        xb = x.astype(BF16)
        return jnp.concatenate(
            [_dot(xb[:, p * LANES:(p + 1) * LANES], same_head) for p in range(RW_PAIRS)], axis=1)

    kk = k * kk_ref[...]
    kk = kk * lax.rsqrt(jnp.maximum(head_sum(kk * kk), 1e-24))
    k = k * (1.0 + (a_sig - 1.0) * ka_ref[...])
    bonus = head_sum(r * k * rk_ref[...])

    ti = lax.broadcasted_iota(jnp.int32, (TT, TT), 0)
    tj = lax.broadcasted_iota(jnp.int32, (TT, TT), 1)
    tri = jnp.where(ti >= tj, jnp.where((ti // L) == (tj // L), 1.0, 0.0), 0.0).astype(BF16)
    w_hi = logw.astype(BF16)
    w_r1 = logw - w_hi.astype(F32)
    w_mid = w_r1.astype(BF16)
    w_lo = (w_r1 - w_mid.astype(F32)).astype(BF16)
    cw = _dot(tri, w_hi) + _dot(tri, w_mid) + _dot(tri, w_lo)
    cw_last = [cw[(ci + 1) * L - 1:(ci + 1) * L, :] for ci in range(NCH)]
    e_in = jnp.exp(cw)
    e_out = jnp.exp(-cw)
    e_end = jnp.exp(jnp.concatenate([jnp.broadcast_to(cl, (L, RW_WIDTH)) for cl in cw_last], axis=0) - cw)
    w_end = [jnp.exp(cl) for cl in cw_last]

    at = -kk * jnp.exp(cw - logw)
    rt = r * e_in
    b = kk * a_sig
    bt = b * e_out
    kt = k * e_out
    bd = b * e_end
    kd = k * e_end

    lane = lax.broadcasted_iota(jnp.int32, (L, LANES), 1)
    lo = lane < RW_N

    def stack(x, ci, p):
        xs = x[ci * L:(ci + 1) * L, p * LANES:(p + 1) * LANES]
        return jnp.concatenate([jnp.where(lo, xs, 0.0), jnp.where(lo, 0.0, xs)], axis=0).astype(BF16)

    si = lax.broadcasted_iota(jnp.int32, (2 * L, 2 * L), 0)
    sj = lax.broadcasted_iota(jnp.int32, (2 * L, 2 * L), 1)
    strict = (si % L) > (sj % L)
    incl = (si % L) >= (sj % L)
    eye = jnp.where(si == sj, 1.0, 0.0)

    pairs = range(RW_PAIRS)
    cat0 = lambda *xs: jnp.concatenate(xs, axis=0)
    cat1 = lambda *xs: jnp.concatenate(xs, axis=1)
    bf = lambda x: x.astype(BF16)
    h = [state_ref[p] for p in pairs]
    ys = []
    for ci in range(NCH):
        a_s = [stack(at, ci, p) for p in pairs]
        r_s = [stack(rt, ci, p) for p in pairs]
        v_s = [stack(v, ci, p) for p in pairs]
        bk = [cat0(stack(bt, ci, p), stack(kt, ci, p)) for p in pairs]
        bkd = [cat0(stack(bd, ci, p), stack(kd, ci, p)) for p in pairs]
        aa = [_dot_nt(a_s[p], bk[p]) for p in pairs]
        rr = [_dot_nt(r_s[p], bk[p]) for p in pairs]
        a_ab = [jnp.where(strict, aa[p][:, 0:2 * L], 0.0) for p in pairs]
        a_ak = [bf(jnp.where(strict, aa[p][:, 2 * L:4 * L], 0.0)) for p in pairs]
        a_rb = [bf(jnp.where(incl, rr[p][:, 0:2 * L], 0.0)) for p in pairs]
        a_rk = [bf(jnp.where(incl, rr[p][:, 2 * L:4 * L], 0.0)) for p in pairs]
        ab = [bf(a) for a in a_ab]
        qpow = [_dot(ab[p], ab[p]) for p in pairs]
        inv = [eye + a_ab[p] for p in pairs]
        for _ in range(int(math.log2(L)) - 2):
            qb = [bf(q) for q in qpow]
            res = [_dot(qb[p], cat1(qb[p], bf(inv[p]))) for p in pairs]
            qpow = [res[p][:, 0:2 * L] for p in pairs]
            inv = [inv[p] + res[p][:, 2 * L:4 * L] for p in pairs]
        inv = [bf(inv[p] + _dot(bf(qpow[p]), bf(inv[p]))) for p in pairs]
        h_b = [bf(hp) for hp in h]
        x = [_dot(cat1(a_s[p], a_ak[p]), cat0(h_b[p], v_s[p])) for p in pairs]
        u_b = [bf(_dot(inv[p], bf(x[p]))) for p in pairs]
        y2 = [_dot(cat1(r_s[p], a_rb[p], a_rk[p]), cat0(h_b[p], u_b[p], v_s[p])) for p in pairs]
        w_col = [jnp.broadcast_to(w_end[ci][:, p * LANES:(p + 1) * LANES], (LANES, LANES)).T for p in pairs]
        h = [w_col[p] * h[p] + _dot_tn(bkd[p], cat0(u_b[p], v_s[p])) for p in pairs]
        ys.append(cat1(*[y2[p][0:L] + y2[p][L:2 * L] for p in pairs]))
    for p in pairs:
        state_ref[p] = h[p]
    y = cat0(*ys)

    mean = head_sum(y) * (1.0 / RW_N)
    yc = y - mean
    var = head_sum(yc * yc) * (1.0 / RW_N)
    y = yc * lax.rsqrt(var + LN_X_EPS) * lnw_ref[...] + lnb_ref[...]
    y = y + bonus * v
    o_ref[...] = (y * g).astype(o_ref.dtype)


def _rwkv(proj, mu_rkv, w0, a0, w2cat, k_k, k_a, r_k, ln_w, ln_b, b, s):
    tt = RW_TILE
    nc = s // tt
    row = lambda n: pl.BlockSpec((1, n), lambda bi, c: (0, 0))
    rkv_col0 = 3 * DA_WIDTH // RW_WIDTH
    lora_col0 = IN_COLS // LORA_PAD
    return pl.pallas_call(
        _rwkv_kernel,
        grid=(b, nc),
        in_specs=[
            pl.BlockSpec((tt, RW_WIDTH), lambda bi, c: (bi * nc + c, rkv_col0)),
            pl.BlockSpec((tt, RW_WIDTH), lambda bi, c: (bi * nc + c, rkv_col0 + 1)),
            pl.BlockSpec((tt, RW_WIDTH), lambda bi, c: (bi * nc + c, rkv_col0 + 2)),
            pl.BlockSpec((tt, LORA_PAD), lambda bi, c: (bi * nc + c, lora_col0)),
            pl.BlockSpec((tt, LORA_PAD), lambda bi, c: (bi * nc + c, lora_col0 + 1)),
            pl.BlockSpec((3, RW_WIDTH), lambda bi, c: (0, 0)),
            row(RW_WIDTH), row(RW_WIDTH),
            pl.BlockSpec((LORA_PAD, 3 * RW_WIDTH), lambda bi, c: (0, 0)),
            row(RW_WIDTH), row(RW_WIDTH), row(RW_WIDTH), row(RW_WIDTH), row(RW_WIDTH),
        ],
        out_specs=pl.BlockSpec((tt, RW_WIDTH), lambda bi, c: (bi * nc + c, 0)),
        out_shape=jax.ShapeDtypeStruct((b * s, RW_WIDTH), BF16),
        scratch_shapes=[
            pltpu.VMEM((RW_PAIRS, LANES, LANES), F32),
            pltpu.VMEM((1, 3 * RW_WIDTH), F32),
            pltpu.VMEM((1, LORA_PAD), F32),
        ],
        compiler_params=pltpu.CompilerParams(
            dimension_semantics=("parallel", "arbitrary"), vmem_limit_bytes=VMEM_LIMIT),
        name="rwkv",
    )(proj, proj, proj, proj, proj, mu_rkv, w0, a0, w2cat, k_k, k_a, r_k, ln_w, ln_b)


ROUTE_E, ROUTE_G, ROUTE_R = 0, 2, 4
EXPERT_LANE0 = N_GROUPS


def _outproj_kernel(x_ref, oda_ref, orw_ref, wout_ref, g_ref, wrh_ref, wrl_ref, br_ref,
                    h_ref, xn_ref, route_ref, cnt_ref, base_ref, *, tm):
    @pl.when(pl.program_id(0) == 0)
    def _():
        base_ref[...] = jnp.zeros_like(base_ref)

    mix = _dot(oda_ref[...], wout_ref[0:DA_WIDTH, :]) + _dot(orw_ref[...], wout_ref[DA_WIDTH:, :])
    h = x_ref[...] + mix
    h_ref[...] = h
    ms = jnp.mean(h * h, axis=-1, keepdims=True)
    xn = h * lax.rsqrt(ms + NORM_EPS) * g_ref[...]
    x_hi = xn.astype(BF16)
    _store_row_tiles(xn_ref, _pack_halves(x_hi))

    x_lo = (xn - x_hi.astype(F32)).astype(BF16)
    lg = (_dot(x_hi, wrh_ref[...]) + _dot(x_lo, wrh_ref[...]) + _dot(x_hi, wrl_ref[...])) + br_ref[...]

    lane = lax.broadcasted_iota(jnp.int32, (tm, LANES), 1)
    big = jnp.int32(1 << 20)
    gl = jnp.where(lane < N_GROUPS, lg, NEG)
    gmax = jnp.max(gl, axis=-1, keepdims=True)
    grp = jnp.min(jnp.where(gl == gmax, lane, big), axis=-1, keepdims=True)
    p_grp = 1.0 / jnp.sum(jnp.exp(gl - gmax), axis=-1, keepdims=True)
    eg = jnp.where(lane >= EXPERT_LANE0, (lane - EXPERT_LANE0) // EXPERTS_PER_GROUP, -1)
    el = jnp.where(eg == grp, lg, NEG)
    emax = jnp.max(el, axis=-1, keepdims=True)
    pe = jnp.exp(el - emax)
    probs = pe / jnp.sum(pe, axis=-1, keepdims=True)
    probs = jnp.where(eg == grp, probs, -1.0)
    p1 = jnp.max(probs, axis=-1, keepdims=True)
    i1 = jnp.min(jnp.where(probs == p1, lane, big), axis=-1, keepdims=True)
    probs2 = jnp.where(lane == i1, -1.0, probs)
    p2 = jnp.max(probs2, axis=-1, keepdims=True)
    i2 = jnp.min(jnp.where(probs2 == p2, lane, big), axis=-1, keepdims=True)
    gate1 = p_grp * p1 / (p1 + p2)
    gate2 = p_grp * p2 / (p1 + p2)

    oh1 = jnp.where(lane == i1, 1.0, 0.0)
    oh2 = jnp.where(lane == i2, 1.0, 0.0)
    ri = lax.broadcasted_iota(jnp.int32, (tm, tm), 0)
    rj = lax.broadcasted_iota(jnp.int32, (tm, tm), 1)
    lower = jnp.where(ri > rj, 1.0, 0.0).astype(BF16)
    base = base_ref[...]
    tot1 = jnp.sum(oh1, axis=0, keepdims=True)
    c1 = base + _dot(lower, oh1.astype(BF16))
    c2 = base + tot1 + _dot(lower, oh2.astype(BF16))
    rank1 = jnp.sum(oh1 * c1, axis=-1, keepdims=True)
    rank2 = jnp.sum(oh2 * c2, axis=-1, keepdims=True)
    base = base + tot1 + jnp.sum(oh2, axis=0, keepdims=True)
    base_ref[...] = base
    cnt_ref[...] = base

    e1 = (i1 - EXPERT_LANE0).astype(F32)
    e2 = (i2 - EXPERT_LANE0).astype(F32)
    rec = jnp.zeros((tm, LANES), F32)
    for ln, val in ((ROUTE_E, e1), (ROUTE_E + 1, e2), (ROUTE_G, gate1), (ROUTE_G + 1, gate2),
                    (ROUTE_R, rank1), (ROUTE_R + 1, rank2)):
        rec = jnp.where(lane == ln, val, rec)
    route_ref[...] = rec


def _outproj(x2, o_da, o_rw, w_out, g_ffn, wr_hi, wr_lo, b_r, tm=512):
    t = x2.shape[0]
    const = lambda shape: pl.BlockSpec(shape, lambda i: (0, 0))
    return pl.pallas_call(
        functools.partial(_outproj_kernel, tm=tm),
        grid=(t // tm,),
        in_specs=[
            pl.BlockSpec((tm, D_MODEL), lambda i: (i, 0)),
            pl.BlockSpec((tm, DA_WIDTH), lambda i: (i, 0)),
            pl.BlockSpec((tm, RW_WIDTH), lambda i: (i, 0)),
            const((D_MODEL, D_MODEL)),
            const((1, D_MODEL)),
            const((D_MODEL, LANES)),
            const((D_MODEL, LANES)),
            const((1, LANES)),
        ],
        out_specs=[
            pl.BlockSpec((tm, D_MODEL), lambda i: (i, 0)),
            pl.BlockSpec((tm * SUBLANES, LANES), lambda i: (i, 0)),
            pl.BlockSpec((tm, LANES), lambda i: (i, 0)),
            const((1, LANES)),
        ],
        out_shape=[
            jax.ShapeDtypeStruct((t, D_MODEL), F32),
            jax.ShapeDtypeStruct((t * SUBLANES, LANES), jnp.uint32),
            jax.ShapeDtypeStruct((t, LANES), F32),
            jax.ShapeDtypeStruct((1, LANES), F32),
        ],
        scratch_shapes=[pltpu.VMEM((1, LANES), F32)],
        compiler_params=pltpu.CompilerParams(
            dimension_semantics=("arbitrary",), vmem_limit_bytes=VMEM_LIMIT),
        name="outproj_router",
    )(x2, o_da, o_rw, w_out, g_ffn, wr_hi, wr_lo, b_r)


def _row_copy(src_ref, dst_ref, src_row8, dst_row8, sem):
    src = src_ref.at[pl.ds(pl.multiple_of(src_row8, SUBLANES), SUBLANES)]
    dst = dst_ref.at[pl.ds(pl.multiple_of(dst_row8, SUBLANES), SUBLANES)]
    return pltpu.make_async_copy(src, dst, sem)


def _dispatch_kernel(dest_ref, x_ref, xb_in_hbm, xb_hbm, sem, *, td):
    del xb_in_hbm
    t0 = pl.program_id(0) * td

    def start(i, _):
        tok = t0 + i
        _row_copy(x_ref, xb_hbm, i * SUBLANES, dest_ref[2 * tok], sem).start(priority=0)
        _row_copy(x_ref, xb_hbm, i * SUBLANES, dest_ref[2 * tok + 1], sem).start(priority=1)
        return 0

    lax.fori_loop(0, td, start, 0, unroll=8)
    for _ in range(2):
        pltpu.make_async_copy(x_ref, xb_hbm.at[pl.ds(0, td * SUBLANES)], sem).wait()


def _dispatch(dest_flat, xn2, xb_zero, td=256):
    t = xn2.shape[0] // SUBLANES
    return pl.pallas_call(
        functools.partial(_dispatch_kernel, td=td),
        grid_spec=pltpu.PrefetchScalarGridSpec(
            num_scalar_prefetch=1,
            grid=(t // td,),
            in_specs=[pl.BlockSpec((td * SUBLANES, LANES), lambda i, d: (i, 0)),
                      pl.BlockSpec(memory_space=pl.ANY)],
            out_specs=pl.BlockSpec(memory_space=pl.ANY),
            scratch_shapes=[pltpu.SemaphoreType.DMA],
        ),
        out_shape=jax.ShapeDtypeStruct(xb_zero.shape, xb_zero.dtype),
        input_output_aliases={2: 0},
        compiler_params=pltpu.CompilerParams(dimension_semantics=("arbitrary",)),
        name="dispatch",
    )(dest_flat, xn2, xb_zero)


def _experts_kernel(be_ref, nu_ref, xb_ref, wg_ref, wu_ref, wd_ref, yb_ref, wgb_ref, wub_ref, wdb_ref):
    i = pl.program_id(0)
    used = i < nu_ref[0]
    new_expert = jnp.logical_or(i == 0, be_ref[i] != be_ref[jnp.maximum(i - 1, 0)])

    def swiglu(wg_lo, wg_hi, wu_lo, wu_hi, wd):
        x_lo, x_hi = (half.astype(BF16) for half in _unpack_halves(_load_row_tiles(xb_ref)))
        hg = _dot(x_lo, wg_lo) + _dot(x_hi, wg_hi)
        hu = _dot(x_lo, wu_lo) + _dot(x_hi, wu_hi)
        hid = hg * _sigmoid(hg) * hu
        _store_row_tiles(yb_ref, _pack_halves(_dot(hid.astype(BF16), wd).astype(BF16)))

    @pl.when(jnp.logical_and(used, new_expert))
    def _():
        wg, wu, wd = (w[...].astype(BF16) for w in (wg_ref, wu_ref, wd_ref))
        wgb_ref[...] = wg
        wub_ref[...] = wu
        wdb_ref[...] = wd
        swiglu(wg[0:HALF, :], wg[HALF:, :], wu[0:HALF, :], wu[HALF:, :], wd)

    @pl.when(jnp.logical_and(used, jnp.logical_not(new_expert)))
    def _():
        swiglu(wgb_ref[0:HALF, :], wgb_ref[HALF:, :], wub_ref[0:HALF, :], wub_ref[HALF:, :], wdb_ref[...])

    @pl.when(jnp.logical_not(used))
    def _():
        yb_ref[...] = jnp.zeros_like(yb_ref)


def _experts(block_e, n_used, xb, w_gate, w_up, w_down):
    block_rows = EXPERT_BLOCK * SUBLANES
    nb = xb.shape[0] // block_rows
    rows = lambda i, be, nu: (jnp.minimum(i, nu[0] - 1), 0)
    return pl.pallas_call(
        _experts_kernel,
        grid_spec=pltpu.PrefetchScalarGridSpec(
            num_scalar_prefetch=2,
            grid=(nb,),
            in_specs=[
                pl.BlockSpec((block_rows, LANES), rows),
                pl.BlockSpec((None, D_MODEL, D_EXPERT), lambda i, be, nu: (be[i], 0, 0)),
                pl.BlockSpec((None, D_MODEL, D_EXPERT), lambda i, be, nu: (be[i], 0, 0)),
                pl.BlockSpec((None, D_EXPERT, D_MODEL), lambda i, be, nu: (be[i], 0, 0)),
            ],
            out_specs=pl.BlockSpec((block_rows, LANES), lambda i, be, nu: (i, 0)),
            scratch_shapes=[
                pltpu.VMEM((D_MODEL, D_EXPERT), BF16),
                pltpu.VMEM((D_MODEL, D_EXPERT), BF16),
                pltpu.VMEM((D_EXPERT, D_MODEL), BF16),
            ],
        ),
        out_shape=jax.ShapeDtypeStruct(xb.shape, jnp.uint32),
        compiler_params=pltpu.CompilerParams(
            dimension_semantics=("arbitrary",), vmem_limit_bytes=EXPERTS_VMEM_LIMIT),
        name="experts",
    )(block_e, n_used, xb, w_gate, w_up, w_down)


def _combine_kernel(dest_ref, route_ref, h_ref, g_ref, yb_hbm, o_ref, y1_ref, y2_ref, sem, *, tc):
    t0 = pl.program_id(0) * tc

    def start(i, _):
        tok = t0 + i
        _row_copy(yb_hbm, y1_ref, dest_ref[2 * tok], i * SUBLANES, sem).start(priority=0)
        _row_copy(yb_hbm, y2_ref, dest_ref[2 * tok + 1], i * SUBLANES, sem).start(priority=1)
        return 0

    lax.fori_loop(0, tc, start, 0, unroll=8)
    pltpu.make_async_copy(yb_hbm.at[pl.ds(0, tc * SUBLANES)], y1_ref, sem).wait()
    pltpu.make_async_copy(yb_hbm.at[pl.ds(0, tc * SUBLANES)], y2_ref, sem).wait()

    rec = route_ref[...]
    lane = lax.broadcasted_iota(jnp.int32, (tc, LANES), 1)
    g1 = jnp.sum(jnp.where(lane == ROUTE_G, rec, 0.0), axis=-1, keepdims=True)
    g2 = jnp.sum(jnp.where(lane == ROUTE_G + 1, rec, 0.0), axis=-1, keepdims=True)
    y1_lo, y1_hi = _unpack_halves(_load_row_tiles(y1_ref))
    y2_lo, y2_hi = _unpack_halves(_load_row_tiles(y2_ref))
    h_lo = h_ref[:, 0:HALF] + (y1_lo * g1 + y2_lo * g2)
    h_hi = h_ref[:, HALF:] + (y1_hi * g1 + y2_hi * g2)
    ms = (jnp.sum(h_lo * h_lo, axis=-1, keepdims=True)
          + jnp.sum(h_hi * h_hi, axis=-1, keepdims=True)) * (1.0 / D_MODEL)
    scale = lax.rsqrt(ms + NORM_EPS)
    o_ref[:, 0:HALF] = h_lo * scale * g_ref[:, 0:HALF]
    o_ref[:, HALF:] = h_hi * scale * g_ref[:, HALF:]


def _combine(dest_flat, route, h1, g_final, yb, tc=256):
    t = h1.shape[0]
    return pl.pallas_call(
        functools.partial(_combine_kernel, tc=tc),
        grid_spec=pltpu.PrefetchScalarGridSpec(
            num_scalar_prefetch=1,
            grid=(t // tc,),
            in_specs=[
                pl.BlockSpec((tc, LANES), lambda i, d: (i, 0)),
                pl.BlockSpec((tc, D_MODEL), lambda i, d: (i, 0)),
                pl.BlockSpec((1, D_MODEL), lambda i, d: (0, 0)),
                pl.BlockSpec(memory_space=pl.ANY),
            ],
            out_specs=pl.BlockSpec((tc, D_MODEL), lambda i, d: (i, 0)),
            scratch_shapes=[
                pltpu.VMEM((tc * SUBLANES, LANES), jnp.uint32),
                pltpu.VMEM((tc * SUBLANES, LANES), jnp.uint32),
                pltpu.SemaphoreType.DMA,
            ],
        ),
        out_shape=jax.ShapeDtypeStruct((t, D_MODEL), F32),
        compiler_params=pltpu.CompilerParams(
            dimension_semantics=("arbitrary",), vmem_limit_bytes=VMEM_LIMIT),
        name="combine",
    )(dest_flat, route, h1, g_final, yb)


def _pad_cols(w, n):
    return jnp.pad(w, ((0, 0), (0, n - w.shape[1])))


def kernel(x, g_mix, w_in, w_out, da_lambda_q1, da_lambda_k1, da_lambda_q2, da_lambda_k2, da_subln_g,
           rw_mu_x, rw_mu_rkv, rw_w0, rw_w1, rw_w2, rw_a0, rw_a1, rw_a2, rw_g1, rw_g2, rw_k_k, rw_k_a,
           rw_r_k, rw_ln_w, rw_ln_b, g_ffn, moe_w_group, moe_b_group, moe_w_expert, moe_b_expert,
           moe_w_gate, moe_w_up, moe_w_down, g_final):
    b, s, d = x.shape
    t = b * s
    x2 = x.reshape(t, d)

    mu = rw_mu_x[0]
    lora = (rw_w1[0], rw_a1[0], rw_g1[0])
    keep = _pad_cols(jnp.concatenate([(1.0 - mu[i])[:, None] * w for i, w in enumerate(lora)], axis=1), LORA_PAD)
    prev = _pad_cols(jnp.concatenate([mu[i][:, None] * w for i, w in enumerate(lora)], axis=1), LORA_PAD)
    w_all = jnp.concatenate([w_in[0], keep, prev], axis=1).astype(BF16)
    proj = _inproj(x2, g_mix, w_all)

    slopes = jnp.asarray([2.0 ** (-8.0 * (i + 1) / DA_HEADS) for i in range(DA_HEADS)], F32)
    lam4 = jnp.concatenate([da_lambda_q1, da_lambda_k1, da_lambda_q2, da_lambda_k2], axis=0)
    o_da = _diffattn(proj, slopes, lam4, da_subln_g.reshape(DA_V_DIM, 1), b, s)

    w2cat = jnp.zeros((LORA_PAD, 3 * RW_WIDTH), F32)
    w2cat = w2cat.at[0:LORA_W, 0:RW_WIDTH].set(rw_w2[0])
    w2cat = w2cat.at[LORA_W:LORA_W + LORA_A, RW_WIDTH:2 * RW_WIDTH].set(rw_a2[0])
    w2cat = w2cat.at[LORA_W + LORA_A:LORA_W + LORA_A + LORA_G, 2 * RW_WIDTH:].set(rw_g2[0])
    o_rw = _rwkv(proj, rw_mu_rkv[0], rw_w0, rw_a0, w2cat.astype(BF16), rw_k_k, rw_k_a,
                 rw_r_k.reshape(1, RW_WIDTH), rw_ln_w, rw_ln_b, b, s)

    w_r = _pad_cols(jnp.concatenate([moe_w_group[0], moe_w_expert[0]], axis=1), LANES)
    wr_hi = w_r.astype(BF16)
    wr_lo = (w_r - wr_hi.astype(F32)).astype(BF16)
    b_r = _pad_cols(jnp.concatenate([moe_b_group[0], moe_b_expert[0].reshape(-1)])[None, :], LANES)
    h1, xn2, route, cnt = _outproj(x2, o_da, o_rw, w_out[0].astype(BF16), g_ffn, wr_hi, wr_lo, b_r)

    counts = cnt[0, EXPERT_LANE0:EXPERT_LANE0 + N_EXPERTS].astype(jnp.int32)
    pcounts = ((counts + EXPERT_BLOCK - 1) // EXPERT_BLOCK) * EXPERT_BLOCK
    pends = jnp.cumsum(pcounts)
    pstarts = pends - pcounts
    e_idx = route[:, ROUTE_E:ROUTE_E + 2].astype(jnp.int32)
    rank = route[:, ROUTE_R:ROUTE_R + 2].astype(jnp.int32)
    expert_ids = jnp.arange(N_EXPERTS, dtype=jnp.int32)
    row0 = jnp.sum(jnp.where(e_idx[..., None] == expert_ids, pstarts, 0), axis=-1)
    dest = ((row0 + rank) * SUBLANES).reshape(-1)
    n_blocks = (2 * t) // EXPERT_BLOCK + N_EXPERTS
    block_row0 = jnp.arange(n_blocks, dtype=jnp.int32) * EXPERT_BLOCK
    block_e = jnp.minimum(jnp.sum(pends[None, :] <= block_row0[:, None], axis=1), N_EXPERTS - 1).astype(jnp.int32)
    n_used = (pends[-1:] // EXPERT_BLOCK).astype(jnp.int32)

    xb = _dispatch(dest, xn2, jnp.zeros((n_blocks * EXPERT_BLOCK * SUBLANES, LANES), jnp.uint32))
    yb = _experts(block_e, n_used, xb, moe_w_gate[0], moe_w_up[0], moe_w_down[0])
    out = _combine(dest, route, h1, g_final[None, :], yb)
    return out.reshape(b, s, d)
```

```python
import functools
import math

import jax
import jax.numpy as jnp
from jax import lax
from jax.experimental import pallas as pl
from jax.experimental.pallas import tpu as pltpu

F32 = jnp.float32
BF16 = jnp.bfloat16

D_MODEL = 2048
DA_WIDTH = 1024
RW_WIDTH = 1024
DA_HEADS = 8
DA_V_DIM = 128
DA_QK_DIM = 64
RW_N = 64
RW_PAIRS = RW_WIDTH // 128
LORA_W, LORA_A, LORA_G = 64, 64, 160
LORA_PAD = 384
IN_COLS = 3 * DA_WIDTH + 3 * RW_WIDTH
NC = IN_COLS + 2 * LORA_PAD
N_GROUPS = 4
EXPERTS_PER_GROUP = 8
N_EXPERTS = 32
D_EXPERT = 512
NORM_EPS = 1e-6
LN_X_EPS = 64e-5
LAM_INIT = 0.8 - 0.6 * math.exp(-0.3 * 0)
LANES = 128
CHUNK = 64
RW_TILE = 256
NCH = RW_TILE // CHUNK
EXPERT_BLOCK = 256
NEG = -1e30
VMEM_LIMIT = 48 * 1024 * 1024
EXPERTS_VMEM_LIMIT = 56 * 1024 * 1024


def _dot(a, b):
    return jnp.dot(a, b, preferred_element_type=F32)


def _dot_nt(a, b):
    return lax.dot_general(a, b, (((1,), (1,)), ((), ())), preferred_element_type=F32)


def _dot_tn(a, b):
    return lax.dot_general(a, b, (((0,), (0,)), ((), ())), preferred_element_type=F32)


def _sigmoid(x):
    return 1.0 / (1.0 + jnp.exp(-x))


HALF = D_MODEL // 2


def _pack_halves(xb):
    lo = lax.bitcast_convert_type(xb[:, :HALF].astype(F32), jnp.uint32) >> 16
    hi = lax.bitcast_convert_type(xb[:, HALF:].astype(F32), jnp.uint32) & jnp.uint32(0xFFFF0000)
    return hi | lo


def _unpack_halves(u):
    lo = lax.bitcast_convert_type(u << 16, F32)
    hi = lax.bitcast_convert_type(u & jnp.uint32(0xFFFF0000), F32)
    return lo, hi


SUBLANES = 8
assert HALF == SUBLANES * LANES


def _store_row_tiles(ref, u):
    n = u.shape[0]
    for k in range(SUBLANES):
        ref[pl.ds(k, n, stride=SUBLANES), :] = u[:, k * LANES:(k + 1) * LANES]


def _load_row_tiles(ref):
    n = ref.shape[0] // SUBLANES
    return jnp.concatenate([ref[pl.ds(k, n, stride=SUBLANES), :] for k in range(SUBLANES)], axis=1)


def _inproj_kernel(x_ref, g_ref, w_ref, o_ref, xn_ref):
    @pl.when(pl.program_id(1) == 0)
    def _():
        x = x_ref[...]
        ms = jnp.mean(x * x, axis=-1, keepdims=True)
        xn_ref[...] = (x * lax.rsqrt(ms + NORM_EPS) * g_ref[...]).astype(BF16)

    o_ref[...] = _dot(xn_ref[...], w_ref[...]).astype(o_ref.dtype)


def _inproj(x2, g, w_all, tm=1024, tn=768):
    t = x2.shape[0]
    tm = min(tm, t)
    return pl.pallas_call(
        _inproj_kernel,
        grid=(t // tm, NC // tn),
        in_specs=[
            pl.BlockSpec((tm, D_MODEL), lambda i, j: (i, 0)),
            pl.BlockSpec((1, D_MODEL), lambda i, j: (0, 0)),
            pl.BlockSpec((D_MODEL, tn), lambda i, j: (0, j)),
        ],
        out_specs=pl.BlockSpec((tm, tn), lambda i, j: (i, j)),
        out_shape=jax.ShapeDtypeStruct((t, NC), BF16),
        scratch_shapes=[pltpu.VMEM((tm, D_MODEL), BF16)],
        compiler_params=pltpu.CompilerParams(
            dimension_semantics=("parallel", "arbitrary"), vmem_limit_bytes=VMEM_LIMIT),
        name="inproj",
    )(x2, g, w_all)


def _da_kernel(qi_tab, kj_tab, slopes_ref, lam_ref, subg_ref, q_ref, k_ref, v_ref, o_ref,
               vt_ref, qz_ref, bias_ref, m_ref, l_ref, lfin_ref, acc_ref, s_ref, p_ref, alpha_ref,
               *, tq, nsub, seq):
    tk = tq
    nq = seq // tq
    n_trips = nq * (nq + 1) // 2
    sub = tk // nsub
    slope = slopes_ref[pl.program_id(1)]
    cu = [(c, u) for c in range(2) for u in range(nsub)]
    rows = lambda u: slice(u * sub, (u + 1) * sub)

    for blk in range(seq // LANES):
        r = slice(blk * LANES, (blk + 1) * LANES)
        vt_ref[:, r] = v_ref[r, :].astype(F32).T.astype(BF16)
    lane = lax.broadcasted_iota(jnp.int32, (tq, LANES), 1)
    for blk in range(nq):
        r = slice(blk * tq, (blk + 1) * tq)
        q = q_ref[r, :] * jnp.asarray(DA_QK_DIM ** -0.5, BF16)
        qz_ref[0, r, :] = jnp.where(lane < DA_QK_DIM, q, jnp.zeros_like(q))
        qz_ref[1, r, :] = jnp.where(lane >= DA_QK_DIM, q, jnp.zeros_like(q))
    krow = lax.broadcasted_iota(jnp.int32, (sub, tq), 0)
    qcol = lax.broadcasted_iota(jnp.int32, (sub, tq), 1)
    for u in range(nsub):
        alibi = slope * (krow + u * sub).astype(F32)
        bias_ref[0, rows(u), :] = alibi
        bias_ref[1, rows(u), :] = jnp.where(krow + u * sub <= qcol, alibi, NEG)

    m_ref[...] = jnp.full_like(m_ref, NEG)
    l_ref[...] = jnp.zeros_like(l_ref)
    lfin_ref[...] = jnp.ones_like(lfin_ref)
    acc_ref[...] = jnp.zeros_like(acc_ref)
    p_ref[...] = jnp.zeros_like(p_ref)
    alpha_ref[...] = jnp.zeros_like(alpha_ref)

    lv = lam_ref[...]
    lam = (jnp.exp(jnp.sum(lv[0:1] * lv[1:2], axis=-1, keepdims=True))
           - jnp.exp(jnp.sum(lv[2:3] * lv[3:4], axis=-1, keepdims=True)) + LAM_INIT)

    def scores_to_scratch(t):
        qs = pl.multiple_of(qi_tab[t] * tq, tq)
        ks = pl.multiple_of(kj_tab[t] * tk, tk)
        diag = (qi_tab[t] == kj_tab[t]).astype(jnp.int32)
        for c, u in cu:
            s_ref[c, rows(u), :] = (_dot_nt(k_ref[pl.ds(ks + u * sub, sub), :], qz_ref[c, pl.ds(qs, tq), :])
                                    + bias_ref[diag, rows(u), :])

    def values_from_scratch(t):
        ks = pl.multiple_of(kj_tab[t] * tk, tk)
        vt = vt_ref[:, pl.ds(ks, tk)]
        for c in range(2):
            acc_ref[c] = alpha_ref[c:c + 1, :] * acc_ref[c] + _dot(vt, p_ref[c])

    def softmax(t):
        first = kj_tab[t] == 0
        diag = kj_tab[t] == qi_tab[t]
        off = slope * ((kj_tab[t] - qi_tab[t]) * tk).astype(F32)
        s = {(c, u): s_ref[c, rows(u), :] for c, u in cu}
        m = [jnp.where(first, NEG, m_ref[c:c + 1, :]) for c in range(2)]
        l = [jnp.where(first, 0.0, l_ref[c:c + 1, :]) for c in range(2)]
        smax = [functools.reduce(jnp.maximum, [s[c, u] for u in range(nsub)]) for c in range(2)]
        m_new = [jnp.maximum(m[c], jnp.max(smax[c], axis=0, keepdims=True) + off) for c in range(2)]
        p = {(c, u): jnp.exp(s[c, u] - (m_new[c] - off)) for c, u in cu}
        psum = [functools.reduce(jnp.add, [p[c, u] for u in range(nsub)]) for c in range(2)]
        alpha = [jnp.exp(m[c] - m_new[c]) for c in range(2)]
        for c in range(2):
            l_new = alpha[c] * l[c] + jnp.sum(psum[c], axis=0, keepdims=True)
            m_ref[c:c + 1, :] = m_new[c]
            l_ref[c:c + 1, :] = l_new
            lfin_ref[c:c + 1, :] = jnp.where(diag, l_new, lfin_ref[c:c + 1, :])
        return p, alpha

    def publish(p, alpha):
        for c in range(2):
            alpha_ref[c:c + 1, :] = alpha[c]
        for c, u in cu:
            p_ref[c, rows(u), :] = p[c, u].astype(BF16)

    def finalize(t):
        qs = pl.multiple_of(qi_tab[t] * tq, tq)
        o = acc_ref[0] / lfin_ref[0:1, :] - lam * (acc_ref[1] / lfin_ref[1:2, :])
        ms = jnp.mean(o * o, axis=0, keepdims=True)
        o = o * lax.rsqrt(ms + NORM_EPS) * subg_ref[...] * (1.0 - LAM_INIT)
        o_ref[pl.ds(qs, tq), :] = o.T.astype(o_ref.dtype)

    scores_to_scratch(0)

    def body(t, _):
        prev = jnp.maximum(t - 1, 0)
        p, alpha = softmax(t)
        values_from_scratch(prev)
        scores_to_scratch(jnp.minimum(t + 1, n_trips - 1))
        publish(p, alpha)

        @pl.when(jnp.logical_and(t > 0, qi_tab[prev] == kj_tab[prev]))
        def _():
            finalize(prev)
        return 0

    lax.fori_loop(0, n_trips, body, 0)
    values_from_scratch(n_trips - 1)
    finalize(n_trips - 1)


def _diffattn(proj, slopes, lam4, subg_col, b, s, tq=256, nsub=2):
    nq = s // tq
    pairs = [(qi, kj) for qi in range(nq) for kj in range(qi + 1)]
    qi_tab = jnp.asarray([p[0] for p in pairs], jnp.int32)
    kj_tab = jnp.asarray([p[1] for p in pairs], jnp.int32)
    return pl.pallas_call(
        functools.partial(_da_kernel, tq=tq, nsub=nsub, seq=s),
        grid_spec=pltpu.PrefetchScalarGridSpec(
            num_scalar_prefetch=2,
            grid=(b, DA_HEADS),
            in_specs=[
                pl.BlockSpec(memory_space=pltpu.SMEM),
                pl.BlockSpec((4, DA_QK_DIM), lambda bi, h, *_: (0, 0)),
                pl.BlockSpec((DA_V_DIM, 1), lambda bi, h, *_: (0, 0)),
                pl.BlockSpec((s, LANES), lambda bi, h, *_: (bi, h)),
                pl.BlockSpec((s, LANES), lambda bi, h, *_: (bi, DA_HEADS + h)),
                pl.BlockSpec((s, LANES), lambda bi, h, *_: (bi, 2 * DA_HEADS + h)),
            ],
            out_specs=pl.BlockSpec((s, LANES), lambda bi, h, *_: (bi, h)),
            scratch_shapes=[
                pltpu.VMEM((DA_V_DIM, s), BF16),
                pltpu.VMEM((2, s, LANES), BF16),
                pltpu.VMEM((2, tq, tq), F32),
                pltpu.VMEM((2, tq), F32),
                pltpu.VMEM((2, tq), F32),
                pltpu.VMEM((2, tq), F32),
                pltpu.VMEM((2, DA_V_DIM, tq), F32),
                pltpu.VMEM((2, tq, tq), F32),
                pltpu.VMEM((2, tq, tq), BF16),
                pltpu.VMEM((2, tq), F32),
            ],
        ),
        out_shape=jax.ShapeDtypeStruct((b * s, DA_WIDTH), BF16),
        compiler_params=pltpu.CompilerParams(
            dimension_semantics=("parallel", "arbitrary"), vmem_limit_bytes=VMEM_LIMIT),
        name="diffattn",
    )(qi_tab, kj_tab, slopes, lam4, subg_col, proj, proj, proj)


def _rwkv_kernel(r_ref, k_ref, v_ref, p1_ref, p2_ref, mu_ref, w0_ref, a0_ref, w2_ref, kk_ref, ka_ref,
                 rk_ref, lnw_ref, lnb_ref, o_ref, state_ref, crkv_ref, cp2_ref):
    L, TT = CHUNK, RW_TILE
    c = pl.program_id(1)

    @pl.when(c == 0)
    def _():
        state_ref[...] = jnp.zeros_like(state_ref)
        crkv_ref[...] = jnp.zeros_like(crkv_ref)
        cp2_ref[...] = jnp.zeros_like(cp2_ref)

    first = lax.broadcasted_iota(jnp.int32, (TT, 1), 0) == 0

    def shift(x, carry):
        return jnp.where(first, carry, pltpu.roll(x, 1, 0))

    r_raw = r_ref[...].astype(F32)
    k_raw = k_ref[...].astype(F32)
    v_raw = v_ref[...].astype(F32)
    p2 = p2_ref[...].astype(F32)
    rs = shift(r_raw, crkv_ref[:, 0:RW_WIDTH])
    ks = shift(k_raw, crkv_ref[:, RW_WIDTH:2 * RW_WIDTH])
    vs = shift(v_raw, crkv_ref[:, 2 * RW_WIDTH:3 * RW_WIDTH])
    lin = p1_ref[...].astype(F32) + shift(p2, cp2_ref[...])
    crkv_ref[:, 0:RW_WIDTH] = r_raw[TT - 1:TT, :]
    crkv_ref[:, RW_WIDTH:2 * RW_WIDTH] = k_raw[TT - 1:TT, :]
    crkv_ref[:, 2 * RW_WIDTH:3 * RW_WIDTH] = v_raw[TT - 1:TT, :]
    cp2_ref[...] = p2[TT - 1:TT, :]

    r = r_raw + (rs - r_raw) * mu_ref[0:1, :]
    k = k_raw + (ks - k_raw) * mu_ref[1:2, :]
    v = v_raw + (vs - v_raw) * mu_ref[2:3, :]

    ll = lax.broadcasted_iota(jnp.int32, (TT, LORA_PAD), 1)
    z = jnp.where(ll < LORA_W, jnp.tanh(lin),
                  jnp.where(ll < LORA_W + LORA_A, lin,
                            jnp.where(ll < LORA_W + LORA_A + LORA_G, _sigmoid(lin), 0.0)))
    d = _dot(z.astype(BF16), w2_ref[...])
    logw = -_sigmoid(w0_ref[...] + d[:, 0:RW_WIDTH]) * math.exp(-0.5)
    a_sig = _sigmoid(a0_ref[...] + d[:, RW_WIDTH:2 * RW_WIDTH])
    g = d[:, 2 * RW_WIDTH:3 * RW_WIDTH]

    li = lax.broadcasted_iota(jnp.int32, (LANES, LANES), 0)
    lj = lax.broadcasted_iota(jnp.int32, (LANES, LANES), 1)
    same_head = jnp.where((li // RW_N) == (lj // RW_N), 1.0, 0.0).astype(BF16)

    def head_sum(x):
        xb = x.astype(BF16)
        return jnp.concatenate(
            [_dot(xb[:, p * LANES:(p + 1) * LANES], same_head) for p in range(RW_PAIRS)], axis=1)

    kk = k * kk_ref[...]
    kk = kk * lax.rsqrt(jnp.maximum(head_sum(kk * kk), 1e-24))
    k = k * (1.0 + (a_sig - 1.0) * ka_ref[...])
    bonus = head_sum(r * k * rk_ref[...])

    ti = lax.broadcasted_iota(jnp.int32, (TT, TT), 0)
    tj = lax.broadcasted_iota(jnp.int32, (TT, TT), 1)
    tri = jnp.where(ti >= tj, jnp.where((ti // L) == (tj // L), 1.0, 0.0), 0.0).astype(BF16)
    w_hi = logw.astype(BF16)
    w_r1 = logw - w_hi.astype(F32)
    w_mid = w_r1.astype(BF16)
    w_lo = (w_r1 - w_mid.astype(F32)).astype(BF16)
    cw = _dot(tri, w_hi) + _dot(tri, w_mid) + _dot(tri, w_lo)
    cw_last = [cw[(ci + 1) * L - 1:(ci + 1) * L, :] for ci in range(NCH)]
    e_in = jnp.exp(cw)
    e_out = jnp.exp(-cw)
    e_end = jnp.exp(jnp.concatenate([jnp.broadcast_to(cl, (L, RW_WIDTH)) for cl in cw_last], axis=0) - cw)
    w_end = [jnp.exp(cl) for cl in cw_last]

    at = -kk * jnp.exp(cw - logw)
    rt = r * e_in
    b = kk * a_sig
    bt = b * e_out
    kt = k * e_out
    bd = b * e_end
    kd = k * e_end

    lane = lax.broadcasted_iota(jnp.int32, (L, LANES), 1)
    lo = lane < RW_N

    def stack(x, ci, p):
        xs = x[ci * L:(ci + 1) * L, p * LANES:(p + 1) * LANES]
        return jnp.concatenate([jnp.where(lo, xs, 0.0), jnp.where(lo, 0.0, xs)], axis=0).astype(BF16)

    si = lax.broadcasted_iota(jnp.int32, (2 * L, 2 * L), 0)
    sj = lax.broadcasted_iota(jnp.int32, (2 * L, 2 * L), 1)
    strict = (si % L) > (sj % L)
    incl = (si % L) >= (sj % L)
    eye = jnp.where(si == sj, 1.0, 0.0)

    pairs = range(RW_PAIRS)
    cat0 = lambda *xs: jnp.concatenate(xs, axis=0)
    cat1 = lambda *xs: jnp.concatenate(xs, axis=1)
    bf = lambda x: x.astype(BF16)
    h = [state_ref[p] for p in pairs]
    ys = []
    for ci in range(NCH):
        a_s = [stack(at, ci, p) for p in pairs]
        r_s = [stack(rt, ci, p) for p in pairs]
        v_s = [stack(v, ci, p) for p in pairs]
        bk = [cat0(stack(bt, ci, p), stack(kt, ci, p)) for p in pairs]
        bkd = [cat0(stack(bd, ci, p), stack(kd, ci, p)) for p in pairs]
        aa = [_dot_nt(a_s[p], bk[p]) for p in pairs]
        rr = [_dot_nt(r_s[p], bk[p]) for p in pairs]
        a_ab = [jnp.where(strict, aa[p][:, 0:2 * L], 0.0) for p in pairs]
        a_ak = [bf(jnp.where(strict, aa[p][:, 2 * L:4 * L], 0.0)) for p in pairs]
        a_rb = [bf(jnp.where(incl, rr[p][:, 0:2 * L], 0.0)) for p in pairs]
        a_rk = [bf(jnp.where(incl, rr[p][:, 2 * L:4 * L], 0.0)) for p in pairs]
        ab = [bf(a) for a in a_ab]
        qpow = [_dot(ab[p], ab[p]) for p in pairs]
        inv = [eye + a_ab[p] for p in pairs]
        for _ in range(int(math.log2(L)) - 2):
            qb = [bf(q) for q in qpow]
            res = [_dot(qb[p], cat1(qb[p], bf(inv[p]))) for p in pairs]
            qpow = [res[p][:, 0:2 * L] for p in pairs]
            inv = [inv[p] + res[p][:, 2 * L:4 * L] for p in pairs]
        inv = [bf(inv[p] + _dot(bf(qpow[p]), bf(inv[p]))) for p in pairs]
        h_b = [bf(hp) for hp in h]
        x = [_dot(cat1(a_s[p], a_ak[p]), cat0(h_b[p], v_s[p])) for p in pairs]
        u_b = [bf(_dot(inv[p], bf(x[p]))) for p in pairs]
        y2 = [_dot(cat1(r_s[p], a_rb[p], a_rk[p]), cat0(h_b[p], u_b[p], v_s[p])) for p in pairs]
        w_col = [jnp.broadcast_to(w_end[ci][:, p * LANES:(p + 1) * LANES], (LANES, LANES)).T for p in pairs]
        h = [w_col[p] * h[p] + _dot_tn(bkd[p], cat0(u_b[p], v_s[p])) for p in pairs]
        ys.append(cat1(*[y2[p][0:L] + y2[p][L:2 * L] for p in pairs]))
    for p in pairs:
        state_ref[p] = h[p]
    y = cat0(*ys)

    mean = head_sum(y) * (1.0 / RW_N)
    yc = y - mean
    var = head_sum(yc * yc) * (1.0 / RW_N)
    y = yc * lax.rsqrt(var + LN_X_EPS) * lnw_ref[...] + lnb_ref[...]
    y = y + bonus * v
    o_ref[...] = (y * g).astype(o_ref.dtype)


def _rwkv(proj, mu_rkv, w0, a0, w2cat, k_k, k_a, r_k, ln_w, ln_b, b, s):
    tt = RW_TILE
    nc = s // tt
    row = lambda n: pl.BlockSpec((1, n), lambda bi, c: (0, 0))
    rkv_col0 = 3 * DA_WIDTH // RW_WIDTH
    lora_col0 = IN_COLS // LORA_PAD
    return pl.pallas_call(
        _rwkv_kernel,
        grid=(b, nc),
        in_specs=[
            pl.BlockSpec((tt, RW_WIDTH), lambda bi, c: (bi * nc + c, rkv_col0)),
            pl.BlockSpec((tt, RW_WIDTH), lambda bi, c: (bi * nc + c, rkv_col0 + 1)),
            pl.BlockSpec((tt, RW_WIDTH), lambda bi, c: (bi * nc + c, rkv_col0 + 2)),
            pl.BlockSpec((tt, LORA_PAD), lambda bi, c: (bi * nc + c, lora_col0)),
            pl.BlockSpec((tt, LORA_PAD), lambda bi, c: (bi * nc + c, lora_col0 + 1)),
            pl.BlockSpec((3, RW_WIDTH), lambda bi, c: (0, 0)),
            row(RW_WIDTH), row(RW_WIDTH),
            pl.BlockSpec((LORA_PAD, 3 * RW_WIDTH), lambda bi, c: (0, 0)),
            row(RW_WIDTH), row(RW_WIDTH), row(RW_WIDTH), row(RW_WIDTH), row(RW_WIDTH),
        ],
        out_specs=pl.BlockSpec((tt, RW_WIDTH), lambda bi, c: (bi * nc + c, 0)),
        out_shape=jax.ShapeDtypeStruct((b * s, RW_WIDTH), BF16),
        scratch_shapes=[
            pltpu.VMEM((RW_PAIRS, LANES, LANES), F32),
            pltpu.VMEM((1, 3 * RW_WIDTH), F32),
            pltpu.VMEM((1, LORA_PAD), F32),
        ],
        compiler_params=pltpu.CompilerParams(
            dimension_semantics=("parallel", "arbitrary"), vmem_limit_bytes=VMEM_LIMIT),
        name="rwkv",
    )(proj, proj, proj, proj, proj, mu_rkv, w0, a0, w2cat, k_k, k_a, r_k, ln_w, ln_b)


ROUTE_E, ROUTE_G, ROUTE_R = 0, 2, 4
EXPERT_LANE0 = N_GROUPS


def _outproj_kernel(x_ref, oda_ref, orw_ref, wout_ref, g_ref, wrh_ref, wrl_ref, br_ref,
                    h_ref, xn_ref, route_ref, cnt_ref, base_ref, *, tm):
    @pl.when(pl.program_id(0) == 0)
    def _():
        base_ref[...] = jnp.zeros_like(base_ref)

    mix = _dot(oda_ref[...], wout_ref[0:DA_WIDTH, :]) + _dot(orw_ref[...], wout_ref[DA_WIDTH:, :])
    h = x_ref[...] + mix
    h_ref[...] = h
    ms = jnp.mean(h * h, axis=-1, keepdims=True)
    xn = h * lax.rsqrt(ms + NORM_EPS) * g_ref[...]
    x_hi = xn.astype(BF16)
    _store_row_tiles(xn_ref, _pack_halves(x_hi))

    x_lo = (xn - x_hi.astype(F32)).astype(BF16)
    lg = (_dot(x_hi, wrh_ref[...]) + _dot(x_lo, wrh_ref[...]) + _dot(x_hi, wrl_ref[...])) + br_ref[...]

    lane = lax.broadcasted_iota(jnp.int32, (tm, LANES), 1)
    big = jnp.int32(1 << 20)
    gl = jnp.where(lane < N_GROUPS, lg, NEG)
    gmax = jnp.max(gl, axis=-1, keepdims=True)
    grp = jnp.min(jnp.where(gl == gmax, lane, big), axis=-1, keepdims=True)
    p_grp = 1.0 / jnp.sum(jnp.exp(gl - gmax), axis=-1, keepdims=True)
    eg = jnp.where(lane >= EXPERT_LANE0, (lane - EXPERT_LANE0) // EXPERTS_PER_GROUP, -1)
    el = jnp.where(eg == grp, lg, NEG)
    emax = jnp.max(el, axis=-1, keepdims=True)
    pe = jnp.exp(el - emax)
    probs = pe / jnp.sum(pe, axis=-1, keepdims=True)
    probs = jnp.where(eg == grp, probs, -1.0)
    p1 = jnp.max(probs, axis=-1, keepdims=True)
    i1 = jnp.min(jnp.where(probs == p1, lane, big), axis=-1, keepdims=True)
    probs2 = jnp.where(lane == i1, -1.0, probs)
    p2 = jnp.max(probs2, axis=-1, keepdims=True)
    i2 = jnp.min(jnp.where(probs2 == p2, lane, big), axis=-1, keepdims=True)
    gate1 = p_grp * p1 / (p1 + p2)
    gate2 = p_grp * p2 / (p1 + p2)

    oh1 = jnp.where(lane == i1, 1.0, 0.0)
    oh2 = jnp.where(lane == i2, 1.0, 0.0)
    ri = lax.broadcasted_iota(jnp.int32, (tm, tm), 0)
    rj = lax.broadcasted_iota(jnp.int32, (tm, tm), 1)
    lower = jnp.where(ri > rj, 1.0, 0.0).astype(BF16)
    base = base_ref[...]
    tot1 = jnp.sum(oh1, axis=0, keepdims=True)
    c1 = base + _dot(lower, oh1.astype(BF16))
    c2 = base + tot1 + _dot(lower, oh2.astype(BF16))
    rank1 = jnp.sum(oh1 * c1, axis=-1, keepdims=True)
    rank2 = jnp.sum(oh2 * c2, axis=-1, keepdims=True)
    base = base + tot1 + jnp.sum(oh2, axis=0, keepdims=True)
    base_ref[...] = base
    cnt_ref[...] = base

    e1 = (i1 - EXPERT_LANE0).astype(F32)
    e2 = (i2 - EXPERT_LANE0).astype(F32)
    rec = jnp.zeros((tm, LANES), F32)
    for ln, val in ((ROUTE_E, e1), (ROUTE_E + 1, e2), (ROUTE_G, gate1), (ROUTE_G + 1, gate2),
                    (ROUTE_R, rank1), (ROUTE_R + 1, rank2)):
        rec = jnp.where(lane == ln, val, rec)
    route_ref[...] = rec


def _outproj(x2, o_da, o_rw, w_out, g_ffn, wr_hi, wr_lo, b_r, tm=512):
    t = x2.shape[0]
    const = lambda shape: pl.BlockSpec(shape, lambda i: (0, 0))
    return pl.pallas_call(
        functools.partial(_outproj_kernel, tm=tm),
        grid=(t // tm,),
        in_specs=[
            pl.BlockSpec((tm, D_MODEL), lambda i: (i, 0)),
            pl.BlockSpec((tm, DA_WIDTH), lambda i: (i, 0)),
            pl.BlockSpec((tm, RW_WIDTH), lambda i: (i, 0)),
            const((D_MODEL, D_MODEL)),
            const((1, D_MODEL)),
            const((D_MODEL, LANES)),
            const((D_MODEL, LANES)),
            const((1, LANES)),
        ],
        out_specs=[
            pl.BlockSpec((tm, D_MODEL), lambda i: (i, 0)),
            pl.BlockSpec((tm * SUBLANES, LANES), lambda i: (i, 0)),
            pl.BlockSpec((tm, LANES), lambda i: (i, 0)),
            const((1, LANES)),
        ],
        out_shape=[
            jax.ShapeDtypeStruct((t, D_MODEL), F32),
            jax.ShapeDtypeStruct((t * SUBLANES, LANES), jnp.uint32),
            jax.ShapeDtypeStruct((t, LANES), F32),
            jax.ShapeDtypeStruct((1, LANES), F32),
        ],
        scratch_shapes=[pltpu.VMEM((1, LANES), F32)],
        compiler_params=pltpu.CompilerParams(
            dimension_semantics=("arbitrary",), vmem_limit_bytes=VMEM_LIMIT),
        name="outproj_router",
    )(x2, o_da, o_rw, w_out, g_ffn, wr_hi, wr_lo, b_r)


def _row_copy(src_ref, dst_ref, src_row8, dst_row8, sem):
    src = src_ref.at[pl.ds(pl.multiple_of(src_row8, SUBLANES), SUBLANES)]
    dst = dst_ref.at[pl.ds(pl.multiple_of(dst_row8, SUBLANES), SUBLANES)]
    return pltpu.make_async_copy(src, dst, sem)


def _dispatch_kernel(dest_ref, x_ref, xb_in_hbm, xb_hbm, sem, *, td):
    del xb_in_hbm
    t0 = pl.program_id(0) * td

    def start(i, _):
        tok = t0 + i
        _row_copy(x_ref, xb_hbm, i * SUBLANES, dest_ref[2 * tok], sem).start(priority=0)
        _row_copy(x_ref, xb_hbm, i * SUBLANES, dest_ref[2 * tok + 1], sem).start(priority=1)
        return 0

    lax.fori_loop(0, td, start, 0, unroll=8)
    for _ in range(2):
        pltpu.make_async_copy(x_ref, xb_hbm.at[pl.ds(0, td * SUBLANES)], sem).wait()


def _dispatch(dest_flat, xn2, xb_zero, td=256):
    t = xn2.shape[0] // SUBLANES
    return pl.pallas_call(
        functools.partial(_dispatch_kernel, td=td),
        grid_spec=pltpu.PrefetchScalarGridSpec(
            num_scalar_prefetch=1,
            grid=(t // td,),
            in_specs=[pl.BlockSpec((td * SUBLANES, LANES), lambda i, d: (i, 0)),
                      pl.BlockSpec(memory_space=pl.ANY)],
            out_specs=pl.BlockSpec(memory_space=pl.ANY),
            scratch_shapes=[pltpu.SemaphoreType.DMA],
        ),
        out_shape=jax.ShapeDtypeStruct(xb_zero.shape, xb_zero.dtype),
        input_output_aliases={2: 0},
        compiler_params=pltpu.CompilerParams(dimension_semantics=("arbitrary",)),
        name="dispatch",
    )(dest_flat, xn2, xb_zero)


def _experts_kernel(be_ref, nu_ref, nxt_ref, slot_ref, xb_ref, wg_hbm, wu_hbm, wd_hbm, yb_ref,
                    wgf_ref, wuf_ref, wdf_ref, wgb_ref, wub_ref, wdb_ref, sem):
    i = pl.program_id(0)
    used = i < nu_ref[0]
    new_expert = jnp.logical_or(i == 0, be_ref[i] != be_ref[jnp.maximum(i - 1, 0)])

    def weight_copies(e, slot):
        return [pltpu.make_async_copy(src.at[e], dst.at[slot], sem.at[slot])
                for src, dst in ((wg_hbm, wgf_ref), (wu_hbm, wuf_ref), (wd_hbm, wdf_ref))]

    def swiglu(wg_lo, wg_hi, wu_lo, wu_hi, wd):
        x_lo, x_hi = (half.astype(BF16) for half in _unpack_halves(_load_row_tiles(xb_ref)))
        hg = _dot(x_lo, wg_lo) + _dot(x_hi, wg_hi)
        hu = _dot(x_lo, wu_lo) + _dot(x_hi, wu_hi)
        hid = hg * _sigmoid(hg) * hu
        _store_row_tiles(yb_ref, _pack_halves(_dot(hid.astype(BF16), wd).astype(BF16)))

    @pl.when(jnp.logical_and(used, new_expert))
    def _():
        slot = slot_ref[i]

        @pl.when(i == 0)
        def _():
            for cp in weight_copies(be_ref[i], slot):
                cp.start()

        for cp in weight_copies(be_ref[i], slot):
            cp.wait()

        @pl.when(nxt_ref[i] >= 0)
        def _():
            for cp in weight_copies(nxt_ref[i], 1 - slot):
                cp.start()

        wg, wu, wd = (w[slot].astype(BF16) for w in (wgf_ref, wuf_ref, wdf_ref))
        wgb_ref[...] = wg
        wub_ref[...] = wu
        wdb_ref[...] = wd
        swiglu(wg[0:HALF, :], wg[HALF:, :], wu[0:HALF, :], wu[HALF:, :], wd)

    @pl.when(jnp.logical_and(used, jnp.logical_not(new_expert)))
    def _():
        swiglu(wgb_ref[0:HALF, :], wgb_ref[HALF:, :], wub_ref[0:HALF, :], wub_ref[HALF:, :], wdb_ref[...])

    @pl.when(jnp.logical_not(used))
    def _():
        yb_ref[...] = jnp.zeros_like(yb_ref)


def _experts(block_e, n_used, next_e, slot, xb, w_gate, w_up, w_down):
    block_rows = EXPERT_BLOCK * SUBLANES
    nb = xb.shape[0] // block_rows
    return pl.pallas_call(
        _experts_kernel,
        grid_spec=pltpu.PrefetchScalarGridSpec(
            num_scalar_prefetch=4,
            grid=(nb,),
            in_specs=[
                pl.BlockSpec((block_rows, LANES), lambda i, be, nu, *_: (jnp.minimum(i, nu[0] - 1), 0)),
                pl.BlockSpec(memory_space=pl.ANY),
                pl.BlockSpec(memory_space=pl.ANY),
                pl.BlockSpec(memory_space=pl.ANY),
            ],
            out_specs=pl.BlockSpec((block_rows, LANES), lambda i, *_: (i, 0)),
            scratch_shapes=[
                pltpu.VMEM((2, D_MODEL, D_EXPERT), F32),
                pltpu.VMEM((2, D_MODEL, D_EXPERT), F32),
                pltpu.VMEM((2, D_EXPERT, D_MODEL), F32),
                pltpu.VMEM((D_MODEL, D_EXPERT), BF16),
                pltpu.VMEM((D_MODEL, D_EXPERT), BF16),
                pltpu.VMEM((D_EXPERT, D_MODEL), BF16),
                pltpu.SemaphoreType.DMA((2,)),
            ],
        ),
        out_shape=jax.ShapeDtypeStruct(xb.shape, jnp.uint32),
        compiler_params=pltpu.CompilerParams(
            dimension_semantics=("arbitrary",), vmem_limit_bytes=EXPERTS_VMEM_LIMIT),
        name="experts",
    )(block_e, n_used, next_e, slot, xb, w_gate, w_up, w_down)


def _combine_kernel(dest_ref, route_ref, h_ref, g_ref, yb_hbm, o_ref, y1_ref, y2_ref, sem, *, tc):
    step = pl.program_id(0)
    n_steps = pl.num_programs(0)

    def gather(tile, slot):
        def start(i, _):
            tok = tile * tc + i
            _row_copy(yb_hbm, y1_ref.at[slot], dest_ref[2 * tok], i * SUBLANES, sem.at[slot]).start(priority=0)
            _row_copy(yb_hbm, y2_ref.at[slot], dest_ref[2 * tok + 1], i * SUBLANES, sem.at[slot]).start(priority=1)
            return 0

        lax.fori_loop(0, tc, start, 0, unroll=8)

    @pl.when(step == 0)
    def _():
        gather(0, 0)

    @pl.when(step + 1 < n_steps)
    def _():
        gather(step + 1, (step + 1) % 2)

    slot = step % 2
    pltpu.make_async_copy(yb_hbm.at[pl.ds(0, tc * SUBLANES)], y1_ref.at[slot], sem.at[slot]).wait()
    pltpu.make_async_copy(yb_hbm.at[pl.ds(0, tc * SUBLANES)], y2_ref.at[slot], sem.at[slot]).wait()

    rec = route_ref[...]
    lane = lax.broadcasted_iota(jnp.int32, (tc, LANES), 1)
    g1 = jnp.sum(jnp.where(lane == ROUTE_G, rec, 0.0), axis=-1, keepdims=True)
    g2 = jnp.sum(jnp.where(lane == ROUTE_G + 1, rec, 0.0), axis=-1, keepdims=True)
    y1_lo, y1_hi = _unpack_halves(_load_row_tiles(y1_ref.at[slot]))
    y2_lo, y2_hi = _unpack_halves(_load_row_tiles(y2_ref.at[slot]))
    h_lo = h_ref[:, 0:HALF] + (y1_lo * g1 + y2_lo * g2)
    h_hi = h_ref[:, HALF:] + (y1_hi * g1 + y2_hi * g2)
    ms = (jnp.sum(h_lo * h_lo, axis=-1, keepdims=True)
          + jnp.sum(h_hi * h_hi, axis=-1, keepdims=True)) * (1.0 / D_MODEL)
    scale = lax.rsqrt(ms + NORM_EPS)
    o_ref[:, 0:HALF] = h_lo * scale * g_ref[:, 0:HALF]
    o_ref[:, HALF:] = h_hi * scale * g_ref[:, HALF:]


def _combine(dest_flat, route, h1, g_final, yb, tc=256):
    t = h1.shape[0]
    return pl.pallas_call(
        functools.partial(_combine_kernel, tc=tc),
        grid_spec=pltpu.PrefetchScalarGridSpec(
            num_scalar_prefetch=1,
            grid=(t // tc,),
            in_specs=[
                pl.BlockSpec((tc, LANES), lambda i, d: (i, 0)),
                pl.BlockSpec((tc, D_MODEL), lambda i, d: (i, 0)),
                pl.BlockSpec((1, D_MODEL), lambda i, d: (0, 0)),
                pl.BlockSpec(memory_space=pl.ANY),
            ],
            out_specs=pl.BlockSpec((tc, D_MODEL), lambda i, d: (i, 0)),
            scratch_shapes=[
                pltpu.VMEM((2, tc * SUBLANES, LANES), jnp.uint32),
                pltpu.VMEM((2, tc * SUBLANES, LANES), jnp.uint32),
                pltpu.SemaphoreType.DMA((2,)),
            ],
        ),
        out_shape=jax.ShapeDtypeStruct((t, D_MODEL), F32),
        compiler_params=pltpu.CompilerParams(
            dimension_semantics=("arbitrary",), vmem_limit_bytes=VMEM_LIMIT),
        name="combine",
    )(dest_flat, route, h1, g_final, yb)


def _pad_cols(w, n):
    return jnp.pad(w, ((0, 0), (0, n - w.shape[1])))


def kernel(x, g_mix, w_in, w_out, da_lambda_q1, da_lambda_k1, da_lambda_q2, da_lambda_k2, da_subln_g,
           rw_mu_x, rw_mu_rkv, rw_w0, rw_w1, rw_w2, rw_a0, rw_a1, rw_a2, rw_g1, rw_g2, rw_k_k, rw_k_a,
           rw_r_k, rw_ln_w, rw_ln_b, g_ffn, moe_w_group, moe_b_group, moe_w_expert, moe_b_expert,
           moe_w_gate, moe_w_up, moe_w_down, g_final):
    b, s, d = x.shape
    t = b * s
    x2 = x.reshape(t, d)

    mu = rw_mu_x[0]
    lora = (rw_w1[0], rw_a1[0], rw_g1[0])
    keep = _pad_cols(jnp.concatenate([(1.0 - mu[i])[:, None] * w for i, w in enumerate(lora)], axis=1), LORA_PAD)
    prev = _pad_cols(jnp.concatenate([mu[i][:, None] * w for i, w in enumerate(lora)], axis=1), LORA_PAD)
    w_all = jnp.concatenate([w_in[0], keep, prev], axis=1).astype(BF16)
    proj = _inproj(x2, g_mix, w_all)

    slopes = jnp.asarray([2.0 ** (-8.0 * (i + 1) / DA_HEADS) for i in range(DA_HEADS)], F32)
    lam4 = jnp.concatenate([da_lambda_q1, da_lambda_k1, da_lambda_q2, da_lambda_k2], axis=0)
    o_da = _diffattn(proj, slopes, lam4, da_subln_g.reshape(DA_V_DIM, 1), b, s)

    w2cat = jnp.zeros((LORA_PAD, 3 * RW_WIDTH), F32)
    w2cat = w2cat.at[0:LORA_W, 0:RW_WIDTH].set(rw_w2[0])
    w2cat = w2cat.at[LORA_W:LORA_W + LORA_A, RW_WIDTH:2 * RW_WIDTH].set(rw_a2[0])
    w2cat = w2cat.at[LORA_W + LORA_A:LORA_W + LORA_A + LORA_G, 2 * RW_WIDTH:].set(rw_g2[0])
    o_rw = _rwkv(proj, rw_mu_rkv[0], rw_w0, rw_a0, w2cat.astype(BF16), rw_k_k, rw_k_a,
                 rw_r_k.reshape(1, RW_WIDTH), rw_ln_w, rw_ln_b, b, s)

    w_r = _pad_cols(jnp.concatenate([moe_w_group[0], moe_w_expert[0]], axis=1), LANES)
    wr_hi = w_r.astype(BF16)
    wr_lo = (w_r - wr_hi.astype(F32)).astype(BF16)
    b_r = _pad_cols(jnp.concatenate([moe_b_group[0], moe_b_expert[0].reshape(-1)])[None, :], LANES)
    h1, xn2, route, cnt = _outproj(x2, o_da, o_rw, w_out[0].astype(BF16), g_ffn, wr_hi, wr_lo, b_r)

    counts = cnt[0, EXPERT_LANE0:EXPERT_LANE0 + N_EXPERTS].astype(jnp.int32)
    pcounts = ((counts + EXPERT_BLOCK - 1) // EXPERT_BLOCK) * EXPERT_BLOCK
    pends = jnp.cumsum(pcounts)
    pstarts = pends - pcounts
    e_idx = route[:, ROUTE_E:ROUTE_E + 2].astype(jnp.int32)
    rank = route[:, ROUTE_R:ROUTE_R + 2].astype(jnp.int32)
    expert_ids = jnp.arange(N_EXPERTS, dtype=jnp.int32)
    row0 = jnp.sum(jnp.where(e_idx[..., None] == expert_ids, pstarts, 0), axis=-1)
    dest = ((row0 + rank) * SUBLANES).reshape(-1)
    n_blocks = (2 * t) // EXPERT_BLOCK + N_EXPERTS
    block_row0 = jnp.arange(n_blocks, dtype=jnp.int32) * EXPERT_BLOCK
    block_e = jnp.minimum(jnp.sum(pends[None, :] <= block_row0[:, None], axis=1), N_EXPERTS - 1).astype(jnp.int32)
    n_used = (pends[-1:] // EXPERT_BLOCK).astype(jnp.int32)
    has_rows = counts > 0
    later = (expert_ids[None, :] > expert_ids[:, None]) & has_rows[None, :]
    next_used = jnp.min(jnp.where(later, expert_ids[None, :], N_EXPERTS), axis=1)
    next_used = jnp.where(next_used == N_EXPERTS, -1, next_used).astype(jnp.int32)
    buffer_id = ((jnp.cumsum(has_rows.astype(jnp.int32)) - 1) % 2).astype(jnp.int32)

    xb = _dispatch(dest, xn2, jnp.zeros((n_blocks * EXPERT_BLOCK * SUBLANES, LANES), jnp.uint32))
    yb = _experts(block_e, n_used, next_used[block_e], buffer_id[block_e], xb,
                  moe_w_gate[0], moe_w_up[0], moe_w_down[0])
    out = _combine(dest, route, h1, g_final[None, :], yb)
    return out.reshape(b, s, d)
```

```python
import functools
import math

import jax
import jax.numpy as jnp
from jax import lax
from jax.experimental import pallas as pl
from jax.experimental.pallas import tpu as pltpu

F32 = jnp.float32
BF16 = jnp.bfloat16

D_MODEL = 2048
DA_WIDTH = 1024
RW_WIDTH = 1024
DA_HEADS = 8
DA_V_DIM = 128
DA_QK_DIM = 64
RW_N = 64
RW_PAIRS = RW_WIDTH // 128
LORA_W, LORA_A, LORA_G = 64, 64, 160
LORA_PAD = 384
IN_COLS = 3 * DA_WIDTH + 3 * RW_WIDTH
NC = IN_COLS + 2 * LORA_PAD
N_GROUPS = 4
EXPERTS_PER_GROUP = 8
N_EXPERTS = 32
D_EXPERT = 512
NORM_EPS = 1e-6
LN_X_EPS = 64e-5
LAM_INIT = 0.8 - 0.6 * math.exp(-0.3 * 0)
LANES = 128
CHUNK = 64
RW_TILE = 256
NCH = RW_TILE // CHUNK
EXPERT_BLOCK = 256
NEG = -1e30
VMEM_LIMIT = 48 * 1024 * 1024
EXPERTS_VMEM_LIMIT = 56 * 1024 * 1024


def _dot(a, b):
    return jnp.dot(a, b, preferred_element_type=F32)


def _dot_nt(a, b):
    return lax.dot_general(a, b, (((1,), (1,)), ((), ())), preferred_element_type=F32)


def _dot_tn(a, b):
    return lax.dot_general(a, b, (((0,), (0,)), ((), ())), preferred_element_type=F32)


def _sigmoid(x):
    return 1.0 / (1.0 + jnp.exp(-x))


HALF = D_MODEL // 2


def _pack_halves(xb):
    lo = lax.bitcast_convert_type(xb[:, :HALF].astype(F32), jnp.uint32) >> 16
    hi = lax.bitcast_convert_type(xb[:, HALF:].astype(F32), jnp.uint32) & jnp.uint32(0xFFFF0000)
    return hi | lo


def _unpack_halves(u):
    lo = lax.bitcast_convert_type(u << 16, F32)
    hi = lax.bitcast_convert_type(u & jnp.uint32(0xFFFF0000), F32)
    return lo, hi


SUBLANES = 8
assert HALF == SUBLANES * LANES


def _store_row_tiles(ref, u):
    n = u.shape[0]
    for k in range(SUBLANES):
        ref[pl.ds(k, n, stride=SUBLANES), :] = u[:, k * LANES:(k + 1) * LANES]


def _load_row_tiles(ref):
    n = ref.shape[0] // SUBLANES
    return jnp.concatenate([ref[pl.ds(k, n, stride=SUBLANES), :] for k in range(SUBLANES)], axis=1)


def _inproj_kernel(x_ref, g_ref, w_ref, o_ref, xn_ref):
    @pl.when(pl.program_id(1) == 0)
    def _():
        x = x_ref[...]
        ms = jnp.mean(x * x, axis=-1, keepdims=True)
        xn_ref[...] = (x * lax.rsqrt(ms + NORM_EPS) * g_ref[...]).astype(BF16)

    o_ref[...] = _dot(xn_ref[...], w_ref[...]).astype(o_ref.dtype)


def _inproj(x2, g, w_all, tm=1024, tn=768):
    t = x2.shape[0]
    tm = min(tm, t)
    return pl.pallas_call(
        _inproj_kernel,
        grid=(t // tm, NC // tn),
        in_specs=[
            pl.BlockSpec((tm, D_MODEL), lambda i, j: (i, 0)),
            pl.BlockSpec((1, D_MODEL), lambda i, j: (0, 0)),
            pl.BlockSpec((D_MODEL, tn), lambda i, j: (0, j)),
        ],
        out_specs=pl.BlockSpec((tm, tn), lambda i, j: (i, j)),
        out_shape=jax.ShapeDtypeStruct((t, NC), BF16),
        scratch_shapes=[pltpu.VMEM((tm, D_MODEL), BF16)],
        compiler_params=pltpu.CompilerParams(
            dimension_semantics=("parallel", "arbitrary"), vmem_limit_bytes=VMEM_LIMIT),
        name="inproj",
    )(x2, g, w_all)


DA_HEADS_PER_STEP = 2


def _da_kernel(qi_tab, kj_tab, slopes_ref, lam_ref, subg_ref, q_ref, k_ref, v_ref, o_ref,
               vt_ref, qz_ref, bias_ref, m_ref, l_ref, lfin_ref, acc_ref, s_ref, p_ref, alpha_ref,
               *, tq, seq):
    tk = tq
    nq = seq // tq
    n_trips = nq * (nq + 1) // 2
    heads = range(DA_HEADS_PER_STEP)
    chains = [(hd, c) for hd in heads for c in range(2)]
    slope = [slopes_ref[pl.program_id(1) * DA_HEADS_PER_STEP + hd] for hd in heads]
    cols = lambda hd: slice(hd * LANES, (hd + 1) * LANES)

    lane = lax.broadcasted_iota(jnp.int32, (tq, LANES), 1)
    krow = lax.broadcasted_iota(jnp.int32, (tk, tq), 0)
    qcol = lax.broadcasted_iota(jnp.int32, (tk, tq), 1)
    for hd in heads:
        for blk in range(seq // LANES):
            r = slice(blk * LANES, (blk + 1) * LANES)
            vt_ref[hd, :, r] = v_ref[r, cols(hd)].astype(F32).T.astype(BF16)
        for blk in range(nq):
            r = slice(blk * tq, (blk + 1) * tq)
            q = q_ref[r, cols(hd)] * jnp.asarray(DA_QK_DIM ** -0.5, BF16)
            qz_ref[2 * hd, r, :] = jnp.where(lane < DA_QK_DIM, q, jnp.zeros_like(q))
            qz_ref[2 * hd + 1, r, :] = jnp.where(lane >= DA_QK_DIM, q, jnp.zeros_like(q))
        alibi = slope[hd] * krow.astype(F32)
        bias_ref[hd, 0] = alibi
        bias_ref[hd, 1] = jnp.where(krow <= qcol, alibi, NEG)

    m_ref[...] = jnp.full_like(m_ref, NEG)
    l_ref[...] = jnp.zeros_like(l_ref)
    lfin_ref[...] = jnp.ones_like(lfin_ref)
    acc_ref[...] = jnp.zeros_like(acc_ref)
    p_ref[...] = jnp.zeros_like(p_ref)
    alpha_ref[...] = jnp.zeros_like(alpha_ref)

    lv = lam_ref[...]
    lam = (jnp.exp(jnp.sum(lv[0:1] * lv[1:2], axis=-1, keepdims=True))
           - jnp.exp(jnp.sum(lv[2:3] * lv[3:4], axis=-1, keepdims=True)) + LAM_INIT)

    def scores_to_scratch(t):
        qs = pl.multiple_of(qi_tab[t] * tq, tq)
        ks = pl.multiple_of(kj_tab[t] * tk, tk)
        diag = (qi_tab[t] == kj_tab[t]).astype(jnp.int32)
        for hd, c in chains:
            ch = 2 * hd + c
            s_ref[ch] = (_dot_nt(k_ref[pl.ds(ks, tk), cols(hd)], qz_ref[ch, pl.ds(qs, tq), :])
                         + bias_ref[hd, diag])

    def values_from_scratch(t):
        ks = pl.multiple_of(kj_tab[t] * tk, tk)
        for hd, c in chains:
            ch = 2 * hd + c
            acc_ref[ch] = alpha_ref[ch:ch + 1, :] * acc_ref[ch] + _dot(vt_ref[hd, :, pl.ds(ks, tk)], p_ref[ch])

    def softmax(t):
        first = kj_tab[t] == 0
        diag = kj_tab[t] == qi_tab[t]
        rel = ((kj_tab[t] - qi_tab[t]) * tk).astype(F32)
        p, alpha = [], []
        for hd, c in chains:
            ch = 2 * hd + c
            off = slope[hd] * rel
            s = s_ref[ch]
            m = jnp.where(first, NEG, m_ref[ch:ch + 1, :])
            l = jnp.where(first, 0.0, l_ref[ch:ch + 1, :])
            m_new = jnp.maximum(m, jnp.max(s, axis=0, keepdims=True) + off)
            pc = jnp.exp(s - (m_new - off))
            a = jnp.exp(m - m_new)
            l_new = a * l + jnp.sum(pc, axis=0, keepdims=True)
            m_ref[ch:ch + 1, :] = m_new
            l_ref[ch:ch + 1, :] = l_new
            lfin_ref[ch:ch + 1, :] = jnp.where(diag, l_new, lfin_ref[ch:ch + 1, :])
            p.append(pc)
            alpha.append(a)
        return p, alpha

    def publish(p, alpha):
        for ch in range(len(chains)):
            alpha_ref[ch:ch + 1, :] = alpha[ch]
            p_ref[ch] = p[ch].astype(BF16)

    def finalize(t):
        qs = pl.multiple_of(qi_tab[t] * tq, tq)
        for hd in heads:
            c0, c1 = 2 * hd, 2 * hd + 1
            o = acc_ref[c0] / lfin_ref[c0:c0 + 1, :] - lam * (acc_ref[c1] / lfin_ref[c1:c1 + 1, :])
            ms = jnp.mean(o * o, axis=0, keepdims=True)
            o = o * lax.rsqrt(ms + NORM_EPS) * subg_ref[...] * (1.0 - LAM_INIT)
            o_ref[pl.ds(qs, tq), cols(hd)] = o.T.astype(o_ref.dtype)

    scores_to_scratch(0)

    def body(t, _):
        prev = jnp.maximum(t - 1, 0)
        p, alpha = softmax(t)
        values_from_scratch(prev)
        scores_to_scratch(jnp.minimum(t + 1, n_trips - 1))
        publish(p, alpha)

        @pl.when(jnp.logical_and(t > 0, qi_tab[prev] == kj_tab[prev]))
        def _():
            finalize(prev)
        return 0

    lax.fori_loop(0, n_trips, body, 0)
    values_from_scratch(n_trips - 1)
    finalize(n_trips - 1)


def _diffattn(proj, slopes, lam4, subg_col, b, s, tq=256):
    nq = s // tq
    nh = DA_HEADS_PER_STEP
    groups = DA_HEADS // nh
    width = nh * LANES
    pairs = [(qi, kj) for qi in range(nq) for kj in range(qi + 1)]
    qi_tab = jnp.asarray([p[0] for p in pairs], jnp.int32)
    kj_tab = jnp.asarray([p[1] for p in pairs], jnp.int32)
    return pl.pallas_call(
        functools.partial(_da_kernel, tq=tq, seq=s),
        grid_spec=pltpu.PrefetchScalarGridSpec(
            num_scalar_prefetch=2,
            grid=(b, groups),
            in_specs=[
                pl.BlockSpec(memory_space=pltpu.SMEM),
                pl.BlockSpec((4, DA_QK_DIM), lambda bi, g, *_: (0, 0)),
                pl.BlockSpec((DA_V_DIM, 1), lambda bi, g, *_: (0, 0)),
                pl.BlockSpec((s, width), lambda bi, g, *_: (bi, g)),
                pl.BlockSpec((s, width), lambda bi, g, *_: (bi, groups + g)),
                pl.BlockSpec((s, width), lambda bi, g, *_: (bi, 2 * groups + g)),
            ],
            out_specs=pl.BlockSpec((s, width), lambda bi, g, *_: (bi, g)),
            scratch_shapes=[
                pltpu.VMEM((nh, DA_V_DIM, s), BF16),
                pltpu.VMEM((2 * nh, s, LANES), BF16),
                pltpu.VMEM((nh, 2, tq, tq), F32),
                pltpu.VMEM((2 * nh, tq), F32),
                pltpu.VMEM((2 * nh, tq), F32),
                pltpu.VMEM((2 * nh, tq), F32),
                pltpu.VMEM((2 * nh, DA_V_DIM, tq), F32),
                pltpu.VMEM((2 * nh, tq, tq), F32),
                pltpu.VMEM((2 * nh, tq, tq), BF16),
                pltpu.VMEM((2 * nh, tq), F32),
            ],
        ),
        out_shape=jax.ShapeDtypeStruct((b * s, DA_WIDTH), BF16),
        compiler_params=pltpu.CompilerParams(
            dimension_semantics=("parallel", "arbitrary"), vmem_limit_bytes=VMEM_LIMIT),
        name="diffattn",
    )(qi_tab, kj_tab, slopes, lam4, subg_col, proj, proj, proj)


def _rwkv_kernel(r_ref, k_ref, v_ref, p1_ref, p2_ref, mu_ref, w0_ref, a0_ref, w2_ref, kk_ref, ka_ref,
                 rk_ref, lnw_ref, lnb_ref, o_ref, state_ref, crkv_ref, cp2_ref):
    L, TT = CHUNK, RW_TILE
    c = pl.program_id(1)

    @pl.when(c == 0)
    def _():
        state_ref[...] = jnp.zeros_like(state_ref)
        crkv_ref[...] = jnp.zeros_like(crkv_ref)
        cp2_ref[...] = jnp.zeros_like(cp2_ref)

    first = lax.broadcasted_iota(jnp.int32, (TT, 1), 0) == 0

    def shift(x, carry):
        return jnp.where(first, carry, pltpu.roll(x, 1, 0))

    r_raw = r_ref[...].astype(F32)
    k_raw = k_ref[...].astype(F32)
    v_raw = v_ref[...].astype(F32)
    p2 = p2_ref[...].astype(F32)
    rs = shift(r_raw, crkv_ref[:, 0:RW_WIDTH])
    ks = shift(k_raw, crkv_ref[:, RW_WIDTH:2 * RW_WIDTH])
    vs = shift(v_raw, crkv_ref[:, 2 * RW_WIDTH:3 * RW_WIDTH])
    lin = p1_ref[...].astype(F32) + shift(p2, cp2_ref[...])
    crkv_ref[:, 0:RW_WIDTH] = r_raw[TT - 1:TT, :]
    crkv_ref[:, RW_WIDTH:2 * RW_WIDTH] = k_raw[TT - 1:TT, :]
    crkv_ref[:, 2 * RW_WIDTH:3 * RW_WIDTH] = v_raw[TT - 1:TT, :]
    cp2_ref[...] = p2[TT - 1:TT, :]

    r = r_raw + (rs - r_raw) * mu_ref[0:1, :]
    k = k_raw + (ks - k_raw) * mu_ref[1:2, :]
    v = v_raw + (vs - v_raw) * mu_ref[2:3, :]

    ll = lax.broadcasted_iota(jnp.int32, (TT, LORA_PAD), 1)
    z = jnp.where(ll < LORA_W, jnp.tanh(lin),
                  jnp.where(ll < LORA_W + LORA_A, lin,
                            jnp.where(ll < LORA_W + LORA_A + LORA_G, _sigmoid(lin), 0.0)))
    d = _dot(z.astype(BF16), w2_ref[...])
    logw = -_sigmoid(w0_ref[...] + d[:, 0:RW_WIDTH]) * math.exp(-0.5)
    a_sig = _sigmoid(a0_ref[...] + d[:, RW_WIDTH:2 * RW_WIDTH])
    g = d[:, 2 * RW_WIDTH:3 * RW_WIDTH]

    li = lax.broadcasted_iota(jnp.int32, (LANES, LANES), 0)
    lj = lax.broadcasted_iota(jnp.int32, (LANES, LANES), 1)
    same_head = jnp.where((li // RW_N) == (lj // RW_N), 1.0, 0.0).astype(BF16)

    def head_sum(x):
        xb = x.astype(BF16)
        return jnp.concatenate(
            [_dot(xb[:, p * LANES:(p + 1) * LANES], same_head) for p in range(RW_PAIRS)], axis=1)

    kk = k * kk_ref[...]
    kk = kk * lax.rsqrt(jnp.maximum(head_sum(kk * kk), 1e-24))
    k = k * (1.0 + (a_sig - 1.0) * ka_ref[...])
    bonus = head_sum(r * k * rk_ref[...])

    ti = lax.broadcasted_iota(jnp.int32, (TT, TT), 0)
    tj = lax.broadcasted_iota(jnp.int32, (TT, TT), 1)
    tri = jnp.where(ti >= tj, jnp.where((ti // L) == (tj // L), 1.0, 0.0), 0.0).astype(BF16)
    w_hi = logw.astype(BF16)
    w_r1 = logw - w_hi.astype(F32)
    w_mid = w_r1.astype(BF16)
    w_lo = (w_r1 - w_mid.astype(F32)).astype(BF16)
    cw = _dot(tri, w_hi) + _dot(tri, w_mid) + _dot(tri, w_lo)
    cw_last = [cw[(ci + 1) * L - 1:(ci + 1) * L, :] for ci in range(NCH)]
    e_in = jnp.exp(cw)
    e_out = jnp.exp(-cw)
    e_end = jnp.exp(jnp.concatenate([jnp.broadcast_to(cl, (L, RW_WIDTH)) for cl in cw_last], axis=0) - cw)
    w_end = [jnp.exp(cl) for cl in cw_last]

    at = -kk * jnp.exp(cw - logw)
    rt = r * e_in
    b = kk * a_sig
    bt = b * e_out
    kt = k * e_out
    bd = b * e_end
    kd = k * e_end

    lane = lax.broadcasted_iota(jnp.int32, (L, LANES), 1)
    lo = lane < RW_N

    def stack(x, ci, p):
        xs = x[ci * L:(ci + 1) * L, p * LANES:(p + 1) * LANES]
        return jnp.concatenate([jnp.where(lo, xs, 0.0), jnp.where(lo, 0.0, xs)], axis=0).astype(BF16)

    si = lax.broadcasted_iota(jnp.int32, (2 * L, 2 * L), 0)
    sj = lax.broadcasted_iota(jnp.int32, (2 * L, 2 * L), 1)
    strict = (si % L) > (sj % L)
    incl = (si % L) >= (sj % L)
    eye = jnp.where(si == sj, 1.0, 0.0)

    pairs = range(RW_PAIRS)
    cat0 = lambda *xs: jnp.concatenate(xs, axis=0)
    cat1 = lambda *xs: jnp.concatenate(xs, axis=1)
    bf = lambda x: x.astype(BF16)
    h = [state_ref[p] for p in pairs]
    ys = []
    for ci in range(NCH):
        a_s = [stack(at, ci, p) for p in pairs]
        r_s = [stack(rt, ci, p) for p in pairs]
        v_s = [stack(v, ci, p) for p in pairs]
        bk = [cat0(stack(bt, ci, p), stack(kt, ci, p)) for p in pairs]
        bkd = [cat0(stack(bd, ci, p), stack(kd, ci, p)) for p in pairs]
        aa = [_dot_nt(a_s[p], bk[p]) for p in pairs]
        rr = [_dot_nt(r_s[p], bk[p]) for p in pairs]
        a_ab = [jnp.where(strict, aa[p][:, 0:2 * L], 0.0) for p in pairs]
        a_ak = [bf(jnp.where(strict, aa[p][:, 2 * L:4 * L], 0.0)) for p in pairs]
        a_rb = [bf(jnp.where(incl, rr[p][:, 0:2 * L], 0.0)) for p in pairs]
        a_rk = [bf(jnp.where(incl, rr[p][:, 2 * L:4 * L], 0.0)) for p in pairs]
        ab = [bf(a) for a in a_ab]
        qpow = [_dot(ab[p], ab[p]) for p in pairs]
        inv = [eye + a_ab[p] for p in pairs]
        for _ in range(int(math.log2(L)) - 2):
            qb = [bf(q) for q in qpow]
            res = [_dot(qb[p], cat1(qb[p], bf(inv[p]))) for p in pairs]
            qpow = [res[p][:, 0:2 * L] for p in pairs]
            inv = [inv[p] + res[p][:, 2 * L:4 * L] for p in pairs]
        inv = [bf(inv[p] + _dot(bf(qpow[p]), bf(inv[p]))) for p in pairs]
        h_b = [bf(hp) for hp in h]
        x = [_dot(cat1(a_s[p], a_ak[p]), cat0(h_b[p], v_s[p])) for p in pairs]
        u_b = [bf(_dot(inv[p], bf(x[p]))) for p in pairs]
        y2 = [_dot(cat1(r_s[p], a_rb[p], a_rk[p]), cat0(h_b[p], u_b[p], v_s[p])) for p in pairs]
        w_col = [jnp.broadcast_to(w_end[ci][:, p * LANES:(p + 1) * LANES], (LANES, LANES)).T for p in pairs]
        h = [w_col[p] * h[p] + _dot_tn(bkd[p], cat0(u_b[p], v_s[p])) for p in pairs]
        ys.append(cat1(*[y2[p][0:L] + y2[p][L:2 * L] for p in pairs]))
    for p in pairs:
        state_ref[p] = h[p]
    y = cat0(*ys)

    mean = head_sum(y) * (1.0 / RW_N)
    yc = y - mean
    var = head_sum(yc * yc) * (1.0 / RW_N)
    y = yc * lax.rsqrt(var + LN_X_EPS) * lnw_ref[...] + lnb_ref[...]
    y = y + bonus * v
    o_ref[...] = (y * g).astype(o_ref.dtype)


def _rwkv(proj, mu_rkv, w0, a0, w2cat, k_k, k_a, r_k, ln_w, ln_b, b, s):
    tt = RW_TILE
    nc = s // tt
    row = lambda n: pl.BlockSpec((1, n), lambda bi, c: (0, 0))
    rkv_col0 = 3 * DA_WIDTH // RW_WIDTH
    lora_col0 = IN_COLS // LORA_PAD
    return pl.pallas_call(
        _rwkv_kernel,
        grid=(b, nc),
        in_specs=[
            pl.BlockSpec((tt, RW_WIDTH), lambda bi, c: (bi * nc + c, rkv_col0)),
            pl.BlockSpec((tt, RW_WIDTH), lambda bi, c: (bi * nc + c, rkv_col0 + 1)),
            pl.BlockSpec((tt, RW_WIDTH), lambda bi, c: (bi * nc + c, rkv_col0 + 2)),
            pl.BlockSpec((tt, LORA_PAD), lambda bi, c: (bi * nc + c, lora_col0)),
            pl.BlockSpec((tt, LORA_PAD), lambda bi, c: (bi * nc + c, lora_col0 + 1)),
            pl.BlockSpec((3, RW_WIDTH), lambda bi, c: (0, 0)),
            row(RW_WIDTH), row(RW_WIDTH),
            pl.BlockSpec((LORA_PAD, 3 * RW_WIDTH), lambda bi, c: (0, 0)),
            row(RW_WIDTH), row(RW_WIDTH), row(RW_WIDTH), row(RW_WIDTH), row(RW_WIDTH),
        ],
        out_specs=pl.BlockSpec((tt, RW_WIDTH), lambda bi, c: (bi * nc + c, 0)),
        out_shape=jax.ShapeDtypeStruct((b * s, RW_WIDTH), BF16),
        scratch_shapes=[
            pltpu.VMEM((RW_PAIRS, LANES, LANES), F32),
            pltpu.VMEM((1, 3 * RW_WIDTH), F32),
            pltpu.VMEM((1, LORA_PAD), F32),
        ],
        compiler_params=pltpu.CompilerParams(
            dimension_semantics=("parallel", "arbitrary"), vmem_limit_bytes=VMEM_LIMIT),
        name="rwkv",
    )(proj, proj, proj, proj, proj, mu_rkv, w0, a0, w2cat, k_k, k_a, r_k, ln_w, ln_b)


ROUTE_E, ROUTE_G, ROUTE_R = 0, 2, 4
EXPERT_LANE0 = N_GROUPS


def _outproj_kernel(x_ref, oda_ref, orw_ref, wout_ref, g_ref, wrh_ref, wrl_ref, br_ref,
                    h_ref, xn_ref, route_ref, cnt_ref, base_ref, *, tm):
    @pl.when(pl.program_id(0) == 0)
    def _():
        base_ref[...] = jnp.zeros_like(base_ref)

    mix = _dot(oda_ref[...], wout_ref[0:DA_WIDTH, :]) + _dot(orw_ref[...], wout_ref[DA_WIDTH:, :])
    h = x_ref[...] + mix
    h_ref[...] = h
    ms = jnp.mean(h * h, axis=-1, keepdims=True)
    xn = h * lax.rsqrt(ms + NORM_EPS) * g_ref[...]
    x_hi = xn.astype(BF16)
    _store_row_tiles(xn_ref, _pack_halves(x_hi))

    x_lo = (xn - x_hi.astype(F32)).astype(BF16)
    lg = (_dot(x_hi, wrh_ref[...]) + _dot(x_lo, wrh_ref[...]) + _dot(x_hi, wrl_ref[...])) + br_ref[...]

    lane = lax.broadcasted_iota(jnp.int32, (tm, LANES), 1)
    big = jnp.int32(1 << 20)
    gl = jnp.where(lane < N_GROUPS, lg, NEG)
    gmax = jnp.max(gl, axis=-1, keepdims=True)
    grp = jnp.min(jnp.where(gl == gmax, lane, big), axis=-1, keepdims=True)
    p_grp = 1.0 / jnp.sum(jnp.exp(gl - gmax), axis=-1, keepdims=True)
    eg = jnp.where(lane >= EXPERT_LANE0, (lane - EXPERT_LANE0) // EXPERTS_PER_GROUP, -1)
    el = jnp.where(eg == grp, lg, NEG)
    emax = jnp.max(el, axis=-1, keepdims=True)
    pe = jnp.exp(el - emax)
    probs = pe / jnp.sum(pe, axis=-1, keepdims=True)
    probs = jnp.where(eg == grp, probs, -1.0)
    p1 = jnp.max(probs, axis=-1, keepdims=True)
    i1 = jnp.min(jnp.where(probs == p1, lane, big), axis=-1, keepdims=True)
    probs2 = jnp.where(lane == i1, -1.0, probs)
    p2 = jnp.max(probs2, axis=-1, keepdims=True)
    i2 = jnp.min(jnp.where(probs2 == p2, lane, big), axis=-1, keepdims=True)
    gate1 = p_grp * p1 / (p1 + p2)
    gate2 = p_grp * p2 / (p1 + p2)

    oh1 = jnp.where(lane == i1, 1.0, 0.0)
    oh2 = jnp.where(lane == i2, 1.0, 0.0)
    ri = lax.broadcasted_iota(jnp.int32, (tm, tm), 0)
    rj = lax.broadcasted_iota(jnp.int32, (tm, tm), 1)
    lower = jnp.where(ri > rj, 1.0, 0.0).astype(BF16)
    base = base_ref[...]
    tot1 = jnp.sum(oh1, axis=0, keepdims=True)
    c1 = base + _dot(lower, oh1.astype(BF16))
    c2 = base + tot1 + _dot(lower, oh2.astype(BF16))
    rank1 = jnp.sum(oh1 * c1, axis=-1, keepdims=True)
    rank2 = jnp.sum(oh2 * c2, axis=-1, keepdims=True)
    base = base + tot1 + jnp.sum(oh2, axis=0, keepdims=True)
    base_ref[...] = base
    cnt_ref[...] = base

    e1 = (i1 - EXPERT_LANE0).astype(F32)
    e2 = (i2 - EXPERT_LANE0).astype(F32)
    rec = jnp.zeros((tm, LANES), F32)
    for ln, val in ((ROUTE_E, e1), (ROUTE_E + 1, e2), (ROUTE_G, gate1), (ROUTE_G + 1, gate2),
                    (ROUTE_R, rank1), (ROUTE_R + 1, rank2)):
        rec = jnp.where(lane == ln, val, rec)
    route_ref[...] = rec


def _outproj(x2, o_da, o_rw, w_out, g_ffn, wr_hi, wr_lo, b_r, tm=512):
    t = x2.shape[0]
    const = lambda shape: pl.BlockSpec(shape, lambda i: (0, 0))
    return pl.pallas_call(
        functools.partial(_outproj_kernel, tm=tm),
        grid=(t // tm,),
        in_specs=[
            pl.BlockSpec((tm, D_MODEL), lambda i: (i, 0)),
            pl.BlockSpec((tm, DA_WIDTH), lambda i: (i, 0)),
            pl.BlockSpec((tm, RW_WIDTH), lambda i: (i, 0)),
            const((D_MODEL, D_MODEL)),
            const((1, D_MODEL)),
            const((D_MODEL, LANES)),
            const((D_MODEL, LANES)),
            const((1, LANES)),
        ],
        out_specs=[
            pl.BlockSpec((tm, D_MODEL), lambda i: (i, 0)),
            pl.BlockSpec((tm * SUBLANES, LANES), lambda i: (i, 0)),
            pl.BlockSpec((tm, LANES), lambda i: (i, 0)),
            const((1, LANES)),
        ],
        out_shape=[
            jax.ShapeDtypeStruct((t, D_MODEL), F32),
            jax.ShapeDtypeStruct((t * SUBLANES, LANES), jnp.uint32),
            jax.ShapeDtypeStruct((t, LANES), F32),
            jax.ShapeDtypeStruct((1, LANES), F32),
        ],
        scratch_shapes=[pltpu.VMEM((1, LANES), F32)],
        compiler_params=pltpu.CompilerParams(
            dimension_semantics=("arbitrary",), vmem_limit_bytes=VMEM_LIMIT),
        name="outproj_router",
    )(x2, o_da, o_rw, w_out, g_ffn, wr_hi, wr_lo, b_r)


def _row_copy(src_ref, dst_ref, src_row8, dst_row8, sem):
    src = src_ref.at[pl.ds(pl.multiple_of(src_row8, SUBLANES), SUBLANES)]
    dst = dst_ref.at[pl.ds(pl.multiple_of(dst_row8, SUBLANES), SUBLANES)]
    return pltpu.make_async_copy(src, dst, sem)


def _dispatch_kernel(dest_ref, x_ref, xb_in_hbm, xb_hbm, sem, *, td):
    del xb_in_hbm
    t0 = pl.program_id(0) * td

    def start(i, _):
        tok = t0 + i
        _row_copy(x_ref, xb_hbm, i * SUBLANES, dest_ref[2 * tok], sem).start(priority=0)
        _row_copy(x_ref, xb_hbm, i * SUBLANES, dest_ref[2 * tok + 1], sem).start(priority=1)
        return 0

    lax.fori_loop(0, td, start, 0, unroll=8)
    for _ in range(2):
        pltpu.make_async_copy(x_ref, xb_hbm.at[pl.ds(0, td * SUBLANES)], sem).wait()


def _dispatch(dest_flat, xn2, xb_zero, td=256):
    t = xn2.shape[0] // SUBLANES
    return pl.pallas_call(
        functools.partial(_dispatch_kernel, td=td),
        grid_spec=pltpu.PrefetchScalarGridSpec(
            num_scalar_prefetch=1,
            grid=(t // td,),
            in_specs=[pl.BlockSpec((td * SUBLANES, LANES), lambda i, d: (i, 0)),
                      pl.BlockSpec(memory_space=pl.ANY)],
            out_specs=pl.BlockSpec(memory_space=pl.ANY),
            scratch_shapes=[pltpu.SemaphoreType.DMA],
        ),
        out_shape=jax.ShapeDtypeStruct(xb_zero.shape, xb_zero.dtype),
        input_output_aliases={2: 0},
        compiler_params=pltpu.CompilerParams(dimension_semantics=("arbitrary",)),
        name="dispatch",
    )(dest_flat, xn2, xb_zero)


def _experts_kernel(be_ref, nu_ref, nxt_ref, slot_ref, xb_ref, wg_hbm, wu_hbm, wd_hbm, yb_ref,
                    wgf_ref, wuf_ref, wdf_ref, wgb_ref, wub_ref, wdb_ref, sem):
    i = pl.program_id(0)
    used = i < nu_ref[0]
    new_expert = jnp.logical_or(i == 0, be_ref[i] != be_ref[jnp.maximum(i - 1, 0)])

    def weight_copies(e, slot):
        return [pltpu.make_async_copy(src.at[e], dst.at[slot], sem.at[slot])
                for src, dst in ((wg_hbm, wgf_ref), (wu_hbm, wuf_ref), (wd_hbm, wdf_ref))]

    def swiglu(wg_lo, wg_hi, wu_lo, wu_hi, wd):
        x_lo, x_hi = (half.astype(BF16) for half in _unpack_halves(_load_row_tiles(xb_ref)))
        hg = _dot(x_lo, wg_lo) + _dot(x_hi, wg_hi)
        hu = _dot(x_lo, wu_lo) + _dot(x_hi, wu_hi)
        hid = hg * _sigmoid(hg) * hu
        _store_row_tiles(yb_ref, _pack_halves(_dot(hid.astype(BF16), wd).astype(BF16)))

    @pl.when(jnp.logical_and(used, new_expert))
    def _():
        slot = slot_ref[i]

        @pl.when(i == 0)
        def _():
            for cp in weight_copies(be_ref[i], slot):
                cp.start()

        for cp in weight_copies(be_ref[i], slot):
            cp.wait()

        @pl.when(nxt_ref[i] >= 0)
        def _():
            for cp in weight_copies(nxt_ref[i], 1 - slot):
                cp.start()

        wg, wu, wd = (w[slot].astype(BF16) for w in (wgf_ref, wuf_ref, wdf_ref))
        wgb_ref[...] = wg
        wub_ref[...] = wu
        wdb_ref[...] = wd
        swiglu(wg[0:HALF, :], wg[HALF:, :], wu[0:HALF, :], wu[HALF:, :], wd)

    @pl.when(jnp.logical_and(used, jnp.logical_not(new_expert)))
    def _():
        swiglu(wgb_ref[0:HALF, :], wgb_ref[HALF:, :], wub_ref[0:HALF, :], wub_ref[HALF:, :], wdb_ref[...])

    @pl.when(jnp.logical_not(used))
    def _():
        yb_ref[...] = jnp.zeros_like(yb_ref)


def _experts(block_e, n_used, next_e, slot, xb, w_gate, w_up, w_down):
    block_rows = EXPERT_BLOCK * SUBLANES
    nb = xb.shape[0] // block_rows
    return pl.pallas_call(
        _experts_kernel,
        grid_spec=pltpu.PrefetchScalarGridSpec(
            num_scalar_prefetch=4,
            grid=(nb,),
            in_specs=[
                pl.BlockSpec((block_rows, LANES), lambda i, be, nu, *_: (jnp.minimum(i, nu[0] - 1), 0)),
                pl.BlockSpec(memory_space=pl.ANY),
                pl.BlockSpec(memory_space=pl.ANY),
                pl.BlockSpec(memory_space=pl.ANY),
            ],
            out_specs=pl.BlockSpec((block_rows, LANES), lambda i, *_: (i, 0)),
            scratch_shapes=[
                pltpu.VMEM((2, D_MODEL, D_EXPERT), F32),
                pltpu.VMEM((2, D_MODEL, D_EXPERT), F32),
                pltpu.VMEM((2, D_EXPERT, D_MODEL), F32),
                pltpu.VMEM((D_MODEL, D_EXPERT), BF16),
                pltpu.VMEM((D_MODEL, D_EXPERT), BF16),
                pltpu.VMEM((D_EXPERT, D_MODEL), BF16),
                pltpu.SemaphoreType.DMA((2,)),
            ],
        ),
        out_shape=jax.ShapeDtypeStruct(xb.shape, jnp.uint32),
        compiler_params=pltpu.CompilerParams(
            dimension_semantics=("arbitrary",), vmem_limit_bytes=EXPERTS_VMEM_LIMIT),
        name="experts",
    )(block_e, n_used, next_e, slot, xb, w_gate, w_up, w_down)


def _combine_kernel(dest_ref, route_ref, h_ref, g_ref, yb_hbm, o_ref, y1_ref, y2_ref, sem, *, tc):
    step = pl.program_id(0)
    n_steps = pl.num_programs(0)

    def gather(tile, slot):
        def start(i, _):
            tok = tile * tc + i
            _row_copy(yb_hbm, y1_ref.at[slot], dest_ref[2 * tok], i * SUBLANES, sem.at[slot]).start(priority=0)
            _row_copy(yb_hbm, y2_ref.at[slot], dest_ref[2 * tok + 1], i * SUBLANES, sem.at[slot]).start(priority=1)
            return 0

        lax.fori_loop(0, tc, start, 0, unroll=8)

    @pl.when(step == 0)
    def _():
        gather(0, 0)

    @pl.when(step + 1 < n_steps)
    def _():
        gather(step + 1, (step + 1) % 2)

    slot = step % 2
    pltpu.make_async_copy(yb_hbm.at[pl.ds(0, tc * SUBLANES)], y1_ref.at[slot], sem.at[slot]).wait()
    pltpu.make_async_copy(yb_hbm.at[pl.ds(0, tc * SUBLANES)], y2_ref.at[slot], sem.at[slot]).wait()

    rec = route_ref[...]
    lane = lax.broadcasted_iota(jnp.int32, (tc, LANES), 1)
    g1 = jnp.sum(jnp.where(lane == ROUTE_G, rec, 0.0), axis=-1, keepdims=True)
    g2 = jnp.sum(jnp.where(lane == ROUTE_G + 1, rec, 0.0), axis=-1, keepdims=True)
    y1_lo, y1_hi = _unpack_halves(_load_row_tiles(y1_ref.at[slot]))
    y2_lo, y2_hi = _unpack_halves(_load_row_tiles(y2_ref.at[slot]))
    h_lo = h_ref[:, 0:HALF] + (y1_lo * g1 + y2_lo * g2)
    h_hi = h_ref[:, HALF:] + (y1_hi * g1 + y2_hi * g2)
    ms = (jnp.sum(h_lo * h_lo, axis=-1, keepdims=True)
          + jnp.sum(h_hi * h_hi, axis=-1, keepdims=True)) * (1.0 / D_MODEL)
    scale = lax.rsqrt(ms + NORM_EPS)
    o_ref[:, 0:HALF] = h_lo * scale * g_ref[:, 0:HALF]
    o_ref[:, HALF:] = h_hi * scale * g_ref[:, HALF:]


def _combine(dest_flat, route, h1, g_final, yb, tc=256):
    t = h1.shape[0]
    return pl.pallas_call(
        functools.partial(_combine_kernel, tc=tc),
        grid_spec=pltpu.PrefetchScalarGridSpec(
            num_scalar_prefetch=1,
            grid=(t // tc,),
            in_specs=[
                pl.BlockSpec((tc, LANES), lambda i, d: (i, 0)),
                pl.BlockSpec((tc, D_MODEL), lambda i, d: (i, 0)),
                pl.BlockSpec((1, D_MODEL), lambda i, d: (0, 0)),
                pl.BlockSpec(memory_space=pl.ANY),
            ],
            out_specs=pl.BlockSpec((tc, D_MODEL), lambda i, d: (i, 0)),
            scratch_shapes=[
                pltpu.VMEM((2, tc * SUBLANES, LANES), jnp.uint32),
                pltpu.VMEM((2, tc * SUBLANES, LANES), jnp.uint32),
                pltpu.SemaphoreType.DMA((2,)),
            ],
        ),
        out_shape=jax.ShapeDtypeStruct((t, D_MODEL), F32),
        compiler_params=pltpu.CompilerParams(
            dimension_semantics=("arbitrary",), vmem_limit_bytes=VMEM_LIMIT),
        name="combine",
    )(dest_flat, route, h1, g_final, yb)


def _pad_cols(w, n):
    return jnp.pad(w, ((0, 0), (0, n - w.shape[1])))


def kernel(x, g_mix, w_in, w_out, da_lambda_q1, da_lambda_k1, da_lambda_q2, da_lambda_k2, da_subln_g,
           rw_mu_x, rw_mu_rkv, rw_w0, rw_w1, rw_w2, rw_a0, rw_a1, rw_a2, rw_g1, rw_g2, rw_k_k, rw_k_a,
           rw_r_k, rw_ln_w, rw_ln_b, g_ffn, moe_w_group, moe_b_group, moe_w_expert, moe_b_expert,
           moe_w_gate, moe_w_up, moe_w_down, g_final):
    b, s, d = x.shape
    t = b * s
    x2 = x.reshape(t, d)

    mu = rw_mu_x[0]
    lora = (rw_w1[0], rw_a1[0], rw_g1[0])
    keep = _pad_cols(jnp.concatenate([(1.0 - mu[i])[:, None] * w for i, w in enumerate(lora)], axis=1), LORA_PAD)
    prev = _pad_cols(jnp.concatenate([mu[i][:, None] * w for i, w in enumerate(lora)], axis=1), LORA_PAD)
    w_all = jnp.concatenate([w_in[0], keep, prev], axis=1).astype(BF16)
    proj = _inproj(x2, g_mix, w_all)

    slopes = jnp.asarray([2.0 ** (-8.0 * (i + 1) / DA_HEADS) for i in range(DA_HEADS)], F32)
    lam4 = jnp.concatenate([da_lambda_q1, da_lambda_k1, da_lambda_q2, da_lambda_k2], axis=0)
    o_da = _diffattn(proj, slopes, lam4, da_subln_g.reshape(DA_V_DIM, 1), b, s)

    w2cat = jnp.zeros((LORA_PAD, 3 * RW_WIDTH), F32)
    w2cat = w2cat.at[0:LORA_W, 0:RW_WIDTH].set(rw_w2[0])
    w2cat = w2cat.at[LORA_W:LORA_W + LORA_A, RW_WIDTH:2 * RW_WIDTH].set(rw_a2[0])
    w2cat = w2cat.at[LORA_W + LORA_A:LORA_W + LORA_A + LORA_G, 2 * RW_WIDTH:].set(rw_g2[0])
    o_rw = _rwkv(proj, rw_mu_rkv[0], rw_w0, rw_a0, w2cat.astype(BF16), rw_k_k, rw_k_a,
                 rw_r_k.reshape(1, RW_WIDTH), rw_ln_w, rw_ln_b, b, s)

    w_r = _pad_cols(jnp.concatenate([moe_w_group[0], moe_w_expert[0]], axis=1), LANES)
    wr_hi = w_r.astype(BF16)
    wr_lo = (w_r - wr_hi.astype(F32)).astype(BF16)
    b_r = _pad_cols(jnp.concatenate([moe_b_group[0], moe_b_expert[0].reshape(-1)])[None, :], LANES)
    h1, xn2, route, cnt = _outproj(x2, o_da, o_rw, w_out[0].astype(BF16), g_ffn, wr_hi, wr_lo, b_r)

    counts = cnt[0, EXPERT_LANE0:EXPERT_LANE0 + N_EXPERTS].astype(jnp.int32)
    pcounts = ((counts + EXPERT_BLOCK - 1) // EXPERT_BLOCK) * EXPERT_BLOCK
    pends = jnp.cumsum(pcounts)
    pstarts = pends - pcounts
    e_idx = route[:, ROUTE_E:ROUTE_E + 2].astype(jnp.int32)
    rank = route[:, ROUTE_R:ROUTE_R + 2].astype(jnp.int32)
    expert_ids = jnp.arange(N_EXPERTS, dtype=jnp.int32)
    row0 = jnp.sum(jnp.where(e_idx[..., None] == expert_ids, pstarts, 0), axis=-1)
    dest = ((row0 + rank) * SUBLANES).reshape(-1)
    n_blocks = (2 * t) // EXPERT_BLOCK + N_EXPERTS
    block_row0 = jnp.arange(n_blocks, dtype=jnp.int32) * EXPERT_BLOCK
    block_e = jnp.minimum(jnp.sum(pends[None, :] <= block_row0[:, None], axis=1), N_EXPERTS - 1).astype(jnp.int32)
    n_used = (pends[-1:] // EXPERT_BLOCK).astype(jnp.int32)
    has_rows = counts > 0
    later = (expert_ids[None, :] > expert_ids[:, None]) & has_rows[None, :]
    next_used = jnp.min(jnp.where(later, expert_ids[None, :], N_EXPERTS), axis=1)
    next_used = jnp.where(next_used == N_EXPERTS, -1, next_used).astype(jnp.int32)
    buffer_id = ((jnp.cumsum(has_rows.astype(jnp.int32)) - 1) % 2).astype(jnp.int32)

    xb = _dispatch(dest, xn2, jnp.zeros((n_blocks * EXPERT_BLOCK * SUBLANES, LANES), jnp.uint32))
    yb = _experts(block_e, n_used, next_used[block_e], buffer_id[block_e], xb,
                  moe_w_gate[0], moe_w_up[0], moe_w_down[0])
    out = _combine(dest, route, h1, g_final[None, :], yb)
    return out.reshape(b, s, d)
```

```python
import functools
import math

import jax
import jax.numpy as jnp
from jax import lax
from jax.experimental import pallas as pl
from jax.experimental.pallas import tpu as pltpu

F32 = jnp.float32
BF16 = jnp.bfloat16

D_MODEL = 2048
DA_WIDTH = 1024
RW_WIDTH = 1024
DA_HEADS = 8
DA_V_DIM = 128
DA_QK_DIM = 64
RW_N = 64
RW_PAIRS = RW_WIDTH // 128
LORA_W, LORA_A, LORA_G = 64, 64, 160
LORA_PAD = 384
IN_COLS = 3 * DA_WIDTH + 3 * RW_WIDTH
NC = IN_COLS + 2 * LORA_PAD
N_GROUPS = 4
EXPERTS_PER_GROUP = 8
N_EXPERTS = 32
D_EXPERT = 512
NORM_EPS = 1e-6
LN_X_EPS = 64e-5
LAM_INIT = 0.8 - 0.6 * math.exp(-0.3 * 0)
LANES = 128
CHUNK = 64
RW_TILE = 256
NCH = RW_TILE // CHUNK
EXPERT_BLOCK = 256
NEG = -1e30
VMEM_LIMIT = 48 * 1024 * 1024
EXPERTS_VMEM_LIMIT = 56 * 1024 * 1024


def _dot(a, b):
    return jnp.dot(a, b, preferred_element_type=F32)


def _dot_nt(a, b):
    return lax.dot_general(a, b, (((1,), (1,)), ((), ())), preferred_element_type=F32)


def _dot_tn(a, b):
    return lax.dot_general(a, b, (((0,), (0,)), ((), ())), preferred_element_type=F32)


def _sigmoid(x):
    return 1.0 / (1.0 + jnp.exp(-x))


HALF = D_MODEL // 2


def _pack_halves(xb):
    lo = lax.bitcast_convert_type(xb[:, :HALF].astype(F32), jnp.uint32) >> 16
    hi = lax.bitcast_convert_type(xb[:, HALF:].astype(F32), jnp.uint32) & jnp.uint32(0xFFFF0000)
    return hi | lo


def _unpack_halves(u):
    lo = lax.bitcast_convert_type(u << 16, F32)
    hi = lax.bitcast_convert_type(u & jnp.uint32(0xFFFF0000), F32)
    return lo, hi


SUBLANES = 8
assert HALF == SUBLANES * LANES


def _store_row_tiles(ref, u):
    n = u.shape[0]
    for k in range(SUBLANES):
        ref[pl.ds(k, n, stride=SUBLANES), :] = u[:, k * LANES:(k + 1) * LANES]


def _load_row_tiles(ref):
    n = ref.shape[0] // SUBLANES
    return jnp.concatenate([ref[pl.ds(k, n, stride=SUBLANES), :] for k in range(SUBLANES)], axis=1)


def _inproj_kernel(x_ref, g_ref, w_ref, o_ref, xn_ref):
    @pl.when(pl.program_id(1) == 0)
    def _():
        x = x_ref[...]
        ms = jnp.mean(x * x, axis=-1, keepdims=True)
        xn_ref[...] = (x * lax.rsqrt(ms + NORM_EPS) * g_ref[...]).astype(BF16)

    o_ref[...] = _dot(xn_ref[...], w_ref[...]).astype(o_ref.dtype)


def _inproj(x2, g, w_all, tm=1024, tn=768):
    t = x2.shape[0]
    tm = min(tm, t)
    return pl.pallas_call(
        _inproj_kernel,
        grid=(t // tm, NC // tn),
        in_specs=[
            pl.BlockSpec((tm, D_MODEL), lambda i, j: (i, 0)),
            pl.BlockSpec((1, D_MODEL), lambda i, j: (0, 0)),
            pl.BlockSpec((D_MODEL, tn), lambda i, j: (0, j)),
        ],
        out_specs=pl.BlockSpec((tm, tn), lambda i, j: (i, j)),
        out_shape=jax.ShapeDtypeStruct((t, NC), BF16),
        scratch_shapes=[pltpu.VMEM((tm, D_MODEL), BF16)],
        compiler_params=pltpu.CompilerParams(
            dimension_semantics=("parallel", "arbitrary"), vmem_limit_bytes=VMEM_LIMIT),
        name="inproj",
    )(x2, g, w_all)


DA_HEADS_PER_STEP = 2


def _da_kernel(qi_tab, kj_tab, slopes_ref, lam_ref, subg_ref, q_ref, k_ref, v_ref, o_ref,
               vt_ref, qz_ref, bias_ref, m_ref, l_ref, lfin_ref, acc_ref, s_ref, p_ref, alpha_ref,
               *, tq, seq):
    tk = tq
    nq = seq // tq
    n_trips = nq * (nq + 1) // 2
    heads = range(DA_HEADS_PER_STEP)
    chains = [(hd, c) for hd in heads for c in range(2)]
    slope = [slopes_ref[pl.program_id(1) * DA_HEADS_PER_STEP + hd] for hd in heads]
    cols = lambda hd: slice(hd * LANES, (hd + 1) * LANES)

    lane = lax.broadcasted_iota(jnp.int32, (tq, LANES), 1)
    krow = lax.broadcasted_iota(jnp.int32, (tk, tq), 0)
    qcol = lax.broadcasted_iota(jnp.int32, (tk, tq), 1)
    for hd in heads:
        for blk in range(seq // LANES):
            r = slice(blk * LANES, (blk + 1) * LANES)
            vt_ref[hd, :, r] = v_ref[r, cols(hd)].astype(F32).T.astype(BF16)
        for blk in range(nq):
            r = slice(blk * tq, (blk + 1) * tq)
            q = q_ref[r, cols(hd)] * jnp.asarray(DA_QK_DIM ** -0.5, BF16)
            qz_ref[2 * hd, r, :] = jnp.where(lane < DA_QK_DIM, q, jnp.zeros_like(q))
            qz_ref[2 * hd + 1, r, :] = jnp.where(lane >= DA_QK_DIM, q, jnp.zeros_like(q))
        alibi = slope[hd] * krow.astype(F32)
        bias_ref[hd, 0] = alibi
        bias_ref[hd, 1] = jnp.where(krow <= qcol, alibi, NEG)

    m_ref[...] = jnp.full_like(m_ref, NEG)
    l_ref[...] = jnp.zeros_like(l_ref)
    lfin_ref[...] = jnp.ones_like(lfin_ref)
    acc_ref[...] = jnp.zeros_like(acc_ref)
    p_ref[...] = jnp.zeros_like(p_ref)
    alpha_ref[...] = jnp.zeros_like(alpha_ref)

    lv = lam_ref[...]
    lam = (jnp.exp(jnp.sum(lv[0:1] * lv[1:2], axis=-1, keepdims=True))
           - jnp.exp(jnp.sum(lv[2:3] * lv[3:4], axis=-1, keepdims=True)) + LAM_INIT)

    def scores_to_scratch(t):
        qs = pl.multiple_of(qi_tab[t] * tq, tq)
        ks = pl.multiple_of(kj_tab[t] * tk, tk)
        diag = (qi_tab[t] == kj_tab[t]).astype(jnp.int32)
        for hd, c in chains:
            ch = 2 * hd + c
            s_ref[ch] = (_dot_nt(k_ref[pl.ds(ks, tk), cols(hd)], qz_ref[ch, pl.ds(qs, tq), :])
                         + bias_ref[hd, diag])

    def values_from_scratch(t):
        ks = pl.multiple_of(kj_tab[t] * tk, tk)
        for hd, c in chains:
            ch = 2 * hd + c
            acc_ref[ch] = alpha_ref[ch:ch + 1, :] * acc_ref[ch] + _dot(vt_ref[hd, :, pl.ds(ks, tk)], p_ref[ch])

    def softmax(t):
        first = kj_tab[t] == 0
        diag = kj_tab[t] == qi_tab[t]
        rel = ((kj_tab[t] - qi_tab[t]) * tk).astype(F32)
        p, alpha = [], []
        for hd, c in chains:
            ch = 2 * hd + c
            off = slope[hd] * rel
            s = s_ref[ch]
            m = jnp.where(first, NEG, m_ref[ch:ch + 1, :])
            l = jnp.where(first, 0.0, l_ref[ch:ch + 1, :])
            m_new = jnp.maximum(m, jnp.max(s, axis=0, keepdims=True) + off)
            pc = jnp.exp(s - (m_new - off))
            a = jnp.exp(m - m_new)
            l_new = a * l + jnp.sum(pc, axis=0, keepdims=True)
            m_ref[ch:ch + 1, :] = m_new
            l_ref[ch:ch + 1, :] = l_new
            lfin_ref[ch:ch + 1, :] = jnp.where(diag, l_new, lfin_ref[ch:ch + 1, :])
            p.append(pc)
            alpha.append(a)
        return p, alpha

    def publish(p, alpha):
        for ch in range(len(chains)):
            alpha_ref[ch:ch + 1, :] = alpha[ch]
            p_ref[ch] = p[ch].astype(BF16)

    def finalize(t):
        qs = pl.multiple_of(qi_tab[t] * tq, tq)
        for hd in heads:
            c0, c1 = 2 * hd, 2 * hd + 1
            o = acc_ref[c0] / lfin_ref[c0:c0 + 1, :] - lam * (acc_ref[c1] / lfin_ref[c1:c1 + 1, :])
            ms = jnp.mean(o * o, axis=0, keepdims=True)
            o = o * lax.rsqrt(ms + NORM_EPS) * subg_ref[...] * (1.0 - LAM_INIT)
            o_ref[pl.ds(qs, tq), cols(hd)] = o.T.astype(o_ref.dtype)

    scores_to_scratch(0)

    def body(t, _):
        prev = jnp.maximum(t - 1, 0)
        p, alpha = softmax(t)
        values_from_scratch(prev)
        scores_to_scratch(jnp.minimum(t + 1, n_trips - 1))
        publish(p, alpha)

        @pl.when(jnp.logical_and(t > 0, qi_tab[prev] == kj_tab[prev]))
        def _():
            finalize(prev)
        return 0

    lax.fori_loop(0, n_trips, body, 0)
    values_from_scratch(n_trips - 1)
    finalize(n_trips - 1)


def _diffattn(proj, slopes, lam4, subg_col, b, s, tq=256):
    nq = s // tq
    nh = DA_HEADS_PER_STEP
    groups = DA_HEADS // nh
    width = nh * LANES
    pairs = [(qi, kj) for qi in range(nq) for kj in range(qi + 1)]
    qi_tab = jnp.asarray([p[0] for p in pairs], jnp.int32)
    kj_tab = jnp.asarray([p[1] for p in pairs], jnp.int32)
    return pl.pallas_call(
        functools.partial(_da_kernel, tq=tq, seq=s),
        grid_spec=pltpu.PrefetchScalarGridSpec(
            num_scalar_prefetch=2,
            grid=(b, groups),
            in_specs=[
                pl.BlockSpec(memory_space=pltpu.SMEM),
                pl.BlockSpec((4, DA_QK_DIM), lambda bi, g, *_: (0, 0)),
                pl.BlockSpec((DA_V_DIM, 1), lambda bi, g, *_: (0, 0)),
                pl.BlockSpec((s, width), lambda bi, g, *_: (bi, g)),
                pl.BlockSpec((s, width), lambda bi, g, *_: (bi, groups + g)),
                pl.BlockSpec((s, width), lambda bi, g, *_: (bi, 2 * groups + g)),
            ],
            out_specs=pl.BlockSpec((s, width), lambda bi, g, *_: (bi, g)),
            scratch_shapes=[
                pltpu.VMEM((nh, DA_V_DIM, s), BF16),
                pltpu.VMEM((2 * nh, s, LANES), BF16),
                pltpu.VMEM((nh, 2, tq, tq), F32),
                pltpu.VMEM((2 * nh, tq), F32),
                pltpu.VMEM((2 * nh, tq), F32),
                pltpu.VMEM((2 * nh, tq), F32),
                pltpu.VMEM((2 * nh, DA_V_DIM, tq), F32),
                pltpu.VMEM((2 * nh, tq, tq), F32),
                pltpu.VMEM((2 * nh, tq, tq), BF16),
                pltpu.VMEM((2 * nh, tq), F32),
            ],
        ),
        out_shape=jax.ShapeDtypeStruct((b * s, DA_WIDTH), BF16),
        compiler_params=pltpu.CompilerParams(
            dimension_semantics=("parallel", "arbitrary"), vmem_limit_bytes=VMEM_LIMIT),
        name="diffattn",
    )(qi_tab, kj_tab, slopes, lam4, subg_col, proj, proj, proj)


def _rwkv_kernel(r_ref, k_ref, v_ref, p1_ref, p2_ref, mu_ref, w0_ref, a0_ref, w2_ref, kk_ref, ka_ref,
                 rk_ref, lnw_ref, lnb_ref, o_ref, state_ref, crkv_ref, cp2_ref):
    L, TT = CHUNK, RW_TILE
    c = pl.program_id(1)

    @pl.when(c == 0)
    def _():
        state_ref[...] = jnp.zeros_like(state_ref)
        crkv_ref[...] = jnp.zeros_like(crkv_ref)
        cp2_ref[...] = jnp.zeros_like(cp2_ref)

    first = lax.broadcasted_iota(jnp.int32, (TT, 1), 0) == 0

    def shift(x, carry):
        return jnp.where(first, carry, pltpu.roll(x, 1, 0))

    r_raw = r_ref[...].astype(F32)
    k_raw = k_ref[...].astype(F32)
    v_raw = v_ref[...].astype(F32)
    p2 = p2_ref[...].astype(F32)
    rs = shift(r_raw, crkv_ref[:, 0:RW_WIDTH])
    ks = shift(k_raw, crkv_ref[:, RW_WIDTH:2 * RW_WIDTH])
    vs = shift(v_raw, crkv_ref[:, 2 * RW_WIDTH:3 * RW_WIDTH])
    lin = p1_ref[...].astype(F32) + shift(p2, cp2_ref[...])
    crkv_ref[:, 0:RW_WIDTH] = r_raw[TT - 1:TT, :]
    crkv_ref[:, RW_WIDTH:2 * RW_WIDTH] = k_raw[TT - 1:TT, :]
    crkv_ref[:, 2 * RW_WIDTH:3 * RW_WIDTH] = v_raw[TT - 1:TT, :]
    cp2_ref[...] = p2[TT - 1:TT, :]

    r = r_raw + (rs - r_raw) * mu_ref[0:1, :]
    k = k_raw + (ks - k_raw) * mu_ref[1:2, :]
    v = v_raw + (vs - v_raw) * mu_ref[2:3, :]

    ll = lax.broadcasted_iota(jnp.int32, (TT, LORA_PAD), 1)
    z = jnp.where(ll < LORA_W, jnp.tanh(lin),
                  jnp.where(ll < LORA_W + LORA_A, lin,
                            jnp.where(ll < LORA_W + LORA_A + LORA_G, _sigmoid(lin), 0.0)))
    d = _dot(z.astype(BF16), w2_ref[...])
    logw = -_sigmoid(w0_ref[...] + d[:, 0:RW_WIDTH]) * math.exp(-0.5)
    a_sig = _sigmoid(a0_ref[...] + d[:, RW_WIDTH:2 * RW_WIDTH])
    g = d[:, 2 * RW_WIDTH:3 * RW_WIDTH]

    li = lax.broadcasted_iota(jnp.int32, (LANES, LANES), 0)
    lj = lax.broadcasted_iota(jnp.int32, (LANES, LANES), 1)
    same_head = jnp.where((li // RW_N) == (lj // RW_N), 1.0, 0.0).astype(BF16)

    def head_sum(x):
        xb = x.astype(BF16)
        return jnp.concatenate(
            [_dot(xb[:, p * LANES:(p + 1) * LANES], same_head) for p in range(RW_PAIRS)], axis=1)

    kk = k * kk_ref[...]
    kk = kk * lax.rsqrt(jnp.maximum(head_sum(kk * kk), 1e-24))
    k = k * (1.0 + (a_sig - 1.0) * ka_ref[...])
    bonus = head_sum(r * k * rk_ref[...])

    ti = lax.broadcasted_iota(jnp.int32, (TT, TT), 0)
    tj = lax.broadcasted_iota(jnp.int32, (TT, TT), 1)
    tri = jnp.where(ti >= tj, jnp.where((ti // L) == (tj // L), 1.0, 0.0), 0.0).astype(BF16)
    w_hi = logw.astype(BF16)
    w_r1 = logw - w_hi.astype(F32)
    w_mid = w_r1.astype(BF16)
    w_lo = (w_r1 - w_mid.astype(F32)).astype(BF16)
    cw = _dot(tri, w_hi) + _dot(tri, w_mid) + _dot(tri, w_lo)
    cw_last = [cw[(ci + 1) * L - 1:(ci + 1) * L, :] for ci in range(NCH)]
    e_in = jnp.exp(cw)
    e_out = jnp.exp(-cw)
    e_end = jnp.exp(jnp.concatenate([jnp.broadcast_to(cl, (L, RW_WIDTH)) for cl in cw_last], axis=0) - cw)
    w_end = [jnp.exp(cl) for cl in cw_last]

    at = -kk * jnp.exp(cw - logw)
    rt = r * e_in
    b = kk * a_sig
    bt = b * e_out
    kt = k * e_out
    bd = b * e_end
    kd = k * e_end

    lane = lax.broadcasted_iota(jnp.int32, (L, LANES), 1)
    lo = lane < RW_N

    def stack(x, ci, p):
        xs = x[ci * L:(ci + 1) * L, p * LANES:(p + 1) * LANES]
        return jnp.concatenate([jnp.where(lo, xs, 0.0), jnp.where(lo, 0.0, xs)], axis=0).astype(BF16)

    si = lax.broadcasted_iota(jnp.int32, (2 * L, 2 * L), 0)
    sj = lax.broadcasted_iota(jnp.int32, (2 * L, 2 * L), 1)
    strict = (si % L) > (sj % L)
    incl = (si % L) >= (sj % L)
    eye = jnp.where(si == sj, 1.0, 0.0)

    pairs = range(RW_PAIRS)
    cat0 = lambda *xs: jnp.concatenate(xs, axis=0)
    cat1 = lambda *xs: jnp.concatenate(xs, axis=1)
    bf = lambda x: x.astype(BF16)
    h = [state_ref[p] for p in pairs]
    ys = []
    for ci in range(NCH):
        a_s = [stack(at, ci, p) for p in pairs]
        r_s = [stack(rt, ci, p) for p in pairs]
        v_s = [stack(v, ci, p) for p in pairs]
        bk = [cat0(stack(bt, ci, p), stack(kt, ci, p)) for p in pairs]
        bkd = [cat0(stack(bd, ci, p), stack(kd, ci, p)) for p in pairs]
        aa = [_dot_nt(a_s[p], bk[p]) for p in pairs]
        rr = [_dot_nt(r_s[p], bk[p]) for p in pairs]
        a_ab = [jnp.where(strict, aa[p][:, 0:2 * L], 0.0) for p in pairs]
        a_ak = [bf(jnp.where(strict, aa[p][:, 2 * L:4 * L], 0.0)) for p in pairs]
        a_rb = [bf(jnp.where(incl, rr[p][:, 0:2 * L], 0.0)) for p in pairs]
        a_rk = [bf(jnp.where(incl, rr[p][:, 2 * L:4 * L], 0.0)) for p in pairs]
        ab = [bf(a) for a in a_ab]
        qpow = [_dot(ab[p], ab[p]) for p in pairs]
        inv = [eye + a_ab[p] for p in pairs]
        for _ in range(int(math.log2(L)) - 2):
            qb = [bf(q) for q in qpow]
            res = [_dot(qb[p], cat1(qb[p], bf(inv[p]))) for p in pairs]
            qpow = [res[p][:, 0:2 * L] for p in pairs]
            inv = [inv[p] + res[p][:, 2 * L:4 * L] for p in pairs]
        inv = [bf(inv[p] + _dot(bf(qpow[p]), bf(inv[p]))) for p in pairs]
        h_b = [bf(hp) for hp in h]
        x = [_dot(cat1(a_s[p], a_ak[p]), cat0(h_b[p], v_s[p])) for p in pairs]
        u_b = [bf(_dot(inv[p], bf(x[p]))) for p in pairs]
        y2 = [_dot(cat1(r_s[p], a_rb[p], a_rk[p]), cat0(h_b[p], u_b[p], v_s[p])) for p in pairs]
        w_col = [jnp.broadcast_to(w_end[ci][:, p * LANES:(p + 1) * LANES], (LANES, LANES)).T for p in pairs]
        h = [w_col[p] * h[p] + _dot_tn(bkd[p], cat0(u_b[p], v_s[p])) for p in pairs]
        ys.append(cat1(*[y2[p][0:L] + y2[p][L:2 * L] for p in pairs]))
    for p in pairs:
        state_ref[p] = h[p]
    y = cat0(*ys)

    mean = head_sum(y) * (1.0 / RW_N)
    yc = y - mean
    var = head_sum(yc * yc) * (1.0 / RW_N)
    y = yc * lax.rsqrt(var + LN_X_EPS) * lnw_ref[...] + lnb_ref[...]
    y = y + bonus * v
    o_ref[...] = (y * g).astype(o_ref.dtype)


def _rwkv(proj, mu_rkv, w0, a0, w2cat, k_k, k_a, r_k, ln_w, ln_b, b, s):
    tt = RW_TILE
    nc = s // tt
    row = lambda n: pl.BlockSpec((1, n), lambda bi, c: (0, 0))
    rkv_col0 = 3 * DA_WIDTH // RW_WIDTH
    lora_col0 = IN_COLS // LORA_PAD
    return pl.pallas_call(
        _rwkv_kernel,
        grid=(b, nc),
        in_specs=[
            pl.BlockSpec((tt, RW_WIDTH), lambda bi, c: (bi * nc + c, rkv_col0)),
            pl.BlockSpec((tt, RW_WIDTH), lambda bi, c: (bi * nc + c, rkv_col0 + 1)),
            pl.BlockSpec((tt, RW_WIDTH), lambda bi, c: (bi * nc + c, rkv_col0 + 2)),
            pl.BlockSpec((tt, LORA_PAD), lambda bi, c: (bi * nc + c, lora_col0)),
            pl.BlockSpec((tt, LORA_PAD), lambda bi, c: (bi * nc + c, lora_col0 + 1)),
            pl.BlockSpec((3, RW_WIDTH), lambda bi, c: (0, 0)),
            row(RW_WIDTH), row(RW_WIDTH),
            pl.BlockSpec((LORA_PAD, 3 * RW_WIDTH), lambda bi, c: (0, 0)),
            row(RW_WIDTH), row(RW_WIDTH), row(RW_WIDTH), row(RW_WIDTH), row(RW_WIDTH),
        ],
        out_specs=pl.BlockSpec((tt, RW_WIDTH), lambda bi, c: (bi * nc + c, 0)),
        out_shape=jax.ShapeDtypeStruct((b * s, RW_WIDTH), BF16),
        scratch_shapes=[
            pltpu.VMEM((RW_PAIRS, LANES, LANES), F32),
            pltpu.VMEM((1, 3 * RW_WIDTH), F32),
            pltpu.VMEM((1, LORA_PAD), F32),
        ],
        compiler_params=pltpu.CompilerParams(
            dimension_semantics=("parallel", "arbitrary"), vmem_limit_bytes=VMEM_LIMIT),
        name="rwkv",
    )(proj, proj, proj, proj, proj, mu_rkv, w0, a0, w2cat, k_k, k_a, r_k, ln_w, ln_b)


ROUTE_E, ROUTE_G, ROUTE_R = 0, 2, 4
EXPERT_LANE0 = N_GROUPS


def _outproj_kernel(x_ref, oda_ref, orw_ref, wout_ref, g_ref, wrh_ref, wrl_ref, br_ref,
                    h_ref, xn_ref, route_ref, cnt_ref, base_ref, *, tm):
    @pl.when(pl.program_id(0) == 0)
    def _():
        base_ref[...] = jnp.zeros_like(base_ref)

    mix = _dot(oda_ref[...], wout_ref[0:DA_WIDTH, :]) + _dot(orw_ref[...], wout_ref[DA_WIDTH:, :])
    h = x_ref[...] + mix
    h_ref[...] = h
    ms = jnp.mean(h * h, axis=-1, keepdims=True)
    xn = h * lax.rsqrt(ms + NORM_EPS) * g_ref[...]
    x_hi = xn.astype(BF16)
    _store_row_tiles(xn_ref, _pack_halves(x_hi))

    x_lo = (xn - x_hi.astype(F32)).astype(BF16)
    hi_both = _dot(x_hi, jnp.concatenate([wrh_ref[...], wrl_ref[...]], axis=1))
    lg = (hi_both[:, 0:LANES] + _dot(x_lo, wrh_ref[...]) + hi_both[:, LANES:]) + br_ref[...]

    lane = lax.broadcasted_iota(jnp.int32, (tm, LANES), 1)
    big = jnp.int32(1 << 20)
    gl = jnp.where(lane < N_GROUPS, lg, NEG)
    gmax = jnp.max(gl, axis=-1, keepdims=True)
    grp = jnp.min(jnp.where(gl == gmax, lane, big), axis=-1, keepdims=True)
    p_grp = 1.0 / jnp.sum(jnp.exp(gl - gmax), axis=-1, keepdims=True)
    eg = jnp.where(lane >= EXPERT_LANE0, (lane - EXPERT_LANE0) // EXPERTS_PER_GROUP, -1)
    el = jnp.where(eg == grp, lg, NEG)
    emax = jnp.max(el, axis=-1, keepdims=True)
    pe = jnp.exp(el - emax)
    probs = pe / jnp.sum(pe, axis=-1, keepdims=True)
    probs = jnp.where(eg == grp, probs, -1.0)
    p1 = jnp.max(probs, axis=-1, keepdims=True)
    i1 = jnp.min(jnp.where(probs == p1, lane, big), axis=-1, keepdims=True)
    probs2 = jnp.where(lane == i1, -1.0, probs)
    p2 = jnp.max(probs2, axis=-1, keepdims=True)
    i2 = jnp.min(jnp.where(probs2 == p2, lane, big), axis=-1, keepdims=True)
    gate1 = p_grp * p1 / (p1 + p2)
    gate2 = p_grp * p2 / (p1 + p2)

    oh1 = jnp.where(lane == i1, 1.0, 0.0)
    oh2 = jnp.where(lane == i2, 1.0, 0.0)
    ri = lax.broadcasted_iota(jnp.int32, (tm, tm), 0)
    rj = lax.broadcasted_iota(jnp.int32, (tm, tm), 1)
    lower = jnp.where(ri > rj, 1.0, 0.0).astype(BF16)
    base = base_ref[...]
    tot1 = jnp.sum(oh1, axis=0, keepdims=True)
    c1 = base + _dot(lower, oh1.astype(BF16))
    c2 = base + tot1 + _dot(lower, oh2.astype(BF16))
    rank1 = jnp.sum(oh1 * c1, axis=-1, keepdims=True)
    rank2 = jnp.sum(oh2 * c2, axis=-1, keepdims=True)
    base = base + tot1 + jnp.sum(oh2, axis=0, keepdims=True)
    base_ref[...] = base
    cnt_ref[...] = base

    e1 = (i1 - EXPERT_LANE0).astype(F32)
    e2 = (i2 - EXPERT_LANE0).astype(F32)
    rec = jnp.zeros((tm, LANES), F32)
    for ln, val in ((ROUTE_E, e1), (ROUTE_E + 1, e2), (ROUTE_G, gate1), (ROUTE_G + 1, gate2),
                    (ROUTE_R, rank1), (ROUTE_R + 1, rank2)):
        rec = jnp.where(lane == ln, val, rec)
    route_ref[...] = rec


def _outproj(x2, o_da, o_rw, w_out, g_ffn, wr_hi, wr_lo, b_r, tm=512):
    t = x2.shape[0]
    const = lambda shape: pl.BlockSpec(shape, lambda i: (0, 0))
    return pl.pallas_call(
        functools.partial(_outproj_kernel, tm=tm),
        grid=(t // tm,),
        in_specs=[
            pl.BlockSpec((tm, D_MODEL), lambda i: (i, 0)),
            pl.BlockSpec((tm, DA_WIDTH), lambda i: (i, 0)),
            pl.BlockSpec((tm, RW_WIDTH), lambda i: (i, 0)),
            const((D_MODEL, D_MODEL)),
            const((1, D_MODEL)),
            const((D_MODEL, LANES)),
            const((D_MODEL, LANES)),
            const((1, LANES)),
        ],
        out_specs=[
            pl.BlockSpec((tm, D_MODEL), lambda i: (i, 0)),
            pl.BlockSpec((tm * SUBLANES, LANES), lambda i: (i, 0)),
            pl.BlockSpec((tm, LANES), lambda i: (i, 0)),
            const((1, LANES)),
        ],
        out_shape=[
            jax.ShapeDtypeStruct((t, D_MODEL), F32),
            jax.ShapeDtypeStruct((t * SUBLANES, LANES), jnp.uint32),
            jax.ShapeDtypeStruct((t, LANES), F32),
            jax.ShapeDtypeStruct((1, LANES), F32),
        ],
        scratch_shapes=[pltpu.VMEM((1, LANES), F32)],
        compiler_params=pltpu.CompilerParams(
            dimension_semantics=("arbitrary",), vmem_limit_bytes=VMEM_LIMIT),
        name="outproj_router",
    )(x2, o_da, o_rw, w_out, g_ffn, wr_hi, wr_lo, b_r)


def _row_copy(src_ref, dst_ref, src_row8, dst_row8, sem):
    src = src_ref.at[pl.ds(pl.multiple_of(src_row8, SUBLANES), SUBLANES)]
    dst = dst_ref.at[pl.ds(pl.multiple_of(dst_row8, SUBLANES), SUBLANES)]
    return pltpu.make_async_copy(src, dst, sem)


def _dispatch_kernel(dest_ref, x_ref, xb_in_hbm, xb_hbm, sem, *, td):
    del xb_in_hbm
    t0 = pl.program_id(0) * td

    def start(i, _):
        tok = t0 + i
        _row_copy(x_ref, xb_hbm, i * SUBLANES, dest_ref[2 * tok], sem).start(priority=0)
        _row_copy(x_ref, xb_hbm, i * SUBLANES, dest_ref[2 * tok + 1], sem).start(priority=1)
        return 0

    lax.fori_loop(0, td, start, 0, unroll=8)
    for _ in range(2):
        pltpu.make_async_copy(x_ref, xb_hbm.at[pl.ds(0, td * SUBLANES)], sem).wait()


def _dispatch(dest_flat, xn2, xb_zero, td=512):
    t = xn2.shape[0] // SUBLANES
    return pl.pallas_call(
        functools.partial(_dispatch_kernel, td=td),
        grid_spec=pltpu.PrefetchScalarGridSpec(
            num_scalar_prefetch=1,
            grid=(t // td,),
            in_specs=[pl.BlockSpec((td * SUBLANES, LANES), lambda i, d: (i, 0)),
                      pl.BlockSpec(memory_space=pl.ANY)],
            out_specs=pl.BlockSpec(memory_space=pl.ANY),
            scratch_shapes=[pltpu.SemaphoreType.DMA],
        ),
        out_shape=jax.ShapeDtypeStruct(xb_zero.shape, xb_zero.dtype),
        input_output_aliases={2: 0},
        compiler_params=pltpu.CompilerParams(dimension_semantics=("arbitrary",)),
        name="dispatch",
    )(dest_flat, xn2, xb_zero)


def _experts_kernel(be_ref, nu_ref, nxt_ref, slot_ref, xb_ref, wg_hbm, wu_hbm, wd_hbm, yb_ref,
                    wgf_ref, wuf_ref, wdf_ref, wgb_ref, wub_ref, wdb_ref, sem):
    i = pl.program_id(0)
    used = i < nu_ref[0]
    new_expert = jnp.logical_or(i == 0, be_ref[i] != be_ref[jnp.maximum(i - 1, 0)])

    def weight_copies(e, slot):
        return [pltpu.make_async_copy(src.at[e], dst.at[slot], sem.at[slot])
                for src, dst in ((wg_hbm, wgf_ref), (wu_hbm, wuf_ref), (wd_hbm, wdf_ref))]

    def swiglu(wg_lo, wg_hi, wu_lo, wu_hi, wd):
        x_lo, x_hi = (half.astype(BF16) for half in _unpack_halves(_load_row_tiles(xb_ref)))
        hg = _dot(x_lo, wg_lo) + _dot(x_hi, wg_hi)
        hu = _dot(x_lo, wu_lo) + _dot(x_hi, wu_hi)
        hid = hg * _sigmoid(hg) * hu
        _store_row_tiles(yb_ref, _pack_halves(_dot(hid.astype(BF16), wd).astype(BF16)))

    @pl.when(jnp.logical_and(used, new_expert))
    def _():
        slot = slot_ref[i]

        @pl.when(i == 0)
        def _():
            for cp in weight_copies(be_ref[i], slot):
                cp.start()

        for cp in weight_copies(be_ref[i], slot):
            cp.wait()

        @pl.when(nxt_ref[i] >= 0)
        def _():
            for cp in weight_copies(nxt_ref[i], 1 - slot):
                cp.start()

        wg, wu, wd = (w[slot].astype(BF16) for w in (wgf_ref, wuf_ref, wdf_ref))
        wgb_ref[...] = wg
        wub_ref[...] = wu
        wdb_ref[...] = wd
        swiglu(wg[0:HALF, :], wg[HALF:, :], wu[0:HALF, :], wu[HALF:, :], wd)

    @pl.when(jnp.logical_and(used, jnp.logical_not(new_expert)))
    def _():
        swiglu(wgb_ref[0:HALF, :], wgb_ref[HALF:, :], wub_ref[0:HALF, :], wub_ref[HALF:, :], wdb_ref[...])

    @pl.when(jnp.logical_not(used))
    def _():
        yb_ref[...] = jnp.zeros_like(yb_ref)


def _experts(block_e, n_used, next_e, slot, xb, w_gate, w_up, w_down):
    block_rows = EXPERT_BLOCK * SUBLANES
    nb = xb.shape[0] // block_rows
    return pl.pallas_call(
        _experts_kernel,
        grid_spec=pltpu.PrefetchScalarGridSpec(
            num_scalar_prefetch=4,
            grid=(nb,),
            in_specs=[
                pl.BlockSpec((block_rows, LANES), lambda i, be, nu, *_: (jnp.minimum(i, nu[0] - 1), 0)),
                pl.BlockSpec(memory_space=pl.ANY),
                pl.BlockSpec(memory_space=pl.ANY),
                pl.BlockSpec(memory_space=pl.ANY),
            ],
            out_specs=pl.BlockSpec((block_rows, LANES), lambda i, *_: (i, 0)),
            scratch_shapes=[
                pltpu.VMEM((2, D_MODEL, D_EXPERT), F32),
                pltpu.VMEM((2, D_MODEL, D_EXPERT), F32),
                pltpu.VMEM((2, D_EXPERT, D_MODEL), F32),
                pltpu.VMEM((D_MODEL, D_EXPERT), BF16),
                pltpu.VMEM((D_MODEL, D_EXPERT), BF16),
                pltpu.VMEM((D_EXPERT, D_MODEL), BF16),
                pltpu.SemaphoreType.DMA((2,)),
            ],
        ),
        out_shape=jax.ShapeDtypeStruct(xb.shape, jnp.uint32),
        compiler_params=pltpu.CompilerParams(
            dimension_semantics=("arbitrary",), vmem_limit_bytes=EXPERTS_VMEM_LIMIT),
        name="experts",
    )(block_e, n_used, next_e, slot, xb, w_gate, w_up, w_down)


def _combine_kernel(dest_ref, route_ref, h_ref, g_ref, yb_hbm, o_ref, y1_ref, y2_ref, sem, *, tc):
    step = pl.program_id(0)
    n_steps = pl.num_programs(0)

    def gather(tile, slot):
        def start(i, _):
            tok = tile * tc + i
            _row_copy(yb_hbm, y1_ref.at[slot], dest_ref[2 * tok], i * SUBLANES, sem.at[slot]).start(priority=0)
            _row_copy(yb_hbm, y2_ref.at[slot], dest_ref[2 * tok + 1], i * SUBLANES, sem.at[slot]).start(priority=1)
            return 0

        lax.fori_loop(0, tc, start, 0, unroll=8)

    @pl.when(step == 0)
    def _():
        gather(0, 0)

    @pl.when(step + 1 < n_steps)
    def _():
        gather(step + 1, (step + 1) % 2)

    slot = step % 2
    pltpu.make_async_copy(yb_hbm.at[pl.ds(0, tc * SUBLANES)], y1_ref.at[slot], sem.at[slot]).wait()
    pltpu.make_async_copy(yb_hbm.at[pl.ds(0, tc * SUBLANES)], y2_ref.at[slot], sem.at[slot]).wait()

    rec = route_ref[...]
    lane = lax.broadcasted_iota(jnp.int32, (tc, LANES), 1)
    g1 = jnp.sum(jnp.where(lane == ROUTE_G, rec, 0.0), axis=-1, keepdims=True)
    g2 = jnp.sum(jnp.where(lane == ROUTE_G + 1, rec, 0.0), axis=-1, keepdims=True)
    y1_lo, y1_hi = _unpack_halves(_load_row_tiles(y1_ref.at[slot]))
    y2_lo, y2_hi = _unpack_halves(_load_row_tiles(y2_ref.at[slot]))
    h_lo = h_ref[:, 0:HALF] + (y1_lo * g1 + y2_lo * g2)
    h_hi = h_ref[:, HALF:] + (y1_hi * g1 + y2_hi * g2)
    ms = (jnp.sum(h_lo * h_lo, axis=-1, keepdims=True)
          + jnp.sum(h_hi * h_hi, axis=-1, keepdims=True)) * (1.0 / D_MODEL)
    scale = lax.rsqrt(ms + NORM_EPS)
    o_ref[:, 0:HALF] = h_lo * scale * g_ref[:, 0:HALF]
    o_ref[:, HALF:] = h_hi * scale * g_ref[:, HALF:]


def _combine(dest_flat, route, h1, g_final, yb, tc=256):
    t = h1.shape[0]
    return pl.pallas_call(
        functools.partial(_combine_kernel, tc=tc),
        grid_spec=pltpu.PrefetchScalarGridSpec(
            num_scalar_prefetch=1,
            grid=(t // tc,),
            in_specs=[
                pl.BlockSpec((tc, LANES), lambda i, d: (i, 0)),
                pl.BlockSpec((tc, D_MODEL), lambda i, d: (i, 0)),
                pl.BlockSpec((1, D_MODEL), lambda i, d: (0, 0)),
                pl.BlockSpec(memory_space=pl.ANY),
            ],
            out_specs=pl.BlockSpec((tc, D_MODEL), lambda i, d: (i, 0)),
            scratch_shapes=[
                pltpu.VMEM((2, tc * SUBLANES, LANES), jnp.uint32),
                pltpu.VMEM((2, tc * SUBLANES, LANES), jnp.uint32),
                pltpu.SemaphoreType.DMA((2,)),
            ],
        ),
        out_shape=jax.ShapeDtypeStruct((t, D_MODEL), F32),
        compiler_params=pltpu.CompilerParams(
            dimension_semantics=("arbitrary",), vmem_limit_bytes=VMEM_LIMIT),
        name="combine",
    )(dest_flat, route, h1, g_final, yb)


def _pad_cols(w, n):
    return jnp.pad(w, ((0, 0), (0, n - w.shape[1])))


def kernel(x, g_mix, w_in, w_out, da_lambda_q1, da_lambda_k1, da_lambda_q2, da_lambda_k2, da_subln_g,
           rw_mu_x, rw_mu_rkv, rw_w0, rw_w1, rw_w2, rw_a0, rw_a1, rw_a2, rw_g1, rw_g2, rw_k_k, rw_k_a,
           rw_r_k, rw_ln_w, rw_ln_b, g_ffn, moe_w_group, moe_b_group, moe_w_expert, moe_b_expert,
           moe_w_gate, moe_w_up, moe_w_down, g_final):
    b, s, d = x.shape
    t = b * s
    x2 = x.reshape(t, d)

    mu = rw_mu_x[0]
    lora = (rw_w1[0], rw_a1[0], rw_g1[0])
    keep = _pad_cols(jnp.concatenate([(1.0 - mu[i])[:, None] * w for i, w in enumerate(lora)], axis=1), LORA_PAD)
    prev = _pad_cols(jnp.concatenate([mu[i][:, None] * w for i, w in enumerate(lora)], axis=1), LORA_PAD)
    w_all = jnp.concatenate([w_in[0], keep, prev], axis=1).astype(BF16)
    proj = _inproj(x2, g_mix, w_all)

    slopes = jnp.asarray([2.0 ** (-8.0 * (i + 1) / DA_HEADS) for i in range(DA_HEADS)], F32)
    lam4 = jnp.concatenate([da_lambda_q1, da_lambda_k1, da_lambda_q2, da_lambda_k2], axis=0)
    o_da = _diffattn(proj, slopes, lam4, da_subln_g.reshape(DA_V_DIM, 1), b, s)

    w2cat = jnp.zeros((LORA_PAD, 3 * RW_WIDTH), F32)
    w2cat = w2cat.at[0:LORA_W, 0:RW_WIDTH].set(rw_w2[0])
    w2cat = w2cat.at[LORA_W:LORA_W + LORA_A, RW_WIDTH:2 * RW_WIDTH].set(rw_a2[0])
    w2cat = w2cat.at[LORA_W + LORA_A:LORA_W + LORA_A + LORA_G, 2 * RW_WIDTH:].set(rw_g2[0])
    o_rw = _rwkv(proj, rw_mu_rkv[0], rw_w0, rw_a0, w2cat.astype(BF16), rw_k_k, rw_k_a,
                 rw_r_k.reshape(1, RW_WIDTH), rw_ln_w, rw_ln_b, b, s)

    w_r = _pad_cols(jnp.concatenate([moe_w_group[0], moe_w_expert[0]], axis=1), LANES)
    wr_hi = w_r.astype(BF16)
    wr_lo = (w_r - wr_hi.astype(F32)).astype(BF16)
    b_r = _pad_cols(jnp.concatenate([moe_b_group[0], moe_b_expert[0].reshape(-1)])[None, :], LANES)
    h1, xn2, route, cnt = _outproj(x2, o_da, o_rw, w_out[0].astype(BF16), g_ffn, wr_hi, wr_lo, b_r)

    counts = cnt[0, EXPERT_LANE0:EXPERT_LANE0 + N_EXPERTS].astype(jnp.int32)
    pcounts = ((counts + EXPERT_BLOCK - 1) // EXPERT_BLOCK) * EXPERT_BLOCK
    pends = jnp.cumsum(pcounts)
    pstarts = pends - pcounts
    e_idx = route[:, ROUTE_E:ROUTE_E + 2].astype(jnp.int32)
    rank = route[:, ROUTE_R:ROUTE_R + 2].astype(jnp.int32)
    expert_ids = jnp.arange(N_EXPERTS, dtype=jnp.int32)
    row0 = jnp.sum(jnp.where(e_idx[..., None] == expert_ids, pstarts, 0), axis=-1)
    dest = ((row0 + rank) * SUBLANES).reshape(-1)
    n_blocks = (2 * t) // EXPERT_BLOCK + N_EXPERTS
    block_row0 = jnp.arange(n_blocks, dtype=jnp.int32) * EXPERT_BLOCK
    block_e = jnp.minimum(jnp.sum(pends[None, :] <= block_row0[:, None], axis=1), N_EXPERTS - 1).astype(jnp.int32)
    n_used = (pends[-1:] // EXPERT_BLOCK).astype(jnp.int32)
    has_rows = counts > 0
    later = (expert_ids[None, :] > expert_ids[:, None]) & has_rows[None, :]
    next_used = jnp.min(jnp.where(later, expert_ids[None, :], N_EXPERTS), axis=1)
    next_used = jnp.where(next_used == N_EXPERTS, -1, next_used).astype(jnp.int32)
    buffer_id = ((jnp.cumsum(has_rows.astype(jnp.int32)) - 1) % 2).astype(jnp.int32)

    xb = _dispatch(dest, xn2, jnp.zeros((n_blocks * EXPERT_BLOCK * SUBLANES, LANES), jnp.uint32))
    yb = _experts(block_e, n_used, next_used[block_e], buffer_id[block_e], xb,
                  moe_w_gate[0], moe_w_up[0], moe_w_down[0])
    out = _combine(dest, route, h1, g_final[None, :], yb)
    return out.reshape(b, s, d)
```

```python
import functools
import math

import jax
import jax.numpy as jnp
from jax import lax
from jax.experimental import pallas as pl
from jax.experimental.pallas import tpu as pltpu

F32 = jnp.float32
BF16 = jnp.bfloat16

D_MODEL = 2048
DA_WIDTH = 1024
RW_WIDTH = 1024
DA_HEADS = 8
DA_V_DIM = 128
DA_QK_DIM = 64
RW_N = 64
RW_PAIRS = RW_WIDTH // 128
LORA_W, LORA_A, LORA_G = 64, 64, 160
LORA_PAD = 384
IN_COLS = 3 * DA_WIDTH + 3 * RW_WIDTH
NC = IN_COLS + 2 * LORA_PAD
N_GROUPS = 4
EXPERTS_PER_GROUP = 8
N_EXPERTS = 32
D_EXPERT = 512
NORM_EPS = 1e-6
LN_X_EPS = 64e-5
LAM_INIT = 0.8 - 0.6 * math.exp(-0.3 * 0)
LANES = 128
CHUNK = 64
RW_TILE = 256
NCH = RW_TILE // CHUNK
EXPERT_BLOCK = 256
NEG = -1e30
VMEM_LIMIT = 48 * 1024 * 1024
EXPERTS_VMEM_LIMIT = 56 * 1024 * 1024


def _dot(a, b):
    return jnp.dot(a, b, preferred_element_type=F32)


def _dot_nt(a, b):
    return lax.dot_general(a, b, (((1,), (1,)), ((), ())), preferred_element_type=F32)


def _dot_tn(a, b):
    return lax.dot_general(a, b, (((0,), (0,)), ((), ())), preferred_element_type=F32)


def _sigmoid(x):
    return 1.0 / (1.0 + jnp.exp(-x))


HALF = D_MODEL // 2


def _pack_halves(xb):
    lo = lax.bitcast_convert_type(xb[:, :HALF].astype(F32), jnp.uint32) >> 16
    hi = lax.bitcast_convert_type(xb[:, HALF:].astype(F32), jnp.uint32) & jnp.uint32(0xFFFF0000)
    return hi | lo


def _unpack_halves(u):
    lo = lax.bitcast_convert_type(u << 16, F32)
    hi = lax.bitcast_convert_type(u & jnp.uint32(0xFFFF0000), F32)
    return lo, hi


SUBLANES = 8
assert HALF == SUBLANES * LANES


def _store_row_tiles(ref, u):
    n = u.shape[0]
    for k in range(SUBLANES):
        ref[pl.ds(k, n, stride=SUBLANES), :] = u[:, k * LANES:(k + 1) * LANES]


def _load_row_tiles(ref):
    n = ref.shape[0] // SUBLANES
    return jnp.concatenate([ref[pl.ds(k, n, stride=SUBLANES), :] for k in range(SUBLANES)], axis=1)


def _inproj_kernel(x_ref, g_ref, w_ref, o_ref, xn_ref):
    @pl.when(pl.program_id(1) == 0)
    def _():
        x = x_ref[...]
        ms = jnp.mean(x * x, axis=-1, keepdims=True)
        xn_ref[...] = (x * lax.rsqrt(ms + NORM_EPS) * g_ref[...]).astype(BF16)

    o_ref[...] = _dot(xn_ref[...], w_ref[...]).astype(o_ref.dtype)


def _inproj(x2, g, w_all, tm=1024, tn=768):
    t = x2.shape[0]
    tm = min(tm, t)
    return pl.pallas_call(
        _inproj_kernel,
        grid=(t // tm, NC // tn),
        in_specs=[
            pl.BlockSpec((tm, D_MODEL), lambda i, j: (i, 0)),
            pl.BlockSpec((1, D_MODEL), lambda i, j: (0, 0)),
            pl.BlockSpec((D_MODEL, tn), lambda i, j: (0, j)),
        ],
        out_specs=pl.BlockSpec((tm, tn), lambda i, j: (i, j)),
        out_shape=jax.ShapeDtypeStruct((t, NC), BF16),
        scratch_shapes=[pltpu.VMEM((tm, D_MODEL), BF16)],
        compiler_params=pltpu.CompilerParams(
            dimension_semantics=("parallel", "arbitrary"), vmem_limit_bytes=VMEM_LIMIT),
        name="inproj",
    )(x2, g, w_all)


DA_HEADS_PER_STEP = 2


def _da_kernel(qi_tab, kj_tab, slopes_ref, lam_ref, subg_ref, q_ref, k_ref, v_ref, o_ref,
               vt_ref, qz_ref, bias_ref, m_ref, l_ref, lfin_ref, acc_ref, s_ref, p_ref, alpha_ref,
               *, tq, seq):
    tk = tq
    nq = seq // tq
    n_trips = nq * (nq + 1) // 2
    heads = range(DA_HEADS_PER_STEP)
    chains = [(hd, c) for hd in heads for c in range(2)]
    slope = [slopes_ref[pl.program_id(1) * DA_HEADS_PER_STEP + hd] for hd in heads]
    cols = lambda hd: slice(hd * LANES, (hd + 1) * LANES)

    lane = lax.broadcasted_iota(jnp.int32, (tq, LANES), 1)
    krow = lax.broadcasted_iota(jnp.int32, (tk, tq), 0)
    qcol = lax.broadcasted_iota(jnp.int32, (tk, tq), 1)
    for hd in heads:
        for blk in range(seq // LANES):
            r = slice(blk * LANES, (blk + 1) * LANES)
            vt_ref[hd, :, r] = v_ref[r, cols(hd)].astype(F32).T.astype(BF16)
        for blk in range(nq):
            r = slice(blk * tq, (blk + 1) * tq)
            q = q_ref[r, cols(hd)] * jnp.asarray(DA_QK_DIM ** -0.5, BF16)
            qz_ref[2 * hd, r, :] = jnp.where(lane < DA_QK_DIM, q, jnp.zeros_like(q))
            qz_ref[2 * hd + 1, r, :] = jnp.where(lane >= DA_QK_DIM, q, jnp.zeros_like(q))
        alibi = slope[hd] * krow.astype(F32)
        bias_ref[hd, 0] = alibi
        bias_ref[hd, 1] = jnp.where(krow <= qcol, alibi, NEG)

    m_ref[...] = jnp.full_like(m_ref, NEG)
    l_ref[...] = jnp.zeros_like(l_ref)
    lfin_ref[...] = jnp.ones_like(lfin_ref)
    acc_ref[...] = jnp.zeros_like(acc_ref)
    p_ref[...] = jnp.zeros_like(p_ref)
    alpha_ref[...] = jnp.zeros_like(alpha_ref)

    lv = lam_ref[...]
    lam = (jnp.exp(jnp.sum(lv[0:1] * lv[1:2], axis=-1, keepdims=True))
           - jnp.exp(jnp.sum(lv[2:3] * lv[3:4], axis=-1, keepdims=True)) + LAM_INIT)

    def scores_to_scratch(t):
        qs = pl.multiple_of(qi_tab[t] * tq, tq)
        ks = pl.multiple_of(kj_tab[t] * tk, tk)
        diag = (qi_tab[t] == kj_tab[t]).astype(jnp.int32)
        for hd, c in chains:
            ch = 2 * hd + c
            s_ref[ch] = (_dot_nt(k_ref[pl.ds(ks, tk), cols(hd)], qz_ref[ch, pl.ds(qs, tq), :])
                         + bias_ref[hd, diag])

    def values_from_scratch(t):
        ks = pl.multiple_of(kj_tab[t] * tk, tk)
        for hd, c in chains:
            ch = 2 * hd + c
            acc_ref[ch] = alpha_ref[ch:ch + 1, :] * acc_ref[ch] + _dot(vt_ref[hd, :, pl.ds(ks, tk)], p_ref[ch])

    def softmax(t):
        first = kj_tab[t] == 0
        diag = kj_tab[t] == qi_tab[t]
        rel = ((kj_tab[t] - qi_tab[t]) * tk).astype(F32)
        p, alpha = [], []
        for hd, c in chains:
            ch = 2 * hd + c
            off = slope[hd] * rel
            s = s_ref[ch]
            m = jnp.where(first, NEG, m_ref[ch:ch + 1, :])
            l = jnp.where(first, 0.0, l_ref[ch:ch + 1, :])
            m_new = jnp.maximum(m, jnp.max(s, axis=0, keepdims=True) + off)
            pc = jnp.exp(s - (m_new - off))
            a = jnp.exp(m - m_new)
            l_new = a * l + jnp.sum(pc, axis=0, keepdims=True)
            m_ref[ch:ch + 1, :] = m_new
            l_ref[ch:ch + 1, :] = l_new
            lfin_ref[ch:ch + 1, :] = jnp.where(diag, l_new, lfin_ref[ch:ch + 1, :])
            p.append(pc)
            alpha.append(a)
        return p, alpha

    def publish(p, alpha):
        for ch in range(len(chains)):
            alpha_ref[ch:ch + 1, :] = alpha[ch]
            p_ref[ch] = p[ch].astype(BF16)

    def finalize(t):
        qs = pl.multiple_of(qi_tab[t] * tq, tq)
        for hd in heads:
            c0, c1 = 2 * hd, 2 * hd + 1
            o = acc_ref[c0] / lfin_ref[c0:c0 + 1, :] - lam * (acc_ref[c1] / lfin_ref[c1:c1 + 1, :])
            ms = jnp.mean(o * o, axis=0, keepdims=True)
            o = o * lax.rsqrt(ms + NORM_EPS) * subg_ref[...] * (1.0 - LAM_INIT)
            o_ref[pl.ds(qs, tq), cols(hd)] = o.T.astype(o_ref.dtype)

    scores_to_scratch(0)

    def body(t, _):
        prev = jnp.maximum(t - 1, 0)
        p, alpha = softmax(t)
        values_from_scratch(prev)
        scores_to_scratch(jnp.minimum(t + 1, n_trips - 1))
        publish(p, alpha)

        @pl.when(jnp.logical_and(t > 0, qi_tab[prev] == kj_tab[prev]))
        def _():
            finalize(prev)
        return 0

    lax.fori_loop(0, n_trips, body, 0)
    values_from_scratch(n_trips - 1)
    finalize(n_trips - 1)


def _diffattn(proj, slopes, lam4, subg_col, b, s, tq=256):
    nq = s // tq
    nh = DA_HEADS_PER_STEP
    groups = DA_HEADS // nh
    width = nh * LANES
    pairs = [(qi, kj) for qi in range(nq) for kj in range(qi + 1)]
    qi_tab = jnp.asarray([p[0] for p in pairs], jnp.int32)
    kj_tab = jnp.asarray([p[1] for p in pairs], jnp.int32)
    return pl.pallas_call(
        functools.partial(_da_kernel, tq=tq, seq=s),
        grid_spec=pltpu.PrefetchScalarGridSpec(
            num_scalar_prefetch=2,
            grid=(b, groups),
            in_specs=[
                pl.BlockSpec(memory_space=pltpu.SMEM),
                pl.BlockSpec((4, DA_QK_DIM), lambda bi, g, *_: (0, 0)),
                pl.BlockSpec((DA_V_DIM, 1), lambda bi, g, *_: (0, 0)),
                pl.BlockSpec((s, width), lambda bi, g, *_: (bi, g)),
                pl.BlockSpec((s, width), lambda bi, g, *_: (bi, groups + g)),
                pl.BlockSpec((s, width), lambda bi, g, *_: (bi, 2 * groups + g)),
            ],
            out_specs=pl.BlockSpec((s, width), lambda bi, g, *_: (bi, g)),
            scratch_shapes=[
                pltpu.VMEM((nh, DA_V_DIM, s), BF16),
                pltpu.VMEM((2 * nh, s, LANES), BF16),
                pltpu.VMEM((nh, 2, tq, tq), F32),
                pltpu.VMEM((2 * nh, tq), F32),
                pltpu.VMEM((2 * nh, tq), F32),
                pltpu.VMEM((2 * nh, tq), F32),
                pltpu.VMEM((2 * nh, DA_V_DIM, tq), F32),
                pltpu.VMEM((2 * nh, tq, tq), F32),
                pltpu.VMEM((2 * nh, tq, tq), BF16),
                pltpu.VMEM((2 * nh, tq), F32),
            ],
        ),
        out_shape=jax.ShapeDtypeStruct((b * s, DA_WIDTH), BF16),
        compiler_params=pltpu.CompilerParams(
            dimension_semantics=("parallel", "arbitrary"), vmem_limit_bytes=VMEM_LIMIT),
        name="diffattn",
    )(qi_tab, kj_tab, slopes, lam4, subg_col, proj, proj, proj)


def _rwkv_kernel(r_ref, k_ref, v_ref, p1_ref, p2_ref, mu_ref, w0_ref, a0_ref, w2_ref, kk_ref, ka_ref,
                 rk_ref, lnw_ref, lnb_ref, o_ref, state_ref, crkv_ref, cp2_ref):
    L, TT = CHUNK, RW_TILE
    c = pl.program_id(1)

    @pl.when(c == 0)
    def _():
        state_ref[...] = jnp.zeros_like(state_ref)
        crkv_ref[...] = jnp.zeros_like(crkv_ref)
        cp2_ref[...] = jnp.zeros_like(cp2_ref)

    first = lax.broadcasted_iota(jnp.int32, (TT, 1), 0) == 0

    def shift(x, carry):
        return jnp.where(first, carry, pltpu.roll(x, 1, 0))

    r_raw = r_ref[...].astype(F32)
    k_raw = k_ref[...].astype(F32)
    v_raw = v_ref[...].astype(F32)
    p2 = p2_ref[...].astype(F32)
    rs = shift(r_raw, crkv_ref[:, 0:RW_WIDTH])
    ks = shift(k_raw, crkv_ref[:, RW_WIDTH:2 * RW_WIDTH])
    vs = shift(v_raw, crkv_ref[:, 2 * RW_WIDTH:3 * RW_WIDTH])
    lin = p1_ref[...].astype(F32) + shift(p2, cp2_ref[...])
    crkv_ref[:, 0:RW_WIDTH] = r_raw[TT - 1:TT, :]
    crkv_ref[:, RW_WIDTH:2 * RW_WIDTH] = k_raw[TT - 1:TT, :]
    crkv_ref[:, 2 * RW_WIDTH:3 * RW_WIDTH] = v_raw[TT - 1:TT, :]
    cp2_ref[...] = p2[TT - 1:TT, :]

    r = r_raw + (rs - r_raw) * mu_ref[0:1, :]
    k = k_raw + (ks - k_raw) * mu_ref[1:2, :]
    v = v_raw + (vs - v_raw) * mu_ref[2:3, :]

    ll = lax.broadcasted_iota(jnp.int32, (TT, LORA_PAD), 1)
    z = jnp.where(ll < LORA_W, jnp.tanh(lin),
                  jnp.where(ll < LORA_W + LORA_A, lin,
                            jnp.where(ll < LORA_W + LORA_A + LORA_G, _sigmoid(lin), 0.0)))
    d = _dot(z.astype(BF16), w2_ref[...])
    logw = -_sigmoid(w0_ref[...] + d[:, 0:RW_WIDTH]) * math.exp(-0.5)
    a_sig = _sigmoid(a0_ref[...] + d[:, RW_WIDTH:2 * RW_WIDTH])
    g = d[:, 2 * RW_WIDTH:3 * RW_WIDTH]

    li = lax.broadcasted_iota(jnp.int32, (LANES, LANES), 0)
    lj = lax.broadcasted_iota(jnp.int32, (LANES, LANES), 1)
    same_head = jnp.where((li // RW_N) == (lj // RW_N), 1.0, 0.0).astype(BF16)

    def head_sum(x):
        xb = x.astype(BF16)
        return jnp.concatenate(
            [_dot(xb[:, p * LANES:(p + 1) * LANES], same_head) for p in range(RW_PAIRS)], axis=1)

    kk = k * kk_ref[...]
    kk = kk * lax.rsqrt(jnp.maximum(head_sum(kk * kk), 1e-24))
    k = k * (1.0 + (a_sig - 1.0) * ka_ref[...])
    bonus = head_sum(r * k * rk_ref[...])

    ti = lax.broadcasted_iota(jnp.int32, (TT, TT), 0)
    tj = lax.broadcasted_iota(jnp.int32, (TT, TT), 1)
    tri = jnp.where(ti >= tj, jnp.where((ti // L) == (tj // L), 1.0, 0.0), 0.0).astype(BF16)
    w_hi = logw.astype(BF16)
    w_r1 = logw - w_hi.astype(F32)
    w_mid = w_r1.astype(BF16)
    w_lo = (w_r1 - w_mid.astype(F32)).astype(BF16)
    cw = _dot(tri, w_hi) + _dot(tri, w_mid) + _dot(tri, w_lo)
    cw_last = [cw[(ci + 1) * L - 1:(ci + 1) * L, :] for ci in range(NCH)]
    e_in = jnp.exp(cw)
    e_out = jnp.exp(-cw)
    e_end = jnp.exp(jnp.concatenate([jnp.broadcast_to(cl, (L, RW_WIDTH)) for cl in cw_last], axis=0) - cw)
    w_end = [jnp.exp(cl) for cl in cw_last]

    at = -kk * jnp.exp(cw - logw)
    rt = r * e_in
    b = kk * a_sig
    bt = b * e_out
    kt = k * e_out
    bd = b * e_end
    kd = k * e_end

    lane = lax.broadcasted_iota(jnp.int32, (L, LANES), 1)
    lo = lane < RW_N

    def stack(x, ci, p):
        xs = x[ci * L:(ci + 1) * L, p * LANES:(p + 1) * LANES]
        return jnp.concatenate([jnp.where(lo, xs, 0.0), jnp.where(lo, 0.0, xs)], axis=0).astype(BF16)

    si = lax.broadcasted_iota(jnp.int32, (2 * L, 2 * L), 0)
    sj = lax.broadcasted_iota(jnp.int32, (2 * L, 2 * L), 1)
    strict = (si % L) > (sj % L)
    incl = (si % L) >= (sj % L)
    eye = jnp.where(si == sj, 1.0, 0.0)

    pairs = range(RW_PAIRS)
    cat0 = lambda *xs: jnp.concatenate(xs, axis=0)
    cat1 = lambda *xs: jnp.concatenate(xs, axis=1)
    bf = lambda x: x.astype(BF16)
    h = [state_ref[p] for p in pairs]
    ys = []
    for ci in range(NCH):
        a_s = [stack(at, ci, p) for p in pairs]
        r_s = [stack(rt, ci, p) for p in pairs]
        v_s = [stack(v, ci, p) for p in pairs]
        bk = [cat0(stack(bt, ci, p), stack(kt, ci, p)) for p in pairs]
        bkd = [cat0(stack(bd, ci, p), stack(kd, ci, p)) for p in pairs]
        aa = [_dot_nt(a_s[p], bk[p]) for p in pairs]
        rr = [_dot_nt(r_s[p], bk[p]) for p in pairs]
        a_ab = [jnp.where(strict, aa[p][:, 0:2 * L], 0.0) for p in pairs]
        a_ak = [bf(jnp.where(strict, aa[p][:, 2 * L:4 * L], 0.0)) for p in pairs]
        a_rb = [bf(jnp.where(incl, rr[p][:, 0:2 * L], 0.0)) for p in pairs]
        a_rk = [bf(jnp.where(incl, rr[p][:, 2 * L:4 * L], 0.0)) for p in pairs]
        ab = [bf(a) for a in a_ab]
        qpow = [_dot(ab[p], ab[p]) for p in pairs]
        inv = [eye + a_ab[p] for p in pairs]
        for _ in range(int(math.log2(L)) - 2):
            qb = [bf(q) for q in qpow]
            res = [_dot(qb[p], cat1(qb[p], bf(inv[p]))) for p in pairs]
            qpow = [res[p][:, 0:2 * L] for p in pairs]
            inv = [inv[p] + res[p][:, 2 * L:4 * L] for p in pairs]
        inv = [bf(inv[p] + _dot(bf(qpow[p]), bf(inv[p]))) for p in pairs]
        h_b = [bf(hp) for hp in h]
        x = [_dot(cat1(a_s[p], a_ak[p]), cat0(h_b[p], v_s[p])) for p in pairs]
        u_b = [bf(_dot(inv[p], bf(x[p]))) for p in pairs]
        y2 = [_dot(cat1(r_s[p], a_rb[p], a_rk[p]), cat0(h_b[p], u_b[p], v_s[p])) for p in pairs]
        w_col = [jnp.broadcast_to(w_end[ci][:, p * LANES:(p + 1) * LANES], (LANES, LANES)).T for p in pairs]
        h = [w_col[p] * h[p] + _dot_tn(bkd[p], cat0(u_b[p], v_s[p])) for p in pairs]
        ys.append(cat1(*[y2[p][0:L] + y2[p][L:2 * L] for p in pairs]))
    for p in pairs:
        state_ref[p] = h[p]
    y = cat0(*ys)

    mean = head_sum(y) * (1.0 / RW_N)
    yc = y - mean
    var = head_sum(yc * yc) * (1.0 / RW_N)
    y = yc * lax.rsqrt(var + LN_X_EPS) * lnw_ref[...] + lnb_ref[...]
    y = y + bonus * v
    o_ref[...] = (y * g).astype(o_ref.dtype)


def _rwkv(proj, mu_rkv, w0, a0, w2cat, k_k, k_a, r_k, ln_w, ln_b, b, s):
    tt = RW_TILE
    nc = s // tt
    row = lambda n: pl.BlockSpec((1, n), lambda bi, c: (0, 0))
    rkv_col0 = 3 * DA_WIDTH // RW_WIDTH
    lora_col0 = IN_COLS // LORA_PAD
    return pl.pallas_call(
        _rwkv_kernel,
        grid=(b, nc),
        in_specs=[
            pl.BlockSpec((tt, RW_WIDTH), lambda bi, c: (bi * nc + c, rkv_col0)),
            pl.BlockSpec((tt, RW_WIDTH), lambda bi, c: (bi * nc + c, rkv_col0 + 1)),
            pl.BlockSpec((tt, RW_WIDTH), lambda bi, c: (bi * nc + c, rkv_col0 + 2)),
            pl.BlockSpec((tt, LORA_PAD), lambda bi, c: (bi * nc + c, lora_col0)),
            pl.BlockSpec((tt, LORA_PAD), lambda bi, c: (bi * nc + c, lora_col0 + 1)),
            pl.BlockSpec((3, RW_WIDTH), lambda bi, c: (0, 0)),
            row(RW_WIDTH), row(RW_WIDTH),
            pl.BlockSpec((LORA_PAD, 3 * RW_WIDTH), lambda bi, c: (0, 0)),
            row(RW_WIDTH), row(RW_WIDTH), row(RW_WIDTH), row(RW_WIDTH), row(RW_WIDTH),
        ],
        out_specs=pl.BlockSpec((tt, RW_WIDTH), lambda bi, c: (bi * nc + c, 0)),
        out_shape=jax.ShapeDtypeStruct((b * s, RW_WIDTH), BF16),
        scratch_shapes=[
            pltpu.VMEM((RW_PAIRS, LANES, LANES), F32),
            pltpu.VMEM((1, 3 * RW_WIDTH), F32),
            pltpu.VMEM((1, LORA_PAD), F32),
        ],
        compiler_params=pltpu.CompilerParams(
            dimension_semantics=("parallel", "arbitrary"), vmem_limit_bytes=VMEM_LIMIT),
        name="rwkv",
    )(proj, proj, proj, proj, proj, mu_rkv, w0, a0, w2cat, k_k, k_a, r_k, ln_w, ln_b)


ROUTE_E, ROUTE_G, ROUTE_R = 0, 2, 4
EXPERT_LANE0 = N_GROUPS


def _outproj_kernel(x_ref, oda_ref, orw_ref, wout_ref, g_ref, wrh_ref, wrl_ref, br_ref,
                    h_ref, xn_ref, route_ref, cnt_ref, base_ref, *, tm):
    @pl.when(pl.program_id(0) == 0)
    def _():
        base_ref[...] = jnp.zeros_like(base_ref)

    mix = _dot(oda_ref[...], wout_ref[0:DA_WIDTH, :]) + _dot(orw_ref[...], wout_ref[DA_WIDTH:, :])
    h = x_ref[...] + mix
    h_ref[...] = h
    ms = jnp.mean(h * h, axis=-1, keepdims=True)
    xn = h * lax.rsqrt(ms + NORM_EPS) * g_ref[...]
    x_hi = xn.astype(BF16)
    _store_row_tiles(xn_ref, _pack_halves(x_hi))

    x_lo = (xn - x_hi.astype(F32)).astype(BF16)
    hi_both = _dot(x_hi, jnp.concatenate([wrh_ref[...], wrl_ref[...]], axis=1))
    lg = (hi_both[:, 0:LANES] + _dot(x_lo, wrh_ref[...]) + hi_both[:, LANES:]) + br_ref[...]

    lane = lax.broadcasted_iota(jnp.int32, (tm, LANES), 1)
    big = jnp.int32(1 << 20)
    gl = jnp.where(lane < N_GROUPS, lg, NEG)
    gmax = jnp.max(gl, axis=-1, keepdims=True)
    grp = jnp.min(jnp.where(gl == gmax, lane, big), axis=-1, keepdims=True)
    p_grp = 1.0 / jnp.sum(jnp.exp(gl - gmax), axis=-1, keepdims=True)
    eg = jnp.where(lane >= EXPERT_LANE0, (lane - EXPERT_LANE0) // EXPERTS_PER_GROUP, -1)
    el = jnp.where(eg == grp, lg, NEG)
    emax = jnp.max(el, axis=-1, keepdims=True)
    pe = jnp.exp(el - emax)
    probs = pe / jnp.sum(pe, axis=-1, keepdims=True)
    probs = jnp.where(eg == grp, probs, -1.0)
    p1 = jnp.max(probs, axis=-1, keepdims=True)
    i1 = jnp.min(jnp.where(probs == p1, lane, big), axis=-1, keepdims=True)
    probs2 = jnp.where(lane == i1, -1.0, probs)
    p2 = jnp.max(probs2, axis=-1, keepdims=True)
    i2 = jnp.min(jnp.where(probs2 == p2, lane, big), axis=-1, keepdims=True)
    gate1 = p_grp * p1 / (p1 + p2)
    gate2 = p_grp * p2 / (p1 + p2)

    oh1 = jnp.where(lane == i1, 1.0, 0.0)
    oh2 = jnp.where(lane == i2, 1.0, 0.0)
    ri = lax.broadcasted_iota(jnp.int32, (tm, tm), 0)
    rj = lax.broadcasted_iota(jnp.int32, (tm, tm), 1)
    lower = jnp.where(ri > rj, 1.0, 0.0).astype(BF16)
    base = base_ref[...]
    tot1 = jnp.sum(oh1, axis=0, keepdims=True)
    before = _dot(lower, jnp.concatenate([oh1, oh2], axis=1).astype(BF16))
    c1 = base + before[:, 0:LANES]
    c2 = base + tot1 + before[:, LANES:]
    rank1 = jnp.sum(oh1 * c1, axis=-1, keepdims=True)
    rank2 = jnp.sum(oh2 * c2, axis=-1, keepdims=True)
    base = base + tot1 + jnp.sum(oh2, axis=0, keepdims=True)
    base_ref[...] = base
    cnt_ref[...] = base

    e1 = (i1 - EXPERT_LANE0).astype(F32)
    e2 = (i2 - EXPERT_LANE0).astype(F32)
    rec = jnp.zeros((tm, LANES), F32)
    for ln, val in ((ROUTE_E, e1), (ROUTE_E + 1, e2), (ROUTE_G, gate1), (ROUTE_G + 1, gate2),
                    (ROUTE_R, rank1), (ROUTE_R + 1, rank2)):
        rec = jnp.where(lane == ln, val, rec)
    route_ref[...] = rec


def _outproj(x2, o_da, o_rw, w_out, g_ffn, wr_hi, wr_lo, b_r, tm=512):
    t = x2.shape[0]
    const = lambda shape: pl.BlockSpec(shape, lambda i: (0, 0))
    return pl.pallas_call(
        functools.partial(_outproj_kernel, tm=tm),
        grid=(t // tm,),
        in_specs=[
            pl.BlockSpec((tm, D_MODEL), lambda i: (i, 0)),
            pl.BlockSpec((tm, DA_WIDTH), lambda i: (i, 0)),
            pl.BlockSpec((tm, RW_WIDTH), lambda i: (i, 0)),
            const((D_MODEL, D_MODEL)),
            const((1, D_MODEL)),
            const((D_MODEL, LANES)),
            const((D_MODEL, LANES)),
            const((1, LANES)),
        ],
        out_specs=[
            pl.BlockSpec((tm, D_MODEL), lambda i: (i, 0)),
            pl.BlockSpec((tm * SUBLANES, LANES), lambda i: (i, 0)),
            pl.BlockSpec((tm, LANES), lambda i: (i, 0)),
            const((1, LANES)),
        ],
        out_shape=[
            jax.ShapeDtypeStruct((t, D_MODEL), F32),
            jax.ShapeDtypeStruct((t * SUBLANES, LANES), jnp.uint32),
            jax.ShapeDtypeStruct((t, LANES), F32),
            jax.ShapeDtypeStruct((1, LANES), F32),
        ],
        scratch_shapes=[pltpu.VMEM((1, LANES), F32)],
        compiler_params=pltpu.CompilerParams(
            dimension_semantics=("arbitrary",), vmem_limit_bytes=VMEM_LIMIT),
        name="outproj_router",
    )(x2, o_da, o_rw, w_out, g_ffn, wr_hi, wr_lo, b_r)


def _row_copy(src_ref, dst_ref, src_row8, dst_row8, sem):
    src = src_ref.at[pl.ds(pl.multiple_of(src_row8, SUBLANES), SUBLANES)]
    dst = dst_ref.at[pl.ds(pl.multiple_of(dst_row8, SUBLANES), SUBLANES)]
    return pltpu.make_async_copy(src, dst, sem)


def _dispatch_kernel(dest_ref, x_ref, xb_in_hbm, xb_hbm, sem, *, td):
    del xb_in_hbm
    t0 = pl.program_id(0) * td

    def start(i, _):
        tok = t0 + i
        _row_copy(x_ref, xb_hbm, i * SUBLANES, dest_ref[2 * tok], sem).start(priority=0)
        _row_copy(x_ref, xb_hbm, i * SUBLANES, dest_ref[2 * tok + 1], sem).start(priority=1)
        return 0

    lax.fori_loop(0, td, start, 0, unroll=8)
    for _ in range(2):
        pltpu.make_async_copy(x_ref, xb_hbm.at[pl.ds(0, td * SUBLANES)], sem).wait()


def _dispatch(dest_flat, xn2, xb_zero, td=512):
    t = xn2.shape[0] // SUBLANES
    return pl.pallas_call(
        functools.partial(_dispatch_kernel, td=td),
        grid_spec=pltpu.PrefetchScalarGridSpec(
            num_scalar_prefetch=1,
            grid=(t // td,),
            in_specs=[pl.BlockSpec((td * SUBLANES, LANES), lambda i, d: (i, 0)),
                      pl.BlockSpec(memory_space=pl.ANY)],
            out_specs=pl.BlockSpec(memory_space=pl.ANY),
            scratch_shapes=[pltpu.SemaphoreType.DMA],
        ),
        out_shape=jax.ShapeDtypeStruct(xb_zero.shape, xb_zero.dtype),
        input_output_aliases={2: 0},
        compiler_params=pltpu.CompilerParams(dimension_semantics=("arbitrary",)),
        name="dispatch",
    )(dest_flat, xn2, xb_zero)


def _experts_kernel(be_ref, nu_ref, nxt_ref, slot_ref, xb_ref, wg_hbm, wu_hbm, wd_hbm, yb_ref,
                    wgf_ref, wuf_ref, wdf_ref, wgb_ref, wub_ref, wdb_ref, sem):
    i = pl.program_id(0)
    used = i < nu_ref[0]
    new_expert = jnp.logical_or(i == 0, be_ref[i] != be_ref[jnp.maximum(i - 1, 0)])

    def weight_copies(e, slot):
        return [pltpu.make_async_copy(src.at[e], dst.at[slot], sem.at[slot])
                for src, dst in ((wg_hbm, wgf_ref), (wu_hbm, wuf_ref), (wd_hbm, wdf_ref))]

    def swiglu(wg_lo, wg_hi, wu_lo, wu_hi, wd):
        x_lo, x_hi = (half.astype(BF16) for half in _unpack_halves(_load_row_tiles(xb_ref)))
        hg = _dot(x_lo, wg_lo) + _dot(x_hi, wg_hi)
        hu = _dot(x_lo, wu_lo) + _dot(x_hi, wu_hi)
        hid = hg * _sigmoid(hg) * hu
        _store_row_tiles(yb_ref, _pack_halves(_dot(hid.astype(BF16), wd).astype(BF16)))

    @pl.when(jnp.logical_and(used, new_expert))
    def _():
        slot = slot_ref[i]

        @pl.when(i == 0)
        def _():
            for cp in weight_copies(be_ref[i], slot):
                cp.start()

        for cp in weight_copies(be_ref[i], slot):
            cp.wait()

        @pl.when(nxt_ref[i] >= 0)
        def _():
            for cp in weight_copies(nxt_ref[i], 1 - slot):
                cp.start()

        wg, wu, wd = (w[slot].astype(BF16) for w in (wgf_ref, wuf_ref, wdf_ref))
        wgb_ref[...] = wg
        wub_ref[...] = wu
        wdb_ref[...] = wd
        swiglu(wg[0:HALF, :], wg[HALF:, :], wu[0:HALF, :], wu[HALF:, :], wd)

    @pl.when(jnp.logical_and(used, jnp.logical_not(new_expert)))
    def _():
        swiglu(wgb_ref[0:HALF, :], wgb_ref[HALF:, :], wub_ref[0:HALF, :], wub_ref[HALF:, :], wdb_ref[...])

    @pl.when(jnp.logical_not(used))
    def _():
        yb_ref[...] = jnp.zeros_like(yb_ref)


def _experts(block_e, n_used, next_e, slot, xb, w_gate, w_up, w_down):
    block_rows = EXPERT_BLOCK * SUBLANES
    nb = xb.shape[0] // block_rows
    return pl.pallas_call(
        _experts_kernel,
        grid_spec=pltpu.PrefetchScalarGridSpec(
            num_scalar_prefetch=4,
            grid=(nb,),
            in_specs=[
                pl.BlockSpec((block_rows, LANES), lambda i, be, nu, *_: (jnp.minimum(i, nu[0] - 1), 0)),
                pl.BlockSpec(memory_space=pl.ANY),
                pl.BlockSpec(memory_space=pl.ANY),
                pl.BlockSpec(memory_space=pl.ANY),
            ],
            out_specs=pl.BlockSpec((block_rows, LANES), lambda i, *_: (i, 0)),
            scratch_shapes=[
                pltpu.VMEM((2, D_MODEL, D_EXPERT), F32),
                pltpu.VMEM((2, D_MODEL, D_EXPERT), F32),
                pltpu.VMEM((2, D_EXPERT, D_MODEL), F32),
                pltpu.VMEM((D_MODEL, D_EXPERT), BF16),
                pltpu.VMEM((D_MODEL, D_EXPERT), BF16),
                pltpu.VMEM((D_EXPERT, D_MODEL), BF16),
                pltpu.SemaphoreType.DMA((2,)),
            ],
        ),
        out_shape=jax.ShapeDtypeStruct(xb.shape, jnp.uint32),
        compiler_params=pltpu.CompilerParams(
            dimension_semantics=("arbitrary",), vmem_limit_bytes=EXPERTS_VMEM_LIMIT),
        name="experts",
    )(block_e, n_used, next_e, slot, xb, w_gate, w_up, w_down)


def _combine_kernel(dest_ref, route_ref, h_ref, g_ref, yb_hbm, o_ref, y1_ref, y2_ref, sem, *, tc):
    step = pl.program_id(0)
    n_steps = pl.num_programs(0)

    def gather(tile, slot):
        def start(i, _):
            tok = tile * tc + i
            _row_copy(yb_hbm, y1_ref.at[slot], dest_ref[2 * tok], i * SUBLANES, sem.at[slot]).start(priority=0)
            _row_copy(yb_hbm, y2_ref.at[slot], dest_ref[2 * tok + 1], i * SUBLANES, sem.at[slot]).start(priority=1)
            return 0

        lax.fori_loop(0, tc, start, 0, unroll=8)

    @pl.when(step == 0)
    def _():
        gather(0, 0)

    @pl.when(step + 1 < n_steps)
    def _():
        gather(step + 1, (step + 1) % 2)

    slot = step % 2
    pltpu.make_async_copy(yb_hbm.at[pl.ds(0, tc * SUBLANES)], y1_ref.at[slot], sem.at[slot]).wait()
    pltpu.make_async_copy(yb_hbm.at[pl.ds(0, tc * SUBLANES)], y2_ref.at[slot], sem.at[slot]).wait()

    rec = route_ref[...]
    lane = lax.broadcasted_iota(jnp.int32, (tc, LANES), 1)
    g1 = jnp.sum(jnp.where(lane == ROUTE_G, rec, 0.0), axis=-1, keepdims=True)
    g2 = jnp.sum(jnp.where(lane == ROUTE_G + 1, rec, 0.0), axis=-1, keepdims=True)
    y1_lo, y1_hi = _unpack_halves(_load_row_tiles(y1_ref.at[slot]))
    y2_lo, y2_hi = _unpack_halves(_load_row_tiles(y2_ref.at[slot]))
    h_lo = h_ref[:, 0:HALF] + (y1_lo * g1 + y2_lo * g2)
    h_hi = h_ref[:, HALF:] + (y1_hi * g1 + y2_hi * g2)
    ms = (jnp.sum(h_lo * h_lo, axis=-1, keepdims=True)
          + jnp.sum(h_hi * h_hi, axis=-1, keepdims=True)) * (1.0 / D_MODEL)
    scale = lax.rsqrt(ms + NORM_EPS)
    o_ref[:, 0:HALF] = h_lo * scale * g_ref[:, 0:HALF]
    o_ref[:, HALF:] = h_hi * scale * g_ref[:, HALF:]


def _combine(dest_flat, route, h1, g_final, yb, tc=256):
    t = h1.shape[0]
    return pl.pallas_call(
        functools.partial(_combine_kernel, tc=tc),
        grid_spec=pltpu.PrefetchScalarGridSpec(
            num_scalar_prefetch=1,
            grid=(t // tc,),
            in_specs=[
                pl.BlockSpec((tc, LANES), lambda i, d: (i, 0)),
                pl.BlockSpec((tc, D_MODEL), lambda i, d: (i, 0)),
                pl.BlockSpec((1, D_MODEL), lambda i, d: (0, 0)),
                pl.BlockSpec(memory_space=pl.ANY),
            ],
            out_specs=pl.BlockSpec((tc, D_MODEL), lambda i, d: (i, 0)),
            scratch_shapes=[
                pltpu.VMEM((2, tc * SUBLANES, LANES), jnp.uint32),
                pltpu.VMEM((2, tc * SUBLANES, LANES), jnp.uint32),
                pltpu.SemaphoreType.DMA((2,)),
            ],
        ),
        out_shape=jax.ShapeDtypeStruct((t, D_MODEL), F32),
        compiler_params=pltpu.CompilerParams(
            dimension_semantics=("arbitrary",), vmem_limit_bytes=VMEM_LIMIT),
        name="combine",
    )(dest_flat, route, h1, g_final, yb)


def _pad_cols(w, n):
    return jnp.pad(w, ((0, 0), (0, n - w.shape[1])))


def kernel(x, g_mix, w_in, w_out, da_lambda_q1, da_lambda_k1, da_lambda_q2, da_lambda_k2, da_subln_g,
           rw_mu_x, rw_mu_rkv, rw_w0, rw_w1, rw_w2, rw_a0, rw_a1, rw_a2, rw_g1, rw_g2, rw_k_k, rw_k_a,
           rw_r_k, rw_ln_w, rw_ln_b, g_ffn, moe_w_group, moe_b_group, moe_w_expert, moe_b_expert,
           moe_w_gate, moe_w_up, moe_w_down, g_final):
    b, s, d = x.shape
    t = b * s
    x2 = x.reshape(t, d)

    mu = rw_mu_x[0]
    lora = (rw_w1[0], rw_a1[0], rw_g1[0])
    keep = _pad_cols(jnp.concatenate([(1.0 - mu[i])[:, None] * w for i, w in enumerate(lora)], axis=1), LORA_PAD)
    prev = _pad_cols(jnp.concatenate([mu[i][:, None] * w for i, w in enumerate(lora)], axis=1), LORA_PAD)
    w_all = jnp.concatenate([w_in[0], keep, prev], axis=1).astype(BF16)
    proj = _inproj(x2, g_mix, w_all)

    slopes = jnp.asarray([2.0 ** (-8.0 * (i + 1) / DA_HEADS) for i in range(DA_HEADS)], F32)
    lam4 = jnp.concatenate([da_lambda_q1, da_lambda_k1, da_lambda_q2, da_lambda_k2], axis=0)
    o_da = _diffattn(proj, slopes, lam4, da_subln_g.reshape(DA_V_DIM, 1), b, s)

    w2cat = jnp.zeros((LORA_PAD, 3 * RW_WIDTH), F32)
    w2cat = w2cat.at[0:LORA_W, 0:RW_WIDTH].set(rw_w2[0])
    w2cat = w2cat.at[LORA_W:LORA_W + LORA_A, RW_WIDTH:2 * RW_WIDTH].set(rw_a2[0])
    w2cat = w2cat.at[LORA_W + LORA_A:LORA_W + LORA_A + LORA_G, 2 * RW_WIDTH:].set(rw_g2[0])
    o_rw = _rwkv(proj, rw_mu_rkv[0], rw_w0, rw_a0, w2cat.astype(BF16), rw_k_k, rw_k_a,
                 rw_r_k.reshape(1, RW_WIDTH), rw_ln_w, rw_ln_b, b, s)

    w_r = _pad_cols(jnp.concatenate([moe_w_group[0], moe_w_expert[0]], axis=1), LANES)
    wr_hi = w_r.astype(BF16)
    wr_lo = (w_r - wr_hi.astype(F32)).astype(BF16)
    b_r = _pad_cols(jnp.concatenate([moe_b_group[0], moe_b_expert[0].reshape(-1)])[None, :], LANES)
    h1, xn2, route, cnt = _outproj(x2, o_da, o_rw, w_out[0].astype(BF16), g_ffn, wr_hi, wr_lo, b_r)

    counts = cnt[0, EXPERT_LANE0:EXPERT_LANE0 + N_EXPERTS].astype(jnp.int32)
    pcounts = ((counts + EXPERT_BLOCK - 1) // EXPERT_BLOCK) * EXPERT_BLOCK
    expert_ids = jnp.arange(N_EXPERTS, dtype=jnp.int32)
    upto = expert_ids[None, :] <= expert_ids[:, None]
    pends = jnp.sum(jnp.where(upto, pcounts[None, :], 0), axis=1)
    pstarts = pends - pcounts
    e_idx = route[:, ROUTE_E:ROUTE_E + 2].astype(jnp.int32)
    rank = route[:, ROUTE_R:ROUTE_R + 2].astype(jnp.int32)
    row0 = jnp.sum(jnp.where(e_idx[..., None] == expert_ids, pstarts, 0), axis=-1)
    dest = ((row0 + rank) * SUBLANES).reshape(-1)
    n_blocks = (2 * t) // EXPERT_BLOCK + N_EXPERTS
    block_row0 = jnp.arange(n_blocks, dtype=jnp.int32) * EXPERT_BLOCK
    block_e = jnp.minimum(jnp.sum(pends[None, :] <= block_row0[:, None], axis=1), N_EXPERTS - 1).astype(jnp.int32)
    n_used = (pends[-1:] // EXPERT_BLOCK).astype(jnp.int32)
    has_rows = counts > 0
    later = (expert_ids[None, :] > expert_ids[:, None]) & has_rows[None, :]
    next_used = jnp.min(jnp.where(later, expert_ids[None, :], N_EXPERTS), axis=1)
    next_used = jnp.where(next_used == N_EXPERTS, -1, next_used).astype(jnp.int32)
    buffer_id = ((jnp.sum(jnp.where(upto & has_rows[None, :], 1, 0), axis=1) - 1) % 2).astype(jnp.int32)

    xb = _dispatch(dest, xn2, jnp.zeros((n_blocks * EXPERT_BLOCK * SUBLANES, LANES), jnp.uint32))
    yb = _experts(block_e, n_used, next_used[block_e], buffer_id[block_e], xb,
                  moe_w_gate[0], moe_w_up[0], moe_w_down[0])
    out = _combine(dest, route, h1, g_final[None, :], yb)
    return out.reshape(b, s, d)
```

```python
import functools
import math

import jax
import jax.numpy as jnp
from jax import lax
from jax.experimental import pallas as pl
from jax.experimental.pallas import tpu as pltpu

F32 = jnp.float32
BF16 = jnp.bfloat16

D_MODEL = 2048
DA_WIDTH = 1024
RW_WIDTH = 1024
DA_HEADS = 8
DA_V_DIM = 128
DA_QK_DIM = 64
RW_N = 64
RW_PAIRS = RW_WIDTH // 128
LORA_W, LORA_A, LORA_G = 64, 64, 160
LORA_PAD = 384
IN_COLS = 3 * DA_WIDTH + 3 * RW_WIDTH
NC = IN_COLS + 2 * LORA_PAD
N_GROUPS = 4
EXPERTS_PER_GROUP = 8
N_EXPERTS = 32
D_EXPERT = 512
NORM_EPS = 1e-6
LN_X_EPS = 64e-5
LAM_INIT = 0.8 - 0.6 * math.exp(-0.3 * 0)
LANES = 128
CHUNK = 64
RW_TILE = 256
NCH = RW_TILE // CHUNK
EXPERT_BLOCK = 256
NEG = -1e30
VMEM_LIMIT = 48 * 1024 * 1024
EXPERTS_VMEM_LIMIT = 56 * 1024 * 1024


def _dot(a, b):
    return jnp.dot(a, b, preferred_element_type=F32)


def _dot_nt(a, b):
    return lax.dot_general(a, b, (((1,), (1,)), ((), ())), preferred_element_type=F32)


def _dot_tn(a, b):
    return lax.dot_general(a, b, (((0,), (0,)), ((), ())), preferred_element_type=F32)


def _sigmoid(x):
    return 1.0 / (1.0 + jnp.exp(-x))


HALF = D_MODEL // 2


def _pack_halves(xb):
    lo = lax.bitcast_convert_type(xb[:, :HALF].astype(F32), jnp.uint32) >> 16
    hi = lax.bitcast_convert_type(xb[:, HALF:].astype(F32), jnp.uint32) & jnp.uint32(0xFFFF0000)
    return hi | lo


def _unpack_halves(u):
    lo = lax.bitcast_convert_type(u << 16, F32)
    hi = lax.bitcast_convert_type(u & jnp.uint32(0xFFFF0000), F32)
    return lo, hi


SUBLANES = 8
assert HALF == SUBLANES * LANES


def _store_row_tiles(ref, u):
    n = u.shape[0]
    for k in range(SUBLANES):
        ref[pl.ds(k, n, stride=SUBLANES), :] = u[:, k * LANES:(k + 1) * LANES]


def _load_row_tiles(ref):
    n = ref.shape[0] // SUBLANES
    return jnp.concatenate([ref[pl.ds(k, n, stride=SUBLANES), :] for k in range(SUBLANES)], axis=1)


def _inproj_kernel(x_ref, g_ref, w_ref, o_ref, xn_ref):
    @pl.when(pl.program_id(1) == 0)
    def _():
        x = x_ref[...]
        ms = jnp.mean(x * x, axis=-1, keepdims=True)
        xn_ref[...] = (x * lax.rsqrt(ms + NORM_EPS) * g_ref[...]).astype(BF16)

    o_ref[...] = _dot(xn_ref[...], w_ref[...]).astype(o_ref.dtype)


def _inproj(x2, g, w_all, tm=1024, tn=2304):
    t = x2.shape[0]
    tm = min(tm, t)
    vmem_limit = 2 * (tm * D_MODEL * 4 + D_MODEL * tn * 2 + tm * tn * 2) + tm * D_MODEL * 2 + (6 << 20)
    return pl.pallas_call(
        _inproj_kernel,
        grid=(t // tm, NC // tn),
        in_specs=[
            pl.BlockSpec((tm, D_MODEL), lambda i, j: (i, 0)),
            pl.BlockSpec((1, D_MODEL), lambda i, j: (0, 0)),
            pl.BlockSpec((D_MODEL, tn), lambda i, j: (0, j)),
        ],
        out_specs=pl.BlockSpec((tm, tn), lambda i, j: (i, j)),
        out_shape=jax.ShapeDtypeStruct((t, NC), BF16),
        scratch_shapes=[pltpu.VMEM((tm, D_MODEL), BF16)],
        compiler_params=pltpu.CompilerParams(
            dimension_semantics=("parallel", "arbitrary"), vmem_limit_bytes=vmem_limit),
        name="inproj",
    )(x2, g, w_all)


DA_HEADS_PER_STEP = 2


def _da_kernel(qi_tab, kj_tab, slopes_ref, lam_ref, subg_ref, q_ref, k_ref, v_ref, o_ref,
               vt_ref, qz_ref, bias_ref, m_ref, l_ref, lfin_ref, acc_ref, s_ref, p_ref, alpha_ref,
               *, tq, seq):
    tk = tq
    nq = seq // tq
    n_trips = nq * (nq + 1) // 2
    heads = range(DA_HEADS_PER_STEP)
    chains = [(hd, c) for hd in heads for c in range(2)]
    slope = [slopes_ref[pl.program_id(1) * DA_HEADS_PER_STEP + hd] for hd in heads]
    cols = lambda hd: slice(hd * LANES, (hd + 1) * LANES)

    lane = lax.broadcasted_iota(jnp.int32, (tq, LANES), 1)
    krow = lax.broadcasted_iota(jnp.int32, (tk, tq), 0)
    qcol = lax.broadcasted_iota(jnp.int32, (tk, tq), 1)
    for hd in heads:
        for blk in range(seq // LANES):
            r = slice(blk * LANES, (blk + 1) * LANES)
            vt_ref[hd, :, r] = v_ref[r, cols(hd)].astype(F32).T.astype(BF16)
        for blk in range(nq):
            r = slice(blk * tq, (blk + 1) * tq)
            q = q_ref[r, cols(hd)] * jnp.asarray(DA_QK_DIM ** -0.5, BF16)
            qz_ref[2 * hd, r, :] = jnp.where(lane < DA_QK_DIM, q, jnp.zeros_like(q))
            qz_ref[2 * hd + 1, r, :] = jnp.where(lane >= DA_QK_DIM, q, jnp.zeros_like(q))
        alibi = slope[hd] * krow.astype(F32)
        bias_ref[hd, 0] = alibi
        bias_ref[hd, 1] = jnp.where(krow <= qcol, alibi, NEG)

    m_ref[...] = jnp.full_like(m_ref, NEG)
    l_ref[...] = jnp.zeros_like(l_ref)
    lfin_ref[...] = jnp.ones_like(lfin_ref)
    acc_ref[...] = jnp.zeros_like(acc_ref)
    p_ref[...] = jnp.zeros_like(p_ref)
    alpha_ref[...] = jnp.zeros_like(alpha_ref)

    lv = lam_ref[...]
    lam = (jnp.exp(jnp.sum(lv[0:1] * lv[1:2], axis=-1, keepdims=True))
           - jnp.exp(jnp.sum(lv[2:3] * lv[3:4], axis=-1, keepdims=True)) + LAM_INIT)

    def scores_to_scratch(t):
        qs = pl.multiple_of(qi_tab[t] * tq, tq)
        ks = pl.multiple_of(kj_tab[t] * tk, tk)
        diag = (qi_tab[t] == kj_tab[t]).astype(jnp.int32)
        for hd, c in chains:
            ch = 2 * hd + c
            s_ref[ch] = (_dot_nt(k_ref[pl.ds(ks, tk), cols(hd)], qz_ref[ch, pl.ds(qs, tq), :])
                         + bias_ref[hd, diag])

    def values_from_scratch(t):
        ks = pl.multiple_of(kj_tab[t] * tk, tk)
        for hd, c in chains:
            ch = 2 * hd + c
            acc_ref[ch] = alpha_ref[ch:ch + 1, :] * acc_ref[ch] + _dot(vt_ref[hd, :, pl.ds(ks, tk)], p_ref[ch])

    def softmax(t):
        first = kj_tab[t] == 0
        diag = kj_tab[t] == qi_tab[t]
        rel = ((kj_tab[t] - qi_tab[t]) * tk).astype(F32)
        p, alpha = [], []
        for hd, c in chains:
            ch = 2 * hd + c
            off = slope[hd] * rel
            s = s_ref[ch]
            m = jnp.where(first, NEG, m_ref[ch:ch + 1, :])
            l = jnp.where(first, 0.0, l_ref[ch:ch + 1, :])
            m_new = jnp.maximum(m, jnp.max(s, axis=0, keepdims=True) + off)
            pc = jnp.exp(s - (m_new - off))
            a = jnp.exp(m - m_new)
            l_new = a * l + jnp.sum(pc, axis=0, keepdims=True)
            m_ref[ch:ch + 1, :] = m_new
            l_ref[ch:ch + 1, :] = l_new
            lfin_ref[ch:ch + 1, :] = jnp.where(diag, l_new, lfin_ref[ch:ch + 1, :])
            p.append(pc)
            alpha.append(a)
        return p, alpha

    def publish(p, alpha):
        for ch in range(len(chains)):
            alpha_ref[ch:ch + 1, :] = alpha[ch]
            p_ref[ch] = p[ch].astype(BF16)

    def finalize(t):
        qs = pl.multiple_of(qi_tab[t] * tq, tq)
        for hd in heads:
            c0, c1 = 2 * hd, 2 * hd + 1
            o = acc_ref[c0] / lfin_ref[c0:c0 + 1, :] - lam * (acc_ref[c1] / lfin_ref[c1:c1 + 1, :])
            ms = jnp.mean(o * o, axis=0, keepdims=True)
            o = o * lax.rsqrt(ms + NORM_EPS) * subg_ref[...] * (1.0 - LAM_INIT)
            o_ref[pl.ds(qs, tq), cols(hd)] = o.T.astype(o_ref.dtype)

    scores_to_scratch(0)

    def body(t, _):
        prev = jnp.maximum(t - 1, 0)
        p, alpha = softmax(t)
        values_from_scratch(prev)
        scores_to_scratch(jnp.minimum(t + 1, n_trips - 1))
        publish(p, alpha)

        @pl.when(jnp.logical_and(t > 0, qi_tab[prev] == kj_tab[prev]))
        def _():
            finalize(prev)
        return 0

    lax.fori_loop(0, n_trips, body, 0)
    values_from_scratch(n_trips - 1)
    finalize(n_trips - 1)


def _diffattn(proj, slopes, lam4, subg_col, b, s, tq=256):
    nq = s // tq
    nh = DA_HEADS_PER_STEP
    groups = DA_HEADS // nh
    width = nh * LANES
    pairs = [(qi, kj) for qi in range(nq) for kj in range(qi + 1)]
    qi_tab = jnp.asarray([p[0] for p in pairs], jnp.int32)
    kj_tab = jnp.asarray([p[1] for p in pairs], jnp.int32)
    return pl.pallas_call(
        functools.partial(_da_kernel, tq=tq, seq=s),
        grid_spec=pltpu.PrefetchScalarGridSpec(
            num_scalar_prefetch=2,
            grid=(b, groups),
            in_specs=[
                pl.BlockSpec(memory_space=pltpu.SMEM),
                pl.BlockSpec((4, DA_QK_DIM), lambda bi, g, *_: (0, 0)),
                pl.BlockSpec((DA_V_DIM, 1), lambda bi, g, *_: (0, 0)),
                pl.BlockSpec((s, width), lambda bi, g, *_: (bi, g)),
                pl.BlockSpec((s, width), lambda bi, g, *_: (bi, groups + g)),
                pl.BlockSpec((s, width), lambda bi, g, *_: (bi, 2 * groups + g)),
            ],
            out_specs=pl.BlockSpec((s, width), lambda bi, g, *_: (bi, g)),
            scratch_shapes=[
                pltpu.VMEM((nh, DA_V_DIM, s), BF16),
                pltpu.VMEM((2 * nh, s, LANES), BF16),
                pltpu.VMEM((nh, 2, tq, tq), F32),
                pltpu.VMEM((2 * nh, tq), F32),
                pltpu.VMEM((2 * nh, tq), F32),
                pltpu.VMEM((2 * nh, tq), F32),
                pltpu.VMEM((2 * nh, DA_V_DIM, tq), F32),
                pltpu.VMEM((2 * nh, tq, tq), F32),
                pltpu.VMEM((2 * nh, tq, tq), BF16),
                pltpu.VMEM((2 * nh, tq), F32),
            ],
        ),
        out_shape=jax.ShapeDtypeStruct((b * s, DA_WIDTH), BF16),
        compiler_params=pltpu.CompilerParams(
            dimension_semantics=("parallel", "arbitrary"), vmem_limit_bytes=VMEM_LIMIT),
        name="diffattn",
    )(qi_tab, kj_tab, slopes, lam4, subg_col, proj, proj, proj)


def _rwkv_kernel(r_ref, k_ref, v_ref, p1_ref, p2_ref, mu_ref, w0_ref, a0_ref, w2_ref, kk_ref, ka_ref,
                 rk_ref, lnw_ref, lnb_ref, o_ref, state_ref, crkv_ref, cp2_ref):
    L, TT = CHUNK, RW_TILE
    c = pl.program_id(1)

    @pl.when(c == 0)
    def _():
        state_ref[...] = jnp.zeros_like(state_ref)
        crkv_ref[...] = jnp.zeros_like(crkv_ref)
        cp2_ref[...] = jnp.zeros_like(cp2_ref)

    first = lax.broadcasted_iota(jnp.int32, (TT, 1), 0) == 0

    def shift(x, carry):
        return jnp.where(first, carry, pltpu.roll(x, 1, 0))

    r_raw = r_ref[...].astype(F32)
    k_raw = k_ref[...].astype(F32)
    v_raw = v_ref[...].astype(F32)
    p2 = p2_ref[...].astype(F32)
    rs = shift(r_raw, crkv_ref[:, 0:RW_WIDTH])
    ks = shift(k_raw, crkv_ref[:, RW_WIDTH:2 * RW_WIDTH])
    vs = shift(v_raw, crkv_ref[:, 2 * RW_WIDTH:3 * RW_WIDTH])
    lin = p1_ref[...].astype(F32) + shift(p2, cp2_ref[...])
    crkv_ref[:, 0:RW_WIDTH] = r_raw[TT - 1:TT, :]
    crkv_ref[:, RW_WIDTH:2 * RW_WIDTH] = k_raw[TT - 1:TT, :]
    crkv_ref[:, 2 * RW_WIDTH:3 * RW_WIDTH] = v_raw[TT - 1:TT, :]
    cp2_ref[...] = p2[TT - 1:TT, :]

    r = r_raw + (rs - r_raw) * mu_ref[0:1, :]
    k = k_raw + (ks - k_raw) * mu_ref[1:2, :]
    v = v_raw + (vs - v_raw) * mu_ref[2:3, :]

    ll = lax.broadcasted_iota(jnp.int32, (TT, LORA_PAD), 1)
    z = jnp.where(ll < LORA_W, jnp.tanh(lin),
                  jnp.where(ll < LORA_W + LORA_A, lin,
                            jnp.where(ll < LORA_W + LORA_A + LORA_G, _sigmoid(lin), 0.0)))
    d = _dot(z.astype(BF16), w2_ref[...])
    logw = -_sigmoid(w0_ref[...] + d[:, 0:RW_WIDTH]) * math.exp(-0.5)
    a_sig = _sigmoid(a0_ref[...] + d[:, RW_WIDTH:2 * RW_WIDTH])
    g = d[:, 2 * RW_WIDTH:3 * RW_WIDTH]

    li = lax.broadcasted_iota(jnp.int32, (LANES, LANES), 0)
    lj = lax.broadcasted_iota(jnp.int32, (LANES, LANES), 1)
    same_head = jnp.where((li // RW_N) == (lj // RW_N), 1.0, 0.0).astype(BF16)

    def head_sum(x):
        xb = x.astype(BF16)
        return jnp.concatenate(
            [_dot(xb[:, p * LANES:(p + 1) * LANES], same_head) for p in range(RW_PAIRS)], axis=1)

    kk = k * kk_ref[...]
    kk = kk * lax.rsqrt(jnp.maximum(head_sum(kk * kk), 1e-24))
    k = k * (1.0 + (a_sig - 1.0) * ka_ref[...])
    bonus = head_sum(r * k * rk_ref[...])

    ti = lax.broadcasted_iota(jnp.int32, (TT, TT), 0)
    tj = lax.broadcasted_iota(jnp.int32, (TT, TT), 1)
    tri = jnp.where(ti >= tj, jnp.where((ti // L) == (tj // L), 1.0, 0.0), 0.0).astype(BF16)
    w_hi = logw.astype(BF16)
    w_r1 = logw - w_hi.astype(F32)
    w_mid = w_r1.astype(BF16)
    w_lo = (w_r1 - w_mid.astype(F32)).astype(BF16)
    cw = _dot(tri, w_hi) + _dot(tri, w_mid) + _dot(tri, w_lo)
    cw_last = [cw[(ci + 1) * L - 1:(ci + 1) * L, :] for ci in range(NCH)]
    e_in = jnp.exp(cw)
    e_out = jnp.exp(-cw)
    e_end = jnp.exp(jnp.concatenate([jnp.broadcast_to(cl, (L, RW_WIDTH)) for cl in cw_last], axis=0) - cw)
    w_end = [jnp.exp(cl) for cl in cw_last]

    at = -kk * jnp.exp(cw - logw)
    rt = r * e_in
    b = kk * a_sig
    bt = b * e_out
    kt = k * e_out
    bd = b * e_end
    kd = k * e_end

    lane = lax.broadcasted_iota(jnp.int32, (L, LANES), 1)
    lo = lane < RW_N

    def stack(x, ci, p):
        xs = x[ci * L:(ci + 1) * L, p * LANES:(p + 1) * LANES]
        return jnp.concatenate([jnp.where(lo, xs, 0.0), jnp.where(lo, 0.0, xs)], axis=0).astype(BF16)

    si = lax.broadcasted_iota(jnp.int32, (2 * L, 2 * L), 0)
    sj = lax.broadcasted_iota(jnp.int32, (2 * L, 2 * L), 1)
    strict = (si % L) > (sj % L)
    incl = (si % L) >= (sj % L)
    eye = jnp.where(si == sj, 1.0, 0.0)

    pairs = range(RW_PAIRS)
    cat0 = lambda *xs: jnp.concatenate(xs, axis=0)
    cat1 = lambda *xs: jnp.concatenate(xs, axis=1)
    bf = lambda x: x.astype(BF16)
    def chunk_setup(ci):
        a_s = [stack(at, ci, p) for p in pairs]
        r_s = [stack(rt, ci, p) for p in pairs]
        v_s = [stack(v, ci, p) for p in pairs]
        bk = [cat0(stack(bt, ci, p), stack(kt, ci, p)) for p in pairs]
        bkd = [cat0(stack(bd, ci, p), stack(kd, ci, p)) for p in pairs]
        aa = [_dot_nt(a_s[p], bk[p]) for p in pairs]
        rr = [_dot_nt(r_s[p], bk[p]) for p in pairs]
        a_ab = [jnp.where(strict, aa[p][:, 0:2 * L], 0.0) for p in pairs]
        a_ak = [bf(jnp.where(strict, aa[p][:, 2 * L:4 * L], 0.0)) for p in pairs]
        a_rb = [bf(jnp.where(incl, rr[p][:, 0:2 * L], 0.0)) for p in pairs]
        a_rk = [bf(jnp.where(incl, rr[p][:, 2 * L:4 * L], 0.0)) for p in pairs]
        ab = [bf(a) for a in a_ab]
        qpow = [_dot(ab[p], ab[p]) for p in pairs]
        inv = [eye + a_ab[p] for p in pairs]
        for _ in range(int(math.log2(L)) - 2):
            qb = [bf(q) for q in qpow]
            res = [_dot(qb[p], cat1(qb[p], bf(inv[p]))) for p in pairs]
            qpow = [res[p][:, 0:2 * L] for p in pairs]
            inv = [inv[p] + res[p][:, 2 * L:4 * L] for p in pairs]
        inv = [bf(inv[p] + _dot(bf(qpow[p]), bf(inv[p]))) for p in pairs]
        w_col = [jnp.broadcast_to(w_end[ci][:, p * LANES:(p + 1) * LANES], (LANES, LANES)).T for p in pairs]
        return a_s, r_s, v_s, bkd, a_ak, a_rb, a_rk, inv, w_col

    def chunk_advance(setup, h):
        a_s, r_s, v_s, bkd, a_ak, a_rb, a_rk, inv, w_col = setup
        h_b = [bf(hp) for hp in h]
        x = [_dot(cat1(a_s[p], a_ak[p]), cat0(h_b[p], v_s[p])) for p in pairs]
        u_b = [bf(_dot(inv[p], bf(x[p]))) for p in pairs]
        y2 = [_dot(cat1(r_s[p], a_rb[p], a_rk[p]), cat0(h_b[p], u_b[p], v_s[p])) for p in pairs]
        h = [w_col[p] * h[p] + _dot_tn(bkd[p], cat0(u_b[p], v_s[p])) for p in pairs]
        return h, cat1(*[y2[p][0:L] + y2[p][L:2 * L] for p in pairs])

    h = [state_ref[p] for p in pairs]
    ys = []
    setup = chunk_setup(0)
    for ci in range(NCH):
        ahead = chunk_setup(ci + 1) if ci + 1 < NCH else None
        h, y_chunk = chunk_advance(setup, h)
        ys.append(y_chunk)
        setup = ahead
    for p in pairs:
        state_ref[p] = h[p]
    y = cat0(*ys)

    mean = head_sum(y) * (1.0 / RW_N)
    yc = y - mean
    var = head_sum(yc * yc) * (1.0 / RW_N)
    y = yc * lax.rsqrt(var + LN_X_EPS) * lnw_ref[...] + lnb_ref[...]
    y = y + bonus * v
    o_ref[...] = (y * g).astype(o_ref.dtype)


def _rwkv(proj, mu_rkv, w0, a0, w2cat, k_k, k_a, r_k, ln_w, ln_b, b, s):
    tt = RW_TILE
    nc = s // tt
    row = lambda n: pl.BlockSpec((1, n), lambda bi, c: (0, 0))
    rkv_col0 = 3 * DA_WIDTH // RW_WIDTH
    lora_col0 = IN_COLS // LORA_PAD
    return pl.pallas_call(
        _rwkv_kernel,
        grid=(b, nc),
        in_specs=[
            pl.BlockSpec((tt, RW_WIDTH), lambda bi, c: (bi * nc + c, rkv_col0)),
            pl.BlockSpec((tt, RW_WIDTH), lambda bi, c: (bi * nc + c, rkv_col0 + 1)),
            pl.BlockSpec((tt, RW_WIDTH), lambda bi, c: (bi * nc + c, rkv_col0 + 2)),
            pl.BlockSpec((tt, LORA_PAD), lambda bi, c: (bi * nc + c, lora_col0)),
            pl.BlockSpec((tt, LORA_PAD), lambda bi, c: (bi * nc + c, lora_col0 + 1)),
            pl.BlockSpec((3, RW_WIDTH), lambda bi, c: (0, 0)),
            row(RW_WIDTH), row(RW_WIDTH),
            pl.BlockSpec((LORA_PAD, 3 * RW_WIDTH), lambda bi, c: (0, 0)),
            row(RW_WIDTH), row(RW_WIDTH), row(RW_WIDTH), row(RW_WIDTH), row(RW_WIDTH),
        ],
        out_specs=pl.BlockSpec((tt, RW_WIDTH), lambda bi, c: (bi * nc + c, 0)),
        out_shape=jax.ShapeDtypeStruct((b * s, RW_WIDTH), BF16),
        scratch_shapes=[
            pltpu.VMEM((RW_PAIRS, LANES, LANES), F32),
            pltpu.VMEM((1, 3 * RW_WIDTH), F32),
            pltpu.VMEM((1, LORA_PAD), F32),
        ],
        compiler_params=pltpu.CompilerParams(
            dimension_semantics=("parallel", "arbitrary"), vmem_limit_bytes=VMEM_LIMIT),
        name="rwkv",
    )(proj, proj, proj, proj, proj, mu_rkv, w0, a0, w2cat, k_k, k_a, r_k, ln_w, ln_b)


ROUTE_E, ROUTE_G, ROUTE_R = 0, 2, 4
EXPERT_LANE0 = N_GROUPS


def _outproj_kernel(x_ref, oda_ref, orw_ref, wout_ref, g_ref, wrh_ref, wrl_ref, br_ref,
                    h_ref, xn_ref, route_ref, cnt_ref, base_ref, *, tm):
    @pl.when(pl.program_id(0) == 0)
    def _():
        base_ref[...] = jnp.zeros_like(base_ref)

    mix = _dot(oda_ref[...], wout_ref[0:DA_WIDTH, :]) + _dot(orw_ref[...], wout_ref[DA_WIDTH:, :])
    h = x_ref[...] + mix
    h_ref[...] = h
    ms = jnp.mean(h * h, axis=-1, keepdims=True)
    xn = h * lax.rsqrt(ms + NORM_EPS) * g_ref[...]
    x_hi = xn.astype(BF16)
    _store_row_tiles(xn_ref, _pack_halves(x_hi))

    x_lo = (xn - x_hi.astype(F32)).astype(BF16)
    hi_both = _dot(x_hi, jnp.concatenate([wrh_ref[...], wrl_ref[...]], axis=1))
    lg = (hi_both[:, 0:LANES] + _dot(x_lo, wrh_ref[...]) + hi_both[:, LANES:]) + br_ref[...]

    lane = lax.broadcasted_iota(jnp.int32, (tm, LANES), 1)
    big = jnp.int32(1 << 20)
    gl = jnp.where(lane < N_GROUPS, lg, NEG)
    gmax = jnp.max(gl, axis=-1, keepdims=True)
    grp = jnp.min(jnp.where(gl == gmax, lane, big), axis=-1, keepdims=True)
    p_grp = 1.0 / jnp.sum(jnp.exp(gl - gmax), axis=-1, keepdims=True)
    eg = jnp.where(lane >= EXPERT_LANE0, (lane - EXPERT_LANE0) // EXPERTS_PER_GROUP, -1)
    el = jnp.where(eg == grp, lg, NEG)
    emax = jnp.max(el, axis=-1, keepdims=True)
    pe = jnp.exp(el - emax)
    probs = pe / jnp.sum(pe, axis=-1, keepdims=True)
    probs = jnp.where(eg == grp, probs, -1.0)
    p1 = jnp.max(probs, axis=-1, keepdims=True)
    i1 = jnp.min(jnp.where(probs == p1, lane, big), axis=-1, keepdims=True)
    probs2 = jnp.where(lane == i1, -1.0, probs)
    p2 = jnp.max(probs2, axis=-1, keepdims=True)
    i2 = jnp.min(jnp.where(probs2 == p2, lane, big), axis=-1, keepdims=True)
    gate1 = p_grp * p1 / (p1 + p2)
    gate2 = p_grp * p2 / (p1 + p2)

    oh1 = jnp.where(lane == i1, 1.0, 0.0)
    oh2 = jnp.where(lane == i2, 1.0, 0.0)
    ri = lax.broadcasted_iota(jnp.int32, (tm, tm), 0)
    rj = lax.broadcasted_iota(jnp.int32, (tm, tm), 1)
    lower = jnp.where(ri > rj, 1.0, 0.0).astype(BF16)
    base = base_ref[...]
    tot1 = jnp.sum(oh1, axis=0, keepdims=True)
    before = _dot(lower, jnp.concatenate([oh1, oh2], axis=1).astype(BF16))
    c1 = base + before[:, 0:LANES]
    c2 = base + tot1 + before[:, LANES:]
    rank1 = jnp.sum(oh1 * c1, axis=-1, keepdims=True)
    rank2 = jnp.sum(oh2 * c2, axis=-1, keepdims=True)
    base = base + tot1 + jnp.sum(oh2, axis=0, keepdims=True)
    base_ref[...] = base
    cnt_ref[...] = base

    e1 = (i1 - EXPERT_LANE0).astype(F32)
    e2 = (i2 - EXPERT_LANE0).astype(F32)
    rec = jnp.zeros((tm, LANES), F32)
    for ln, val in ((ROUTE_E, e1), (ROUTE_E + 1, e2), (ROUTE_G, gate1), (ROUTE_G + 1, gate2),
                    (ROUTE_R, rank1), (ROUTE_R + 1, rank2)):
        rec = jnp.where(lane == ln, val, rec)
    route_ref[...] = rec


def _outproj(x2, o_da, o_rw, w_out, g_ffn, wr_hi, wr_lo, b_r, tm=512):
    t = x2.shape[0]
    const = lambda shape: pl.BlockSpec(shape, lambda i: (0, 0))
    return pl.pallas_call(
        functools.partial(_outproj_kernel, tm=tm),
        grid=(t // tm,),
        in_specs=[
            pl.BlockSpec((tm, D_MODEL), lambda i: (i, 0)),
            pl.BlockSpec((tm, DA_WIDTH), lambda i: (i, 0)),
            pl.BlockSpec((tm, RW_WIDTH), lambda i: (i, 0)),
            const((D_MODEL, D_MODEL)),
            const((1, D_MODEL)),
            const((D_MODEL, LANES)),
            const((D_MODEL, LANES)),
            const((1, LANES)),
        ],
        out_specs=[
            pl.BlockSpec((tm, D_MODEL), lambda i: (i, 0)),
            pl.BlockSpec((tm * SUBLANES, LANES), lambda i: (i, 0)),
            pl.BlockSpec((tm, LANES), lambda i: (i, 0)),
            const((1, LANES)),
        ],
        out_shape=[
            jax.ShapeDtypeStruct((t, D_MODEL), F32),
            jax.ShapeDtypeStruct((t * SUBLANES, LANES), jnp.uint32),
            jax.ShapeDtypeStruct((t, LANES), F32),
            jax.ShapeDtypeStruct((1, LANES), F32),
        ],
        scratch_shapes=[pltpu.VMEM((1, LANES), F32)],
        compiler_params=pltpu.CompilerParams(
            dimension_semantics=("arbitrary",), vmem_limit_bytes=VMEM_LIMIT),
        name="outproj_router",
    )(x2, o_da, o_rw, w_out, g_ffn, wr_hi, wr_lo, b_r)


def _row_copy(src_ref, dst_ref, src_row8, dst_row8, sem):
    src = src_ref.at[pl.ds(pl.multiple_of(src_row8, SUBLANES), SUBLANES)]
    dst = dst_ref.at[pl.ds(pl.multiple_of(dst_row8, SUBLANES), SUBLANES)]
    return pltpu.make_async_copy(src, dst, sem)


def _dispatch_kernel(dest_ref, x_ref, xb_in_hbm, xb_hbm, sem, *, td):
    del xb_in_hbm
    t0 = pl.program_id(0) * td

    def start(i, _):
        tok = t0 + i
        _row_copy(x_ref, xb_hbm, i * SUBLANES, dest_ref[2 * tok], sem).start(priority=0)
        _row_copy(x_ref, xb_hbm, i * SUBLANES, dest_ref[2 * tok + 1], sem).start(priority=1)
        return 0

    lax.fori_loop(0, td, start, 0, unroll=8)
    for _ in range(2):
        pltpu.make_async_copy(x_ref, xb_hbm.at[pl.ds(0, td * SUBLANES)], sem).wait()


def _dispatch(dest_flat, xn2, xb_zero, td=512):
    t = xn2.shape[0] // SUBLANES
    return pl.pallas_call(
        functools.partial(_dispatch_kernel, td=td),
        grid_spec=pltpu.PrefetchScalarGridSpec(
            num_scalar_prefetch=1,
            grid=(t // td,),
            in_specs=[pl.BlockSpec((td * SUBLANES, LANES), lambda i, d: (i, 0)),
                      pl.BlockSpec(memory_space=pl.ANY)],
            out_specs=pl.BlockSpec(memory_space=pl.ANY),
            scratch_shapes=[pltpu.SemaphoreType.DMA],
        ),
        out_shape=jax.ShapeDtypeStruct(xb_zero.shape, xb_zero.dtype),
        input_output_aliases={2: 0},
        compiler_params=pltpu.CompilerParams(dimension_semantics=("arbitrary",)),
        name="dispatch",
    )(dest_flat, xn2, xb_zero)


def _experts_kernel(be_ref, nu_ref, nxt_ref, slot_ref, xb_ref, wg_hbm, wu_hbm, wd_hbm, yb_ref,
                    wgf_ref, wuf_ref, wdf_ref, wgb_ref, wub_ref, wdb_ref, sem):
    i = pl.program_id(0)
    used = i < nu_ref[0]
    new_expert = jnp.logical_or(i == 0, be_ref[i] != be_ref[jnp.maximum(i - 1, 0)])

    def weight_copies(e, slot):
        return [pltpu.make_async_copy(src.at[e], dst.at[slot], sem.at[slot])
                for src, dst in ((wg_hbm, wgf_ref), (wu_hbm, wuf_ref), (wd_hbm, wdf_ref))]

    def swiglu(wg_lo, wg_hi, wu_lo, wu_hi, wd):
        x_lo, x_hi = (half.astype(BF16) for half in _unpack_halves(_load_row_tiles(xb_ref)))
        hg = _dot(x_lo, wg_lo) + _dot(x_hi, wg_hi)
        hu = _dot(x_lo, wu_lo) + _dot(x_hi, wu_hi)
        hid = hg * _sigmoid(hg) * hu
        _store_row_tiles(yb_ref, _pack_halves(_dot(hid.astype(BF16), wd).astype(BF16)))

    @pl.when(jnp.logical_and(used, new_expert))
    def _():
        slot = slot_ref[i]

        @pl.when(i == 0)
        def _():
            for cp in weight_copies(be_ref[i], slot):
                cp.start()

        for cp in weight_copies(be_ref[i], slot):
            cp.wait()

        @pl.when(nxt_ref[i] >= 0)
        def _():
            for cp in weight_copies(nxt_ref[i], 1 - slot):
                cp.start()

        wg, wu, wd = (w[slot].astype(BF16) for w in (wgf_ref, wuf_ref, wdf_ref))
        wgb_ref[...] = wg
        wub_ref[...] = wu
        wdb_ref[...] = wd
        swiglu(wg[0:HALF, :], wg[HALF:, :], wu[0:HALF, :], wu[HALF:, :], wd)

    @pl.when(jnp.logical_and(used, jnp.logical_not(new_expert)))
    def _():
        swiglu(wgb_ref[0:HALF, :], wgb_ref[HALF:, :], wub_ref[0:HALF, :], wub_ref[HALF:, :], wdb_ref[...])

    @pl.when(jnp.logical_not(used))
    def _():
        yb_ref[...] = jnp.zeros_like(yb_ref)


def _experts(block_e, n_used, next_e, slot, xb, w_gate, w_up, w_down):
    block_rows = EXPERT_BLOCK * SUBLANES
    nb = xb.shape[0] // block_rows
    return pl.pallas_call(
        _experts_kernel,
        grid_spec=pltpu.PrefetchScalarGridSpec(
            num_scalar_prefetch=4,
            grid=(nb,),
            in_specs=[
                pl.BlockSpec((block_rows, LANES), lambda i, be, nu, *_: (jnp.minimum(i, nu[0] - 1), 0)),
                pl.BlockSpec(memory_space=pl.ANY),
                pl.BlockSpec(memory_space=pl.ANY),
                pl.BlockSpec(memory_space=pl.ANY),
            ],
            out_specs=pl.BlockSpec((block_rows, LANES), lambda i, *_: (i, 0)),
            scratch_shapes=[
                pltpu.VMEM((2, D_MODEL, D_EXPERT), F32),
                pltpu.VMEM((2, D_MODEL, D_EXPERT), F32),
                pltpu.VMEM((2, D_EXPERT, D_MODEL), F32),
                pltpu.VMEM((D_MODEL, D_EXPERT), BF16),
                pltpu.VMEM((D_MODEL, D_EXPERT), BF16),
                pltpu.VMEM((D_EXPERT, D_MODEL), BF16),
                pltpu.SemaphoreType.DMA((2,)),
            ],
        ),
        out_shape=jax.ShapeDtypeStruct(xb.shape, jnp.uint32),
        compiler_params=pltpu.CompilerParams(
            dimension_semantics=("arbitrary",), vmem_limit_bytes=EXPERTS_VMEM_LIMIT),
        name="experts",
    )(block_e, n_used, next_e, slot, xb, w_gate, w_up, w_down)


def _combine_kernel(dest_ref, route_ref, h_ref, g_ref, yb_hbm, o_ref, y1_ref, y2_ref, sem, *, tc):
    step = pl.program_id(0)
    n_steps = pl.num_programs(0)

    def gather(tile, slot):
        def start(i, _):
            tok = tile * tc + i
            _row_copy(yb_hbm, y1_ref.at[slot], dest_ref[2 * tok], i * SUBLANES, sem.at[slot]).start(priority=0)
            _row_copy(yb_hbm, y2_ref.at[slot], dest_ref[2 * tok + 1], i * SUBLANES, sem.at[slot]).start(priority=1)
            return 0

        lax.fori_loop(0, tc, start, 0, unroll=8)

    @pl.when(step == 0)
    def _():
        gather(0, 0)

    @pl.when(step + 1 < n_steps)
    def _():
        gather(step + 1, (step + 1) % 2)

    slot = step % 2
    pltpu.make_async_copy(yb_hbm.at[pl.ds(0, tc * SUBLANES)], y1_ref.at[slot], sem.at[slot]).wait()
    pltpu.make_async_copy(yb_hbm.at[pl.ds(0, tc * SUBLANES)], y2_ref.at[slot], sem.at[slot]).wait()

    rec = route_ref[...]
    lane = lax.broadcasted_iota(jnp.int32, (tc, LANES), 1)
    g1 = jnp.sum(jnp.where(lane == ROUTE_G, rec, 0.0), axis=-1, keepdims=True)
    g2 = jnp.sum(jnp.where(lane == ROUTE_G + 1, rec, 0.0), axis=-1, keepdims=True)
    y1_lo, y1_hi = _unpack_halves(_load_row_tiles(y1_ref.at[slot]))
    y2_lo, y2_hi = _unpack_halves(_load_row_tiles(y2_ref.at[slot]))
    h_lo = h_ref[:, 0:HALF] + (y1_lo * g1 + y2_lo * g2)
    h_hi = h_ref[:, HALF:] + (y1_hi * g1 + y2_hi * g2)
    ms = (jnp.sum(h_lo * h_lo, axis=-1, keepdims=True)
          + jnp.sum(h_hi * h_hi, axis=-1, keepdims=True)) * (1.0 / D_MODEL)
    scale = lax.rsqrt(ms + NORM_EPS)
    o_ref[:, 0:HALF] = h_lo * scale * g_ref[:, 0:HALF]
    o_ref[:, HALF:] = h_hi * scale * g_ref[:, HALF:]


def _combine(dest_flat, route, h1, g_final, yb, tc=256):
    t = h1.shape[0]
    return pl.pallas_call(
        functools.partial(_combine_kernel, tc=tc),
        grid_spec=pltpu.PrefetchScalarGridSpec(
            num_scalar_prefetch=1,
            grid=(t // tc,),
            in_specs=[
                pl.BlockSpec((tc, LANES), lambda i, d: (i, 0)),
                pl.BlockSpec((tc, D_MODEL), lambda i, d: (i, 0)),
                pl.BlockSpec((1, D_MODEL), lambda i, d: (0, 0)),
                pl.BlockSpec(memory_space=pl.ANY),
            ],
            out_specs=pl.BlockSpec((tc, D_MODEL), lambda i, d: (i, 0)),
            scratch_shapes=[
                pltpu.VMEM((2, tc * SUBLANES, LANES), jnp.uint32),
                pltpu.VMEM((2, tc * SUBLANES, LANES), jnp.uint32),
                pltpu.SemaphoreType.DMA((2,)),
            ],
        ),
        out_shape=jax.ShapeDtypeStruct((t, D_MODEL), F32),
        compiler_params=pltpu.CompilerParams(
            dimension_semantics=("arbitrary",), vmem_limit_bytes=VMEM_LIMIT),
        name="combine",
    )(dest_flat, route, h1, g_final, yb)


def _pad_cols(w, n):
    return jnp.pad(w, ((0, 0), (0, n - w.shape[1])))


def kernel(x, g_mix, w_in, w_out, da_lambda_q1, da_lambda_k1, da_lambda_q2, da_lambda_k2, da_subln_g,
           rw_mu_x, rw_mu_rkv, rw_w0, rw_w1, rw_w2, rw_a0, rw_a1, rw_a2, rw_g1, rw_g2, rw_k_k, rw_k_a,
           rw_r_k, rw_ln_w, rw_ln_b, g_ffn, moe_w_group, moe_b_group, moe_w_expert, moe_b_expert,
           moe_w_gate, moe_w_up, moe_w_down, g_final):
    b, s, d = x.shape
    t = b * s
    x2 = x.reshape(t, d)

    mu = rw_mu_x[0]
    lora = (rw_w1[0], rw_a1[0], rw_g1[0])
    keep = _pad_cols(jnp.concatenate([(1.0 - mu[i])[:, None] * w for i, w in enumerate(lora)], axis=1), LORA_PAD)
    prev = _pad_cols(jnp.concatenate([mu[i][:, None] * w for i, w in enumerate(lora)], axis=1), LORA_PAD)
    w_all = jnp.concatenate([w_in[0], keep, prev], axis=1).astype(BF16)
    proj = _inproj(x2, g_mix, w_all)

    slopes = jnp.asarray([2.0 ** (-8.0 * (i + 1) / DA_HEADS) for i in range(DA_HEADS)], F32)
    lam4 = jnp.concatenate([da_lambda_q1, da_lambda_k1, da_lambda_q2, da_lambda_k2], axis=0)
    o_da = _diffattn(proj, slopes, lam4, da_subln_g.reshape(DA_V_DIM, 1), b, s)

    w2cat = jnp.zeros((LORA_PAD, 3 * RW_WIDTH), F32)
    w2cat = w2cat.at[0:LORA_W, 0:RW_WIDTH].set(rw_w2[0])
    w2cat = w2cat.at[LORA_W:LORA_W + LORA_A, RW_WIDTH:2 * RW_WIDTH].set(rw_a2[0])
    w2cat = w2cat.at[LORA_W + LORA_A:LORA_W + LORA_A + LORA_G, 2 * RW_WIDTH:].set(rw_g2[0])
    o_rw = _rwkv(proj, rw_mu_rkv[0], rw_w0, rw_a0, w2cat.astype(BF16), rw_k_k, rw_k_a,
                 rw_r_k.reshape(1, RW_WIDTH), rw_ln_w, rw_ln_b, b, s)

    w_r = _pad_cols(jnp.concatenate([moe_w_group[0], moe_w_expert[0]], axis=1), LANES)
    wr_hi = w_r.astype(BF16)
    wr_lo = (w_r - wr_hi.astype(F32)).astype(BF16)
    b_r = _pad_cols(jnp.concatenate([moe_b_group[0], moe_b_expert[0].reshape(-1)])[None, :], LANES)
    h1, xn2, route, cnt = _outproj(x2, o_da, o_rw, w_out[0].astype(BF16), g_ffn, wr_hi, wr_lo, b_r)

    counts = cnt[0, EXPERT_LANE0:EXPERT_LANE0 + N_EXPERTS].astype(jnp.int32)
    pcounts = ((counts + EXPERT_BLOCK - 1) // EXPERT_BLOCK) * EXPERT_BLOCK
    expert_ids = jnp.arange(N_EXPERTS, dtype=jnp.int32)
    upto = expert_ids[None, :] <= expert_ids[:, None]
    pends = jnp.sum(jnp.where(upto, pcounts[None, :], 0), axis=1)
    pstarts = pends - pcounts
    e_idx = route[:, ROUTE_E:ROUTE_E + 2].astype(jnp.int32)
    rank = route[:, ROUTE_R:ROUTE_R + 2].astype(jnp.int32)
    row0 = jnp.sum(jnp.where(e_idx[..., None] == expert_ids, pstarts, 0), axis=-1)
    dest = ((row0 + rank) * SUBLANES).reshape(-1)
    n_blocks = (2 * t) // EXPERT_BLOCK + N_EXPERTS
    block_row0 = jnp.arange(n_blocks, dtype=jnp.int32) * EXPERT_BLOCK
    block_e = jnp.minimum(jnp.sum(pends[None, :] <= block_row0[:, None], axis=1), N_EXPERTS - 1).astype(jnp.int32)
    n_used = (pends[-1:] // EXPERT_BLOCK).astype(jnp.int32)
    has_rows = counts > 0
    later = (expert_ids[None, :] > expert_ids[:, None]) & has_rows[None, :]
    next_used = jnp.min(jnp.where(later, expert_ids[None, :], N_EXPERTS), axis=1)
    next_used = jnp.where(next_used == N_EXPERTS, -1, next_used).astype(jnp.int32)
    buffer_id = ((jnp.sum(jnp.where(upto & has_rows[None, :], 1, 0), axis=1) - 1) % 2).astype(jnp.int32)

    xb = _dispatch(dest, xn2, jnp.zeros((n_blocks * EXPERT_BLOCK * SUBLANES, LANES), jnp.uint32))
    yb = _experts(block_e, n_used, next_used[block_e], buffer_id[block_e], xb,
                  moe_w_gate[0], moe_w_up[0], moe_w_down[0])
    out = _combine(dest, route, h1, g_final[None, :], yb)
    return out.reshape(b, s, d)
```

```python
import functools
import math

import jax
import jax.numpy as jnp
from jax import lax
from jax.experimental import pallas as pl
from jax.experimental.pallas import tpu as pltpu

F32 = jnp.float32
BF16 = jnp.bfloat16

D_MODEL = 2048
DA_WIDTH = 1024
RW_WIDTH = 1024
DA_HEADS = 8
DA_V_DIM = 128
DA_QK_DIM = 64
RW_N = 64
RW_PAIRS = RW_WIDTH // 128
LORA_W, LORA_A, LORA_G = 64, 64, 160
LORA_PAD = 384
IN_COLS = 3 * DA_WIDTH + 3 * RW_WIDTH
NC = IN_COLS + 2 * LORA_PAD
N_GROUPS = 4
EXPERTS_PER_GROUP = 8
N_EXPERTS = 32
D_EXPERT = 512
NORM_EPS = 1e-6
LN_X_EPS = 64e-5
LAM_INIT = 0.8 - 0.6 * math.exp(-0.3 * 0)
LANES = 128
CHUNK = 64
RW_TILE = 256
NCH = RW_TILE // CHUNK
EXPERT_BLOCK = 256
NEG = -1e30
VMEM_LIMIT = 48 * 1024 * 1024
EXPERTS_VMEM_LIMIT = 56 * 1024 * 1024


def _dot(a, b):
    return jnp.dot(a, b, preferred_element_type=F32)


def _dot_nt(a, b):
    return lax.dot_general(a, b, (((1,), (1,)), ((), ())), preferred_element_type=F32)


def _dot_tn(a, b):
    return lax.dot_general(a, b, (((0,), (0,)), ((), ())), preferred_element_type=F32)


def _sigmoid(x):
    return 1.0 / (1.0 + jnp.exp(-x))


HALF = D_MODEL // 2


def _pack_halves(xb):
    lo = lax.bitcast_convert_type(xb[:, :HALF].astype(F32), jnp.uint32) >> 16
    hi = lax.bitcast_convert_type(xb[:, HALF:].astype(F32), jnp.uint32) & jnp.uint32(0xFFFF0000)
    return hi | lo


def _unpack_halves(u):
    lo = lax.bitcast_convert_type(u << 16, F32)
    hi = lax.bitcast_convert_type(u & jnp.uint32(0xFFFF0000), F32)
    return lo, hi


SUBLANES = 8
assert HALF == SUBLANES * LANES


def _store_row_tiles(ref, u):
    n = u.shape[0]
    for k in range(SUBLANES):
        ref[pl.ds(k, n, stride=SUBLANES), :] = u[:, k * LANES:(k + 1) * LANES]


def _load_row_tiles(ref):
    n = ref.shape[0] // SUBLANES
    return jnp.concatenate([ref[pl.ds(k, n, stride=SUBLANES), :] for k in range(SUBLANES)], axis=1)


def _inproj_kernel(x_ref, g_ref, w_ref, o_ref, xn_ref):
    @pl.when(pl.program_id(1) == 0)
    def _():
        x = x_ref[...]
        ms = jnp.mean(x * x, axis=-1, keepdims=True)
        xn_ref[...] = (x * lax.rsqrt(ms + NORM_EPS) * g_ref[...]).astype(BF16)

    o_ref[...] = _dot(xn_ref[...], w_ref[...]).astype(o_ref.dtype)


def _inproj(x2, g, w_all, tm=1024, tn=2304):
    t = x2.shape[0]
    tm = min(tm, t)
    vmem_limit = 2 * (tm * D_MODEL * 4 + D_MODEL * tn * 2 + tm * tn * 2) + tm * D_MODEL * 2 + (6 << 20)
    return pl.pallas_call(
        _inproj_kernel,
        grid=(t // tm, NC // tn),
        in_specs=[
            pl.BlockSpec((tm, D_MODEL), lambda i, j: (i, 0)),
            pl.BlockSpec((1, D_MODEL), lambda i, j: (0, 0)),
            pl.BlockSpec((D_MODEL, tn), lambda i, j: (0, j)),
        ],
        out_specs=pl.BlockSpec((tm, tn), lambda i, j: (i, j)),
        out_shape=jax.ShapeDtypeStruct((t, NC), BF16),
        scratch_shapes=[pltpu.VMEM((tm, D_MODEL), BF16)],
        compiler_params=pltpu.CompilerParams(
            dimension_semantics=("parallel", "arbitrary"), vmem_limit_bytes=vmem_limit),
        name="inproj",
    )(x2, g, w_all)


DA_HEADS_PER_STEP = 2


def _da_kernel(qi_tab, kj_tab, slopes_ref, lam_ref, subg_ref, q_ref, k_ref, v_ref, o_ref,
               vt_ref, qz_ref, bias_ref, m_ref, l_ref, lfin_ref, acc_ref, s_ref, p_ref, alpha_ref,
               *, tq, seq):
    tk = tq
    nq = seq // tq
    n_trips = nq * (nq + 1) // 2
    heads = range(DA_HEADS_PER_STEP)
    chains = [(hd, c) for hd in heads for c in range(2)]
    slope = [slopes_ref[pl.program_id(1) * DA_HEADS_PER_STEP + hd] for hd in heads]
    cols = lambda hd: slice(hd * LANES, (hd + 1) * LANES)

    lane = lax.broadcasted_iota(jnp.int32, (tq, LANES), 1)
    krow = lax.broadcasted_iota(jnp.int32, (tk, tq), 0)
    qcol = lax.broadcasted_iota(jnp.int32, (tk, tq), 1)
    for hd in heads:
        for blk in range(seq // LANES):
            r = slice(blk * LANES, (blk + 1) * LANES)
            vt_ref[hd, :, r] = v_ref[r, cols(hd)].astype(F32).T.astype(BF16)
        for blk in range(nq):
            r = slice(blk * tq, (blk + 1) * tq)
            q = q_ref[r, cols(hd)] * jnp.asarray(DA_QK_DIM ** -0.5, BF16)
            qz_ref[2 * hd, r, :] = jnp.where(lane < DA_QK_DIM, q, jnp.zeros_like(q))
            qz_ref[2 * hd + 1, r, :] = jnp.where(lane >= DA_QK_DIM, q, jnp.zeros_like(q))
        alibi = slope[hd] * krow.astype(F32)
        bias_ref[hd, 0] = alibi
        bias_ref[hd, 1] = jnp.where(krow <= qcol, alibi, NEG)

    m_ref[...] = jnp.full_like(m_ref, NEG)
    l_ref[...] = jnp.zeros_like(l_ref)
    lfin_ref[...] = jnp.ones_like(lfin_ref)
    acc_ref[...] = jnp.zeros_like(acc_ref)
    p_ref[...] = jnp.zeros_like(p_ref)
    alpha_ref[...] = jnp.zeros_like(alpha_ref)

    lv = lam_ref[...]
    lam = (jnp.exp(jnp.sum(lv[0:1] * lv[1:2], axis=-1, keepdims=True))
           - jnp.exp(jnp.sum(lv[2:3] * lv[3:4], axis=-1, keepdims=True)) + LAM_INIT)

    def scores_to_scratch(t):
        qs = pl.multiple_of(qi_tab[t] * tq, tq)
        ks = pl.multiple_of(kj_tab[t] * tk, tk)
        diag = (qi_tab[t] == kj_tab[t]).astype(jnp.int32)
        for hd, c in chains:
            ch = 2 * hd + c
            s_ref[ch] = (_dot_nt(k_ref[pl.ds(ks, tk), cols(hd)], qz_ref[ch, pl.ds(qs, tq), :])
                         + bias_ref[hd, diag])

    def values_from_scratch(t):
        ks = pl.multiple_of(kj_tab[t] * tk, tk)
        for hd, c in chains:
            ch = 2 * hd + c
            acc_ref[ch] = alpha_ref[ch:ch + 1, :] * acc_ref[ch] + _dot(vt_ref[hd, :, pl.ds(ks, tk)], p_ref[ch])

    def softmax(t):
        first = kj_tab[t] == 0
        diag = kj_tab[t] == qi_tab[t]
        rel = ((kj_tab[t] - qi_tab[t]) * tk).astype(F32)
        p, alpha = [], []
        for hd, c in chains:
            ch = 2 * hd + c
            off = slope[hd] * rel
            s = s_ref[ch]
            m = jnp.where(first, NEG, m_ref[ch:ch + 1, :])
            l = jnp.where(first, 0.0, l_ref[ch:ch + 1, :])
            m_new = jnp.maximum(m, jnp.max(s, axis=0, keepdims=True) + off)
            pc = jnp.exp(s - (m_new - off))
            a = jnp.exp(m - m_new)
            l_new = a * l + jnp.sum(pc, axis=0, keepdims=True)
            m_ref[ch:ch + 1, :] = m_new
            l_ref[ch:ch + 1, :] = l_new
            lfin_ref[ch:ch + 1, :] = jnp.where(diag, l_new, lfin_ref[ch:ch + 1, :])
            p.append(pc)
            alpha.append(a)
        return p, alpha

    def publish(p, alpha):
        for ch in range(len(chains)):
            alpha_ref[ch:ch + 1, :] = alpha[ch]
            p_ref[ch] = p[ch].astype(BF16)

    def finalize(t):
        qs = pl.multiple_of(qi_tab[t] * tq, tq)
        for hd in heads:
            c0, c1 = 2 * hd, 2 * hd + 1
            o = acc_ref[c0] / lfin_ref[c0:c0 + 1, :] - lam * (acc_ref[c1] / lfin_ref[c1:c1 + 1, :])
            ms = jnp.mean(o * o, axis=0, keepdims=True)
            o = o * lax.rsqrt(ms + NORM_EPS) * subg_ref[...] * (1.0 - LAM_INIT)
            o_ref[pl.ds(qs, tq), cols(hd)] = o.T.astype(o_ref.dtype)

    scores_to_scratch(0)

    def body(t, _):
        prev = jnp.maximum(t - 1, 0)
        p, alpha = softmax(t)
        values_from_scratch(prev)
        scores_to_scratch(jnp.minimum(t + 1, n_trips - 1))
        publish(p, alpha)

        @pl.when(jnp.logical_and(t > 0, qi_tab[prev] == kj_tab[prev]))
        def _():
            finalize(prev)
        return 0

    lax.fori_loop(0, n_trips, body, 0)
    values_from_scratch(n_trips - 1)
    finalize(n_trips - 1)


def _diffattn(proj, slopes, lam4, subg_col, b, s, tq=256):
    nq = s // tq
    nh = DA_HEADS_PER_STEP
    groups = DA_HEADS // nh
    width = nh * LANES
    pairs = [(qi, kj) for qi in range(nq) for kj in range(qi + 1)]
    qi_tab = jnp.asarray([p[0] for p in pairs], jnp.int32)
    kj_tab = jnp.asarray([p[1] for p in pairs], jnp.int32)
    return pl.pallas_call(
        functools.partial(_da_kernel, tq=tq, seq=s),
        grid_spec=pltpu.PrefetchScalarGridSpec(
            num_scalar_prefetch=2,
            grid=(b, groups),
            in_specs=[
                pl.BlockSpec(memory_space=pltpu.SMEM),
                pl.BlockSpec((4, DA_QK_DIM), lambda bi, g, *_: (0, 0)),
                pl.BlockSpec((DA_V_DIM, 1), lambda bi, g, *_: (0, 0)),
                pl.BlockSpec((s, width), lambda bi, g, *_: (bi, g)),
                pl.BlockSpec((s, width), lambda bi, g, *_: (bi, groups + g)),
                pl.BlockSpec((s, width), lambda bi, g, *_: (bi, 2 * groups + g)),
            ],
            out_specs=pl.BlockSpec((s, width), lambda bi, g, *_: (bi, g)),
            scratch_shapes=[
                pltpu.VMEM((nh, DA_V_DIM, s), BF16),
                pltpu.VMEM((2 * nh, s, LANES), BF16),
                pltpu.VMEM((nh, 2, tq, tq), F32),
                pltpu.VMEM((2 * nh, tq), F32),
                pltpu.VMEM((2 * nh, tq), F32),
                pltpu.VMEM((2 * nh, tq), F32),
                pltpu.VMEM((2 * nh, DA_V_DIM, tq), F32),
                pltpu.VMEM((2 * nh, tq, tq), F32),
                pltpu.VMEM((2 * nh, tq, tq), BF16),
                pltpu.VMEM((2 * nh, tq), F32),
            ],
        ),
        out_shape=jax.ShapeDtypeStruct((b * s, DA_WIDTH), BF16),
        compiler_params=pltpu.CompilerParams(
            dimension_semantics=("parallel", "arbitrary"), vmem_limit_bytes=VMEM_LIMIT),
        name="diffattn",
    )(qi_tab, kj_tab, slopes, lam4, subg_col, proj, proj, proj)


def _rwkv_kernel(r_ref, k_ref, v_ref, p1_ref, p2_ref, mu_ref, w0_ref, a0_ref, w2_ref, kk_ref, ka_ref,
                 rk_ref, lnw_ref, lnb_ref, o_ref, state_ref, crkv_ref, cp2_ref):
    L, TT = CHUNK, RW_TILE
    c = pl.program_id(1)

    @pl.when(c == 0)
    def _():
        state_ref[...] = jnp.zeros_like(state_ref)
        crkv_ref[...] = jnp.zeros_like(crkv_ref)
        cp2_ref[...] = jnp.zeros_like(cp2_ref)

    first = lax.broadcasted_iota(jnp.int32, (TT, 1), 0) == 0

    def shift(x, carry):
        return jnp.where(first, carry, pltpu.roll(x, 1, 0))

    r_raw = r_ref[...].astype(F32)
    k_raw = k_ref[...].astype(F32)
    v_raw = v_ref[...].astype(F32)
    p2 = p2_ref[...].astype(F32)
    rs = shift(r_raw, crkv_ref[:, 0:RW_WIDTH])
    ks = shift(k_raw, crkv_ref[:, RW_WIDTH:2 * RW_WIDTH])
    vs = shift(v_raw, crkv_ref[:, 2 * RW_WIDTH:3 * RW_WIDTH])
    lin = p1_ref[...].astype(F32) + shift(p2, cp2_ref[...])
    crkv_ref[:, 0:RW_WIDTH] = r_raw[TT - 1:TT, :]
    crkv_ref[:, RW_WIDTH:2 * RW_WIDTH] = k_raw[TT - 1:TT, :]
    crkv_ref[:, 2 * RW_WIDTH:3 * RW_WIDTH] = v_raw[TT - 1:TT, :]
    cp2_ref[...] = p2[TT - 1:TT, :]

    r = r_raw + (rs - r_raw) * mu_ref[0:1, :]
    k = k_raw + (ks - k_raw) * mu_ref[1:2, :]
    v = v_raw + (vs - v_raw) * mu_ref[2:3, :]

    ll = lax.broadcasted_iota(jnp.int32, (TT, LORA_PAD), 1)
    z = jnp.where(ll < LORA_W, jnp.tanh(lin),
                  jnp.where(ll < LORA_W + LORA_A, lin,
                            jnp.where(ll < LORA_W + LORA_A + LORA_G, _sigmoid(lin), 0.0)))
    d = _dot(z.astype(BF16), w2_ref[...])
    logw = -_sigmoid(w0_ref[...] + d[:, 0:RW_WIDTH]) * math.exp(-0.5)
    a_sig = _sigmoid(a0_ref[...] + d[:, RW_WIDTH:2 * RW_WIDTH])
    g = d[:, 2 * RW_WIDTH:3 * RW_WIDTH]

    li = lax.broadcasted_iota(jnp.int32, (LANES, LANES), 0)
    lj = lax.broadcasted_iota(jnp.int32, (LANES, LANES), 1)
    same_head = jnp.where((li // RW_N) == (lj // RW_N), 1.0, 0.0).astype(BF16)

    def head_sum(x):
        xb = x.astype(BF16)
        return jnp.concatenate(
            [_dot(xb[:, p * LANES:(p + 1) * LANES], same_head) for p in range(RW_PAIRS)], axis=1)

    kk = k * kk_ref[...]
    kk = kk * lax.rsqrt(jnp.maximum(head_sum(kk * kk), 1e-24))
    k = k * (1.0 + (a_sig - 1.0) * ka_ref[...])
    bonus = head_sum(r * k * rk_ref[...])

    ti = lax.broadcasted_iota(jnp.int32, (TT, TT), 0)
    tj = lax.broadcasted_iota(jnp.int32, (TT, TT), 1)
    tri = jnp.where(ti >= tj, jnp.where((ti // L) == (tj // L), 1.0, 0.0), 0.0).astype(BF16)
    w_hi = logw.astype(BF16)
    w_r1 = logw - w_hi.astype(F32)
    w_mid = w_r1.astype(BF16)
    w_lo = (w_r1 - w_mid.astype(F32)).astype(BF16)
    cw = _dot(tri, w_hi) + _dot(tri, w_mid) + _dot(tri, w_lo)
    cw_last = [cw[(ci + 1) * L - 1:(ci + 1) * L, :] for ci in range(NCH)]
    e_in = jnp.exp(cw)
    e_out = jnp.exp(-cw)
    e_end = jnp.exp(jnp.concatenate([jnp.broadcast_to(cl, (L, RW_WIDTH)) for cl in cw_last], axis=0) - cw)
    w_end = [jnp.exp(cl) for cl in cw_last]

    at = -kk * jnp.exp(cw - logw)
    rt = r * e_in
    b = kk * a_sig
    bt = b * e_out
    kt = k * e_out
    bd = b * e_end
    kd = k * e_end

    lane = lax.broadcasted_iota(jnp.int32, (L, LANES), 1)
    lo = lane < RW_N

    def stack(x, ci, p):
        xs = x[ci * L:(ci + 1) * L, p * LANES:(p + 1) * LANES]
        return jnp.concatenate([jnp.where(lo, xs, 0.0), jnp.where(lo, 0.0, xs)], axis=0).astype(BF16)

    si = lax.broadcasted_iota(jnp.int32, (2 * L, 2 * L), 0)
    sj = lax.broadcasted_iota(jnp.int32, (2 * L, 2 * L), 1)
    strict = (si % L) > (sj % L)
    incl = (si % L) >= (sj % L)
    eye = jnp.where(si == sj, 1.0, 0.0)

    pairs = range(RW_PAIRS)
    cat0 = lambda *xs: jnp.concatenate(xs, axis=0)
    cat1 = lambda *xs: jnp.concatenate(xs, axis=1)
    bf = lambda x: x.astype(BF16)
    def chunk_setup(ci):
        a_s = [stack(at, ci, p) for p in pairs]
        r_s = [stack(rt, ci, p) for p in pairs]
        v_s = [stack(v, ci, p) for p in pairs]
        bk = [cat0(stack(bt, ci, p), stack(kt, ci, p)) for p in pairs]
        bkd = [cat0(stack(bd, ci, p), stack(kd, ci, p)) for p in pairs]
        aa = [_dot_nt(a_s[p], bk[p]) for p in pairs]
        rr = [_dot_nt(r_s[p], bk[p]) for p in pairs]
        a_ab = [jnp.where(strict, aa[p][:, 0:2 * L], 0.0) for p in pairs]
        a_ak = [bf(jnp.where(strict, aa[p][:, 2 * L:4 * L], 0.0)) for p in pairs]
        a_rb = [bf(jnp.where(incl, rr[p][:, 0:2 * L], 0.0)) for p in pairs]
        a_rk = [bf(jnp.where(incl, rr[p][:, 2 * L:4 * L], 0.0)) for p in pairs]
        ab = [bf(a) for a in a_ab]
        qpow = [_dot(ab[p], ab[p]) for p in pairs]
        inv = [eye + a_ab[p] for p in pairs]
        for _ in range(int(math.log2(L)) - 2):
            qb = [bf(q) for q in qpow]
            res = [_dot(qb[p], cat1(qb[p], bf(inv[p]))) for p in pairs]
            qpow = [res[p][:, 0:2 * L] for p in pairs]
            inv = [inv[p] + res[p][:, 2 * L:4 * L] for p in pairs]
        inv = [bf(inv[p] + _dot(bf(qpow[p]), bf(inv[p]))) for p in pairs]
        w_col = [jnp.broadcast_to(w_end[ci][:, p * LANES:(p + 1) * LANES], (LANES, LANES)).T for p in pairs]
        return a_s, r_s, v_s, bkd, a_ak, a_rb, a_rk, inv, w_col

    def chunk_advance(setup, h):
        a_s, r_s, v_s, bkd, a_ak, a_rb, a_rk, inv, w_col = setup
        h_b = [bf(hp) for hp in h]
        x = [_dot(cat1(a_s[p], a_ak[p]), cat0(h_b[p], v_s[p])) for p in pairs]
        u_b = [bf(_dot(inv[p], bf(x[p]))) for p in pairs]
        y2 = [_dot(cat1(r_s[p], a_rb[p], a_rk[p]), cat0(h_b[p], u_b[p], v_s[p])) for p in pairs]
        h = [w_col[p] * h[p] + _dot_tn(bkd[p], cat0(u_b[p], v_s[p])) for p in pairs]
        return h, cat1(*[y2[p][0:L] + y2[p][L:2 * L] for p in pairs])

    h = [state_ref[p] for p in pairs]
    ys = []
    setup = chunk_setup(0)
    for ci in range(NCH):
        ahead = chunk_setup(ci + 1) if ci + 1 < NCH else None
        h, y_chunk = chunk_advance(setup, h)
        ys.append(y_chunk)
        setup = ahead
    for p in pairs:
        state_ref[p] = h[p]
    y = cat0(*ys)

    mean = head_sum(y) * (1.0 / RW_N)
    yc = y - mean
    var = head_sum(yc * yc) * (1.0 / RW_N)
    y = yc * lax.rsqrt(var + LN_X_EPS) * lnw_ref[...] + lnb_ref[...]
    y = y + bonus * v
    o_ref[...] = (y * g).astype(o_ref.dtype)


def _rwkv(proj, mu_rkv, w0, a0, w2cat, k_k, k_a, r_k, ln_w, ln_b, b, s):
    tt = RW_TILE
    nc = s // tt
    row = lambda n: pl.BlockSpec((1, n), lambda bi, c: (0, 0))
    rkv_col0 = 3 * DA_WIDTH // RW_WIDTH
    lora_col0 = IN_COLS // LORA_PAD
    return pl.pallas_call(
        _rwkv_kernel,
        grid=(b, nc),
        in_specs=[
            pl.BlockSpec((tt, RW_WIDTH), lambda bi, c: (bi * nc + c, rkv_col0)),
            pl.BlockSpec((tt, RW_WIDTH), lambda bi, c: (bi * nc + c, rkv_col0 + 1)),
            pl.BlockSpec((tt, RW_WIDTH), lambda bi, c: (bi * nc + c, rkv_col0 + 2)),
            pl.BlockSpec((tt, LORA_PAD), lambda bi, c: (bi * nc + c, lora_col0)),
            pl.BlockSpec((tt, LORA_PAD), lambda bi, c: (bi * nc + c, lora_col0 + 1)),
            pl.BlockSpec((3, RW_WIDTH), lambda bi, c: (0, 0)),
            row(RW_WIDTH), row(RW_WIDTH),
            pl.BlockSpec((LORA_PAD, 3 * RW_WIDTH), lambda bi, c: (0, 0)),
            row(RW_WIDTH), row(RW_WIDTH), row(RW_WIDTH), row(RW_WIDTH), row(RW_WIDTH),
        ],
        out_specs=pl.BlockSpec((tt, RW_WIDTH), lambda bi, c: (bi * nc + c, 0)),
        out_shape=jax.ShapeDtypeStruct((b * s, RW_WIDTH), BF16),
        scratch_shapes=[
            pltpu.VMEM((RW_PAIRS, LANES, LANES), F32),
            pltpu.VMEM((1, 3 * RW_WIDTH), F32),
            pltpu.VMEM((1, LORA_PAD), F32),
        ],
        compiler_params=pltpu.CompilerParams(
            dimension_semantics=("parallel", "arbitrary"), vmem_limit_bytes=VMEM_LIMIT),
        name="rwkv",
    )(proj, proj, proj, proj, proj, mu_rkv, w0, a0, w2cat, k_k, k_a, r_k, ln_w, ln_b)


ROUTE_E, ROUTE_G, ROUTE_R = 0, 2, 4
EXPERT_LANE0 = N_GROUPS


def _outproj_kernel(x_ref, oda_ref, orw_ref, wout_ref, g_ref, wrh_ref, wrl_ref, br_ref,
                    h_ref, xn_ref, route_ref, cnt_ref, base_ref, *, tm):
    @pl.when(pl.program_id(0) == 0)
    def _():
        base_ref[...] = jnp.zeros_like(base_ref)

    mix = _dot(oda_ref[...], wout_ref[0:DA_WIDTH, :]) + _dot(orw_ref[...], wout_ref[DA_WIDTH:, :])
    h = x_ref[...] + mix
    h_ref[...] = h
    ms = jnp.mean(h * h, axis=-1, keepdims=True)
    xn = h * lax.rsqrt(ms + NORM_EPS) * g_ref[...]
    x_hi = xn.astype(BF16)
    _store_row_tiles(xn_ref, _pack_halves(x_hi))

    x_lo = (xn - x_hi.astype(F32)).astype(BF16)
    hi_both = _dot(x_hi, jnp.concatenate([wrh_ref[...], wrl_ref[...]], axis=1))
    lg = (hi_both[:, 0:LANES] + _dot(x_lo, wrh_ref[...]) + hi_both[:, LANES:]) + br_ref[...]

    lane = lax.broadcasted_iota(jnp.int32, (tm, LANES), 1)
    big = jnp.int32(1 << 20)
    gl = jnp.where(lane < N_GROUPS, lg, NEG)
    gmax = jnp.max(gl, axis=-1, keepdims=True)
    grp = jnp.min(jnp.where(gl == gmax, lane, big), axis=-1, keepdims=True)
    p_grp = 1.0 / jnp.sum(jnp.exp(gl - gmax), axis=-1, keepdims=True)
    eg = jnp.where(lane >= EXPERT_LANE0, (lane - EXPERT_LANE0) // EXPERTS_PER_GROUP, -1)
    el = jnp.where(eg == grp, lg, NEG)
    emax = jnp.max(el, axis=-1, keepdims=True)
    pe = jnp.exp(el - emax)
    probs = pe / jnp.sum(pe, axis=-1, keepdims=True)
    probs = jnp.where(eg == grp, probs, -1.0)
    p1 = jnp.max(probs, axis=-1, keepdims=True)
    i1 = jnp.min(jnp.where(probs == p1, lane, big), axis=-1, keepdims=True)
    probs2 = jnp.where(lane == i1, -1.0, probs)
    p2 = jnp.max(probs2, axis=-1, keepdims=True)
    i2 = jnp.min(jnp.where(probs2 == p2, lane, big), axis=-1, keepdims=True)
    gate1 = p_grp * p1 / (p1 + p2)
    gate2 = p_grp * p2 / (p1 + p2)

    oh1 = jnp.where(lane == i1, 1.0, 0.0)
    oh2 = jnp.where(lane == i2, 1.0, 0.0)
    ri = lax.broadcasted_iota(jnp.int32, (tm, tm), 0)
    rj = lax.broadcasted_iota(jnp.int32, (tm, tm), 1)
    lower = jnp.where(ri > rj, 1.0, 0.0).astype(BF16)
    base = base_ref[...]
    tot1 = jnp.sum(oh1, axis=0, keepdims=True)
    before = _dot(lower, jnp.concatenate([oh1, oh2], axis=1).astype(BF16))
    c1 = base + before[:, 0:LANES]
    c2 = base + tot1 + before[:, LANES:]
    rank1 = jnp.sum(oh1 * c1, axis=-1, keepdims=True)
    rank2 = jnp.sum(oh2 * c2, axis=-1, keepdims=True)
    base = base + tot1 + jnp.sum(oh2, axis=0, keepdims=True)
    base_ref[...] = base
    cnt_ref[...] = base

    e1 = (i1 - EXPERT_LANE0).astype(F32)
    e2 = (i2 - EXPERT_LANE0).astype(F32)
    rec = jnp.zeros((tm, LANES), F32)
    for ln, val in ((ROUTE_E, e1), (ROUTE_E + 1, e2), (ROUTE_G, gate1), (ROUTE_G + 1, gate2),
                    (ROUTE_R, rank1), (ROUTE_R + 1, rank2)):
        rec = jnp.where(lane == ln, val, rec)
    route_ref[...] = rec


def _outproj(x2, o_da, o_rw, w_out, g_ffn, wr_hi, wr_lo, b_r, tm=512):
    t = x2.shape[0]
    const = lambda shape: pl.BlockSpec(shape, lambda i: (0, 0))
    return pl.pallas_call(
        functools.partial(_outproj_kernel, tm=tm),
        grid=(t // tm,),
        in_specs=[
            pl.BlockSpec((tm, D_MODEL), lambda i: (i, 0)),
            pl.BlockSpec((tm, DA_WIDTH), lambda i: (i, 0)),
            pl.BlockSpec((tm, RW_WIDTH), lambda i: (i, 0)),
            const((D_MODEL, D_MODEL)),
            const((1, D_MODEL)),
            const((D_MODEL, LANES)),
            const((D_MODEL, LANES)),
            const((1, LANES)),
        ],
        out_specs=[
            pl.BlockSpec((tm, D_MODEL), lambda i: (i, 0)),
            pl.BlockSpec((tm * SUBLANES, LANES), lambda i: (i, 0)),
            pl.BlockSpec((tm, LANES), lambda i: (i, 0)),
            const((1, LANES)),
        ],
        out_shape=[
            jax.ShapeDtypeStruct((t, D_MODEL), F32),
            jax.ShapeDtypeStruct((t * SUBLANES, LANES), jnp.uint32),
            jax.ShapeDtypeStruct((t, LANES), F32),
            jax.ShapeDtypeStruct((1, LANES), F32),
        ],
        scratch_shapes=[pltpu.VMEM((1, LANES), F32)],
        compiler_params=pltpu.CompilerParams(
            dimension_semantics=("arbitrary",), vmem_limit_bytes=VMEM_LIMIT),
        name="outproj_router",
    )(x2, o_da, o_rw, w_out, g_ffn, wr_hi, wr_lo, b_r)


def _row_copy(src_ref, dst_ref, src_row8, dst_row8, sem):
    src = src_ref.at[pl.ds(pl.multiple_of(src_row8, SUBLANES), SUBLANES)]
    dst = dst_ref.at[pl.ds(pl.multiple_of(dst_row8, SUBLANES), SUBLANES)]
    return pltpu.make_async_copy(src, dst, sem)


def _dispatch_kernel(dest_ref, x_ref, xb_in_hbm, xb_hbm, sem, *, td):
    del xb_in_hbm
    t0 = pl.program_id(0) * td

    def start(i, _):
        tok = t0 + i
        _row_copy(x_ref, xb_hbm, i * SUBLANES, dest_ref[2 * tok], sem).start(priority=0)
        _row_copy(x_ref, xb_hbm, i * SUBLANES, dest_ref[2 * tok + 1], sem).start(priority=1)
        return 0

    lax.fori_loop(0, td, start, 0, unroll=8)
    for _ in range(2):
        pltpu.make_async_copy(x_ref, xb_hbm.at[pl.ds(0, td * SUBLANES)], sem).wait()


def _dispatch(dest_flat, xn2, xb_zero, td=512):
    t = xn2.shape[0] // SUBLANES
    return pl.pallas_call(
        functools.partial(_dispatch_kernel, td=td),
        grid_spec=pltpu.PrefetchScalarGridSpec(
            num_scalar_prefetch=1,
            grid=(t // td,),
            in_specs=[pl.BlockSpec((td * SUBLANES, LANES), lambda i, d: (i, 0)),
                      pl.BlockSpec(memory_space=pl.ANY)],
            out_specs=pl.BlockSpec(memory_space=pl.ANY),
            scratch_shapes=[pltpu.SemaphoreType.DMA],
        ),
        out_shape=jax.ShapeDtypeStruct(xb_zero.shape, xb_zero.dtype),
        input_output_aliases={2: 0},
        compiler_params=pltpu.CompilerParams(dimension_semantics=("arbitrary",)),
        name="dispatch",
    )(dest_flat, xn2, xb_zero)


EXPERT_BLOCKS_PER_STEP = 2


def _experts_kernel(be_ref, nu_ref, nxt_ref, slot_ref, xb_ref, wg_hbm, wu_hbm, wd_hbm, yb_ref,
                    wgf_ref, wuf_ref, wdf_ref, wgb_ref, wub_ref, wdb_ref, sem):
    block_rows = EXPERT_BLOCK * SUBLANES

    def weight_copies(e, slot):
        return [pltpu.make_async_copy(src.at[e], dst.at[slot], sem.at[slot])
                for src, dst in ((wg_hbm, wgf_ref), (wu_hbm, wuf_ref), (wd_hbm, wdf_ref))]

    def row_block(sub):
        i = pl.program_id(0) * EXPERT_BLOCKS_PER_STEP + sub
        x_ref = xb_ref.at[pl.ds(sub * block_rows, block_rows)]
        y_ref = yb_ref.at[pl.ds(sub * block_rows, block_rows)]
        used = i < nu_ref[0]
        new_expert = jnp.logical_or(i == 0, be_ref[i] != be_ref[jnp.maximum(i - 1, 0)])

        def swiglu(wg_lo, wg_hi, wu_lo, wu_hi, wd):
            x_lo, x_hi = (half.astype(BF16) for half in _unpack_halves(_load_row_tiles(x_ref)))
            hg = _dot(x_lo, wg_lo) + _dot(x_hi, wg_hi)
            hu = _dot(x_lo, wu_lo) + _dot(x_hi, wu_hi)
            hid = hg * _sigmoid(hg) * hu
            _store_row_tiles(y_ref, _pack_halves(_dot(hid.astype(BF16), wd).astype(BF16)))

        @pl.when(jnp.logical_and(used, new_expert))
        def _():
            slot = slot_ref[i]

            @pl.when(i == 0)
            def _():
                for cp in weight_copies(be_ref[i], slot):
                    cp.start()

            for cp in weight_copies(be_ref[i], slot):
                cp.wait()

            @pl.when(nxt_ref[i] >= 0)
            def _():
                for cp in weight_copies(nxt_ref[i], 1 - slot):
                    cp.start()

            wg, wu, wd = (w[slot].astype(BF16) for w in (wgf_ref, wuf_ref, wdf_ref))
            wgb_ref[...] = wg
            wub_ref[...] = wu
            wdb_ref[...] = wd
            swiglu(wg[0:HALF, :], wg[HALF:, :], wu[0:HALF, :], wu[HALF:, :], wd)

        @pl.when(jnp.logical_and(used, jnp.logical_not(new_expert)))
        def _():
            swiglu(wgb_ref[0:HALF, :], wgb_ref[HALF:, :], wub_ref[0:HALF, :], wub_ref[HALF:, :], wdb_ref[...])

        @pl.when(jnp.logical_not(used))
        def _():
            y_ref[...] = jnp.zeros_like(y_ref)

    for sub in range(EXPERT_BLOCKS_PER_STEP):
        row_block(sub)


def _experts(block_e, n_used, next_e, slot, xb, w_gate, w_up, w_down):
    step_rows = EXPERT_BLOCKS_PER_STEP * EXPERT_BLOCK * SUBLANES
    nb = xb.shape[0] // step_rows
    last_step = lambda nu: (nu[0] - 1) // EXPERT_BLOCKS_PER_STEP
    return pl.pallas_call(
        _experts_kernel,
        grid_spec=pltpu.PrefetchScalarGridSpec(
            num_scalar_prefetch=4,
            grid=(nb,),
            in_specs=[
                pl.BlockSpec((step_rows, LANES), lambda i, be, nu, *_: (jnp.minimum(i, last_step(nu)), 0)),
                pl.BlockSpec(memory_space=pl.ANY),
                pl.BlockSpec(memory_space=pl.ANY),
                pl.BlockSpec(memory_space=pl.ANY),
            ],
            out_specs=pl.BlockSpec((step_rows, LANES), lambda i, *_: (i, 0)),
            scratch_shapes=[
                pltpu.VMEM((2, D_MODEL, D_EXPERT), F32),
                pltpu.VMEM((2, D_MODEL, D_EXPERT), F32),
                pltpu.VMEM((2, D_EXPERT, D_MODEL), F32),
                pltpu.VMEM((D_MODEL, D_EXPERT), BF16),
                pltpu.VMEM((D_MODEL, D_EXPERT), BF16),
                pltpu.VMEM((D_EXPERT, D_MODEL), BF16),
                pltpu.SemaphoreType.DMA((2,)),
            ],
        ),
        out_shape=jax.ShapeDtypeStruct(xb.shape, jnp.uint32),
        compiler_params=pltpu.CompilerParams(
            dimension_semantics=("arbitrary",), vmem_limit_bytes=EXPERTS_VMEM_LIMIT),
        name="experts",
    )(block_e, n_used, next_e, slot, xb, w_gate, w_up, w_down)


def _combine_kernel(dest_ref, route_ref, h_ref, g_ref, yb_hbm, o_ref, y1_ref, y2_ref, sem, *, tc):
    step = pl.program_id(0)
    n_steps = pl.num_programs(0)

    def gather(tile, slot):
        def start(i, _):
            tok = tile * tc + i
            _row_copy(yb_hbm, y1_ref.at[slot], dest_ref[2 * tok], i * SUBLANES, sem.at[slot]).start(priority=0)
            _row_copy(yb_hbm, y2_ref.at[slot], dest_ref[2 * tok + 1], i * SUBLANES, sem.at[slot]).start(priority=1)
            return 0

        lax.fori_loop(0, tc, start, 0, unroll=8)

    @pl.when(step == 0)
    def _():
        gather(0, 0)

    @pl.when(step + 1 < n_steps)
    def _():
        gather(step + 1, (step + 1) % 2)

    slot = step % 2
    pltpu.make_async_copy(yb_hbm.at[pl.ds(0, tc * SUBLANES)], y1_ref.at[slot], sem.at[slot]).wait()
    pltpu.make_async_copy(yb_hbm.at[pl.ds(0, tc * SUBLANES)], y2_ref.at[slot], sem.at[slot]).wait()

    rec = route_ref[...]
    lane = lax.broadcasted_iota(jnp.int32, (tc, LANES), 1)
    g1 = jnp.sum(jnp.where(lane == ROUTE_G, rec, 0.0), axis=-1, keepdims=True)
    g2 = jnp.sum(jnp.where(lane == ROUTE_G + 1, rec, 0.0), axis=-1, keepdims=True)
    y1_lo, y1_hi = _unpack_halves(_load_row_tiles(y1_ref.at[slot]))
    y2_lo, y2_hi = _unpack_halves(_load_row_tiles(y2_ref.at[slot]))
    h_lo = h_ref[:, 0:HALF] + (y1_lo * g1 + y2_lo * g2)
    h_hi = h_ref[:, HALF:] + (y1_hi * g1 + y2_hi * g2)
    ms = (jnp.sum(h_lo * h_lo, axis=-1, keepdims=True)
          + jnp.sum(h_hi * h_hi, axis=-1, keepdims=True)) * (1.0 / D_MODEL)
    scale = lax.rsqrt(ms + NORM_EPS)
    o_ref[:, 0:HALF] = h_lo * scale * g_ref[:, 0:HALF]
    o_ref[:, HALF:] = h_hi * scale * g_ref[:, HALF:]


def _combine(dest_flat, route, h1, g_final, yb, tc=512):
    t = h1.shape[0]
    return pl.pallas_call(
        functools.partial(_combine_kernel, tc=tc),
        grid_spec=pltpu.PrefetchScalarGridSpec(
            num_scalar_prefetch=1,
            grid=(t // tc,),
            in_specs=[
                pl.BlockSpec((tc, LANES), lambda i, d: (i, 0)),
                pl.BlockSpec((tc, D_MODEL), lambda i, d: (i, 0)),
                pl.BlockSpec((1, D_MODEL), lambda i, d: (0, 0)),
                pl.BlockSpec(memory_space=pl.ANY),
            ],
            out_specs=pl.BlockSpec((tc, D_MODEL), lambda i, d: (i, 0)),
            scratch_shapes=[
                pltpu.VMEM((2, tc * SUBLANES, LANES), jnp.uint32),
                pltpu.VMEM((2, tc * SUBLANES, LANES), jnp.uint32),
                pltpu.SemaphoreType.DMA((2,)),
            ],
        ),
        out_shape=jax.ShapeDtypeStruct((t, D_MODEL), F32),
        compiler_params=pltpu.CompilerParams(
            dimension_semantics=("arbitrary",), vmem_limit_bytes=VMEM_LIMIT),
        name="combine",
    )(dest_flat, route, h1, g_final, yb)


def _pad_cols(w, n):
    return jnp.pad(w, ((0, 0), (0, n - w.shape[1])))


def kernel(x, g_mix, w_in, w_out, da_lambda_q1, da_lambda_k1, da_lambda_q2, da_lambda_k2, da_subln_g,
           rw_mu_x, rw_mu_rkv, rw_w0, rw_w1, rw_w2, rw_a0, rw_a1, rw_a2, rw_g1, rw_g2, rw_k_k, rw_k_a,
           rw_r_k, rw_ln_w, rw_ln_b, g_ffn, moe_w_group, moe_b_group, moe_w_expert, moe_b_expert,
           moe_w_gate, moe_w_up, moe_w_down, g_final):
    b, s, d = x.shape
    t = b * s
    x2 = x.reshape(t, d)

    mu = rw_mu_x[0]
    lora = (rw_w1[0], rw_a1[0], rw_g1[0])
    keep = _pad_cols(jnp.concatenate([(1.0 - mu[i])[:, None] * w for i, w in enumerate(lora)], axis=1), LORA_PAD)
    prev = _pad_cols(jnp.concatenate([mu[i][:, None] * w for i, w in enumerate(lora)], axis=1), LORA_PAD)
    w_all = jnp.concatenate([w_in[0], keep, prev], axis=1).astype(BF16)
    proj = _inproj(x2, g_mix, w_all)

    slopes = jnp.asarray([2.0 ** (-8.0 * (i + 1) / DA_HEADS) for i in range(DA_HEADS)], F32)
    lam4 = jnp.concatenate([da_lambda_q1, da_lambda_k1, da_lambda_q2, da_lambda_k2], axis=0)
    o_da = _diffattn(proj, slopes, lam4, da_subln_g.reshape(DA_V_DIM, 1), b, s)

    w2cat = jnp.zeros((LORA_PAD, 3 * RW_WIDTH), F32)
    w2cat = w2cat.at[0:LORA_W, 0:RW_WIDTH].set(rw_w2[0])
    w2cat = w2cat.at[LORA_W:LORA_W + LORA_A, RW_WIDTH:2 * RW_WIDTH].set(rw_a2[0])
    w2cat = w2cat.at[LORA_W + LORA_A:LORA_W + LORA_A + LORA_G, 2 * RW_WIDTH:].set(rw_g2[0])
    o_rw = _rwkv(proj, rw_mu_rkv[0], rw_w0, rw_a0, w2cat.astype(BF16), rw_k_k, rw_k_a,
                 rw_r_k.reshape(1, RW_WIDTH), rw_ln_w, rw_ln_b, b, s)

    w_r = _pad_cols(jnp.concatenate([moe_w_group[0], moe_w_expert[0]], axis=1), LANES)
    wr_hi = w_r.astype(BF16)
    wr_lo = (w_r - wr_hi.astype(F32)).astype(BF16)
    b_r = _pad_cols(jnp.concatenate([moe_b_group[0], moe_b_expert[0].reshape(-1)])[None, :], LANES)
    h1, xn2, route, cnt = _outproj(x2, o_da, o_rw, w_out[0].astype(BF16), g_ffn, wr_hi, wr_lo, b_r)

    counts = cnt[0, EXPERT_LANE0:EXPERT_LANE0 + N_EXPERTS].astype(jnp.int32)
    pcounts = ((counts + EXPERT_BLOCK - 1) // EXPERT_BLOCK) * EXPERT_BLOCK
    expert_ids = jnp.arange(N_EXPERTS, dtype=jnp.int32)
    upto = expert_ids[None, :] <= expert_ids[:, None]
    pends = jnp.sum(jnp.where(upto, pcounts[None, :], 0), axis=1)
    pstarts = pends - pcounts
    e_idx = route[:, ROUTE_E:ROUTE_E + 2].astype(jnp.int32)
    rank = route[:, ROUTE_R:ROUTE_R + 2].astype(jnp.int32)
    row0 = jnp.sum(jnp.where(e_idx[..., None] == expert_ids, pstarts, 0), axis=-1)
    dest = ((row0 + rank) * SUBLANES).reshape(-1)
    n_blocks = (2 * t) // EXPERT_BLOCK + N_EXPERTS
    block_row0 = jnp.arange(n_blocks, dtype=jnp.int32) * EXPERT_BLOCK
    block_e = jnp.minimum(jnp.sum(pends[None, :] <= block_row0[:, None], axis=1), N_EXPERTS - 1).astype(jnp.int32)
    n_used = (pends[-1:] // EXPERT_BLOCK).astype(jnp.int32)
    has_rows = counts > 0
    later = (expert_ids[None, :] > expert_ids[:, None]) & has_rows[None, :]
    next_used = jnp.min(jnp.where(later, expert_ids[None, :], N_EXPERTS), axis=1)
    next_used = jnp.where(next_used == N_EXPERTS, -1, next_used).astype(jnp.int32)
    buffer_id = ((jnp.sum(jnp.where(upto & has_rows[None, :], 1, 0), axis=1) - 1) % 2).astype(jnp.int32)

    xb = _dispatch(dest, xn2, jnp.zeros((n_blocks * EXPERT_BLOCK * SUBLANES, LANES), jnp.uint32))
    yb = _experts(block_e, n_used, next_used[block_e], buffer_id[block_e], xb,
                  moe_w_gate[0], moe_w_up[0], moe_w_down[0])
    out = _combine(dest, route, h1, g_final[None, :], yb)
    return out.reshape(b, s, d)
```

```python
import functools
import math

import jax
import jax.numpy as jnp
from jax import lax
from jax.experimental import pallas as pl
from jax.experimental.pallas import tpu as pltpu

F32 = jnp.float32
BF16 = jnp.bfloat16

D_MODEL = 2048
DA_WIDTH = 1024
RW_WIDTH = 1024
DA_HEADS = 8
DA_V_DIM = 128
DA_QK_DIM = 64
RW_N = 64
RW_PAIRS = RW_WIDTH // 128
LORA_W, LORA_A, LORA_G = 64, 64, 160
LORA_PAD = 384
IN_COLS = 3 * DA_WIDTH + 3 * RW_WIDTH
NC = IN_COLS + 2 * LORA_PAD
N_GROUPS = 4
EXPERTS_PER_GROUP = 8
N_EXPERTS = 32
D_EXPERT = 512
NORM_EPS = 1e-6
LN_X_EPS = 64e-5
LAM_INIT = 0.8 - 0.6 * math.exp(-0.3 * 0)
LANES = 128
CHUNK = 64
RW_TILE = 256
NCH = RW_TILE // CHUNK
EXPERT_BLOCK = 256
NEG = -1e30
VMEM_LIMIT = 48 * 1024 * 1024
EXPERTS_VMEM_LIMIT = 56 * 1024 * 1024


def _dot(a, b):
    return jnp.dot(a, b, preferred_element_type=F32)


def _dot_nt(a, b):
    return lax.dot_general(a, b, (((1,), (1,)), ((), ())), preferred_element_type=F32)


def _dot_tn(a, b):
    return lax.dot_general(a, b, (((0,), (0,)), ((), ())), preferred_element_type=F32)


def _sigmoid(x):
    return 1.0 / (1.0 + jnp.exp(-x))


HALF = D_MODEL // 2


def _pack_halves(xb):
    lo = lax.bitcast_convert_type(xb[:, :HALF].astype(F32), jnp.uint32) >> 16
    hi = lax.bitcast_convert_type(xb[:, HALF:].astype(F32), jnp.uint32) & jnp.uint32(0xFFFF0000)
    return hi | lo


def _unpack_halves(u):
    lo = lax.bitcast_convert_type(u << 16, F32)
    hi = lax.bitcast_convert_type(u & jnp.uint32(0xFFFF0000), F32)
    return lo, hi


SUBLANES = 8
assert HALF == SUBLANES * LANES


def _store_row_tiles(ref, u):
    n = u.shape[0]
    for k in range(SUBLANES):
        ref[pl.ds(k, n, stride=SUBLANES), :] = u[:, k * LANES:(k + 1) * LANES]


def _load_row_tiles(ref):
    n = ref.shape[0] // SUBLANES
    return jnp.concatenate([ref[pl.ds(k, n, stride=SUBLANES), :] for k in range(SUBLANES)], axis=1)


def _inproj_kernel(x_ref, g_ref, w_ref, o_ref, xn_ref):
    @pl.when(pl.program_id(1) == 0)
    def _():
        x = x_ref[...]
        ms = jnp.mean(x * x, axis=-1, keepdims=True)
        xn_ref[...] = (x * lax.rsqrt(ms + NORM_EPS) * g_ref[...]).astype(BF16)

    o_ref[...] = _dot(xn_ref[...], w_ref[...]).astype(o_ref.dtype)


def _inproj(x2, g, w_all, tm=1024, tn=2304):
    t = x2.shape[0]
    tm = min(tm, t)
    vmem_limit = 2 * (tm * D_MODEL * 4 + D_MODEL * tn * 2 + tm * tn * 2) + tm * D_MODEL * 2 + (6 << 20)
    return pl.pallas_call(
        _inproj_kernel,
        grid=(t // tm, NC // tn),
        in_specs=[
            pl.BlockSpec((tm, D_MODEL), lambda i, j: (i, 0)),
            pl.BlockSpec((1, D_MODEL), lambda i, j: (0, 0)),
            pl.BlockSpec((D_MODEL, tn), lambda i, j: (0, j)),
        ],
        out_specs=pl.BlockSpec((tm, tn), lambda i, j: (i, j)),
        out_shape=jax.ShapeDtypeStruct((t, NC), BF16),
        scratch_shapes=[pltpu.VMEM((tm, D_MODEL), BF16)],
        compiler_params=pltpu.CompilerParams(
            dimension_semantics=("parallel", "arbitrary"), vmem_limit_bytes=vmem_limit),
        name="inproj",
    )(x2, g, w_all)


DA_HEADS_PER_STEP = 2


def _da_kernel(qi_tab, kj_tab, slopes_ref, lam_ref, subg_ref, q_ref, k_ref, v_ref, o_ref,
               vt_ref, qz_ref, bias_ref, m_ref, l_ref, lfin_ref, acc_ref, s_ref, p_ref, alpha_ref,
               *, tq, seq):
    tk = tq
    nq = seq // tq
    n_trips = nq * (nq + 1) // 2
    heads = range(DA_HEADS_PER_STEP)
    chains = [(hd, c) for hd in heads for c in range(2)]
    slope = [slopes_ref[pl.program_id(1) * DA_HEADS_PER_STEP + hd] for hd in heads]
    cols = lambda hd: slice(hd * LANES, (hd + 1) * LANES)

    lane = lax.broadcasted_iota(jnp.int32, (tq, LANES), 1)
    krow = lax.broadcasted_iota(jnp.int32, (tk, tq), 0)
    qcol = lax.broadcasted_iota(jnp.int32, (tk, tq), 1)
    for hd in heads:
        for blk in range(seq // LANES):
            r = slice(blk * LANES, (blk + 1) * LANES)
            vt_ref[hd, :, r] = v_ref[r, cols(hd)].astype(F32).T.astype(BF16)
        for blk in range(nq):
            r = slice(blk * tq, (blk + 1) * tq)
            q = q_ref[r, cols(hd)] * jnp.asarray(DA_QK_DIM ** -0.5, BF16)
            qz_ref[2 * hd, r, :] = jnp.where(lane < DA_QK_DIM, q, jnp.zeros_like(q))
            qz_ref[2 * hd + 1, r, :] = jnp.where(lane >= DA_QK_DIM, q, jnp.zeros_like(q))
        alibi = slope[hd] * krow.astype(F32)
        bias_ref[hd, 0] = alibi
        bias_ref[hd, 1] = jnp.where(krow <= qcol, alibi, NEG)

    m_ref[...] = jnp.full_like(m_ref, NEG)
    l_ref[...] = jnp.zeros_like(l_ref)
    lfin_ref[...] = jnp.ones_like(lfin_ref)
    acc_ref[...] = jnp.zeros_like(acc_ref)
    p_ref[...] = jnp.zeros_like(p_ref)
    alpha_ref[...] = jnp.zeros_like(alpha_ref)

    lv = lam_ref[...]
    lam = (jnp.exp(jnp.sum(lv[0:1] * lv[1:2], axis=-1, keepdims=True))
           - jnp.exp(jnp.sum(lv[2:3] * lv[3:4], axis=-1, keepdims=True)) + LAM_INIT)

    def scores_to_scratch(t):
        qs = pl.multiple_of(qi_tab[t] * tq, tq)
        ks = pl.multiple_of(kj_tab[t] * tk, tk)
        diag = (qi_tab[t] == kj_tab[t]).astype(jnp.int32)
        for hd, c in chains:
            ch = 2 * hd + c
            s_ref[ch] = (_dot_nt(k_ref[pl.ds(ks, tk), cols(hd)], qz_ref[ch, pl.ds(qs, tq), :])
                         + bias_ref[hd, diag])

    def values_from_scratch(t):
        ks = pl.multiple_of(kj_tab[t] * tk, tk)
        for hd, c in chains:
            ch = 2 * hd + c
            acc_ref[ch] = alpha_ref[ch:ch + 1, :] * acc_ref[ch] + _dot(vt_ref[hd, :, pl.ds(ks, tk)], p_ref[ch])

    def softmax(t):
        first = kj_tab[t] == 0
        diag = kj_tab[t] == qi_tab[t]
        rel = ((kj_tab[t] - qi_tab[t]) * tk).astype(F32)
        p, alpha = [], []
        for hd, c in chains:
            ch = 2 * hd + c
            off = slope[hd] * rel
            s = s_ref[ch]
            m = jnp.where(first, NEG, m_ref[ch:ch + 1, :])
            l = jnp.where(first, 0.0, l_ref[ch:ch + 1, :])
            m_new = jnp.maximum(m, jnp.max(s, axis=0, keepdims=True) + off)
            pc = jnp.exp(s - (m_new - off))
            a = jnp.exp(m - m_new)
            l_new = a * l + jnp.sum(pc, axis=0, keepdims=True)
            m_ref[ch:ch + 1, :] = m_new
            l_ref[ch:ch + 1, :] = l_new
            lfin_ref[ch:ch + 1, :] = jnp.where(diag, l_new, lfin_ref[ch:ch + 1, :])
            p.append(pc)
            alpha.append(a)
        return p, alpha

    def publish(p, alpha):
        for ch in range(len(chains)):
            alpha_ref[ch:ch + 1, :] = alpha[ch]
            p_ref[ch] = p[ch].astype(BF16)

    def finalize(t):
        qs = pl.multiple_of(qi_tab[t] * tq, tq)
        for hd in heads:
            c0, c1 = 2 * hd, 2 * hd + 1
            o = acc_ref[c0] / lfin_ref[c0:c0 + 1, :] - lam * (acc_ref[c1] / lfin_ref[c1:c1 + 1, :])
            ms = jnp.mean(o * o, axis=0, keepdims=True)
            o = o * lax.rsqrt(ms + NORM_EPS) * subg_ref[...] * (1.0 - LAM_INIT)
            o_ref[pl.ds(qs, tq), cols(hd)] = o.T.astype(o_ref.dtype)

    scores_to_scratch(0)

    def body(t, _):
        prev = jnp.maximum(t - 1, 0)
        p, alpha = softmax(t)
        values_from_scratch(prev)
        scores_to_scratch(jnp.minimum(t + 1, n_trips - 1))
        publish(p, alpha)

        @pl.when(jnp.logical_and(t > 0, qi_tab[prev] == kj_tab[prev]))
        def _():
            finalize(prev)
        return 0

    lax.fori_loop(0, n_trips, body, 0)
    values_from_scratch(n_trips - 1)
    finalize(n_trips - 1)


def _diffattn(proj, slopes, lam4, subg_col, b, s, tq=256):
    nq = s // tq
    nh = DA_HEADS_PER_STEP
    groups = DA_HEADS // nh
    width = nh * LANES
    pairs = [(qi, kj) for qi in range(nq) for kj in range(qi + 1)]
    qi_tab = jnp.asarray([p[0] for p in pairs], jnp.int32)
    kj_tab = jnp.asarray([p[1] for p in pairs], jnp.int32)
    return pl.pallas_call(
        functools.partial(_da_kernel, tq=tq, seq=s),
        grid_spec=pltpu.PrefetchScalarGridSpec(
            num_scalar_prefetch=2,
            grid=(b, groups),
            in_specs=[
                pl.BlockSpec(memory_space=pltpu.SMEM),
                pl.BlockSpec((4, DA_QK_DIM), lambda bi, g, *_: (0, 0)),
                pl.BlockSpec((DA_V_DIM, 1), lambda bi, g, *_: (0, 0)),
                pl.BlockSpec((s, width), lambda bi, g, *_: (bi, g)),
                pl.BlockSpec((s, width), lambda bi, g, *_: (bi, groups + g)),
                pl.BlockSpec((s, width), lambda bi, g, *_: (bi, 2 * groups + g)),
            ],
            out_specs=pl.BlockSpec((s, width), lambda bi, g, *_: (bi, g)),
            scratch_shapes=[
                pltpu.VMEM((nh, DA_V_DIM, s), BF16),
                pltpu.VMEM((2 * nh, s, LANES), BF16),
                pltpu.VMEM((nh, 2, tq, tq), F32),
                pltpu.VMEM((2 * nh, tq), F32),
                pltpu.VMEM((2 * nh, tq), F32),
                pltpu.VMEM((2 * nh, tq), F32),
                pltpu.VMEM((2 * nh, DA_V_DIM, tq), F32),
                pltpu.VMEM((2 * nh, tq, tq), F32),
                pltpu.VMEM((2 * nh, tq, tq), BF16),
                pltpu.VMEM((2 * nh, tq), F32),
            ],
        ),
        out_shape=jax.ShapeDtypeStruct((b * s, DA_WIDTH), BF16),
        compiler_params=pltpu.CompilerParams(
            dimension_semantics=("parallel", "arbitrary"), vmem_limit_bytes=VMEM_LIMIT),
        name="diffattn",
    )(qi_tab, kj_tab, slopes, lam4, subg_col, proj, proj, proj)


def _rwkv_kernel(r_ref, k_ref, v_ref, p1_ref, p2_ref, mu_ref, w0_ref, a0_ref, w2_ref, kk_ref, ka_ref,
                 rk_ref, lnw_ref, lnb_ref, o_ref, state_ref, crkv_ref, cp2_ref):
    L, TT = CHUNK, RW_TILE
    c = pl.program_id(1)

    @pl.when(c == 0)
    def _():
        state_ref[...] = jnp.zeros_like(state_ref)
        crkv_ref[...] = jnp.zeros_like(crkv_ref)
        cp2_ref[...] = jnp.zeros_like(cp2_ref)

    first = lax.broadcasted_iota(jnp.int32, (TT, 1), 0) == 0

    def shift(x, carry):
        return jnp.where(first, carry, pltpu.roll(x, 1, 0))

    r_raw = r_ref[...].astype(F32)
    k_raw = k_ref[...].astype(F32)
    v_raw = v_ref[...].astype(F32)
    p2 = p2_ref[...].astype(F32)
    rs = shift(r_raw, crkv_ref[:, 0:RW_WIDTH])
    ks = shift(k_raw, crkv_ref[:, RW_WIDTH:2 * RW_WIDTH])
    vs = shift(v_raw, crkv_ref[:, 2 * RW_WIDTH:3 * RW_WIDTH])
    lin = p1_ref[...].astype(F32) + shift(p2, cp2_ref[...])
    crkv_ref[:, 0:RW_WIDTH] = r_raw[TT - 1:TT, :]
    crkv_ref[:, RW_WIDTH:2 * RW_WIDTH] = k_raw[TT - 1:TT, :]
    crkv_ref[:, 2 * RW_WIDTH:3 * RW_WIDTH] = v_raw[TT - 1:TT, :]
    cp2_ref[...] = p2[TT - 1:TT, :]

    r = r_raw + (rs - r_raw) * mu_ref[0:1, :]
    k = k_raw + (ks - k_raw) * mu_ref[1:2, :]
    v = v_raw + (vs - v_raw) * mu_ref[2:3, :]

    ll = lax.broadcasted_iota(jnp.int32, (TT, LORA_PAD), 1)
    z = jnp.where(ll < LORA_W, jnp.tanh(lin),
                  jnp.where(ll < LORA_W + LORA_A, lin,
                            jnp.where(ll < LORA_W + LORA_A + LORA_G, _sigmoid(lin), 0.0)))
    d = _dot(z.astype(BF16), w2_ref[...])
    logw = -_sigmoid(w0_ref[...] + d[:, 0:RW_WIDTH]) * math.exp(-0.5)
    a_sig = _sigmoid(a0_ref[...] + d[:, RW_WIDTH:2 * RW_WIDTH])
    g = d[:, 2 * RW_WIDTH:3 * RW_WIDTH]

    li = lax.broadcasted_iota(jnp.int32, (LANES, LANES), 0)
    lj = lax.broadcasted_iota(jnp.int32, (LANES, LANES), 1)
    same_head = jnp.where((li // RW_N) == (lj // RW_N), 1.0, 0.0).astype(BF16)

    def head_sum(x):
        xb = x.astype(BF16)
        return jnp.concatenate(
            [_dot(xb[:, p * LANES:(p + 1) * LANES], same_head) for p in range(RW_PAIRS)], axis=1)

    kk = k * kk_ref[...]
    kk = kk * lax.rsqrt(jnp.maximum(head_sum(kk * kk), 1e-24))
    k = k * (1.0 + (a_sig - 1.0) * ka_ref[...])
    bonus = head_sum(r * k * rk_ref[...])

    ti = lax.broadcasted_iota(jnp.int32, (TT, TT), 0)
    tj = lax.broadcasted_iota(jnp.int32, (TT, TT), 1)
    tri = jnp.where(ti >= tj, jnp.where((ti // L) == (tj // L), 1.0, 0.0), 0.0).astype(BF16)
    w_hi = logw.astype(BF16)
    w_r1 = logw - w_hi.astype(F32)
    w_mid = w_r1.astype(BF16)
    w_lo = (w_r1 - w_mid.astype(F32)).astype(BF16)
    cw = _dot(tri, w_hi) + _dot(tri, w_mid) + _dot(tri, w_lo)
    cw_last = [cw[(ci + 1) * L - 1:(ci + 1) * L, :] for ci in range(NCH)]
    e_in = jnp.exp(cw)
    e_out = jnp.exp(-cw)
    e_end = jnp.exp(jnp.concatenate([jnp.broadcast_to(cl, (L, RW_WIDTH)) for cl in cw_last], axis=0) - cw)
    w_end = [jnp.exp(cl) for cl in cw_last]

    at = -kk * jnp.exp(cw - logw)
    rt = r * e_in
    b = kk * a_sig
    bt = b * e_out
    kt = k * e_out
    bd = b * e_end
    kd = k * e_end

    lane = lax.broadcasted_iota(jnp.int32, (L, LANES), 1)
    lo = lane < RW_N

    def stack(x, ci, p):
        xs = x[ci * L:(ci + 1) * L, p * LANES:(p + 1) * LANES]
        return jnp.concatenate([jnp.where(lo, xs, 0.0), jnp.where(lo, 0.0, xs)], axis=0).astype(BF16)

    si = lax.broadcasted_iota(jnp.int32, (2 * L, 2 * L), 0)
    sj = lax.broadcasted_iota(jnp.int32, (2 * L, 2 * L), 1)
    strict = (si % L) > (sj % L)
    incl = (si % L) >= (sj % L)
    eye = jnp.where(si == sj, 1.0, 0.0)

    pairs = range(RW_PAIRS)
    cat0 = lambda *xs: jnp.concatenate(xs, axis=0)
    cat1 = lambda *xs: jnp.concatenate(xs, axis=1)
    bf = lambda x: x.astype(BF16)
    def chunk_setup(ci):
        a_s = [stack(at, ci, p) for p in pairs]
        r_s = [stack(rt, ci, p) for p in pairs]
        v_s = [stack(v, ci, p) for p in pairs]
        bk = [cat0(stack(bt, ci, p), stack(kt, ci, p)) for p in pairs]
        bkd = [cat0(stack(bd, ci, p), stack(kd, ci, p)) for p in pairs]
        aa = [_dot_nt(a_s[p], bk[p]) for p in pairs]
        rr = [_dot_nt(r_s[p], bk[p]) for p in pairs]
        a_ab = [jnp.where(strict, aa[p][:, 0:2 * L], 0.0) for p in pairs]
        a_ak = [bf(jnp.where(strict, aa[p][:, 2 * L:4 * L], 0.0)) for p in pairs]
        a_rb = [bf(jnp.where(incl, rr[p][:, 0:2 * L], 0.0)) for p in pairs]
        a_rk = [bf(jnp.where(incl, rr[p][:, 2 * L:4 * L], 0.0)) for p in pairs]
        ab = [bf(a) for a in a_ab]
        qpow = [_dot(ab[p], ab[p]) for p in pairs]
        inv = [eye + a_ab[p] for p in pairs]
        for _ in range(int(math.log2(L)) - 2):
            qb = [bf(q) for q in qpow]
            res = [_dot(qb[p], cat1(qb[p], bf(inv[p]))) for p in pairs]
            qpow = [res[p][:, 0:2 * L] for p in pairs]
            inv = [inv[p] + res[p][:, 2 * L:4 * L] for p in pairs]
        inv = [bf(inv[p] + _dot(bf(qpow[p]), bf(inv[p]))) for p in pairs]
        w_col = [jnp.broadcast_to(w_end[ci][:, p * LANES:(p + 1) * LANES], (LANES, LANES)).T for p in pairs]
        return a_s, r_s, v_s, bkd, a_ak, a_rb, a_rk, inv, w_col

    def chunk_advance(setup, h):
        a_s, r_s, v_s, bkd, a_ak, a_rb, a_rk, inv, w_col = setup
        h_b = [bf(hp) for hp in h]
        x = [_dot(cat1(a_s[p], a_ak[p]), cat0(h_b[p], v_s[p])) for p in pairs]
        u_b = [bf(_dot(inv[p], bf(x[p]))) for p in pairs]
        y2 = [_dot(cat1(r_s[p], a_rb[p], a_rk[p]), cat0(h_b[p], u_b[p], v_s[p])) for p in pairs]
        h = [w_col[p] * h[p] + _dot_tn(bkd[p], cat0(u_b[p], v_s[p])) for p in pairs]
        return h, cat1(*[y2[p][0:L] + y2[p][L:2 * L] for p in pairs])

    h = [state_ref[p] for p in pairs]
    ys = []
    setup = chunk_setup(0)
    for ci in range(NCH):
        ahead = chunk_setup(ci + 1) if ci + 1 < NCH else None
        h, y_chunk = chunk_advance(setup, h)
        ys.append(y_chunk)
        setup = ahead
    for p in pairs:
        state_ref[p] = h[p]
    y = cat0(*ys)

    mean = head_sum(y) * (1.0 / RW_N)
    yc = y - mean
    var = head_sum(yc * yc) * (1.0 / RW_N)
    y = yc * lax.rsqrt(var + LN_X_EPS) * lnw_ref[...] + lnb_ref[...]
    y = y + bonus * v
    o_ref[...] = (y * g).astype(o_ref.dtype)


def _rwkv(proj, mu_rkv, w0, a0, w2cat, k_k, k_a, r_k, ln_w, ln_b, b, s):
    tt = RW_TILE
    nc = s // tt
    row = lambda n: pl.BlockSpec((1, n), lambda bi, c: (0, 0))
    rkv_col0 = 3 * DA_WIDTH // RW_WIDTH
    lora_col0 = IN_COLS // LORA_PAD
    return pl.pallas_call(
        _rwkv_kernel,
        grid=(b, nc),
        in_specs=[
            pl.BlockSpec((tt, RW_WIDTH), lambda bi, c: (bi * nc + c, rkv_col0)),
            pl.BlockSpec((tt, RW_WIDTH), lambda bi, c: (bi * nc + c, rkv_col0 + 1)),
            pl.BlockSpec((tt, RW_WIDTH), lambda bi, c: (bi * nc + c, rkv_col0 + 2)),
            pl.BlockSpec((tt, LORA_PAD), lambda bi, c: (bi * nc + c, lora_col0)),
            pl.BlockSpec((tt, LORA_PAD), lambda bi, c: (bi * nc + c, lora_col0 + 1)),
            pl.BlockSpec((3, RW_WIDTH), lambda bi, c: (0, 0)),
            row(RW_WIDTH), row(RW_WIDTH),
            pl.BlockSpec((LORA_PAD, 3 * RW_WIDTH), lambda bi, c: (0, 0)),
            row(RW_WIDTH), row(RW_WIDTH), row(RW_WIDTH), row(RW_WIDTH), row(RW_WIDTH),
        ],
        out_specs=pl.BlockSpec((tt, RW_WIDTH), lambda bi, c: (bi * nc + c, 0)),
        out_shape=jax.ShapeDtypeStruct((b * s, RW_WIDTH), BF16),
        scratch_shapes=[
            pltpu.VMEM((RW_PAIRS, LANES, LANES), F32),
            pltpu.VMEM((1, 3 * RW_WIDTH), F32),
            pltpu.VMEM((1, LORA_PAD), F32),
        ],
        compiler_params=pltpu.CompilerParams(
            dimension_semantics=("parallel", "arbitrary"), vmem_limit_bytes=VMEM_LIMIT),
        name="rwkv",
    )(proj, proj, proj, proj, proj, mu_rkv, w0, a0, w2cat, k_k, k_a, r_k, ln_w, ln_b)


ROUTE_E, ROUTE_G, ROUTE_R = 0, 2, 4
EXPERT_LANE0 = N_GROUPS


def _outproj_kernel(x_ref, oda_ref, orw_ref, wout_ref, g_ref, wrh_ref, wrl_ref, br_ref,
                    h_ref, xn_ref, route_ref, cnt_ref, base_ref, *, tm):
    @pl.when(pl.program_id(0) == 0)
    def _():
        base_ref[...] = jnp.zeros_like(base_ref)

    mix = _dot(oda_ref[...], wout_ref[0:DA_WIDTH, :]) + _dot(orw_ref[...], wout_ref[DA_WIDTH:, :])
    h = x_ref[...] + mix
    h_ref[...] = h
    ms = jnp.mean(h * h, axis=-1, keepdims=True)
    xn = h * lax.rsqrt(ms + NORM_EPS) * g_ref[...]
    x_hi = xn.astype(BF16)
    _store_row_tiles(xn_ref, _pack_halves(x_hi))

    x_lo = (xn - x_hi.astype(F32)).astype(BF16)
    hi_both = _dot(x_hi, jnp.concatenate([wrh_ref[...], wrl_ref[...]], axis=1))
    lg = (hi_both[:, 0:LANES] + _dot(x_lo, wrh_ref[...]) + hi_both[:, LANES:]) + br_ref[...]

    lane = lax.broadcasted_iota(jnp.int32, (tm, LANES), 1)
    big = jnp.int32(1 << 20)
    gl = jnp.where(lane < N_GROUPS, lg, NEG)
    gmax = jnp.max(gl, axis=-1, keepdims=True)
    grp = jnp.min(jnp.where(gl == gmax, lane, big), axis=-1, keepdims=True)
    p_grp = 1.0 / jnp.sum(jnp.exp(gl - gmax), axis=-1, keepdims=True)
    eg = jnp.where(lane >= EXPERT_LANE0, (lane - EXPERT_LANE0) // EXPERTS_PER_GROUP, -1)
    el = jnp.where(eg == grp, lg, NEG)
    emax = jnp.max(el, axis=-1, keepdims=True)
    pe = jnp.exp(el - emax)
    probs = pe / jnp.sum(pe, axis=-1, keepdims=True)
    probs = jnp.where(eg == grp, probs, -1.0)
    p1 = jnp.max(probs, axis=-1, keepdims=True)
    i1 = jnp.min(jnp.where(probs == p1, lane, big), axis=-1, keepdims=True)
    probs2 = jnp.where(lane == i1, -1.0, probs)
    p2 = jnp.max(probs2, axis=-1, keepdims=True)
    i2 = jnp.min(jnp.where(probs2 == p2, lane, big), axis=-1, keepdims=True)
    gate1 = p_grp * p1 / (p1 + p2)
    gate2 = p_grp * p2 / (p1 + p2)

    oh1 = jnp.where(lane == i1, 1.0, 0.0)
    oh2 = jnp.where(lane == i2, 1.0, 0.0)
    ri = lax.broadcasted_iota(jnp.int32, (tm, tm), 0)
    rj = lax.broadcasted_iota(jnp.int32, (tm, tm), 1)
    lower = jnp.where(ri > rj, 1.0, 0.0).astype(BF16)
    base = base_ref[...]
    tot1 = jnp.sum(oh1, axis=0, keepdims=True)
    before = _dot(lower, jnp.concatenate([oh1, oh2], axis=1).astype(BF16))
    c1 = base + before[:, 0:LANES]
    c2 = base + tot1 + before[:, LANES:]
    rank1 = jnp.sum(oh1 * c1, axis=-1, keepdims=True)
    rank2 = jnp.sum(oh2 * c2, axis=-1, keepdims=True)
    base = base + tot1 + jnp.sum(oh2, axis=0, keepdims=True)
    base_ref[...] = base
    cnt_ref[...] = base

    e1 = (i1 - EXPERT_LANE0).astype(F32)
    e2 = (i2 - EXPERT_LANE0).astype(F32)
    rec = jnp.zeros((tm, LANES), F32)
    for ln, val in ((ROUTE_E, e1), (ROUTE_E + 1, e2), (ROUTE_G, gate1), (ROUTE_G + 1, gate2),
                    (ROUTE_R, rank1), (ROUTE_R + 1, rank2)):
        rec = jnp.where(lane == ln, val, rec)
    route_ref[...] = rec


def _outproj(x2, o_da, o_rw, w_out, g_ffn, wr_hi, wr_lo, b_r, tm=512):
    t = x2.shape[0]
    const = lambda shape: pl.BlockSpec(shape, lambda i: (0, 0))
    return pl.pallas_call(
        functools.partial(_outproj_kernel, tm=tm),
        grid=(t // tm,),
        in_specs=[
            pl.BlockSpec((tm, D_MODEL), lambda i: (i, 0)),
            pl.BlockSpec((tm, DA_WIDTH), lambda i: (i, 0)),
            pl.BlockSpec((tm, RW_WIDTH), lambda i: (i, 0)),
            const((D_MODEL, D_MODEL)),
            const((1, D_MODEL)),
            const((D_MODEL, LANES)),
            const((D_MODEL, LANES)),
            const((1, LANES)),
        ],
        out_specs=[
            pl.BlockSpec((tm, D_MODEL), lambda i: (i, 0)),
            pl.BlockSpec((tm * SUBLANES, LANES), lambda i: (i, 0)),
            pl.BlockSpec((tm, LANES), lambda i: (i, 0)),
            const((1, LANES)),
        ],
        out_shape=[
            jax.ShapeDtypeStruct((t, D_MODEL), F32),
            jax.ShapeDtypeStruct((t * SUBLANES, LANES), jnp.uint32),
            jax.ShapeDtypeStruct((t, LANES), F32),
            jax.ShapeDtypeStruct((1, LANES), F32),
        ],
        scratch_shapes=[pltpu.VMEM((1, LANES), F32)],
        compiler_params=pltpu.CompilerParams(
            dimension_semantics=("arbitrary",), vmem_limit_bytes=VMEM_LIMIT),
        name="outproj_router",
    )(x2, o_da, o_rw, w_out, g_ffn, wr_hi, wr_lo, b_r)


def _row_copy(src_ref, dst_ref, src_row8, dst_row8, sem):
    src = src_ref.at[pl.ds(pl.multiple_of(src_row8, SUBLANES), SUBLANES)]
    dst = dst_ref.at[pl.ds(pl.multiple_of(dst_row8, SUBLANES), SUBLANES)]
    return pltpu.make_async_copy(src, dst, sem)


def _dispatch_kernel(dest_ref, x_ref, xb_in_hbm, xb_hbm, sem, *, td):
    del xb_in_hbm
    t0 = pl.program_id(0) * td

    def start(i, _):
        tok = t0 + i
        _row_copy(x_ref, xb_hbm, i * SUBLANES, dest_ref[2 * tok], sem).start(priority=0)
        _row_copy(x_ref, xb_hbm, i * SUBLANES, dest_ref[2 * tok + 1], sem).start(priority=1)
        return 0

    lax.fori_loop(0, td, start, 0, unroll=8)
    for _ in range(2):
        pltpu.make_async_copy(x_ref, xb_hbm.at[pl.ds(0, td * SUBLANES)], sem).wait()


def _dispatch(dest_flat, xn2, xb_zero, td=1024):
    t = xn2.shape[0] // SUBLANES
    td = min(td, t)
    return pl.pallas_call(
        functools.partial(_dispatch_kernel, td=td),
        grid_spec=pltpu.PrefetchScalarGridSpec(
            num_scalar_prefetch=1,
            grid=(t // td,),
            in_specs=[pl.BlockSpec((td * SUBLANES, LANES), lambda i, d: (i, 0)),
                      pl.BlockSpec(memory_space=pl.ANY)],
            out_specs=pl.BlockSpec(memory_space=pl.ANY),
            scratch_shapes=[pltpu.SemaphoreType.DMA],
        ),
        out_shape=jax.ShapeDtypeStruct(xb_zero.shape, xb_zero.dtype),
        input_output_aliases={2: 0},
        compiler_params=pltpu.CompilerParams(dimension_semantics=("arbitrary",)),
        name="dispatch",
    )(dest_flat, xn2, xb_zero)


EXPERT_BLOCKS_PER_STEP = 4


def _experts_kernel(be_ref, nu_ref, nxt_ref, slot_ref, xb_ref, wg_hbm, wu_hbm, wd_hbm, yb_ref,
                    wgf_ref, wuf_ref, wdf_ref, wgb_ref, wub_ref, wdb_ref, sem):
    block_rows = EXPERT_BLOCK * SUBLANES

    def weight_copies(e, slot):
        return [pltpu.make_async_copy(src.at[e], dst.at[slot], sem.at[slot])
                for src, dst in ((wg_hbm, wgf_ref), (wu_hbm, wuf_ref), (wd_hbm, wdf_ref))]

    def row_block(sub):
        i = pl.program_id(0) * EXPERT_BLOCKS_PER_STEP + sub
        x_ref = xb_ref.at[pl.ds(sub * block_rows, block_rows)]
        y_ref = yb_ref.at[pl.ds(sub * block_rows, block_rows)]
        used = i < nu_ref[0]
        new_expert = jnp.logical_or(i == 0, be_ref[i] != be_ref[jnp.maximum(i - 1, 0)])

        def swiglu(wg_lo, wg_hi, wu_lo, wu_hi, wd):
            x_lo, x_hi = (half.astype(BF16) for half in _unpack_halves(_load_row_tiles(x_ref)))
            hg = _dot(x_lo, wg_lo) + _dot(x_hi, wg_hi)
            hu = _dot(x_lo, wu_lo) + _dot(x_hi, wu_hi)
            hid = hg * _sigmoid(hg) * hu
            _store_row_tiles(y_ref, _pack_halves(_dot(hid.astype(BF16), wd).astype(BF16)))

        @pl.when(jnp.logical_and(used, new_expert))
        def _():
            slot = slot_ref[i]

            @pl.when(i == 0)
            def _():
                for cp in weight_copies(be_ref[i], slot):
                    cp.start()

            for cp in weight_copies(be_ref[i], slot):
                cp.wait()

            @pl.when(nxt_ref[i] >= 0)
            def _():
                for cp in weight_copies(nxt_ref[i], 1 - slot):
                    cp.start()

            wg, wu, wd = (w[slot].astype(BF16) for w in (wgf_ref, wuf_ref, wdf_ref))
            wgb_ref[...] = wg
            wub_ref[...] = wu
            wdb_ref[...] = wd
            swiglu(wg[0:HALF, :], wg[HALF:, :], wu[0:HALF, :], wu[HALF:, :], wd)

        @pl.when(jnp.logical_and(used, jnp.logical_not(new_expert)))
        def _():
            swiglu(wgb_ref[0:HALF, :], wgb_ref[HALF:, :], wub_ref[0:HALF, :], wub_ref[HALF:, :], wdb_ref[...])

        @pl.when(jnp.logical_not(used))
        def _():
            y_ref[...] = jnp.zeros_like(y_ref)

    for sub in range(EXPERT_BLOCKS_PER_STEP):
        row_block(sub)


def _experts(block_e, n_used, next_e, slot, xb, w_gate, w_up, w_down):
    step_rows = EXPERT_BLOCKS_PER_STEP * EXPERT_BLOCK * SUBLANES
    nb = xb.shape[0] // step_rows
    last_step = lambda nu: (nu[0] - 1) // EXPERT_BLOCKS_PER_STEP
    return pl.pallas_call(
        _experts_kernel,
        grid_spec=pltpu.PrefetchScalarGridSpec(
            num_scalar_prefetch=4,
            grid=(nb,),
            in_specs=[
                pl.BlockSpec((step_rows, LANES), lambda i, be, nu, *_: (jnp.minimum(i, last_step(nu)), 0)),
                pl.BlockSpec(memory_space=pl.ANY),
                pl.BlockSpec(memory_space=pl.ANY),
                pl.BlockSpec(memory_space=pl.ANY),
            ],
            out_specs=pl.BlockSpec((step_rows, LANES), lambda i, *_: (i, 0)),
            scratch_shapes=[
                pltpu.VMEM((2, D_MODEL, D_EXPERT), F32),
                pltpu.VMEM((2, D_MODEL, D_EXPERT), F32),
                pltpu.VMEM((2, D_EXPERT, D_MODEL), F32),
                pltpu.VMEM((D_MODEL, D_EXPERT), BF16),
                pltpu.VMEM((D_MODEL, D_EXPERT), BF16),
                pltpu.VMEM((D_EXPERT, D_MODEL), BF16),
                pltpu.SemaphoreType.DMA((2,)),
            ],
        ),
        out_shape=jax.ShapeDtypeStruct(xb.shape, jnp.uint32),
        compiler_params=pltpu.CompilerParams(
            dimension_semantics=("arbitrary",), vmem_limit_bytes=EXPERTS_VMEM_LIMIT),
        name="experts",
    )(block_e, n_used, next_e, slot, xb, w_gate, w_up, w_down)


def _combine_kernel(dest_ref, route_ref, h_ref, g_ref, yb_hbm, o_ref, y1_ref, y2_ref, sem, *, tc):
    step = pl.program_id(0)
    n_steps = pl.num_programs(0)

    def gather(tile, slot):
        def start(i, _):
            tok = tile * tc + i
            _row_copy(yb_hbm, y1_ref.at[slot], dest_ref[2 * tok], i * SUBLANES, sem.at[slot]).start(priority=0)
            _row_copy(yb_hbm, y2_ref.at[slot], dest_ref[2 * tok + 1], i * SUBLANES, sem.at[slot]).start(priority=1)
            return 0

        lax.fori_loop(0, tc, start, 0, unroll=8)

    @pl.when(step == 0)
    def _():
        gather(0, 0)

    @pl.when(step + 1 < n_steps)
    def _():
        gather(step + 1, (step + 1) % 2)

    slot = step % 2
    pltpu.make_async_copy(yb_hbm.at[pl.ds(0, tc * SUBLANES)], y1_ref.at[slot], sem.at[slot]).wait()
    pltpu.make_async_copy(yb_hbm.at[pl.ds(0, tc * SUBLANES)], y2_ref.at[slot], sem.at[slot]).wait()

    rec = route_ref[...]
    lane = lax.broadcasted_iota(jnp.int32, (tc, LANES), 1)
    g1 = jnp.sum(jnp.where(lane == ROUTE_G, rec, 0.0), axis=-1, keepdims=True)
    g2 = jnp.sum(jnp.where(lane == ROUTE_G + 1, rec, 0.0), axis=-1, keepdims=True)
    y1_lo, y1_hi = _unpack_halves(_load_row_tiles(y1_ref.at[slot]))
    y2_lo, y2_hi = _unpack_halves(_load_row_tiles(y2_ref.at[slot]))
    h_lo = h_ref[:, 0:HALF] + (y1_lo * g1 + y2_lo * g2)
    h_hi = h_ref[:, HALF:] + (y1_hi * g1 + y2_hi * g2)
    ms = (jnp.sum(h_lo * h_lo, axis=-1, keepdims=True)
          + jnp.sum(h_hi * h_hi, axis=-1, keepdims=True)) * (1.0 / D_MODEL)
    scale = lax.rsqrt(ms + NORM_EPS)
    o_ref[:, 0:HALF] = h_lo * scale * g_ref[:, 0:HALF]
    o_ref[:, HALF:] = h_hi * scale * g_ref[:, HALF:]


def _combine(dest_flat, route, h1, g_final, yb, tc=256):
    t = h1.shape[0]
    return pl.pallas_call(
        functools.partial(_combine_kernel, tc=tc),
        grid_spec=pltpu.PrefetchScalarGridSpec(
            num_scalar_prefetch=1,
            grid=(t // tc,),
            in_specs=[
                pl.BlockSpec((tc, LANES), lambda i, d: (i, 0)),
                pl.BlockSpec((tc, D_MODEL), lambda i, d: (i, 0)),
                pl.BlockSpec((1, D_MODEL), lambda i, d: (0, 0)),
                pl.BlockSpec(memory_space=pl.ANY),
            ],
            out_specs=pl.BlockSpec((tc, D_MODEL), lambda i, d: (i, 0)),
            scratch_shapes=[
                pltpu.VMEM((2, tc * SUBLANES, LANES), jnp.uint32),
                pltpu.VMEM((2, tc * SUBLANES, LANES), jnp.uint32),
                pltpu.SemaphoreType.DMA((2,)),
            ],
        ),
        out_shape=jax.ShapeDtypeStruct((t, D_MODEL), F32),
        compiler_params=pltpu.CompilerParams(
            dimension_semantics=("arbitrary",), vmem_limit_bytes=VMEM_LIMIT),
        name="combine",
    )(dest_flat, route, h1, g_final, yb)


def _pad_cols(w, n):
    return jnp.pad(w, ((0, 0), (0, n - w.shape[1])))


def kernel(x, g_mix, w_in, w_out, da_lambda_q1, da_lambda_k1, da_lambda_q2, da_lambda_k2, da_subln_g,
           rw_mu_x, rw_mu_rkv, rw_w0, rw_w1, rw_w2, rw_a0, rw_a1, rw_a2, rw_g1, rw_g2, rw_k_k, rw_k_a,
           rw_r_k, rw_ln_w, rw_ln_b, g_ffn, moe_w_group, moe_b_group, moe_w_expert, moe_b_expert,
           moe_w_gate, moe_w_up, moe_w_down, g_final):
    b, s, d = x.shape
    t = b * s
    x2 = x.reshape(t, d)

    mu = rw_mu_x[0]
    lora = (rw_w1[0], rw_a1[0], rw_g1[0])
    keep = _pad_cols(jnp.concatenate([(1.0 - mu[i])[:, None] * w for i, w in enumerate(lora)], axis=1), LORA_PAD)
    prev = _pad_cols(jnp.concatenate([mu[i][:, None] * w for i, w in enumerate(lora)], axis=1), LORA_PAD)
    w_all = jnp.concatenate([w_in[0], keep, prev], axis=1).astype(BF16)
    proj = _inproj(x2, g_mix, w_all)

    slopes = jnp.asarray([2.0 ** (-8.0 * (i + 1) / DA_HEADS) for i in range(DA_HEADS)], F32)
    lam4 = jnp.concatenate([da_lambda_q1, da_lambda_k1, da_lambda_q2, da_lambda_k2], axis=0)
    o_da = _diffattn(proj, slopes, lam4, da_subln_g.reshape(DA_V_DIM, 1), b, s)

    w2cat = jnp.zeros((LORA_PAD, 3 * RW_WIDTH), F32)
    w2cat = w2cat.at[0:LORA_W, 0:RW_WIDTH].set(rw_w2[0])
    w2cat = w2cat.at[LORA_W:LORA_W + LORA_A, RW_WIDTH:2 * RW_WIDTH].set(rw_a2[0])
    w2cat = w2cat.at[LORA_W + LORA_A:LORA_W + LORA_A + LORA_G, 2 * RW_WIDTH:].set(rw_g2[0])
    o_rw = _rwkv(proj, rw_mu_rkv[0], rw_w0, rw_a0, w2cat.astype(BF16), rw_k_k, rw_k_a,
                 rw_r_k.reshape(1, RW_WIDTH), rw_ln_w, rw_ln_b, b, s)

    w_r = _pad_cols(jnp.concatenate([moe_w_group[0], moe_w_expert[0]], axis=1), LANES)
    wr_hi = w_r.astype(BF16)
    wr_lo = (w_r - wr_hi.astype(F32)).astype(BF16)
    b_r = _pad_cols(jnp.concatenate([moe_b_group[0], moe_b_expert[0].reshape(-1)])[None, :], LANES)
    h1, xn2, route, cnt = _outproj(x2, o_da, o_rw, w_out[0].astype(BF16), g_ffn, wr_hi, wr_lo, b_r)

    counts = cnt[0, EXPERT_LANE0:EXPERT_LANE0 + N_EXPERTS].astype(jnp.int32)
    pcounts = ((counts + EXPERT_BLOCK - 1) // EXPERT_BLOCK) * EXPERT_BLOCK
    expert_ids = jnp.arange(N_EXPERTS, dtype=jnp.int32)
    upto = expert_ids[None, :] <= expert_ids[:, None]
    pends = jnp.sum(jnp.where(upto, pcounts[None, :], 0), axis=1)
    pstarts = pends - pcounts
    e_idx = route[:, ROUTE_E:ROUTE_E + 2].astype(jnp.int32)
    rank = route[:, ROUTE_R:ROUTE_R + 2].astype(jnp.int32)
    row0 = jnp.sum(jnp.where(e_idx[..., None] == expert_ids, pstarts, 0), axis=-1)
    dest = ((row0 + rank) * SUBLANES).reshape(-1)
    n_blocks = (2 * t) // EXPERT_BLOCK + N_EXPERTS
    block_row0 = jnp.arange(n_blocks, dtype=jnp.int32) * EXPERT_BLOCK
    block_e = jnp.minimum(jnp.sum(pends[None, :] <= block_row0[:, None], axis=1), N_EXPERTS - 1).astype(jnp.int32)
    n_used = (pends[-1:] // EXPERT_BLOCK).astype(jnp.int32)
    has_rows = counts > 0
    later = (expert_ids[None, :] > expert_ids[:, None]) & has_rows[None, :]
    next_used = jnp.min(jnp.where(later, expert_ids[None, :], N_EXPERTS), axis=1)
    next_used = jnp.where(next_used == N_EXPERTS, -1, next_used).astype(jnp.int32)
    buffer_id = ((jnp.sum(jnp.where(upto & has_rows[None, :], 1, 0), axis=1) - 1) % 2).astype(jnp.int32)

    xb = _dispatch(dest, xn2, jnp.zeros((n_blocks * EXPERT_BLOCK * SUBLANES, LANES), jnp.uint32))
    yb = _experts(block_e, n_used, next_used[block_e], buffer_id[block_e], xb,
                  moe_w_gate[0], moe_w_up[0], moe_w_down[0])
    out = _combine(dest, route, h1, g_final[None, :], yb)
    return out.reshape(b, s, d)
```

```python
import functools
import math

import jax
import jax.numpy as jnp
from jax import lax
from jax.experimental import pallas as pl
from jax.experimental.pallas import tpu as pltpu

F32 = jnp.float32
BF16 = jnp.bfloat16

D_MODEL = 2048
DA_WIDTH = 1024
RW_WIDTH = 1024
DA_HEADS = 8
DA_V_DIM = 128
DA_QK_DIM = 64
RW_N = 64
RW_PAIRS = RW_WIDTH // 128
LORA_W, LORA_A, LORA_G = 64, 64, 160
LORA_PAD = 384
IN_COLS = 3 * DA_WIDTH + 3 * RW_WIDTH
NC = IN_COLS + 2 * LORA_PAD
N_GROUPS = 4
EXPERTS_PER_GROUP = 8
N_EXPERTS = 32
D_EXPERT = 512
NORM_EPS = 1e-6
LN_X_EPS = 64e-5
LAM_INIT = 0.8 - 0.6 * math.exp(-0.3 * 0)
LANES = 128
CHUNK = 64
RW_TILE = 256
NCH = RW_TILE // CHUNK
EXPERT_BLOCK = 256
NEG = -1e30
VMEM_LIMIT = 48 * 1024 * 1024
EXPERTS_VMEM_LIMIT = 56 * 1024 * 1024


def _dot(a, b):
    return jnp.dot(a, b, preferred_element_type=F32)


def _dot_nt(a, b):
    return lax.dot_general(a, b, (((1,), (1,)), ((), ())), preferred_element_type=F32)


def _dot_tn(a, b):
    return lax.dot_general(a, b, (((0,), (0,)), ((), ())), preferred_element_type=F32)


def _sigmoid(x):
    return 1.0 / (1.0 + jnp.exp(-x))


HALF = D_MODEL // 2


def _pack_halves(xb):
    lo = lax.bitcast_convert_type(xb[:, :HALF].astype(F32), jnp.uint32) >> 16
    hi = lax.bitcast_convert_type(xb[:, HALF:].astype(F32), jnp.uint32) & jnp.uint32(0xFFFF0000)
    return hi | lo


def _unpack_halves(u):
    lo = lax.bitcast_convert_type(u << 16, F32)
    hi = lax.bitcast_convert_type(u & jnp.uint32(0xFFFF0000), F32)
    return lo, hi


SUBLANES = 8
assert HALF == SUBLANES * LANES


def _store_row_tiles(ref, u):
    n = u.shape[0]
    for k in range(SUBLANES):
        ref[pl.ds(k, n, stride=SUBLANES), :] = u[:, k * LANES:(k + 1) * LANES]


def _load_row_tiles(ref):
    n = ref.shape[0] // SUBLANES
    return jnp.concatenate([ref[pl.ds(k, n, stride=SUBLANES), :] for k in range(SUBLANES)], axis=1)


def _inproj_kernel(x_ref, g_ref, w_ref, o_ref, xn_ref):
    @pl.when(pl.program_id(1) == 0)
    def _():
        x = x_ref[...]
        ms = jnp.mean(x * x, axis=-1, keepdims=True)
        xn_ref[...] = (x * lax.rsqrt(ms + NORM_EPS) * g_ref[...]).astype(BF16)

    o_ref[...] = _dot(xn_ref[...], w_ref[...]).astype(o_ref.dtype)


def _inproj(x2, g, w_all, tm=1024, tn=2304):
    t = x2.shape[0]
    tm = min(tm, t)
    vmem_limit = 2 * (tm * D_MODEL * 4 + D_MODEL * tn * 2 + tm * tn * 2) + tm * D_MODEL * 2 + (6 << 20)
    return pl.pallas_call(
        _inproj_kernel,
        grid=(t // tm, NC // tn),
        in_specs=[
            pl.BlockSpec((tm, D_MODEL), lambda i, j: (i, 0)),
            pl.BlockSpec((1, D_MODEL), lambda i, j: (0, 0)),
            pl.BlockSpec((D_MODEL, tn), lambda i, j: (0, j)),
        ],
        out_specs=pl.BlockSpec((tm, tn), lambda i, j: (i, j)),
        out_shape=jax.ShapeDtypeStruct((t, NC), BF16),
        scratch_shapes=[pltpu.VMEM((tm, D_MODEL), BF16)],
        compiler_params=pltpu.CompilerParams(
            dimension_semantics=("parallel", "arbitrary"), vmem_limit_bytes=vmem_limit),
        name="inproj",
    )(x2, g, w_all)


DA_HEADS_PER_STEP = 2


def _da_kernel(qi_tab, kj_tab, slopes_ref, lam_ref, subg_ref, q_ref, k_ref, v_ref, o_ref,
               vt_ref, qz_ref, bias_ref, m_ref, l_ref, lfin_ref, acc_ref, s_ref, p_ref, alpha_ref,
               *, tq, seq):
    tk = tq
    nq = seq // tq
    n_trips = nq * (nq + 1) // 2
    heads = range(DA_HEADS_PER_STEP)
    chains = [(hd, c) for hd in heads for c in range(2)]
    slope = [slopes_ref[pl.program_id(1) * DA_HEADS_PER_STEP + hd] for hd in heads]
    cols = lambda hd: slice(hd * LANES, (hd + 1) * LANES)

    lane = lax.broadcasted_iota(jnp.int32, (tq, LANES), 1)
    krow = lax.broadcasted_iota(jnp.int32, (tk, tq), 0)
    qcol = lax.broadcasted_iota(jnp.int32, (tk, tq), 1)
    for hd in heads:
        for blk in range(seq // LANES):
            r = slice(blk * LANES, (blk + 1) * LANES)
            vt_ref[hd, :, r] = v_ref[r, cols(hd)].astype(F32).T.astype(BF16)
        for blk in range(nq):
            r = slice(blk * tq, (blk + 1) * tq)
            q = q_ref[r, cols(hd)] * jnp.asarray(DA_QK_DIM ** -0.5, BF16)
            qz_ref[2 * hd, r, :] = jnp.where(lane < DA_QK_DIM, q, jnp.zeros_like(q))
            qz_ref[2 * hd + 1, r, :] = jnp.where(lane >= DA_QK_DIM, q, jnp.zeros_like(q))
        alibi = slope[hd] * krow.astype(F32)
        bias_ref[hd, 0] = alibi
        bias_ref[hd, 1] = jnp.where(krow <= qcol, alibi, NEG)

    m_ref[...] = jnp.full_like(m_ref, NEG)
    l_ref[...] = jnp.zeros_like(l_ref)
    lfin_ref[...] = jnp.ones_like(lfin_ref)
    acc_ref[...] = jnp.zeros_like(acc_ref)
    p_ref[...] = jnp.zeros_like(p_ref)
    alpha_ref[...] = jnp.zeros_like(alpha_ref)

    lv = lam_ref[...]
    lam = (jnp.exp(jnp.sum(lv[0:1] * lv[1:2], axis=-1, keepdims=True))
           - jnp.exp(jnp.sum(lv[2:3] * lv[3:4], axis=-1, keepdims=True)) + LAM_INIT)

    def scores_to_scratch(t):
        qs = pl.multiple_of(qi_tab[t] * tq, tq)
        ks = pl.multiple_of(kj_tab[t] * tk, tk)
        diag = (qi_tab[t] == kj_tab[t]).astype(jnp.int32)
        for hd, c in chains:
            ch = 2 * hd + c
            s_ref[ch] = (_dot_nt(k_ref[pl.ds(ks, tk), cols(hd)], qz_ref[ch, pl.ds(qs, tq), :])
                         + bias_ref[hd, diag])

    def values_from_scratch(t):
        ks = pl.multiple_of(kj_tab[t] * tk, tk)
        for hd, c in chains:
            ch = 2 * hd + c
            acc_ref[ch] = alpha_ref[ch:ch + 1, :] * acc_ref[ch] + _dot(vt_ref[hd, :, pl.ds(ks, tk)], p_ref[ch])

    def softmax(t):
        first = kj_tab[t] == 0
        diag = kj_tab[t] == qi_tab[t]
        rel = ((kj_tab[t] - qi_tab[t]) * tk).astype(F32)
        p, alpha = [], []
        for hd, c in chains:
            ch = 2 * hd + c
            off = slope[hd] * rel
            s = s_ref[ch]
            m = jnp.where(first, NEG, m_ref[ch:ch + 1, :])
            l = jnp.where(first, 0.0, l_ref[ch:ch + 1, :])
            m_new = jnp.maximum(m, jnp.max(s, axis=0, keepdims=True) + off)
            pc = jnp.exp(s - (m_new - off))
            a = jnp.exp(m - m_new)
            l_new = a * l + jnp.sum(pc, axis=0, keepdims=True)
            m_ref[ch:ch + 1, :] = m_new
            l_ref[ch:ch + 1, :] = l_new
            lfin_ref[ch:ch + 1, :] = jnp.where(diag, l_new, lfin_ref[ch:ch + 1, :])
            p.append(pc)
            alpha.append(a)
        return p, alpha

    def publish(p, alpha):
        for ch in range(len(chains)):
            alpha_ref[ch:ch + 1, :] = alpha[ch]
            p_ref[ch] = p[ch].astype(BF16)

    def finalize(t):
        qs = pl.multiple_of(qi_tab[t] * tq, tq)
        for hd in heads:
            c0, c1 = 2 * hd, 2 * hd + 1
            o = acc_ref[c0] / lfin_ref[c0:c0 + 1, :] - lam * (acc_ref[c1] / lfin_ref[c1:c1 + 1, :])
            ms = jnp.mean(o * o, axis=0, keepdims=True)
            o = o * lax.rsqrt(ms + NORM_EPS) * subg_ref[...] * (1.0 - LAM_INIT)
            o_ref[pl.ds(qs, tq), cols(hd)] = o.T.astype(o_ref.dtype)

    scores_to_scratch(0)

    def body(t, _):
        prev = jnp.maximum(t - 1, 0)
        p, alpha = softmax(t)
        values_from_scratch(prev)
        scores_to_scratch(jnp.minimum(t + 1, n_trips - 1))
        publish(p, alpha)

        @pl.when(jnp.logical_and(t > 0, qi_tab[prev] == kj_tab[prev]))
        def _():
            finalize(prev)
        return 0

    lax.fori_loop(0, n_trips, body, 0)
    values_from_scratch(n_trips - 1)
    finalize(n_trips - 1)


def _diffattn(proj, slopes, lam4, subg_col, b, s, tq=256):
    nq = s // tq
    nh = DA_HEADS_PER_STEP
    groups = DA_HEADS // nh
    width = nh * LANES
    pairs = [(qi, kj) for qi in range(nq) for kj in range(qi + 1)]
    qi_tab = jnp.asarray([p[0] for p in pairs], jnp.int32)
    kj_tab = jnp.asarray([p[1] for p in pairs], jnp.int32)
    return pl.pallas_call(
        functools.partial(_da_kernel, tq=tq, seq=s),
        grid_spec=pltpu.PrefetchScalarGridSpec(
            num_scalar_prefetch=2,
            grid=(b, groups),
            in_specs=[
                pl.BlockSpec(memory_space=pltpu.SMEM),
                pl.BlockSpec((4, DA_QK_DIM), lambda bi, g, *_: (0, 0)),
                pl.BlockSpec((DA_V_DIM, 1), lambda bi, g, *_: (0, 0)),
                pl.BlockSpec((s, width), lambda bi, g, *_: (bi, g)),
                pl.BlockSpec((s, width), lambda bi, g, *_: (bi, groups + g)),
                pl.BlockSpec((s, width), lambda bi, g, *_: (bi, 2 * groups + g)),
            ],
            out_specs=pl.BlockSpec((s, width), lambda bi, g, *_: (bi, g)),
            scratch_shapes=[
                pltpu.VMEM((nh, DA_V_DIM, s), BF16),
                pltpu.VMEM((2 * nh, s, LANES), BF16),
                pltpu.VMEM((nh, 2, tq, tq), F32),
                pltpu.VMEM((2 * nh, tq), F32),
                pltpu.VMEM((2 * nh, tq), F32),
                pltpu.VMEM((2 * nh, tq), F32),
                pltpu.VMEM((2 * nh, DA_V_DIM, tq), F32),
                pltpu.VMEM((2 * nh, tq, tq), F32),
                pltpu.VMEM((2 * nh, tq, tq), BF16),
                pltpu.VMEM((2 * nh, tq), F32),
            ],
        ),
        out_shape=jax.ShapeDtypeStruct((b * s, DA_WIDTH), BF16),
        compiler_params=pltpu.CompilerParams(
            dimension_semantics=("parallel", "arbitrary"), vmem_limit_bytes=VMEM_LIMIT),
        name="diffattn",
    )(qi_tab, kj_tab, slopes, lam4, subg_col, proj, proj, proj)


def _rwkv_kernel(r_ref, k_ref, v_ref, p1_ref, p2_ref, mu_ref, w0_ref, a0_ref, w2_ref, kk_ref, ka_ref,
                 rk_ref, lnw_ref, lnb_ref, o_ref, state_ref, crkv_ref, cp2_ref):
    L, TT = CHUNK, RW_TILE
    c = pl.program_id(1)

    @pl.when(c == 0)
    def _():
        state_ref[...] = jnp.zeros_like(state_ref)
        crkv_ref[...] = jnp.zeros_like(crkv_ref)
        cp2_ref[...] = jnp.zeros_like(cp2_ref)

    first = lax.broadcasted_iota(jnp.int32, (TT, 1), 0) == 0

    def shift(x, carry):
        return jnp.where(first, carry, pltpu.roll(x, 1, 0))

    r_raw = r_ref[...].astype(F32)
    k_raw = k_ref[...].astype(F32)
    v_raw = v_ref[...].astype(F32)
    p2 = p2_ref[...].astype(F32)
    rs = shift(r_raw, crkv_ref[:, 0:RW_WIDTH])
    ks = shift(k_raw, crkv_ref[:, RW_WIDTH:2 * RW_WIDTH])
    vs = shift(v_raw, crkv_ref[:, 2 * RW_WIDTH:3 * RW_WIDTH])
    lin = p1_ref[...].astype(F32) + shift(p2, cp2_ref[...])
    crkv_ref[:, 0:RW_WIDTH] = r_raw[TT - 1:TT, :]
    crkv_ref[:, RW_WIDTH:2 * RW_WIDTH] = k_raw[TT - 1:TT, :]
    crkv_ref[:, 2 * RW_WIDTH:3 * RW_WIDTH] = v_raw[TT - 1:TT, :]
    cp2_ref[...] = p2[TT - 1:TT, :]

    r = r_raw + (rs - r_raw) * mu_ref[0:1, :]
    k = k_raw + (ks - k_raw) * mu_ref[1:2, :]
    v = v_raw + (vs - v_raw) * mu_ref[2:3, :]

    ll = lax.broadcasted_iota(jnp.int32, (TT, LORA_PAD), 1)
    z = jnp.where(ll < LORA_W, jnp.tanh(lin),
                  jnp.where(ll < LORA_W + LORA_A, lin,
                            jnp.where(ll < LORA_W + LORA_A + LORA_G, _sigmoid(lin), 0.0)))
    d = _dot(z.astype(BF16), w2_ref[...])
    logw = -_sigmoid(w0_ref[...] + d[:, 0:RW_WIDTH]) * math.exp(-0.5)
    a_sig = _sigmoid(a0_ref[...] + d[:, RW_WIDTH:2 * RW_WIDTH])
    g = d[:, 2 * RW_WIDTH:3 * RW_WIDTH]

    li = lax.broadcasted_iota(jnp.int32, (LANES, LANES), 0)
    lj = lax.broadcasted_iota(jnp.int32, (LANES, LANES), 1)
    same_head = jnp.where((li // RW_N) == (lj // RW_N), 1.0, 0.0).astype(BF16)

    def head_sum(x):
        xb = x.astype(BF16)
        return jnp.concatenate(
            [_dot(xb[:, p * LANES:(p + 1) * LANES], same_head) for p in range(RW_PAIRS)], axis=1)

    kk = k * kk_ref[...]
    kk = kk * lax.rsqrt(jnp.maximum(head_sum(kk * kk), 1e-24))
    k = k * (1.0 + (a_sig - 1.0) * ka_ref[...])
    bonus = head_sum(r * k * rk_ref[...])

    ti = lax.broadcasted_iota(jnp.int32, (TT, TT), 0)
    tj = lax.broadcasted_iota(jnp.int32, (TT, TT), 1)
    tri = jnp.where(ti >= tj, jnp.where((ti // L) == (tj // L), 1.0, 0.0), 0.0).astype(BF16)
    w_hi = logw.astype(BF16)
    w_r1 = logw - w_hi.astype(F32)
    w_mid = w_r1.astype(BF16)
    w_lo = (w_r1 - w_mid.astype(F32)).astype(BF16)
    cw = _dot(tri, w_hi) + _dot(tri, w_mid) + _dot(tri, w_lo)
    cw_last = [cw[(ci + 1) * L - 1:(ci + 1) * L, :] for ci in range(NCH)]
    e_in = jnp.exp(cw)
    e_out = jnp.exp(-cw)
    e_end = jnp.exp(jnp.concatenate([jnp.broadcast_to(cl, (L, RW_WIDTH)) for cl in cw_last], axis=0) - cw)
    w_end = [jnp.exp(cl) for cl in cw_last]

    at = -kk * jnp.exp(cw - logw)
    rt = r * e_in
    b = kk * a_sig
    bt = b * e_out
    kt = k * e_out
    bd = b * e_end
    kd = k * e_end

    lane = lax.broadcasted_iota(jnp.int32, (L, LANES), 1)
    lo = lane < RW_N

    def stack(x, ci, p):
        xs = x[ci * L:(ci + 1) * L, p * LANES:(p + 1) * LANES]
        return jnp.concatenate([jnp.where(lo, xs, 0.0), jnp.where(lo, 0.0, xs)], axis=0).astype(BF16)

    si = lax.broadcasted_iota(jnp.int32, (2 * L, 2 * L), 0)
    sj = lax.broadcasted_iota(jnp.int32, (2 * L, 2 * L), 1)
    strict = (si % L) > (sj % L)
    incl = (si % L) >= (sj % L)
    eye = jnp.where(si == sj, 1.0, 0.0)

    pairs = range(RW_PAIRS)
    cat0 = lambda *xs: jnp.concatenate(xs, axis=0)
    cat1 = lambda *xs: jnp.concatenate(xs, axis=1)
    bf = lambda x: x.astype(BF16)
    def chunk_setup(ci):
        a_s = [stack(at, ci, p) for p in pairs]
        r_s = [stack(rt, ci, p) for p in pairs]
        v_s = [stack(v, ci, p) for p in pairs]
        bk = [cat0(stack(bt, ci, p), stack(kt, ci, p)) for p in pairs]
        bkd = [cat0(stack(bd, ci, p), stack(kd, ci, p)) for p in pairs]
        aa = [_dot_nt(a_s[p], bk[p]) for p in pairs]
        rr = [_dot_nt(r_s[p], bk[p]) for p in pairs]
        a_ab = [jnp.where(strict, aa[p][:, 0:2 * L], 0.0) for p in pairs]
        a_ak = [bf(jnp.where(strict, aa[p][:, 2 * L:4 * L], 0.0)) for p in pairs]
        a_rb = [bf(jnp.where(incl, rr[p][:, 0:2 * L], 0.0)) for p in pairs]
        a_rk = [bf(jnp.where(incl, rr[p][:, 2 * L:4 * L], 0.0)) for p in pairs]
        ab = [bf(a) for a in a_ab]
        qpow = [_dot(ab[p], ab[p]) for p in pairs]
        inv = [eye + a_ab[p] for p in pairs]
        for _ in range(int(math.log2(L)) - 2):
            qb = [bf(q) for q in qpow]
            res = [_dot(qb[p], cat1(qb[p], bf(inv[p]))) for p in pairs]
            qpow = [res[p][:, 0:2 * L] for p in pairs]
            inv = [inv[p] + res[p][:, 2 * L:4 * L] for p in pairs]
        inv = [bf(inv[p] + _dot(bf(qpow[p]), bf(inv[p]))) for p in pairs]
        w_col = [jnp.broadcast_to(w_end[ci][:, p * LANES:(p + 1) * LANES], (LANES, LANES)).T for p in pairs]
        return a_s, r_s, v_s, bkd, a_ak, a_rb, a_rk, inv, w_col

    def chunk_advance(setup, h):
        a_s, r_s, v_s, bkd, a_ak, a_rb, a_rk, inv, w_col = setup
        h_b = [bf(hp) for hp in h]
        x = [_dot(cat1(a_s[p], a_ak[p]), cat0(h_b[p], v_s[p])) for p in pairs]
        u_b = [bf(_dot(inv[p], bf(x[p]))) for p in pairs]
        y2 = [_dot(cat1(r_s[p], a_rb[p], a_rk[p]), cat0(h_b[p], u_b[p], v_s[p])) for p in pairs]
        h = [w_col[p] * h[p] + _dot_tn(bkd[p], cat0(u_b[p], v_s[p])) for p in pairs]
        return h, cat1(*[y2[p][0:L] + y2[p][L:2 * L] for p in pairs])

    h = [state_ref[p] for p in pairs]
    ys = []
    setup = chunk_setup(0)
    for ci in range(NCH):
        ahead = chunk_setup(ci + 1) if ci + 1 < NCH else None
        h, y_chunk = chunk_advance(setup, h)
        ys.append(y_chunk)
        setup = ahead
    for p in pairs:
        state_ref[p] = h[p]
    y = cat0(*ys)

    mean = head_sum(y) * (1.0 / RW_N)
    yc = y - mean
    var = head_sum(yc * yc) * (1.0 / RW_N)
    y = yc * lax.rsqrt(var + LN_X_EPS) * lnw_ref[...] + lnb_ref[...]
    y = y + bonus * v
    o_ref[...] = (y * g).astype(o_ref.dtype)


def _rwkv(proj, mu_rkv, w0, a0, w2cat, k_k, k_a, r_k, ln_w, ln_b, b, s):
    tt = RW_TILE
    nc = s // tt
    row = lambda n: pl.BlockSpec((1, n), lambda bi, c: (0, 0))
    rkv_col0 = 3 * DA_WIDTH // RW_WIDTH
    lora_col0 = IN_COLS // LORA_PAD
    return pl.pallas_call(
        _rwkv_kernel,
        grid=(b, nc),
        in_specs=[
            pl.BlockSpec((tt, RW_WIDTH), lambda bi, c: (bi * nc + c, rkv_col0)),
            pl.BlockSpec((tt, RW_WIDTH), lambda bi, c: (bi * nc + c, rkv_col0 + 1)),
            pl.BlockSpec((tt, RW_WIDTH), lambda bi, c: (bi * nc + c, rkv_col0 + 2)),
            pl.BlockSpec((tt, LORA_PAD), lambda bi, c: (bi * nc + c, lora_col0)),
            pl.BlockSpec((tt, LORA_PAD), lambda bi, c: (bi * nc + c, lora_col0 + 1)),
            pl.BlockSpec((3, RW_WIDTH), lambda bi, c: (0, 0)),
            row(RW_WIDTH), row(RW_WIDTH),
            pl.BlockSpec((LORA_PAD, 3 * RW_WIDTH), lambda bi, c: (0, 0)),
            row(RW_WIDTH), row(RW_WIDTH), row(RW_WIDTH), row(RW_WIDTH), row(RW_WIDTH),
        ],
        out_specs=pl.BlockSpec((tt, RW_WIDTH), lambda bi, c: (bi * nc + c, 0)),
        out_shape=jax.ShapeDtypeStruct((b * s, RW_WIDTH), BF16),
        scratch_shapes=[
            pltpu.VMEM((RW_PAIRS, LANES, LANES), F32),
            pltpu.VMEM((1, 3 * RW_WIDTH), F32),
            pltpu.VMEM((1, LORA_PAD), F32),
        ],
        compiler_params=pltpu.CompilerParams(
            dimension_semantics=("parallel", "arbitrary"), vmem_limit_bytes=VMEM_LIMIT),
        name="rwkv",
    )(proj, proj, proj, proj, proj, mu_rkv, w0, a0, w2cat, k_k, k_a, r_k, ln_w, ln_b)


ROUTE_E, ROUTE_G, ROUTE_R = 0, 2, 4
EXPERT_LANE0 = N_GROUPS


def _outproj_kernel(x_ref, oda_ref, orw_ref, wout_ref, g_ref, wrh_ref, wrl_ref, br_ref,
                    h_ref, xn_ref, route_ref, cnt_ref, base_ref, *, tm):
    @pl.when(pl.program_id(0) == 0)
    def _():
        base_ref[...] = jnp.zeros_like(base_ref)

    mix = _dot(oda_ref[...], wout_ref[0:DA_WIDTH, :]) + _dot(orw_ref[...], wout_ref[DA_WIDTH:, :])
    h = x_ref[...] + mix
    h_ref[...] = h
    ms = jnp.mean(h * h, axis=-1, keepdims=True)
    xn = h * lax.rsqrt(ms + NORM_EPS) * g_ref[...]
    x_hi = xn.astype(BF16)
    _store_row_tiles(xn_ref, _pack_halves(x_hi))

    x_lo = (xn - x_hi.astype(F32)).astype(BF16)
    hi_both = _dot(x_hi, jnp.concatenate([wrh_ref[...], wrl_ref[...]], axis=1))
    lg = (hi_both[:, 0:LANES] + _dot(x_lo, wrh_ref[...]) + hi_both[:, LANES:]) + br_ref[...]

    lane = lax.broadcasted_iota(jnp.int32, (tm, LANES), 1)
    big = jnp.int32(1 << 20)
    gl = jnp.where(lane < N_GROUPS, lg, NEG)
    gmax = jnp.max(gl, axis=-1, keepdims=True)
    grp = jnp.min(jnp.where(gl == gmax, lane, big), axis=-1, keepdims=True)
    p_grp = 1.0 / jnp.sum(jnp.exp(gl - gmax), axis=-1, keepdims=True)
    eg = jnp.where(lane >= EXPERT_LANE0, (lane - EXPERT_LANE0) // EXPERTS_PER_GROUP, -1)
    el = jnp.where(eg == grp, lg, NEG)
    emax = jnp.max(el, axis=-1, keepdims=True)
    pe = jnp.exp(el - emax)
    probs = pe / jnp.sum(pe, axis=-1, keepdims=True)
    probs = jnp.where(eg == grp, probs, -1.0)
    p1 = jnp.max(probs, axis=-1, keepdims=True)
    i1 = jnp.min(jnp.where(probs == p1, lane, big), axis=-1, keepdims=True)
    probs2 = jnp.where(lane == i1, -1.0, probs)
    p2 = jnp.max(probs2, axis=-1, keepdims=True)
    i2 = jnp.min(jnp.where(probs2 == p2, lane, big), axis=-1, keepdims=True)
    gate1 = p_grp * p1 / (p1 + p2)
    gate2 = p_grp * p2 / (p1 + p2)

    oh1 = jnp.where(lane == i1, 1.0, 0.0)
    oh2 = jnp.where(lane == i2, 1.0, 0.0)
    ri = lax.broadcasted_iota(jnp.int32, (tm, tm), 0)
    rj = lax.broadcasted_iota(jnp.int32, (tm, tm), 1)
    lower = jnp.where(ri > rj, 1.0, 0.0).astype(BF16)
    base = base_ref[...]
    tot1 = jnp.sum(oh1, axis=0, keepdims=True)
    before = _dot(lower, jnp.concatenate([oh1, oh2], axis=1).astype(BF16))
    c1 = base + before[:, 0:LANES]
    c2 = base + tot1 + before[:, LANES:]
    rank1 = jnp.sum(oh1 * c1, axis=-1, keepdims=True)
    rank2 = jnp.sum(oh2 * c2, axis=-1, keepdims=True)
    base = base + tot1 + jnp.sum(oh2, axis=0, keepdims=True)
    base_ref[...] = base
    cnt_ref[...] = base

    e1 = (i1 - EXPERT_LANE0).astype(F32)
    e2 = (i2 - EXPERT_LANE0).astype(F32)
    rec = jnp.zeros((tm, LANES), F32)
    for ln, val in ((ROUTE_E, e1), (ROUTE_E + 1, e2), (ROUTE_G, gate1), (ROUTE_G + 1, gate2),
                    (ROUTE_R, rank1), (ROUTE_R + 1, rank2)):
        rec = jnp.where(lane == ln, val, rec)
    route_ref[...] = rec


def _outproj(x2, o_da, o_rw, w_out, g_ffn, wr_hi, wr_lo, b_r, tm=512):
    t = x2.shape[0]
    const = lambda shape: pl.BlockSpec(shape, lambda i: (0, 0))
    return pl.pallas_call(
        functools.partial(_outproj_kernel, tm=tm),
        grid=(t // tm,),
        in_specs=[
            pl.BlockSpec((tm, D_MODEL), lambda i: (i, 0)),
            pl.BlockSpec((tm, DA_WIDTH), lambda i: (i, 0)),
            pl.BlockSpec((tm, RW_WIDTH), lambda i: (i, 0)),
            const((D_MODEL, D_MODEL)),
            const((1, D_MODEL)),
            const((D_MODEL, LANES)),
            const((D_MODEL, LANES)),
            const((1, LANES)),
        ],
        out_specs=[
            pl.BlockSpec((tm, D_MODEL), lambda i: (i, 0)),
            pl.BlockSpec((tm * SUBLANES, LANES), lambda i: (i, 0)),
            pl.BlockSpec((tm, LANES), lambda i: (i, 0)),
            const((1, LANES)),
        ],
        out_shape=[
            jax.ShapeDtypeStruct((t, D_MODEL), F32),
            jax.ShapeDtypeStruct((t * SUBLANES, LANES), jnp.uint32),
            jax.ShapeDtypeStruct((t, LANES), F32),
            jax.ShapeDtypeStruct((1, LANES), F32),
        ],
        scratch_shapes=[pltpu.VMEM((1, LANES), F32)],
        compiler_params=pltpu.CompilerParams(
            dimension_semantics=("arbitrary",), vmem_limit_bytes=VMEM_LIMIT),
        name="outproj_router",
    )(x2, o_da, o_rw, w_out, g_ffn, wr_hi, wr_lo, b_r)


def _row_copy(src_ref, dst_ref, src_row8, dst_row8, sem):
    src = src_ref.at[pl.ds(pl.multiple_of(src_row8, SUBLANES), SUBLANES)]
    dst = dst_ref.at[pl.ds(pl.multiple_of(dst_row8, SUBLANES), SUBLANES)]
    return pltpu.make_async_copy(src, dst, sem)


def _dispatch_kernel(dest_ref, zflag_ref, x_ref, xb_hbm, zero_ref, sem, zsem, *, td, n_blocks):
    t0 = pl.program_id(0) * td
    block_rows = EXPERT_BLOCK * SUBLANES

    @pl.when(pl.program_id(0) == 0)
    def _():
        zero_ref[...] = jnp.zeros_like(zero_ref)

        def clear(blk):
            dst = xb_hbm.at[pl.ds(pl.multiple_of(blk * block_rows, block_rows), block_rows)]
            return pltpu.make_async_copy(zero_ref, dst, zsem)

        def start_clear(blk, _):
            @pl.when(zflag_ref[blk] != 0)
            def _():
                clear(blk).start()
            return 0

        def wait_clear(blk, _):
            @pl.when(zflag_ref[blk] != 0)
            def _():
                clear(blk).wait()
            return 0

        lax.fori_loop(0, n_blocks, start_clear, 0)
        lax.fori_loop(0, n_blocks, wait_clear, 0)

    def start(i, _):
        tok = t0 + i
        _row_copy(x_ref, xb_hbm, i * SUBLANES, dest_ref[2 * tok], sem).start(priority=0)
        _row_copy(x_ref, xb_hbm, i * SUBLANES, dest_ref[2 * tok + 1], sem).start(priority=1)
        return 0

    lax.fori_loop(0, td, start, 0, unroll=8)
    for _ in range(2):
        pltpu.make_async_copy(x_ref, xb_hbm.at[pl.ds(0, td * SUBLANES)], sem).wait()


def _dispatch(dest_flat, clear_flags, xn2, n_blocks, td=1024):
    t = xn2.shape[0] // SUBLANES
    td = min(td, t)
    block_rows = EXPERT_BLOCK * SUBLANES
    return pl.pallas_call(
        functools.partial(_dispatch_kernel, td=td, n_blocks=n_blocks),
        grid_spec=pltpu.PrefetchScalarGridSpec(
            num_scalar_prefetch=2,
            grid=(t // td,),
            in_specs=[pl.BlockSpec((td * SUBLANES, LANES), lambda i, *_: (i, 0))],
            out_specs=pl.BlockSpec(memory_space=pl.ANY),
            scratch_shapes=[
                pltpu.VMEM((block_rows, LANES), jnp.uint32),
                pltpu.SemaphoreType.DMA,
                pltpu.SemaphoreType.DMA,
            ],
        ),
        out_shape=jax.ShapeDtypeStruct((n_blocks * block_rows, LANES), jnp.uint32),
        compiler_params=pltpu.CompilerParams(dimension_semantics=("arbitrary",)),
        name="dispatch",
    )(dest_flat, clear_flags, xn2)


EXPERT_BLOCKS_PER_STEP = 4


def _experts_kernel(be_ref, nu_ref, nxt_ref, slot_ref, xb_ref, wg_hbm, wu_hbm, wd_hbm, yb_ref,
                    wgf_ref, wuf_ref, wdf_ref, wgb_ref, wub_ref, wdb_ref, sem):
    block_rows = EXPERT_BLOCK * SUBLANES

    def weight_copies(e, slot):
        return [pltpu.make_async_copy(src.at[e], dst.at[slot], sem.at[slot])
                for src, dst in ((wg_hbm, wgf_ref), (wu_hbm, wuf_ref), (wd_hbm, wdf_ref))]

    def row_block(sub):
        i = pl.program_id(0) * EXPERT_BLOCKS_PER_STEP + sub
        x_ref = xb_ref.at[pl.ds(sub * block_rows, block_rows)]
        y_ref = yb_ref.at[pl.ds(sub * block_rows, block_rows)]
        used = i < nu_ref[0]
        new_expert = jnp.logical_or(i == 0, be_ref[i] != be_ref[jnp.maximum(i - 1, 0)])

        def swiglu(wg_lo, wg_hi, wu_lo, wu_hi, wd):
            x_lo, x_hi = (half.astype(BF16) for half in _unpack_halves(_load_row_tiles(x_ref)))
            hg = _dot(x_lo, wg_lo) + _dot(x_hi, wg_hi)
            hu = _dot(x_lo, wu_lo) + _dot(x_hi, wu_hi)
            hid = hg * _sigmoid(hg) * hu
            _store_row_tiles(y_ref, _pack_halves(_dot(hid.astype(BF16), wd).astype(BF16)))

        @pl.when(jnp.logical_and(used, new_expert))
        def _():
            slot = slot_ref[i]

            @pl.when(i == 0)
            def _():
                for cp in weight_copies(be_ref[i], slot):
                    cp.start()

            for cp in weight_copies(be_ref[i], slot):
                cp.wait()

            @pl.when(nxt_ref[i] >= 0)
            def _():
                for cp in weight_copies(nxt_ref[i], 1 - slot):
                    cp.start()

            wg, wu, wd = (w[slot].astype(BF16) for w in (wgf_ref, wuf_ref, wdf_ref))
            wgb_ref[...] = wg
            wub_ref[...] = wu
            wdb_ref[...] = wd
            swiglu(wg[0:HALF, :], wg[HALF:, :], wu[0:HALF, :], wu[HALF:, :], wd)

        @pl.when(jnp.logical_and(used, jnp.logical_not(new_expert)))
        def _():
            swiglu(wgb_ref[0:HALF, :], wgb_ref[HALF:, :], wub_ref[0:HALF, :], wub_ref[HALF:, :], wdb_ref[...])

        @pl.when(jnp.logical_not(used))
        def _():
            y_ref[...] = jnp.zeros_like(y_ref)

    for sub in range(EXPERT_BLOCKS_PER_STEP):
        row_block(sub)


def _experts(block_e, n_used, next_e, slot, xb, w_gate, w_up, w_down):
    step_rows = EXPERT_BLOCKS_PER_STEP * EXPERT_BLOCK * SUBLANES
    nb = xb.shape[0] // step_rows
    last_step = lambda nu: (nu[0] - 1) // EXPERT_BLOCKS_PER_STEP
    return pl.pallas_call(
        _experts_kernel,
        grid_spec=pltpu.PrefetchScalarGridSpec(
            num_scalar_prefetch=4,
            grid=(nb,),
            in_specs=[
                pl.BlockSpec((step_rows, LANES), lambda i, be, nu, *_: (jnp.minimum(i, last_step(nu)), 0)),
                pl.BlockSpec(memory_space=pl.ANY),
                pl.BlockSpec(memory_space=pl.ANY),
                pl.BlockSpec(memory_space=pl.ANY),
            ],
            out_specs=pl.BlockSpec((step_rows, LANES), lambda i, *_: (i, 0)),
            scratch_shapes=[
                pltpu.VMEM((2, D_MODEL, D_EXPERT), F32),
                pltpu.VMEM((2, D_MODEL, D_EXPERT), F32),
                pltpu.VMEM((2, D_EXPERT, D_MODEL), F32),
                pltpu.VMEM((D_MODEL, D_EXPERT), BF16),
                pltpu.VMEM((D_MODEL, D_EXPERT), BF16),
                pltpu.VMEM((D_EXPERT, D_MODEL), BF16),
                pltpu.SemaphoreType.DMA((2,)),
            ],
        ),
        out_shape=jax.ShapeDtypeStruct(xb.shape, jnp.uint32),
        compiler_params=pltpu.CompilerParams(
            dimension_semantics=("arbitrary",), vmem_limit_bytes=EXPERTS_VMEM_LIMIT),
        name="experts",
    )(block_e, n_used, next_e, slot, xb, w_gate, w_up, w_down)


def _combine_kernel(dest_ref, route_ref, h_ref, g_ref, yb_hbm, o_ref, y1_ref, y2_ref, sem, *, tc):
    step = pl.program_id(0)
    n_steps = pl.num_programs(0)

    def gather(tile, slot):
        def start(i, _):
            tok = tile * tc + i
            _row_copy(yb_hbm, y1_ref.at[slot], dest_ref[2 * tok], i * SUBLANES, sem.at[slot]).start(priority=0)
            _row_copy(yb_hbm, y2_ref.at[slot], dest_ref[2 * tok + 1], i * SUBLANES, sem.at[slot]).start(priority=1)
            return 0

        lax.fori_loop(0, tc, start, 0, unroll=8)

    @pl.when(step == 0)
    def _():
        gather(0, 0)

    @pl.when(step + 1 < n_steps)
    def _():
        gather(step + 1, (step + 1) % 2)

    slot = step % 2
    pltpu.make_async_copy(yb_hbm.at[pl.ds(0, tc * SUBLANES)], y1_ref.at[slot], sem.at[slot]).wait()
    pltpu.make_async_copy(yb_hbm.at[pl.ds(0, tc * SUBLANES)], y2_ref.at[slot], sem.at[slot]).wait()

    rec = route_ref[...]
    lane = lax.broadcasted_iota(jnp.int32, (tc, LANES), 1)
    g1 = jnp.sum(jnp.where(lane == ROUTE_G, rec, 0.0), axis=-1, keepdims=True)
    g2 = jnp.sum(jnp.where(lane == ROUTE_G + 1, rec, 0.0), axis=-1, keepdims=True)
    y1_lo, y1_hi = _unpack_halves(_load_row_tiles(y1_ref.at[slot]))
    y2_lo, y2_hi = _unpack_halves(_load_row_tiles(y2_ref.at[slot]))
    h_lo = h_ref[:, 0:HALF] + (y1_lo * g1 + y2_lo * g2)
    h_hi = h_ref[:, HALF:] + (y1_hi * g1 + y2_hi * g2)
    ms = (jnp.sum(h_lo * h_lo, axis=-1, keepdims=True)
          + jnp.sum(h_hi * h_hi, axis=-1, keepdims=True)) * (1.0 / D_MODEL)
    scale = lax.rsqrt(ms + NORM_EPS)
    o_ref[:, 0:HALF] = h_lo * scale * g_ref[:, 0:HALF]
    o_ref[:, HALF:] = h_hi * scale * g_ref[:, HALF:]


def _combine(dest_flat, route, h1, g_final, yb, tc=256):
    t = h1.shape[0]
    return pl.pallas_call(
        functools.partial(_combine_kernel, tc=tc),
        grid_spec=pltpu.PrefetchScalarGridSpec(
            num_scalar_prefetch=1,
            grid=(t // tc,),
            in_specs=[
                pl.BlockSpec((tc, LANES), lambda i, d: (i, 0)),
                pl.BlockSpec((tc, D_MODEL), lambda i, d: (i, 0)),
                pl.BlockSpec((1, D_MODEL), lambda i, d: (0, 0)),
                pl.BlockSpec(memory_space=pl.ANY),
            ],
            out_specs=pl.BlockSpec((tc, D_MODEL), lambda i, d: (i, 0)),
            scratch_shapes=[
                pltpu.VMEM((2, tc * SUBLANES, LANES), jnp.uint32),
                pltpu.VMEM((2, tc * SUBLANES, LANES), jnp.uint32),
                pltpu.SemaphoreType.DMA((2,)),
            ],
        ),
        out_shape=jax.ShapeDtypeStruct((t, D_MODEL), F32),
        compiler_params=pltpu.CompilerParams(
            dimension_semantics=("arbitrary",), vmem_limit_bytes=VMEM_LIMIT),
        name="combine",
    )(dest_flat, route, h1, g_final, yb)


def _pad_cols(w, n):
    return jnp.pad(w, ((0, 0), (0, n - w.shape[1])))


def kernel(x, g_mix, w_in, w_out, da_lambda_q1, da_lambda_k1, da_lambda_q2, da_lambda_k2, da_subln_g,
           rw_mu_x, rw_mu_rkv, rw_w0, rw_w1, rw_w2, rw_a0, rw_a1, rw_a2, rw_g1, rw_g2, rw_k_k, rw_k_a,
           rw_r_k, rw_ln_w, rw_ln_b, g_ffn, moe_w_group, moe_b_group, moe_w_expert, moe_b_expert,
           moe_w_gate, moe_w_up, moe_w_down, g_final):
    b, s, d = x.shape
    t = b * s
    x2 = x.reshape(t, d)

    mu = rw_mu_x[0]
    lora = (rw_w1[0], rw_a1[0], rw_g1[0])
    keep = _pad_cols(jnp.concatenate([(1.0 - mu[i])[:, None] * w for i, w in enumerate(lora)], axis=1), LORA_PAD)
    prev = _pad_cols(jnp.concatenate([mu[i][:, None] * w for i, w in enumerate(lora)], axis=1), LORA_PAD)
    w_all = jnp.concatenate([w_in[0], keep, prev], axis=1).astype(BF16)
    proj = _inproj(x2, g_mix, w_all)

    slopes = jnp.asarray([2.0 ** (-8.0 * (i + 1) / DA_HEADS) for i in range(DA_HEADS)], F32)
    lam4 = jnp.concatenate([da_lambda_q1, da_lambda_k1, da_lambda_q2, da_lambda_k2], axis=0)
    o_da = _diffattn(proj, slopes, lam4, da_subln_g.reshape(DA_V_DIM, 1), b, s)

    w2cat = jnp.zeros((LORA_PAD, 3 * RW_WIDTH), F32)
    w2cat = w2cat.at[0:LORA_W, 0:RW_WIDTH].set(rw_w2[0])
    w2cat = w2cat.at[LORA_W:LORA_W + LORA_A, RW_WIDTH:2 * RW_WIDTH].set(rw_a2[0])
    w2cat = w2cat.at[LORA_W + LORA_A:LORA_W + LORA_A + LORA_G, 2 * RW_WIDTH:].set(rw_g2[0])
    o_rw = _rwkv(proj, rw_mu_rkv[0], rw_w0, rw_a0, w2cat.astype(BF16), rw_k_k, rw_k_a,
                 rw_r_k.reshape(1, RW_WIDTH), rw_ln_w, rw_ln_b, b, s)

    w_r = _pad_cols(jnp.concatenate([moe_w_group[0], moe_w_expert[0]], axis=1), LANES)
    wr_hi = w_r.astype(BF16)
    wr_lo = (w_r - wr_hi.astype(F32)).astype(BF16)
    b_r = _pad_cols(jnp.concatenate([moe_b_group[0], moe_b_expert[0].reshape(-1)])[None, :], LANES)
    h1, xn2, route, cnt = _outproj(x2, o_da, o_rw, w_out[0].astype(BF16), g_ffn, wr_hi, wr_lo, b_r)

    counts = cnt[0, EXPERT_LANE0:EXPERT_LANE0 + N_EXPERTS].astype(jnp.int32)
    pcounts = ((counts + EXPERT_BLOCK - 1) // EXPERT_BLOCK) * EXPERT_BLOCK
    expert_ids = jnp.arange(N_EXPERTS, dtype=jnp.int32)
    upto = expert_ids[None, :] <= expert_ids[:, None]
    pends = jnp.sum(jnp.where(upto, pcounts[None, :], 0), axis=1)
    pstarts = pends - pcounts
    e_idx = route[:, ROUTE_E:ROUTE_E + 2].astype(jnp.int32)
    rank = route[:, ROUTE_R:ROUTE_R + 2].astype(jnp.int32)
    row0 = jnp.sum(jnp.where(e_idx[..., None] == expert_ids, pstarts, 0), axis=-1)
    dest = ((row0 + rank) * SUBLANES).reshape(-1)
    n_blocks = (2 * t) // EXPERT_BLOCK + N_EXPERTS
    block_row0 = jnp.arange(n_blocks, dtype=jnp.int32) * EXPERT_BLOCK
    block_e = jnp.minimum(jnp.sum(pends[None, :] <= block_row0[:, None], axis=1), N_EXPERTS - 1).astype(jnp.int32)
    n_used = (pends[-1:] // EXPERT_BLOCK).astype(jnp.int32)
    has_rows = counts > 0
    later = (expert_ids[None, :] > expert_ids[:, None]) & has_rows[None, :]
    next_used = jnp.min(jnp.where(later, expert_ids[None, :], N_EXPERTS), axis=1)
    next_used = jnp.where(next_used == N_EXPERTS, -1, next_used).astype(jnp.int32)
    buffer_id = ((jnp.sum(jnp.where(upto & has_rows[None, :], 1, 0), axis=1) - 1) % 2).astype(jnp.int32)

    block_ids = jnp.arange(n_blocks, dtype=jnp.int32)
    last_block = pends // EXPERT_BLOCK - 1
    is_last = jnp.any(has_rows[None, :] & (block_ids[:, None] == last_block[None, :]), axis=1)
    clear_flags = (is_last | (block_ids >= n_used[0])).astype(jnp.int32)
    xb = _dispatch(dest, clear_flags, xn2, n_blocks)
    yb = _experts(block_e, n_used, next_used[block_e], buffer_id[block_e], xb,
                  moe_w_gate[0], moe_w_up[0], moe_w_down[0])
    out = _combine(dest, route, h1, g_final[None, :], yb)
    return out.reshape(b, s, d)
```

```python
import functools
import math

import jax
import jax.numpy as jnp
from jax import lax
from jax.experimental import pallas as pl
from jax.experimental.pallas import tpu as pltpu

F32 = jnp.float32
BF16 = jnp.bfloat16

D_MODEL = 2048
DA_WIDTH = 1024
RW_WIDTH = 1024
DA_HEADS = 8
DA_V_DIM = 128
DA_QK_DIM = 64
RW_N = 64
RW_PAIRS = RW_WIDTH // 128
LORA_W, LORA_A, LORA_G = 64, 64, 160
LORA_PAD = 384
IN_COLS = 3 * DA_WIDTH + 3 * RW_WIDTH
NC = IN_COLS + 2 * LORA_PAD
N_GROUPS = 4
EXPERTS_PER_GROUP = 8
N_EXPERTS = 32
D_EXPERT = 512
NORM_EPS = 1e-6
LN_X_EPS = 64e-5
LAM_INIT = 0.8 - 0.6 * math.exp(-0.3 * 0)
LANES = 128
CHUNK = 64
RW_TILE = 256
NCH = RW_TILE // CHUNK
EXPERT_BLOCK = 256
NEG = -1e30
VMEM_LIMIT = 48 * 1024 * 1024
EXPERTS_VMEM_LIMIT = 56 * 1024 * 1024


def _dot(a, b):
    return jnp.dot(a, b, preferred_element_type=F32)


def _dot_nt(a, b):
    return lax.dot_general(a, b, (((1,), (1,)), ((), ())), preferred_element_type=F32)


def _dot_tn(a, b):
    return lax.dot_general(a, b, (((0,), (0,)), ((), ())), preferred_element_type=F32)


def _sigmoid(x):
    return 1.0 / (1.0 + jnp.exp(-x))


HALF = D_MODEL // 2


def _pack_halves(xb):
    lo = lax.bitcast_convert_type(xb[:, :HALF].astype(F32), jnp.uint32) >> 16
    hi = lax.bitcast_convert_type(xb[:, HALF:].astype(F32), jnp.uint32) & jnp.uint32(0xFFFF0000)
    return hi | lo


def _unpack_halves(u):
    lo = lax.bitcast_convert_type(u << 16, F32)
    hi = lax.bitcast_convert_type(u & jnp.uint32(0xFFFF0000), F32)
    return lo, hi


SUBLANES = 8
assert HALF == SUBLANES * LANES


def _store_row_tiles(ref, u):
    n = u.shape[0]
    for k in range(SUBLANES):
        ref[pl.ds(k, n, stride=SUBLANES), :] = u[:, k * LANES:(k + 1) * LANES]


def _load_row_tiles(ref):
    n = ref.shape[0] // SUBLANES
    return jnp.concatenate([ref[pl.ds(k, n, stride=SUBLANES), :] for k in range(SUBLANES)], axis=1)


def _inproj_kernel(x_ref, g_ref, w_ref, o_ref, xn_ref):
    @pl.when(pl.program_id(1) == 0)
    def _():
        x = x_ref[...]
        ms = jnp.mean(x * x, axis=-1, keepdims=True)
        xn_ref[...] = (x * lax.rsqrt(ms + NORM_EPS) * g_ref[...]).astype(BF16)

    o_ref[...] = _dot(xn_ref[...], w_ref[...]).astype(o_ref.dtype)


def _inproj(x2, g, w_all, tm=1024, tn=2304):
    t = x2.shape[0]
    tm = min(tm, t)
    vmem_limit = 2 * (tm * D_MODEL * 4 + D_MODEL * tn * 2 + tm * tn * 2) + tm * D_MODEL * 2 + (6 << 20)
    return pl.pallas_call(
        _inproj_kernel,
        grid=(t // tm, NC // tn),
        in_specs=[
            pl.BlockSpec((tm, D_MODEL), lambda i, j: (i, 0)),
            pl.BlockSpec((1, D_MODEL), lambda i, j: (0, 0)),
            pl.BlockSpec((D_MODEL, tn), lambda i, j: (0, j)),
        ],
        out_specs=pl.BlockSpec((tm, tn), lambda i, j: (i, j)),
        out_shape=jax.ShapeDtypeStruct((t, NC), BF16),
        scratch_shapes=[pltpu.VMEM((tm, D_MODEL), BF16)],
        compiler_params=pltpu.CompilerParams(
            dimension_semantics=("parallel", "arbitrary"), vmem_limit_bytes=vmem_limit),
        name="inproj",
    )(x2, g, w_all)


DA_HEADS_PER_STEP = 2
SCORE_SLOTS = 3


def _da_kernel(qi_tab, kj_tab, slopes_ref, lam_ref, subg_ref, q_ref, k_ref, v_ref, o_ref,
               vt_ref, qz_ref, bias_ref, m_ref, l_ref, lfin_ref, acc_ref, s_ref, p_ref, alpha_ref,
               *, tq, seq):
    tk = tq
    nq = seq // tq
    n_trips = nq * (nq + 1) // 2
    heads = range(DA_HEADS_PER_STEP)
    chains = [(hd, c) for hd in heads for c in range(2)]
    slope = [slopes_ref[pl.program_id(1) * DA_HEADS_PER_STEP + hd] for hd in heads]
    cols = lambda hd: slice(hd * LANES, (hd + 1) * LANES)

    lane = lax.broadcasted_iota(jnp.int32, (tq, LANES), 1)
    krow = lax.broadcasted_iota(jnp.int32, (tk, tq), 0)
    qcol = lax.broadcasted_iota(jnp.int32, (tk, tq), 1)
    for hd in heads:
        for blk in range(seq // LANES):
            r = slice(blk * LANES, (blk + 1) * LANES)
            vt_ref[hd, :, r] = v_ref[r, cols(hd)].astype(F32).T.astype(BF16)
        for blk in range(nq):
            r = slice(blk * tq, (blk + 1) * tq)
            q = q_ref[r, cols(hd)] * jnp.asarray(DA_QK_DIM ** -0.5, BF16)
            qz_ref[2 * hd, r, :] = jnp.where(lane < DA_QK_DIM, q, jnp.zeros_like(q))
            qz_ref[2 * hd + 1, r, :] = jnp.where(lane >= DA_QK_DIM, q, jnp.zeros_like(q))
        alibi = slope[hd] * krow.astype(F32)
        bias_ref[hd, 0] = alibi
        bias_ref[hd, 1] = jnp.where(krow <= qcol, alibi, NEG)

    m_ref[...] = jnp.full_like(m_ref, NEG)
    l_ref[...] = jnp.zeros_like(l_ref)
    lfin_ref[...] = jnp.ones_like(lfin_ref)
    acc_ref[...] = jnp.zeros_like(acc_ref)
    p_ref[...] = jnp.zeros_like(p_ref)
    alpha_ref[...] = jnp.zeros_like(alpha_ref)

    lv = lam_ref[...]
    lam = (jnp.exp(jnp.sum(lv[0:1] * lv[1:2], axis=-1, keepdims=True))
           - jnp.exp(jnp.sum(lv[2:3] * lv[3:4], axis=-1, keepdims=True)) + LAM_INIT)

    def scores_to_scratch(t, slot):
        qs = pl.multiple_of(qi_tab[t] * tq, tq)
        ks = pl.multiple_of(kj_tab[t] * tk, tk)
        diag = (qi_tab[t] == kj_tab[t]).astype(jnp.int32)
        for hd, c in chains:
            ch = 2 * hd + c
            s_ref[slot, ch] = (_dot_nt(k_ref[pl.ds(ks, tk), cols(hd)], qz_ref[ch, pl.ds(qs, tq), :])
                               + bias_ref[hd, diag])

    def values_from_scratch(t):
        ks = pl.multiple_of(kj_tab[t] * tk, tk)
        for hd, c in chains:
            ch = 2 * hd + c
            acc_ref[ch] = alpha_ref[ch:ch + 1, :] * acc_ref[ch] + _dot(vt_ref[hd, :, pl.ds(ks, tk)], p_ref[ch])

    def softmax(t):
        slot = lax.rem(t, SCORE_SLOTS)
        first = kj_tab[t] == 0
        diag = kj_tab[t] == qi_tab[t]
        rel = ((kj_tab[t] - qi_tab[t]) * tk).astype(F32)
        p, alpha = [], []
        for hd, c in chains:
            ch = 2 * hd + c
            off = slope[hd] * rel
            s = s_ref[slot, ch]
            m = jnp.where(first, NEG, m_ref[ch:ch + 1, :])
            l = jnp.where(first, 0.0, l_ref[ch:ch + 1, :])
            m_new = jnp.maximum(m, jnp.max(s, axis=0, keepdims=True) + off)
            pc = jnp.exp(s - (m_new - off))
            a = jnp.exp(m - m_new)
            l_new = a * l + jnp.sum(pc, axis=0, keepdims=True)
            m_ref[ch:ch + 1, :] = m_new
            l_ref[ch:ch + 1, :] = l_new
            lfin_ref[ch:ch + 1, :] = jnp.where(diag, l_new, lfin_ref[ch:ch + 1, :])
            p.append(pc)
            alpha.append(a)
        return p, alpha

    def publish(p, alpha):
        for ch in range(len(chains)):
            alpha_ref[ch:ch + 1, :] = alpha[ch]
            p_ref[ch] = p[ch].astype(BF16)

    def finalize(t):
        qs = pl.multiple_of(qi_tab[t] * tq, tq)
        for hd in heads:
            c0, c1 = 2 * hd, 2 * hd + 1
            o = acc_ref[c0] / lfin_ref[c0:c0 + 1, :] - lam * (acc_ref[c1] / lfin_ref[c1:c1 + 1, :])
            ms = jnp.mean(o * o, axis=0, keepdims=True)
            o = o * lax.rsqrt(ms + NORM_EPS) * subg_ref[...] * (1.0 - LAM_INIT)
            o_ref[pl.ds(qs, tq), cols(hd)] = o.T.astype(o_ref.dtype)

    last = n_trips - 1
    scores_to_scratch(0, 0)
    scores_to_scratch(min(1, last), 1)

    def body(t, _):
        prev = jnp.maximum(t - 1, 0)
        p, alpha = softmax(t)
        values_from_scratch(prev)
        scores_to_scratch(jnp.minimum(t + 2, last), lax.rem(t + 2, SCORE_SLOTS))
        publish(p, alpha)

        @pl.when(jnp.logical_and(t > 0, qi_tab[prev] == kj_tab[prev]))
        def _():
            finalize(prev)
        return 0

    lax.fori_loop(0, n_trips, body, 0)
    values_from_scratch(n_trips - 1)
    finalize(n_trips - 1)


def _diffattn(proj, slopes, lam4, subg_col, b, s, tq=256):
    nq = s // tq
    nh = DA_HEADS_PER_STEP
    groups = DA_HEADS // nh
    width = nh * LANES
    pairs = [(qi, kj) for qi in range(nq) for kj in range(qi + 1)]
    qi_tab = jnp.asarray([p[0] for p in pairs], jnp.int32)
    kj_tab = jnp.asarray([p[1] for p in pairs], jnp.int32)
    return pl.pallas_call(
        functools.partial(_da_kernel, tq=tq, seq=s),
        grid_spec=pltpu.PrefetchScalarGridSpec(
            num_scalar_prefetch=2,
            grid=(b, groups),
            in_specs=[
                pl.BlockSpec(memory_space=pltpu.SMEM),
                pl.BlockSpec((4, DA_QK_DIM), lambda bi, g, *_: (0, 0)),
                pl.BlockSpec((DA_V_DIM, 1), lambda bi, g, *_: (0, 0)),
                pl.BlockSpec((s, width), lambda bi, g, *_: (bi, g)),
                pl.BlockSpec((s, width), lambda bi, g, *_: (bi, groups + g)),
                pl.BlockSpec((s, width), lambda bi, g, *_: (bi, 2 * groups + g)),
            ],
            out_specs=pl.BlockSpec((s, width), lambda bi, g, *_: (bi, g)),
            scratch_shapes=[
                pltpu.VMEM((nh, DA_V_DIM, s), BF16),
                pltpu.VMEM((2 * nh, s, LANES), BF16),
                pltpu.VMEM((nh, 2, tq, tq), F32),
                pltpu.VMEM((2 * nh, tq), F32),
                pltpu.VMEM((2 * nh, tq), F32),
                pltpu.VMEM((2 * nh, tq), F32),
                pltpu.VMEM((2 * nh, DA_V_DIM, tq), F32),
                pltpu.VMEM((SCORE_SLOTS, 2 * nh, tq, tq), F32),
                pltpu.VMEM((2 * nh, tq, tq), BF16),
                pltpu.VMEM((2 * nh, tq), F32),
            ],
        ),
        out_shape=jax.ShapeDtypeStruct((b * s, DA_WIDTH), BF16),
        compiler_params=pltpu.CompilerParams(
            dimension_semantics=("parallel", "arbitrary"), vmem_limit_bytes=VMEM_LIMIT),
        name="diffattn",
    )(qi_tab, kj_tab, slopes, lam4, subg_col, proj, proj, proj)


def _rwkv_kernel(r_ref, k_ref, v_ref, p1_ref, p2_ref, mu_ref, w0_ref, a0_ref, w2_ref, kk_ref, ka_ref,
                 rk_ref, lnw_ref, lnb_ref, o_ref, state_ref, crkv_ref, cp2_ref):
    L, TT = CHUNK, RW_TILE
    c = pl.program_id(1)

    @pl.when(c == 0)
    def _():
        state_ref[...] = jnp.zeros_like(state_ref)
        crkv_ref[...] = jnp.zeros_like(crkv_ref)
        cp2_ref[...] = jnp.zeros_like(cp2_ref)

    first = lax.broadcasted_iota(jnp.int32, (TT, 1), 0) == 0

    def shift(x, carry):
        return jnp.where(first, carry, pltpu.roll(x, 1, 0))

    r_raw = r_ref[...].astype(F32)
    k_raw = k_ref[...].astype(F32)
    v_raw = v_ref[...].astype(F32)
    p2 = p2_ref[...].astype(F32)
    rs = shift(r_raw, crkv_ref[:, 0:RW_WIDTH])
    ks = shift(k_raw, crkv_ref[:, RW_WIDTH:2 * RW_WIDTH])
    vs = shift(v_raw, crkv_ref[:, 2 * RW_WIDTH:3 * RW_WIDTH])
    lin = p1_ref[...].astype(F32) + shift(p2, cp2_ref[...])
    crkv_ref[:, 0:RW_WIDTH] = r_raw[TT - 1:TT, :]
    crkv_ref[:, RW_WIDTH:2 * RW_WIDTH] = k_raw[TT - 1:TT, :]
    crkv_ref[:, 2 * RW_WIDTH:3 * RW_WIDTH] = v_raw[TT - 1:TT, :]
    cp2_ref[...] = p2[TT - 1:TT, :]

    r = r_raw + (rs - r_raw) * mu_ref[0:1, :]
    k = k_raw + (ks - k_raw) * mu_ref[1:2, :]
    v = v_raw + (vs - v_raw) * mu_ref[2:3, :]

    ll = lax.broadcasted_iota(jnp.int32, (TT, LORA_PAD), 1)
    z = jnp.where(ll < LORA_W, jnp.tanh(lin),
                  jnp.where(ll < LORA_W + LORA_A, lin,
                            jnp.where(ll < LORA_W + LORA_A + LORA_G, _sigmoid(lin), 0.0)))
    d = _dot(z.astype(BF16), w2_ref[...])
    logw = -_sigmoid(w0_ref[...] + d[:, 0:RW_WIDTH]) * math.exp(-0.5)
    a_sig = _sigmoid(a0_ref[...] + d[:, RW_WIDTH:2 * RW_WIDTH])
    g = d[:, 2 * RW_WIDTH:3 * RW_WIDTH]

    li = lax.broadcasted_iota(jnp.int32, (LANES, LANES), 0)
    lj = lax.broadcasted_iota(jnp.int32, (LANES, LANES), 1)
    same_head = jnp.where((li // RW_N) == (lj // RW_N), 1.0, 0.0).astype(BF16)

    def head_sum(x):
        xb = x.astype(BF16)
        return jnp.concatenate(
            [_dot(xb[:, p * LANES:(p + 1) * LANES], same_head) for p in range(RW_PAIRS)], axis=1)

    kk = k * kk_ref[...]
    kk = kk * lax.rsqrt(jnp.maximum(head_sum(kk * kk), 1e-24))
    k = k * (1.0 + (a_sig - 1.0) * ka_ref[...])
    bonus = head_sum(r * k * rk_ref[...])

    ti = lax.broadcasted_iota(jnp.int32, (TT, TT), 0)
    tj = lax.broadcasted_iota(jnp.int32, (TT, TT), 1)
    tri = jnp.where(ti >= tj, jnp.where((ti // L) == (tj // L), 1.0, 0.0), 0.0).astype(BF16)
    w_hi = logw.astype(BF16)
    w_r1 = logw - w_hi.astype(F32)
    w_mid = w_r1.astype(BF16)
    w_lo = (w_r1 - w_mid.astype(F32)).astype(BF16)
    cw = _dot(tri, w_hi) + _dot(tri, w_mid) + _dot(tri, w_lo)
    cw_last = [cw[(ci + 1) * L - 1:(ci + 1) * L, :] for ci in range(NCH)]
    e_in = jnp.exp(cw)
    e_out = jnp.exp(-cw)
    e_end = jnp.exp(jnp.concatenate([jnp.broadcast_to(cl, (L, RW_WIDTH)) for cl in cw_last], axis=0) - cw)
    w_end = [jnp.exp(cl) for cl in cw_last]

    at = -kk * jnp.exp(cw - logw)
    rt = r * e_in
    b = kk * a_sig
    bt = b * e_out
    kt = k * e_out
    bd = b * e_end
    kd = k * e_end

    lane = lax.broadcasted_iota(jnp.int32, (L, LANES), 1)
    lo = lane < RW_N

    def stack(x, ci, p):
        xs = x[ci * L:(ci + 1) * L, p * LANES:(p + 1) * LANES]
        return jnp.concatenate([jnp.where(lo, xs, 0.0), jnp.where(lo, 0.0, xs)], axis=0).astype(BF16)

    si = lax.broadcasted_iota(jnp.int32, (2 * L, 2 * L), 0)
    sj = lax.broadcasted_iota(jnp.int32, (2 * L, 2 * L), 1)
    strict = (si % L) > (sj % L)
    incl = (si % L) >= (sj % L)
    eye = jnp.where(si == sj, 1.0, 0.0)

    pairs = range(RW_PAIRS)
    cat0 = lambda *xs: jnp.concatenate(xs, axis=0)
    cat1 = lambda *xs: jnp.concatenate(xs, axis=1)
    bf = lambda x: x.astype(BF16)
    def chunk_setup(ci):
        a_s = [stack(at, ci, p) for p in pairs]
        r_s = [stack(rt, ci, p) for p in pairs]
        v_s = [stack(v, ci, p) for p in pairs]
        bk = [cat0(stack(bt, ci, p), stack(kt, ci, p)) for p in pairs]
        bkd = [cat0(stack(bd, ci, p), stack(kd, ci, p)) for p in pairs]
        aa = [_dot_nt(a_s[p], bk[p]) for p in pairs]
        rr = [_dot_nt(r_s[p], bk[p]) for p in pairs]
        a_ab = [jnp.where(strict, aa[p][:, 0:2 * L], 0.0) for p in pairs]
        a_ak = [bf(jnp.where(strict, aa[p][:, 2 * L:4 * L], 0.0)) for p in pairs]
        a_rb = [bf(jnp.where(incl, rr[p][:, 0:2 * L], 0.0)) for p in pairs]
        a_rk = [bf(jnp.where(incl, rr[p][:, 2 * L:4 * L], 0.0)) for p in pairs]
        ab = [bf(a) for a in a_ab]
        qpow = [_dot(ab[p], ab[p]) for p in pairs]
        inv = [eye + a_ab[p] for p in pairs]
        for _ in range(int(math.log2(L)) - 2):
            qb = [bf(q) for q in qpow]
            res = [_dot(qb[p], cat1(qb[p], bf(inv[p]))) for p in pairs]
            qpow = [res[p][:, 0:2 * L] for p in pairs]
            inv = [inv[p] + res[p][:, 2 * L:4 * L] for p in pairs]
        inv = [bf(inv[p] + _dot(bf(qpow[p]), bf(inv[p]))) for p in pairs]
        w_col = [jnp.broadcast_to(w_end[ci][:, p * LANES:(p + 1) * LANES], (LANES, LANES)).T for p in pairs]
        return a_s, r_s, v_s, bkd, a_ak, a_rb, a_rk, inv, w_col

    def chunk_advance(setup, h):
        a_s, r_s, v_s, bkd, a_ak, a_rb, a_rk, inv, w_col = setup
        h_b = [bf(hp) for hp in h]
        x = [_dot(cat1(a_s[p], a_ak[p]), cat0(h_b[p], v_s[p])) for p in pairs]
        u_b = [bf(_dot(inv[p], bf(x[p]))) for p in pairs]
        y2 = [_dot(cat1(r_s[p], a_rb[p], a_rk[p]), cat0(h_b[p], u_b[p], v_s[p])) for p in pairs]
        h = [w_col[p] * h[p] + _dot_tn(bkd[p], cat0(u_b[p], v_s[p])) for p in pairs]
        return h, cat1(*[y2[p][0:L] + y2[p][L:2 * L] for p in pairs])

    h = [state_ref[p] for p in pairs]
    ys = []
    setup = chunk_setup(0)
    for ci in range(NCH):
        ahead = chunk_setup(ci + 1) if ci + 1 < NCH else None
        h, y_chunk = chunk_advance(setup, h)
        ys.append(y_chunk)
        setup = ahead
    for p in pairs:
        state_ref[p] = h[p]
    y = cat0(*ys)

    mean = head_sum(y) * (1.0 / RW_N)
    yc = y - mean
    var = head_sum(yc * yc) * (1.0 / RW_N)
    y = yc * lax.rsqrt(var + LN_X_EPS) * lnw_ref[...] + lnb_ref[...]
    y = y + bonus * v
    o_ref[...] = (y * g).astype(o_ref.dtype)


def _rwkv(proj, mu_rkv, w0, a0, w2cat, k_k, k_a, r_k, ln_w, ln_b, b, s):
    tt = RW_TILE
    nc = s // tt
    row = lambda n: pl.BlockSpec((1, n), lambda bi, c: (0, 0))
    rkv_col0 = 3 * DA_WIDTH // RW_WIDTH
    lora_col0 = IN_COLS // LORA_PAD
    return pl.pallas_call(
        _rwkv_kernel,
        grid=(b, nc),
        in_specs=[
            pl.BlockSpec((tt, RW_WIDTH), lambda bi, c: (bi * nc + c, rkv_col0)),
            pl.BlockSpec((tt, RW_WIDTH), lambda bi, c: (bi * nc + c, rkv_col0 + 1)),
            pl.BlockSpec((tt, RW_WIDTH), lambda bi, c: (bi * nc + c, rkv_col0 + 2)),
            pl.BlockSpec((tt, LORA_PAD), lambda bi, c: (bi * nc + c, lora_col0)),
            pl.BlockSpec((tt, LORA_PAD), lambda bi, c: (bi * nc + c, lora_col0 + 1)),
            pl.BlockSpec((3, RW_WIDTH), lambda bi, c: (0, 0)),
            row(RW_WIDTH), row(RW_WIDTH),
            pl.BlockSpec((LORA_PAD, 3 * RW_WIDTH), lambda bi, c: (0, 0)),
            row(RW_WIDTH), row(RW_WIDTH), row(RW_WIDTH), row(RW_WIDTH), row(RW_WIDTH),
        ],
        out_specs=pl.BlockSpec((tt, RW_WIDTH), lambda bi, c: (bi * nc + c, 0)),
        out_shape=jax.ShapeDtypeStruct((b * s, RW_WIDTH), BF16),
        scratch_shapes=[
            pltpu.VMEM((RW_PAIRS, LANES, LANES), F32),
            pltpu.VMEM((1, 3 * RW_WIDTH), F32),
            pltpu.VMEM((1, LORA_PAD), F32),
        ],
        compiler_params=pltpu.CompilerParams(
            dimension_semantics=("parallel", "arbitrary"), vmem_limit_bytes=VMEM_LIMIT),
        name="rwkv",
    )(proj, proj, proj, proj, proj, mu_rkv, w0, a0, w2cat, k_k, k_a, r_k, ln_w, ln_b)


ROUTE_E, ROUTE_G, ROUTE_R = 0, 2, 4
EXPERT_LANE0 = N_GROUPS


def _outproj_kernel(x_ref, oda_ref, orw_ref, wout_ref, g_ref, wrh_ref, wrl_ref, br_ref,
                    h_ref, xn_ref, route_ref, cnt_ref, base_ref, *, tm):
    @pl.when(pl.program_id(0) == 0)
    def _():
        base_ref[...] = jnp.zeros_like(base_ref)

    mix = _dot(oda_ref[...], wout_ref[0:DA_WIDTH, :]) + _dot(orw_ref[...], wout_ref[DA_WIDTH:, :])
    h = x_ref[...] + mix
    h_ref[...] = h
    ms = jnp.mean(h * h, axis=-1, keepdims=True)
    xn = h * lax.rsqrt(ms + NORM_EPS) * g_ref[...]
    x_hi = xn.astype(BF16)
    _store_row_tiles(xn_ref, _pack_halves(x_hi))

    x_lo = (xn - x_hi.astype(F32)).astype(BF16)
    hi_both = _dot(x_hi, jnp.concatenate([wrh_ref[...], wrl_ref[...]], axis=1))
    lg = (hi_both[:, 0:LANES] + _dot(x_lo, wrh_ref[...]) + hi_both[:, LANES:]) + br_ref[...]

    lane = lax.broadcasted_iota(jnp.int32, (tm, LANES), 1)
    big = jnp.int32(1 << 20)
    gl = jnp.where(lane < N_GROUPS, lg, NEG)
    gmax = jnp.max(gl, axis=-1, keepdims=True)
    grp = jnp.min(jnp.where(gl == gmax, lane, big), axis=-1, keepdims=True)
    p_grp = 1.0 / jnp.sum(jnp.exp(gl - gmax), axis=-1, keepdims=True)
    eg = jnp.where(lane >= EXPERT_LANE0, (lane - EXPERT_LANE0) // EXPERTS_PER_GROUP, -1)
    el = jnp.where(eg == grp, lg, NEG)
    emax = jnp.max(el, axis=-1, keepdims=True)
    pe = jnp.exp(el - emax)
    probs = pe / jnp.sum(pe, axis=-1, keepdims=True)
    probs = jnp.where(eg == grp, probs, -1.0)
    p1 = jnp.max(probs, axis=-1, keepdims=True)
    i1 = jnp.min(jnp.where(probs == p1, lane, big), axis=-1, keepdims=True)
    probs2 = jnp.where(lane == i1, -1.0, probs)
    p2 = jnp.max(probs2, axis=-1, keepdims=True)
    i2 = jnp.min(jnp.where(probs2 == p2, lane, big), axis=-1, keepdims=True)
    gate1 = p_grp * p1 / (p1 + p2)
    gate2 = p_grp * p2 / (p1 + p2)

    oh1 = jnp.where(lane == i1, 1.0, 0.0)
    oh2 = jnp.where(lane == i2, 1.0, 0.0)
    ri = lax.broadcasted_iota(jnp.int32, (tm, tm), 0)
    rj = lax.broadcasted_iota(jnp.int32, (tm, tm), 1)
    lower = jnp.where(ri > rj, 1.0, 0.0).astype(BF16)
    base = base_ref[...]
    tot1 = jnp.sum(oh1, axis=0, keepdims=True)
    before = _dot(lower, jnp.concatenate([oh1, oh2], axis=1).astype(BF16))
    c1 = base + before[:, 0:LANES]
    c2 = base + tot1 + before[:, LANES:]
    rank1 = jnp.sum(oh1 * c1, axis=-1, keepdims=True)
    rank2 = jnp.sum(oh2 * c2, axis=-1, keepdims=True)
    base = base + tot1 + jnp.sum(oh2, axis=0, keepdims=True)
    base_ref[...] = base
    cnt_ref[...] = base

    e1 = (i1 - EXPERT_LANE0).astype(F32)
    e2 = (i2 - EXPERT_LANE0).astype(F32)
    rec = jnp.zeros((tm, LANES), F32)
    for ln, val in ((ROUTE_E, e1), (ROUTE_E + 1, e2), (ROUTE_G, gate1), (ROUTE_G + 1, gate2),
                    (ROUTE_R, rank1), (ROUTE_R + 1, rank2)):
        rec = jnp.where(lane == ln, val, rec)
    route_ref[...] = rec


def _outproj(x2, o_da, o_rw, w_out, g_ffn, wr_hi, wr_lo, b_r, tm=512):
    t = x2.shape[0]
    const = lambda shape: pl.BlockSpec(shape, lambda i: (0, 0))
    return pl.pallas_call(
        functools.partial(_outproj_kernel, tm=tm),
        grid=(t // tm,),
        in_specs=[
            pl.BlockSpec((tm, D_MODEL), lambda i: (i, 0)),
            pl.BlockSpec((tm, DA_WIDTH), lambda i: (i, 0)),
            pl.BlockSpec((tm, RW_WIDTH), lambda i: (i, 0)),
            const((D_MODEL, D_MODEL)),
            const((1, D_MODEL)),
            const((D_MODEL, LANES)),
            const((D_MODEL, LANES)),
            const((1, LANES)),
        ],
        out_specs=[
            pl.BlockSpec((tm, D_MODEL), lambda i: (i, 0)),
            pl.BlockSpec((tm * SUBLANES, LANES), lambda i: (i, 0)),
            pl.BlockSpec((tm, LANES), lambda i: (i, 0)),
            const((1, LANES)),
        ],
        out_shape=[
            jax.ShapeDtypeStruct((t, D_MODEL), F32),
            jax.ShapeDtypeStruct((t * SUBLANES, LANES), jnp.uint32),
            jax.ShapeDtypeStruct((t, LANES), F32),
            jax.ShapeDtypeStruct((1, LANES), F32),
        ],
        scratch_shapes=[pltpu.VMEM((1, LANES), F32)],
        compiler_params=pltpu.CompilerParams(
            dimension_semantics=("arbitrary",), vmem_limit_bytes=VMEM_LIMIT),
        name="outproj_router",
    )(x2, o_da, o_rw, w_out, g_ffn, wr_hi, wr_lo, b_r)


def _row_copy(src_ref, dst_ref, src_row8, dst_row8, sem):
    src = src_ref.at[pl.ds(pl.multiple_of(src_row8, SUBLANES), SUBLANES)]
    dst = dst_ref.at[pl.ds(pl.multiple_of(dst_row8, SUBLANES), SUBLANES)]
    return pltpu.make_async_copy(src, dst, sem)


def _dispatch_kernel(dest_ref, zflag_ref, x_ref, xb_hbm, zero_ref, sem, zsem, *, td, n_blocks):
    t0 = pl.program_id(0) * td
    block_rows = EXPERT_BLOCK * SUBLANES

    @pl.when(pl.program_id(0) == 0)
    def _():
        zero_ref[...] = jnp.zeros_like(zero_ref)

        def clear(blk):
            dst = xb_hbm.at[pl.ds(pl.multiple_of(blk * block_rows, block_rows), block_rows)]
            return pltpu.make_async_copy(zero_ref, dst, zsem)

        def start_clear(blk, _):
            @pl.when(zflag_ref[blk] != 0)
            def _():
                clear(blk).start()
            return 0

        def wait_clear(blk, _):
            @pl.when(zflag_ref[blk] != 0)
            def _():
                clear(blk).wait()
            return 0

        lax.fori_loop(0, n_blocks, start_clear, 0)
        lax.fori_loop(0, n_blocks, wait_clear, 0)

    def start(i, _):
        tok = t0 + i
        _row_copy(x_ref, xb_hbm, i * SUBLANES, dest_ref[2 * tok], sem).start(priority=0)
        _row_copy(x_ref, xb_hbm, i * SUBLANES, dest_ref[2 * tok + 1], sem).start(priority=1)
        return 0

    lax.fori_loop(0, td, start, 0, unroll=8)
    for _ in range(2):
        pltpu.make_async_copy(x_ref, xb_hbm.at[pl.ds(0, td * SUBLANES)], sem).wait()


def _dispatch(dest_flat, clear_flags, xn2, n_blocks, td=1024):
    t = xn2.shape[0] // SUBLANES
    td = min(td, t)
    block_rows = EXPERT_BLOCK * SUBLANES
    return pl.pallas_call(
        functools.partial(_dispatch_kernel, td=td, n_blocks=n_blocks),
        grid_spec=pltpu.PrefetchScalarGridSpec(
            num_scalar_prefetch=2,
            grid=(t // td,),
            in_specs=[pl.BlockSpec((td * SUBLANES, LANES), lambda i, *_: (i, 0))],
            out_specs=pl.BlockSpec(memory_space=pl.ANY),
            scratch_shapes=[
                pltpu.VMEM((block_rows, LANES), jnp.uint32),
                pltpu.SemaphoreType.DMA,
                pltpu.SemaphoreType.DMA,
            ],
        ),
        out_shape=jax.ShapeDtypeStruct((n_blocks * block_rows, LANES), jnp.uint32),
        compiler_params=pltpu.CompilerParams(dimension_semantics=("arbitrary",)),
        name="dispatch",
    )(dest_flat, clear_flags, xn2)


EXPERT_BLOCKS_PER_STEP = 4


def _experts_kernel(be_ref, nu_ref, nxt_ref, slot_ref, xb_ref, wg_hbm, wu_hbm, wd_hbm, yb_ref,
                    wgf_ref, wuf_ref, wdf_ref, wgb_ref, wub_ref, wdb_ref, sem):
    block_rows = EXPERT_BLOCK * SUBLANES

    def weight_copies(e, slot):
        return [pltpu.make_async_copy(src.at[e], dst.at[slot], sem.at[slot])
                for src, dst in ((wg_hbm, wgf_ref), (wu_hbm, wuf_ref), (wd_hbm, wdf_ref))]

    def row_block(sub):
        i = pl.program_id(0) * EXPERT_BLOCKS_PER_STEP + sub
        x_ref = xb_ref.at[pl.ds(sub * block_rows, block_rows)]
        y_ref = yb_ref.at[pl.ds(sub * block_rows, block_rows)]
        used = i < nu_ref[0]
        new_expert = jnp.logical_or(i == 0, be_ref[i] != be_ref[jnp.maximum(i - 1, 0)])

        def swiglu(wg_lo, wg_hi, wu_lo, wu_hi, wd):
            x_lo, x_hi = (half.astype(BF16) for half in _unpack_halves(_load_row_tiles(x_ref)))
            hg = _dot(x_lo, wg_lo) + _dot(x_hi, wg_hi)
            hu = _dot(x_lo, wu_lo) + _dot(x_hi, wu_hi)
            hid = hg * _sigmoid(hg) * hu
            _store_row_tiles(y_ref, _pack_halves(_dot(hid.astype(BF16), wd).astype(BF16)))

        @pl.when(jnp.logical_and(used, new_expert))
        def _():
            slot = slot_ref[i]

            @pl.when(i == 0)
            def _():
                for cp in weight_copies(be_ref[i], slot):
                    cp.start()

            for cp in weight_copies(be_ref[i], slot):
                cp.wait()

            @pl.when(nxt_ref[i] >= 0)
            def _():
                for cp in weight_copies(nxt_ref[i], 1 - slot):
                    cp.start()

            wg, wu, wd = (w[slot].astype(BF16) for w in (wgf_ref, wuf_ref, wdf_ref))
            wgb_ref[...] = wg
            wub_ref[...] = wu
            wdb_ref[...] = wd
            swiglu(wg[0:HALF, :], wg[HALF:, :], wu[0:HALF, :], wu[HALF:, :], wd)

        @pl.when(jnp.logical_and(used, jnp.logical_not(new_expert)))
        def _():
            swiglu(wgb_ref[0:HALF, :], wgb_ref[HALF:, :], wub_ref[0:HALF, :], wub_ref[HALF:, :], wdb_ref[...])

        @pl.when(jnp.logical_not(used))
        def _():
            y_ref[...] = jnp.zeros_like(y_ref)

    for sub in range(EXPERT_BLOCKS_PER_STEP):
        row_block(sub)


def _experts(block_e, n_used, next_e, slot, xb, w_gate, w_up, w_down):
    step_rows = EXPERT_BLOCKS_PER_STEP * EXPERT_BLOCK * SUBLANES
    nb = xb.shape[0] // step_rows
    last_step = lambda nu: (nu[0] - 1) // EXPERT_BLOCKS_PER_STEP
    return pl.pallas_call(
        _experts_kernel,
        grid_spec=pltpu.PrefetchScalarGridSpec(
            num_scalar_prefetch=4,
            grid=(nb,),
            in_specs=[
                pl.BlockSpec((step_rows, LANES), lambda i, be, nu, *_: (jnp.minimum(i, last_step(nu)), 0)),
                pl.BlockSpec(memory_space=pl.ANY),
                pl.BlockSpec(memory_space=pl.ANY),
                pl.BlockSpec(memory_space=pl.ANY),
            ],
            out_specs=pl.BlockSpec((step_rows, LANES), lambda i, *_: (i, 0)),
            scratch_shapes=[
                pltpu.VMEM((2, D_MODEL, D_EXPERT), F32),
                pltpu.VMEM((2, D_MODEL, D_EXPERT), F32),
                pltpu.VMEM((2, D_EXPERT, D_MODEL), F32),
                pltpu.VMEM((D_MODEL, D_EXPERT), BF16),
                pltpu.VMEM((D_MODEL, D_EXPERT), BF16),
                pltpu.VMEM((D_EXPERT, D_MODEL), BF16),
                pltpu.SemaphoreType.DMA((2,)),
            ],
        ),
        out_shape=jax.ShapeDtypeStruct(xb.shape, jnp.uint32),
        compiler_params=pltpu.CompilerParams(
            dimension_semantics=("arbitrary",), vmem_limit_bytes=EXPERTS_VMEM_LIMIT),
        name="experts",
    )(block_e, n_used, next_e, slot, xb, w_gate, w_up, w_down)


def _combine_kernel(dest_ref, route_ref, h_ref, g_ref, yb_hbm, o_ref, y1_ref, y2_ref, sem, *, tc):
    step = pl.program_id(0)
    n_steps = pl.num_programs(0)

    def gather(tile, slot):
        def start(i, _):
            tok = tile * tc + i
            _row_copy(yb_hbm, y1_ref.at[slot], dest_ref[2 * tok], i * SUBLANES, sem.at[slot]).start(priority=0)
            _row_copy(yb_hbm, y2_ref.at[slot], dest_ref[2 * tok + 1], i * SUBLANES, sem.at[slot]).start(priority=1)
            return 0

        lax.fori_loop(0, tc, start, 0, unroll=8)

    @pl.when(step == 0)
    def _():
        gather(0, 0)

    @pl.when(step + 1 < n_steps)
    def _():
        gather(step + 1, (step + 1) % 2)

    slot = step % 2
    pltpu.make_async_copy(yb_hbm.at[pl.ds(0, tc * SUBLANES)], y1_ref.at[slot], sem.at[slot]).wait()
    pltpu.make_async_copy(yb_hbm.at[pl.ds(0, tc * SUBLANES)], y2_ref.at[slot], sem.at[slot]).wait()

    rec = route_ref[...]
    lane = lax.broadcasted_iota(jnp.int32, (tc, LANES), 1)
    g1 = jnp.sum(jnp.where(lane == ROUTE_G, rec, 0.0), axis=-1, keepdims=True)
    g2 = jnp.sum(jnp.where(lane == ROUTE_G + 1, rec, 0.0), axis=-1, keepdims=True)
    y1_lo, y1_hi = _unpack_halves(_load_row_tiles(y1_ref.at[slot]))
    y2_lo, y2_hi = _unpack_halves(_load_row_tiles(y2_ref.at[slot]))
    h_lo = h_ref[:, 0:HALF] + (y1_lo * g1 + y2_lo * g2)
    h_hi = h_ref[:, HALF:] + (y1_hi * g1 + y2_hi * g2)
    ms = (jnp.sum(h_lo * h_lo, axis=-1, keepdims=True)
          + jnp.sum(h_hi * h_hi, axis=-1, keepdims=True)) * (1.0 / D_MODEL)
    scale = lax.rsqrt(ms + NORM_EPS)
    o_ref[:, 0:HALF] = h_lo * scale * g_ref[:, 0:HALF]
    o_ref[:, HALF:] = h_hi * scale * g_ref[:, HALF:]


def _combine(dest_flat, route, h1, g_final, yb, tc=256):
    t = h1.shape[0]
    return pl.pallas_call(
        functools.partial(_combine_kernel, tc=tc),
        grid_spec=pltpu.PrefetchScalarGridSpec(
            num_scalar_prefetch=1,
            grid=(t // tc,),
            in_specs=[
                pl.BlockSpec((tc, LANES), lambda i, d: (i, 0)),
                pl.BlockSpec((tc, D_MODEL), lambda i, d: (i, 0)),
                pl.BlockSpec((1, D_MODEL), lambda i, d: (0, 0)),
                pl.BlockSpec(memory_space=pl.ANY),
            ],
            out_specs=pl.BlockSpec((tc, D_MODEL), lambda i, d: (i, 0)),
            scratch_shapes=[
                pltpu.VMEM((2, tc * SUBLANES, LANES), jnp.uint32),
                pltpu.VMEM((2, tc * SUBLANES, LANES), jnp.uint32),
                pltpu.SemaphoreType.DMA((2,)),
            ],
        ),
        out_shape=jax.ShapeDtypeStruct((t, D_MODEL), F32),
        compiler_params=pltpu.CompilerParams(
            dimension_semantics=("arbitrary",), vmem_limit_bytes=VMEM_LIMIT),
        name="combine",
    )(dest_flat, route, h1, g_final, yb)


def _pad_cols(w, n):
    return jnp.pad(w, ((0, 0), (0, n - w.shape[1])))


def kernel(x, g_mix, w_in, w_out, da_lambda_q1, da_lambda_k1, da_lambda_q2, da_lambda_k2, da_subln_g,
           rw_mu_x, rw_mu_rkv, rw_w0, rw_w1, rw_w2, rw_a0, rw_a1, rw_a2, rw_g1, rw_g2, rw_k_k, rw_k_a,
           rw_r_k, rw_ln_w, rw_ln_b, g_ffn, moe_w_group, moe_b_group, moe_w_expert, moe_b_expert,
           moe_w_gate, moe_w_up, moe_w_down, g_final):
    b, s, d = x.shape
    t = b * s
    x2 = x.reshape(t, d)

    mu = rw_mu_x[0]
    lora = (rw_w1[0], rw_a1[0], rw_g1[0])
    keep = _pad_cols(jnp.concatenate([(1.0 - mu[i])[:, None] * w for i, w in enumerate(lora)], axis=1), LORA_PAD)
    prev = _pad_cols(jnp.concatenate([mu[i][:, None] * w for i, w in enumerate(lora)], axis=1), LORA_PAD)
    w_all = jnp.concatenate([w_in[0], keep, prev], axis=1).astype(BF16)
    proj = _inproj(x2, g_mix, w_all)

    slopes = jnp.asarray([2.0 ** (-8.0 * (i + 1) / DA_HEADS) for i in range(DA_HEADS)], F32)
    lam4 = jnp.concatenate([da_lambda_q1, da_lambda_k1, da_lambda_q2, da_lambda_k2], axis=0)
    o_da = _diffattn(proj, slopes, lam4, da_subln_g.reshape(DA_V_DIM, 1), b, s)

    w2cat = jnp.zeros((LORA_PAD, 3 * RW_WIDTH), F32)
    w2cat = w2cat.at[0:LORA_W, 0:RW_WIDTH].set(rw_w2[0])
    w2cat = w2cat.at[LORA_W:LORA_W + LORA_A, RW_WIDTH:2 * RW_WIDTH].set(rw_a2[0])
    w2cat = w2cat.at[LORA_W + LORA_A:LORA_W + LORA_A + LORA_G, 2 * RW_WIDTH:].set(rw_g2[0])
    o_rw = _rwkv(proj, rw_mu_rkv[0], rw_w0, rw_a0, w2cat.astype(BF16), rw_k_k, rw_k_a,
                 rw_r_k.reshape(1, RW_WIDTH), rw_ln_w, rw_ln_b, b, s)

    w_r = _pad_cols(jnp.concatenate([moe_w_group[0], moe_w_expert[0]], axis=1), LANES)
    wr_hi = w_r.astype(BF16)
    wr_lo = (w_r - wr_hi.astype(F32)).astype(BF16)
    b_r = _pad_cols(jnp.concatenate([moe_b_group[0], moe_b_expert[0].reshape(-1)])[None, :], LANES)
    h1, xn2, route, cnt = _outproj(x2, o_da, o_rw, w_out[0].astype(BF16), g_ffn, wr_hi, wr_lo, b_r)

    counts = cnt[0, EXPERT_LANE0:EXPERT_LANE0 + N_EXPERTS].astype(jnp.int32)
    pcounts = ((counts + EXPERT_BLOCK - 1) // EXPERT_BLOCK) * EXPERT_BLOCK
    expert_ids = jnp.arange(N_EXPERTS, dtype=jnp.int32)
    upto = expert_ids[None, :] <= expert_ids[:, None]
    pends = jnp.sum(jnp.where(upto, pcounts[None, :], 0), axis=1)
    pstarts = pends - pcounts
    e_idx = route[:, ROUTE_E:ROUTE_E + 2].astype(jnp.int32)
    rank = route[:, ROUTE_R:ROUTE_R + 2].astype(jnp.int32)
    row0 = jnp.sum(jnp.where(e_idx[..., None] == expert_ids, pstarts, 0), axis=-1)
    dest = ((row0 + rank) * SUBLANES).reshape(-1)
    n_blocks = (2 * t) // EXPERT_BLOCK + N_EXPERTS
    block_row0 = jnp.arange(n_blocks, dtype=jnp.int32) * EXPERT_BLOCK
    block_e = jnp.minimum(jnp.sum(pends[None, :] <= block_row0[:, None], axis=1), N_EXPERTS - 1).astype(jnp.int32)
    n_used = (pends[-1:] // EXPERT_BLOCK).astype(jnp.int32)
    has_rows = counts > 0
    later = (expert_ids[None, :] > expert_ids[:, None]) & has_rows[None, :]
    next_used = jnp.min(jnp.where(later, expert_ids[None, :], N_EXPERTS), axis=1)
    next_used = jnp.where(next_used == N_EXPERTS, -1, next_used).astype(jnp.int32)
    buffer_id = ((jnp.sum(jnp.where(upto & has_rows[None, :], 1, 0), axis=1) - 1) % 2).astype(jnp.int32)

    block_ids = jnp.arange(n_blocks, dtype=jnp.int32)
    last_block = pends // EXPERT_BLOCK - 1
    is_last = jnp.any(has_rows[None, :] & (block_ids[:, None] == last_block[None, :]), axis=1)
    clear_flags = (is_last | (block_ids >= n_used[0])).astype(jnp.int32)
    xb = _dispatch(dest, clear_flags, xn2, n_blocks)
    yb = _experts(block_e, n_used, next_used[block_e], buffer_id[block_e], xb,
                  moe_w_gate[0], moe_w_up[0], moe_w_down[0])
    out = _combine(dest, route, h1, g_final[None, :], yb)
    return out.reshape(b, s, d)
```

```python
import functools
import math

import jax
import jax.numpy as jnp
from jax import lax
from jax.experimental import pallas as pl
from jax.experimental.pallas import tpu as pltpu

F32 = jnp.float32
BF16 = jnp.bfloat16

D_MODEL = 2048
DA_WIDTH = 1024
RW_WIDTH = 1024
DA_HEADS = 8
DA_V_DIM = 128
DA_QK_DIM = 64
RW_N = 64
RW_PAIRS = RW_WIDTH // 128
LORA_W, LORA_A, LORA_G = 64, 64, 160
LORA_PAD = 384
IN_COLS = 3 * DA_WIDTH + 3 * RW_WIDTH
NC = IN_COLS + 2 * LORA_PAD
N_GROUPS = 4
EXPERTS_PER_GROUP = 8
N_EXPERTS = 32
D_EXPERT = 512
NORM_EPS = 1e-6
LN_X_EPS = 64e-5
LAM_INIT = 0.8 - 0.6 * math.exp(-0.3 * 0)
LANES = 128
CHUNK = 64
RW_TILE = 256
NCH = RW_TILE // CHUNK
EXPERT_BLOCK = 256
NEG = -1e30
VMEM_LIMIT = 48 * 1024 * 1024
EXPERTS_VMEM_LIMIT = 56 * 1024 * 1024


def _dot(a, b):
    return jnp.dot(a, b, preferred_element_type=F32)


def _dot_nt(a, b):
    return lax.dot_general(a, b, (((1,), (1,)), ((), ())), preferred_element_type=F32)


def _dot_tn(a, b):
    return lax.dot_general(a, b, (((0,), (0,)), ((), ())), preferred_element_type=F32)


def _sigmoid(x):
    return 1.0 / (1.0 + jnp.exp(-x))


HALF = D_MODEL // 2


def _pack_halves(xb):
    lo = lax.bitcast_convert_type(xb[:, :HALF].astype(F32), jnp.uint32) >> 16
    hi = lax.bitcast_convert_type(xb[:, HALF:].astype(F32), jnp.uint32) & jnp.uint32(0xFFFF0000)
    return hi | lo


def _unpack_halves(u):
    lo = lax.bitcast_convert_type(u << 16, F32)
    hi = lax.bitcast_convert_type(u & jnp.uint32(0xFFFF0000), F32)
    return lo, hi


SUBLANES = 8
assert HALF == SUBLANES * LANES


def _store_row_tiles(ref, u):
    n = u.shape[0]
    for k in range(SUBLANES):
        ref[pl.ds(k, n, stride=SUBLANES), :] = u[:, k * LANES:(k + 1) * LANES]


def _load_row_tiles(ref):
    n = ref.shape[0] // SUBLANES
    return jnp.concatenate([ref[pl.ds(k, n, stride=SUBLANES), :] for k in range(SUBLANES)], axis=1)


def _inproj_kernel(x_ref, g_ref, w_ref, o_ref, xn_ref):
    @pl.when(pl.program_id(1) == 0)
    def _():
        x = x_ref[...]
        ms = jnp.mean(x * x, axis=-1, keepdims=True)
        xn_ref[...] = (x * lax.rsqrt(ms + NORM_EPS) * g_ref[...]).astype(BF16)

    o_ref[...] = _dot(xn_ref[...], w_ref[...]).astype(o_ref.dtype)


def _inproj(x2, g, w_all, tm=1024, tn=2304):
    t = x2.shape[0]
    tm = min(tm, t)
    vmem_limit = 2 * (tm * D_MODEL * 4 + D_MODEL * tn * 2 + tm * tn * 2) + tm * D_MODEL * 2 + (6 << 20)
    return pl.pallas_call(
        _inproj_kernel,
        grid=(t // tm, NC // tn),
        in_specs=[
            pl.BlockSpec((tm, D_MODEL), lambda i, j: (i, 0)),
            pl.BlockSpec((1, D_MODEL), lambda i, j: (0, 0)),
            pl.BlockSpec((D_MODEL, tn), lambda i, j: (0, j)),
        ],
        out_specs=pl.BlockSpec((tm, tn), lambda i, j: (i, j)),
        out_shape=jax.ShapeDtypeStruct((t, NC), BF16),
        scratch_shapes=[pltpu.VMEM((tm, D_MODEL), BF16)],
        compiler_params=pltpu.CompilerParams(
            dimension_semantics=("parallel", "arbitrary"), vmem_limit_bytes=vmem_limit),
        name="inproj",
    )(x2, g, w_all)


DA_HEADS_PER_STEP = 4


def _da_kernel(qi_tab, kj_tab, slopes_ref, lam_ref, subg_ref, q_ref, k_ref, v_ref, o_ref,
               vt_ref, qz_ref, bias_ref, m_ref, l_ref, lfin_ref, acc_ref, s_ref, p_ref, alpha_ref,
               *, tq, seq):
    tk = tq
    nq = seq // tq
    n_trips = nq * (nq + 1) // 2
    heads = range(DA_HEADS_PER_STEP)
    chains = [(hd, c) for hd in heads for c in range(2)]
    slope = [slopes_ref[pl.program_id(1) * DA_HEADS_PER_STEP + hd] for hd in heads]
    cols = lambda hd: slice(hd * LANES, (hd + 1) * LANES)

    lane = lax.broadcasted_iota(jnp.int32, (tq, LANES), 1)
    krow = lax.broadcasted_iota(jnp.int32, (tk, tq), 0)
    qcol = lax.broadcasted_iota(jnp.int32, (tk, tq), 1)
    for hd in heads:
        for blk in range(seq // LANES):
            r = slice(blk * LANES, (blk + 1) * LANES)
            vt_ref[hd, :, r] = v_ref[r, cols(hd)].astype(F32).T.astype(BF16)
        for blk in range(nq):
            r = slice(blk * tq, (blk + 1) * tq)
            q = q_ref[r, cols(hd)] * jnp.asarray(DA_QK_DIM ** -0.5, BF16)
            qz_ref[2 * hd, r, :] = jnp.where(lane < DA_QK_DIM, q, jnp.zeros_like(q))
            qz_ref[2 * hd + 1, r, :] = jnp.where(lane >= DA_QK_DIM, q, jnp.zeros_like(q))
        alibi = slope[hd] * krow.astype(F32)
        bias_ref[hd, 0] = alibi
        bias_ref[hd, 1] = jnp.where(krow <= qcol, alibi, NEG)

    m_ref[...] = jnp.full_like(m_ref, NEG)
    l_ref[...] = jnp.zeros_like(l_ref)
    lfin_ref[...] = jnp.ones_like(lfin_ref)
    acc_ref[...] = jnp.zeros_like(acc_ref)
    p_ref[...] = jnp.zeros_like(p_ref)
    alpha_ref[...] = jnp.zeros_like(alpha_ref)

    lv = lam_ref[...]
    lam = (jnp.exp(jnp.sum(lv[0:1] * lv[1:2], axis=-1, keepdims=True))
           - jnp.exp(jnp.sum(lv[2:3] * lv[3:4], axis=-1, keepdims=True)) + LAM_INIT)

    def scores_to_scratch(t):
        qs = pl.multiple_of(qi_tab[t] * tq, tq)
        ks = pl.multiple_of(kj_tab[t] * tk, tk)
        diag = (qi_tab[t] == kj_tab[t]).astype(jnp.int32)
        for hd, c in chains:
            ch = 2 * hd + c
            s_ref[ch] = (_dot_nt(k_ref[pl.ds(ks, tk), cols(hd)], qz_ref[ch, pl.ds(qs, tq), :])
                         + bias_ref[hd, diag])

    def values_from_scratch(t):
        ks = pl.multiple_of(kj_tab[t] * tk, tk)
        for hd, c in chains:
            ch = 2 * hd + c
            acc_ref[ch] = alpha_ref[ch:ch + 1, :] * acc_ref[ch] + _dot(vt_ref[hd, :, pl.ds(ks, tk)], p_ref[ch])

    def softmax(t):
        first = kj_tab[t] == 0
        diag = kj_tab[t] == qi_tab[t]
        rel = ((kj_tab[t] - qi_tab[t]) * tk).astype(F32)
        p, alpha = [], []
        for hd, c in chains:
            ch = 2 * hd + c
            off = slope[hd] * rel
            s = s_ref[ch]
            m = jnp.where(first, NEG, m_ref[ch:ch + 1, :])
            l = jnp.where(first, 0.0, l_ref[ch:ch + 1, :])
            m_new = jnp.maximum(m, jnp.max(s, axis=0, keepdims=True) + off)
            pc = jnp.exp(s - (m_new - off))
            a = jnp.exp(m - m_new)
            l_new = a * l + jnp.sum(pc, axis=0, keepdims=True)
            m_ref[ch:ch + 1, :] = m_new
            l_ref[ch:ch + 1, :] = l_new
            lfin_ref[ch:ch + 1, :] = jnp.where(diag, l_new, lfin_ref[ch:ch + 1, :])
            p.append(pc)
            alpha.append(a)
        return p, alpha

    def publish(p, alpha):
        for ch in range(len(chains)):
            alpha_ref[ch:ch + 1, :] = alpha[ch]
            p_ref[ch] = p[ch].astype(BF16)

    def finalize(t):
        qs = pl.multiple_of(qi_tab[t] * tq, tq)
        for hd in heads:
            c0, c1 = 2 * hd, 2 * hd + 1
            o = acc_ref[c0] / lfin_ref[c0:c0 + 1, :] - lam * (acc_ref[c1] / lfin_ref[c1:c1 + 1, :])
            ms = jnp.mean(o * o, axis=0, keepdims=True)
            o = o * lax.rsqrt(ms + NORM_EPS) * subg_ref[...] * (1.0 - LAM_INIT)
            o_ref[pl.ds(qs, tq), cols(hd)] = o.T.astype(o_ref.dtype)

    scores_to_scratch(0)

    def body(t, _):
        prev = jnp.maximum(t - 1, 0)
        p, alpha = softmax(t)
        values_from_scratch(prev)
        scores_to_scratch(jnp.minimum(t + 1, n_trips - 1))
        publish(p, alpha)

        @pl.when(jnp.logical_and(t > 0, qi_tab[prev] == kj_tab[prev]))
        def _():
            finalize(prev)
        return 0

    lax.fori_loop(0, n_trips, body, 0)
    values_from_scratch(n_trips - 1)
    finalize(n_trips - 1)


def _diffattn(proj, slopes, lam4, subg_col, b, s, tq=256):
    nq = s // tq
    nh = DA_HEADS_PER_STEP
    groups = DA_HEADS // nh
    width = nh * LANES
    pairs = [(qi, kj) for qi in range(nq) for kj in range(qi + 1)]
    qi_tab = jnp.asarray([p[0] for p in pairs], jnp.int32)
    kj_tab = jnp.asarray([p[1] for p in pairs], jnp.int32)
    return pl.pallas_call(
        functools.partial(_da_kernel, tq=tq, seq=s),
        grid_spec=pltpu.PrefetchScalarGridSpec(
            num_scalar_prefetch=2,
            grid=(b, groups),
            in_specs=[
                pl.BlockSpec(memory_space=pltpu.SMEM),
                pl.BlockSpec((4, DA_QK_DIM), lambda bi, g, *_: (0, 0)),
                pl.BlockSpec((DA_V_DIM, 1), lambda bi, g, *_: (0, 0)),
                pl.BlockSpec((s, width), lambda bi, g, *_: (bi, g)),
                pl.BlockSpec((s, width), lambda bi, g, *_: (bi, groups + g)),
                pl.BlockSpec((s, width), lambda bi, g, *_: (bi, 2 * groups + g)),
            ],
            out_specs=pl.BlockSpec((s, width), lambda bi, g, *_: (bi, g)),
            scratch_shapes=[
                pltpu.VMEM((nh, DA_V_DIM, s), BF16),
                pltpu.VMEM((2 * nh, s, LANES), BF16),
                pltpu.VMEM((nh, 2, tq, tq), F32),
                pltpu.VMEM((2 * nh, tq), F32),
                pltpu.VMEM((2 * nh, tq), F32),
                pltpu.VMEM((2 * nh, tq), F32),
                pltpu.VMEM((2 * nh, DA_V_DIM, tq), F32),
                pltpu.VMEM((2 * nh, tq, tq), F32),
                pltpu.VMEM((2 * nh, tq, tq), BF16),
                pltpu.VMEM((2 * nh, tq), F32),
            ],
        ),
        out_shape=jax.ShapeDtypeStruct((b * s, DA_WIDTH), BF16),
        compiler_params=pltpu.CompilerParams(
            dimension_semantics=("parallel", "arbitrary"), vmem_limit_bytes=VMEM_LIMIT),
        name="diffattn",
    )(qi_tab, kj_tab, slopes, lam4, subg_col, proj, proj, proj)


def _rwkv_kernel(r_ref, k_ref, v_ref, p1_ref, p2_ref, mu_ref, w0_ref, a0_ref, w2_ref, kk_ref, ka_ref,
                 rk_ref, lnw_ref, lnb_ref, o_ref, state_ref, crkv_ref, cp2_ref):
    L, TT = CHUNK, RW_TILE
    c = pl.program_id(1)

    @pl.when(c == 0)
    def _():
        state_ref[...] = jnp.zeros_like(state_ref)
        crkv_ref[...] = jnp.zeros_like(crkv_ref)
        cp2_ref[...] = jnp.zeros_like(cp2_ref)

    first = lax.broadcasted_iota(jnp.int32, (TT, 1), 0) == 0

    def shift(x, carry):
        return jnp.where(first, carry, pltpu.roll(x, 1, 0))

    r_raw = r_ref[...].astype(F32)
    k_raw = k_ref[...].astype(F32)
    v_raw = v_ref[...].astype(F32)
    p2 = p2_ref[...].astype(F32)
    rs = shift(r_raw, crkv_ref[:, 0:RW_WIDTH])
    ks = shift(k_raw, crkv_ref[:, RW_WIDTH:2 * RW_WIDTH])
    vs = shift(v_raw, crkv_ref[:, 2 * RW_WIDTH:3 * RW_WIDTH])
    lin = p1_ref[...].astype(F32) + shift(p2, cp2_ref[...])
    crkv_ref[:, 0:RW_WIDTH] = r_raw[TT - 1:TT, :]
    crkv_ref[:, RW_WIDTH:2 * RW_WIDTH] = k_raw[TT - 1:TT, :]
    crkv_ref[:, 2 * RW_WIDTH:3 * RW_WIDTH] = v_raw[TT - 1:TT, :]
    cp2_ref[...] = p2[TT - 1:TT, :]

    r = r_raw + (rs - r_raw) * mu_ref[0:1, :]
    k = k_raw + (ks - k_raw) * mu_ref[1:2, :]
    v = v_raw + (vs - v_raw) * mu_ref[2:3, :]

    ll = lax.broadcasted_iota(jnp.int32, (TT, LORA_PAD), 1)
    z = jnp.where(ll < LORA_W, jnp.tanh(lin),
                  jnp.where(ll < LORA_W + LORA_A, lin,
                            jnp.where(ll < LORA_W + LORA_A + LORA_G, _sigmoid(lin), 0.0)))
    d = _dot(z.astype(BF16), w2_ref[...])
    logw = -_sigmoid(w0_ref[...] + d[:, 0:RW_WIDTH]) * math.exp(-0.5)
    a_sig = _sigmoid(a0_ref[...] + d[:, RW_WIDTH:2 * RW_WIDTH])
    g = d[:, 2 * RW_WIDTH:3 * RW_WIDTH]

    li = lax.broadcasted_iota(jnp.int32, (LANES, LANES), 0)
    lj = lax.broadcasted_iota(jnp.int32, (LANES, LANES), 1)
    same_head = jnp.where((li // RW_N) == (lj // RW_N), 1.0, 0.0).astype(BF16)

    def head_sum(x):
        xb = x.astype(BF16)
        return jnp.concatenate(
            [_dot(xb[:, p * LANES:(p + 1) * LANES], same_head) for p in range(RW_PAIRS)], axis=1)

    kk = k * kk_ref[...]
    kk = kk * lax.rsqrt(jnp.maximum(head_sum(kk * kk), 1e-24))
    k = k * (1.0 + (a_sig - 1.0) * ka_ref[...])
    bonus = head_sum(r * k * rk_ref[...])

    ti = lax.broadcasted_iota(jnp.int32, (TT, TT), 0)
    tj = lax.broadcasted_iota(jnp.int32, (TT, TT), 1)
    tri = jnp.where(ti >= tj, jnp.where((ti // L) == (tj // L), 1.0, 0.0), 0.0).astype(BF16)
    w_hi = logw.astype(BF16)
    w_r1 = logw - w_hi.astype(F32)
    w_mid = w_r1.astype(BF16)
    w_lo = (w_r1 - w_mid.astype(F32)).astype(BF16)
    cw = _dot(tri, w_hi) + _dot(tri, w_mid) + _dot(tri, w_lo)
    cw_last = [cw[(ci + 1) * L - 1:(ci + 1) * L, :] for ci in range(NCH)]
    e_in = jnp.exp(cw)
    e_out = jnp.exp(-cw)
    e_end = jnp.exp(jnp.concatenate([jnp.broadcast_to(cl, (L, RW_WIDTH)) for cl in cw_last], axis=0) - cw)
    w_end = [jnp.exp(cl) for cl in cw_last]

    at = -kk * jnp.exp(cw - logw)
    rt = r * e_in
    b = kk * a_sig
    bt = b * e_out
    kt = k * e_out
    bd = b * e_end
    kd = k * e_end

    lane = lax.broadcasted_iota(jnp.int32, (L, LANES), 1)
    lo = lane < RW_N

    def stack(x, ci, p):
        xs = x[ci * L:(ci + 1) * L, p * LANES:(p + 1) * LANES]
        return jnp.concatenate([jnp.where(lo, xs, 0.0), jnp.where(lo, 0.0, xs)], axis=0).astype(BF16)

    si = lax.broadcasted_iota(jnp.int32, (2 * L, 2 * L), 0)
    sj = lax.broadcasted_iota(jnp.int32, (2 * L, 2 * L), 1)
    strict = (si % L) > (sj % L)
    incl = (si % L) >= (sj % L)
    eye = jnp.where(si == sj, 1.0, 0.0)

    pairs = range(RW_PAIRS)
    cat0 = lambda *xs: jnp.concatenate(xs, axis=0)
    cat1 = lambda *xs: jnp.concatenate(xs, axis=1)
    bf = lambda x: x.astype(BF16)
    def chunk_setup(ci):
        a_s = [stack(at, ci, p) for p in pairs]
        r_s = [stack(rt, ci, p) for p in pairs]
        v_s = [stack(v, ci, p) for p in pairs]
        bk = [cat0(stack(bt, ci, p), stack(kt, ci, p)) for p in pairs]
        bkd = [cat0(stack(bd, ci, p), stack(kd, ci, p)) for p in pairs]
        aa = [_dot_nt(a_s[p], bk[p]) for p in pairs]
        rr = [_dot_nt(r_s[p], bk[p]) for p in pairs]
        a_ab = [jnp.where(strict, aa[p][:, 0:2 * L], 0.0) for p in pairs]
        a_ak = [bf(jnp.where(strict, aa[p][:, 2 * L:4 * L], 0.0)) for p in pairs]
        a_rb = [bf(jnp.where(incl, rr[p][:, 0:2 * L], 0.0)) for p in pairs]
        a_rk = [bf(jnp.where(incl, rr[p][:, 2 * L:4 * L], 0.0)) for p in pairs]
        ab = [bf(a) for a in a_ab]
        qpow = [_dot(ab[p], ab[p]) for p in pairs]
        inv = [eye + a_ab[p] for p in pairs]
        for _ in range(int(math.log2(L)) - 2):
            qb = [bf(q) for q in qpow]
            res = [_dot(qb[p], cat1(qb[p], bf(inv[p]))) for p in pairs]
            qpow = [res[p][:, 0:2 * L] for p in pairs]
            inv = [inv[p] + res[p][:, 2 * L:4 * L] for p in pairs]
        inv = [bf(inv[p] + _dot(bf(qpow[p]), bf(inv[p]))) for p in pairs]
        w_col = [jnp.broadcast_to(w_end[ci][:, p * LANES:(p + 1) * LANES], (LANES, LANES)).T for p in pairs]
        return a_s, r_s, v_s, bkd, a_ak, a_rb, a_rk, inv, w_col

    def chunk_advance(setup, h):
        a_s, r_s, v_s, bkd, a_ak, a_rb, a_rk, inv, w_col = setup
        h_b = [bf(hp) for hp in h]
        x = [_dot(cat1(a_s[p], a_ak[p]), cat0(h_b[p], v_s[p])) for p in pairs]
        u_b = [bf(_dot(inv[p], bf(x[p]))) for p in pairs]
        y2 = [_dot(cat1(r_s[p], a_rb[p], a_rk[p]), cat0(h_b[p], u_b[p], v_s[p])) for p in pairs]
        h = [w_col[p] * h[p] + _dot_tn(bkd[p], cat0(u_b[p], v_s[p])) for p in pairs]
        return h, cat1(*[y2[p][0:L] + y2[p][L:2 * L] for p in pairs])

    h = [state_ref[p] for p in pairs]
    ys = []
    setup = chunk_setup(0)
    for ci in range(NCH):
        ahead = chunk_setup(ci + 1) if ci + 1 < NCH else None
        h, y_chunk = chunk_advance(setup, h)
        ys.append(y_chunk)
        setup = ahead
    for p in pairs:
        state_ref[p] = h[p]
    y = cat0(*ys)

    mean = head_sum(y) * (1.0 / RW_N)
    yc = y - mean
    var = head_sum(yc * yc) * (1.0 / RW_N)
    y = yc * lax.rsqrt(var + LN_X_EPS) * lnw_ref[...] + lnb_ref[...]
    y = y + bonus * v
    o_ref[...] = (y * g).astype(o_ref.dtype)


def _rwkv(proj, mu_rkv, w0, a0, w2cat, k_k, k_a, r_k, ln_w, ln_b, b, s):
    tt = RW_TILE
    nc = s // tt
    row = lambda n: pl.BlockSpec((1, n), lambda bi, c: (0, 0))
    rkv_col0 = 3 * DA_WIDTH // RW_WIDTH
    lora_col0 = IN_COLS // LORA_PAD
    return pl.pallas_call(
        _rwkv_kernel,
        grid=(b, nc),
        in_specs=[
            pl.BlockSpec((tt, RW_WIDTH), lambda bi, c: (bi * nc + c, rkv_col0)),
            pl.BlockSpec((tt, RW_WIDTH), lambda bi, c: (bi * nc + c, rkv_col0 + 1)),
            pl.BlockSpec((tt, RW_WIDTH), lambda bi, c: (bi * nc + c, rkv_col0 + 2)),
            pl.BlockSpec((tt, LORA_PAD), lambda bi, c: (bi * nc + c, lora_col0)),
            pl.BlockSpec((tt, LORA_PAD), lambda bi, c: (bi * nc + c, lora_col0 + 1)),
            pl.BlockSpec((3, RW_WIDTH), lambda bi, c: (0, 0)),
            row(RW_WIDTH), row(RW_WIDTH),
            pl.BlockSpec((LORA_PAD, 3 * RW_WIDTH), lambda bi, c: (0, 0)),
            row(RW_WIDTH), row(RW_WIDTH), row(RW_WIDTH), row(RW_WIDTH), row(RW_WIDTH),
        ],
        out_specs=pl.BlockSpec((tt, RW_WIDTH), lambda bi, c: (bi * nc + c, 0)),
        out_shape=jax.ShapeDtypeStruct((b * s, RW_WIDTH), BF16),
        scratch_shapes=[
            pltpu.VMEM((RW_PAIRS, LANES, LANES), F32),
            pltpu.VMEM((1, 3 * RW_WIDTH), F32),
            pltpu.VMEM((1, LORA_PAD), F32),
        ],
        compiler_params=pltpu.CompilerParams(
            dimension_semantics=("parallel", "arbitrary"), vmem_limit_bytes=VMEM_LIMIT),
        name="rwkv",
    )(proj, proj, proj, proj, proj, mu_rkv, w0, a0, w2cat, k_k, k_a, r_k, ln_w, ln_b)


ROUTE_E, ROUTE_G, ROUTE_R = 0, 2, 4
EXPERT_LANE0 = N_GROUPS


def _outproj_kernel(x_ref, oda_ref, orw_ref, wout_ref, g_ref, wrh_ref, wrl_ref, br_ref,
                    h_ref, xn_ref, route_ref, cnt_ref, base_ref, *, tm):
    @pl.when(pl.program_id(0) == 0)
    def _():
        base_ref[...] = jnp.zeros_like(base_ref)

    mix = _dot(oda_ref[...], wout_ref[0:DA_WIDTH, :]) + _dot(orw_ref[...], wout_ref[DA_WIDTH:, :])
    h = x_ref[...] + mix
    h_ref[...] = h
    ms = jnp.mean(h * h, axis=-1, keepdims=True)
    xn = h * lax.rsqrt(ms + NORM_EPS) * g_ref[...]
    x_hi = xn.astype(BF16)
    _store_row_tiles(xn_ref, _pack_halves(x_hi))

    x_lo = (xn - x_hi.astype(F32)).astype(BF16)
    hi_both = _dot(x_hi, jnp.concatenate([wrh_ref[...], wrl_ref[...]], axis=1))
    lg = (hi_both[:, 0:LANES] + _dot(x_lo, wrh_ref[...]) + hi_both[:, LANES:]) + br_ref[...]

    lane = lax.broadcasted_iota(jnp.int32, (tm, LANES), 1)
    big = jnp.int32(1 << 20)
    gl = jnp.where(lane < N_GROUPS, lg, NEG)
    gmax = jnp.max(gl, axis=-1, keepdims=True)
    grp = jnp.min(jnp.where(gl == gmax, lane, big), axis=-1, keepdims=True)
    p_grp = 1.0 / jnp.sum(jnp.exp(gl - gmax), axis=-1, keepdims=True)
    eg = jnp.where(lane >= EXPERT_LANE0, (lane - EXPERT_LANE0) // EXPERTS_PER_GROUP, -1)
    el = jnp.where(eg == grp, lg, NEG)
    emax = jnp.max(el, axis=-1, keepdims=True)
    pe = jnp.exp(el - emax)
    probs = pe / jnp.sum(pe, axis=-1, keepdims=True)
    probs = jnp.where(eg == grp, probs, -1.0)
    p1 = jnp.max(probs, axis=-1, keepdims=True)
    i1 = jnp.min(jnp.where(probs == p1, lane, big), axis=-1, keepdims=True)
    probs2 = jnp.where(lane == i1, -1.0, probs)
    p2 = jnp.max(probs2, axis=-1, keepdims=True)
    i2 = jnp.min(jnp.where(probs2 == p2, lane, big), axis=-1, keepdims=True)
    gate1 = p_grp * p1 / (p1 + p2)
    gate2 = p_grp * p2 / (p1 + p2)

    oh1 = jnp.where(lane == i1, 1.0, 0.0)
    oh2 = jnp.where(lane == i2, 1.0, 0.0)
    ri = lax.broadcasted_iota(jnp.int32, (tm, tm), 0)
    rj = lax.broadcasted_iota(jnp.int32, (tm, tm), 1)
    lower = jnp.where(ri > rj, 1.0, 0.0).astype(BF16)
    base = base_ref[...]
    tot1 = jnp.sum(oh1, axis=0, keepdims=True)
    before = _dot(lower, jnp.concatenate([oh1, oh2], axis=1).astype(BF16))
    c1 = base + before[:, 0:LANES]
    c2 = base + tot1 + before[:, LANES:]
    rank1 = jnp.sum(oh1 * c1, axis=-1, keepdims=True)
    rank2 = jnp.sum(oh2 * c2, axis=-1, keepdims=True)
    base = base + tot1 + jnp.sum(oh2, axis=0, keepdims=True)
    base_ref[...] = base
    cnt_ref[...] = base

    e1 = (i1 - EXPERT_LANE0).astype(F32)
    e2 = (i2 - EXPERT_LANE0).astype(F32)
    rec = jnp.zeros((tm, LANES), F32)
    for ln, val in ((ROUTE_E, e1), (ROUTE_E + 1, e2), (ROUTE_G, gate1), (ROUTE_G + 1, gate2),
                    (ROUTE_R, rank1), (ROUTE_R + 1, rank2)):
        rec = jnp.where(lane == ln, val, rec)
    route_ref[...] = rec


def _outproj(x2, o_da, o_rw, w_out, g_ffn, wr_hi, wr_lo, b_r, tm=512):
    t = x2.shape[0]
    const = lambda shape: pl.BlockSpec(shape, lambda i: (0, 0))
    return pl.pallas_call(
        functools.partial(_outproj_kernel, tm=tm),
        grid=(t // tm,),
        in_specs=[
            pl.BlockSpec((tm, D_MODEL), lambda i: (i, 0)),
            pl.BlockSpec((tm, DA_WIDTH), lambda i: (i, 0)),
            pl.BlockSpec((tm, RW_WIDTH), lambda i: (i, 0)),
            const((D_MODEL, D_MODEL)),
            const((1, D_MODEL)),
            const((D_MODEL, LANES)),
            const((D_MODEL, LANES)),
            const((1, LANES)),
        ],
        out_specs=[
            pl.BlockSpec((tm, D_MODEL), lambda i: (i, 0)),
            pl.BlockSpec((tm * SUBLANES, LANES), lambda i: (i, 0)),
            pl.BlockSpec((tm, LANES), lambda i: (i, 0)),
            const((1, LANES)),
        ],
        out_shape=[
            jax.ShapeDtypeStruct((t, D_MODEL), F32),
            jax.ShapeDtypeStruct((t * SUBLANES, LANES), jnp.uint32),
            jax.ShapeDtypeStruct((t, LANES), F32),
            jax.ShapeDtypeStruct((1, LANES), F32),
        ],
        scratch_shapes=[pltpu.VMEM((1, LANES), F32)],
        compiler_params=pltpu.CompilerParams(
            dimension_semantics=("arbitrary",), vmem_limit_bytes=VMEM_LIMIT),
        name="outproj_router",
    )(x2, o_da, o_rw, w_out, g_ffn, wr_hi, wr_lo, b_r)


def _row_copy(src_ref, dst_ref, src_row8, dst_row8, sem):
    src = src_ref.at[pl.ds(pl.multiple_of(src_row8, SUBLANES), SUBLANES)]
    dst = dst_ref.at[pl.ds(pl.multiple_of(dst_row8, SUBLANES), SUBLANES)]
    return pltpu.make_async_copy(src, dst, sem)


def _dispatch_kernel(dest_ref, zflag_ref, x_ref, xb_hbm, zero_ref, sem, zsem, *, td, n_blocks):
    t0 = pl.program_id(0) * td
    block_rows = EXPERT_BLOCK * SUBLANES

    @pl.when(pl.program_id(0) == 0)
    def _():
        zero_ref[...] = jnp.zeros_like(zero_ref)

        def clear(blk):
            dst = xb_hbm.at[pl.ds(pl.multiple_of(blk * block_rows, block_rows), block_rows)]
            return pltpu.make_async_copy(zero_ref, dst, zsem)

        def start_clear(blk, _):
            @pl.when(zflag_ref[blk] != 0)
            def _():
                clear(blk).start()
            return 0

        def wait_clear(blk, _):
            @pl.when(zflag_ref[blk] != 0)
            def _():
                clear(blk).wait()
            return 0

        lax.fori_loop(0, n_blocks, start_clear, 0)
        lax.fori_loop(0, n_blocks, wait_clear, 0)

    def start(i, _):
        tok = t0 + i
        _row_copy(x_ref, xb_hbm, i * SUBLANES, dest_ref[2 * tok], sem).start(priority=0)
        _row_copy(x_ref, xb_hbm, i * SUBLANES, dest_ref[2 * tok + 1], sem).start(priority=1)
        return 0

    lax.fori_loop(0, td, start, 0, unroll=8)
    for _ in range(2):
        pltpu.make_async_copy(x_ref, xb_hbm.at[pl.ds(0, td * SUBLANES)], sem).wait()


def _dispatch(dest_flat, clear_flags, xn2, n_blocks, td=1024):
    t = xn2.shape[0] // SUBLANES
    td = min(td, t)
    block_rows = EXPERT_BLOCK * SUBLANES
    return pl.pallas_call(
        functools.partial(_dispatch_kernel, td=td, n_blocks=n_blocks),
        grid_spec=pltpu.PrefetchScalarGridSpec(
            num_scalar_prefetch=2,
            grid=(t // td,),
            in_specs=[pl.BlockSpec((td * SUBLANES, LANES), lambda i, *_: (i, 0))],
            out_specs=pl.BlockSpec(memory_space=pl.ANY),
            scratch_shapes=[
                pltpu.VMEM((block_rows, LANES), jnp.uint32),
                pltpu.SemaphoreType.DMA,
                pltpu.SemaphoreType.DMA,
            ],
        ),
        out_shape=jax.ShapeDtypeStruct((n_blocks * block_rows, LANES), jnp.uint32),
        compiler_params=pltpu.CompilerParams(dimension_semantics=("arbitrary",)),
        name="dispatch",
    )(dest_flat, clear_flags, xn2)


EXPERT_BLOCKS_PER_STEP = 4


def _experts_kernel(be_ref, nu_ref, nxt_ref, slot_ref, xb_ref, wg_hbm, wu_hbm, wd_hbm, yb_ref,
                    wgf_ref, wuf_ref, wdf_ref, wgb_ref, wub_ref, wdb_ref, sem):
    block_rows = EXPERT_BLOCK * SUBLANES

    def weight_copies(e, slot):
        return [pltpu.make_async_copy(src.at[e], dst.at[slot], sem.at[slot])
                for src, dst in ((wg_hbm, wgf_ref), (wu_hbm, wuf_ref), (wd_hbm, wdf_ref))]

    def row_block(sub):
        i = pl.program_id(0) * EXPERT_BLOCKS_PER_STEP + sub
        x_ref = xb_ref.at[pl.ds(sub * block_rows, block_rows)]
        y_ref = yb_ref.at[pl.ds(sub * block_rows, block_rows)]
        used = i < nu_ref[0]
        new_expert = jnp.logical_or(i == 0, be_ref[i] != be_ref[jnp.maximum(i - 1, 0)])

        def swiglu(wg_lo, wg_hi, wu_lo, wu_hi, wd):
            x_lo, x_hi = (half.astype(BF16) for half in _unpack_halves(_load_row_tiles(x_ref)))
            hg = _dot(x_lo, wg_lo) + _dot(x_hi, wg_hi)
            hu = _dot(x_lo, wu_lo) + _dot(x_hi, wu_hi)
            hid = hg * _sigmoid(hg) * hu
            _store_row_tiles(y_ref, _pack_halves(_dot(hid.astype(BF16), wd).astype(BF16)))

        @pl.when(jnp.logical_and(used, new_expert))
        def _():
            slot = slot_ref[i]

            @pl.when(i == 0)
            def _():
                for cp in weight_copies(be_ref[i], slot):
                    cp.start()

            for cp in weight_copies(be_ref[i], slot):
                cp.wait()

            @pl.when(nxt_ref[i] >= 0)
            def _():
                for cp in weight_copies(nxt_ref[i], 1 - slot):
                    cp.start()

            wg, wu, wd = (w[slot].astype(BF16) for w in (wgf_ref, wuf_ref, wdf_ref))
            wgb_ref[...] = wg
            wub_ref[...] = wu
            wdb_ref[...] = wd
            swiglu(wg[0:HALF, :], wg[HALF:, :], wu[0:HALF, :], wu[HALF:, :], wd)

        @pl.when(jnp.logical_and(used, jnp.logical_not(new_expert)))
        def _():
            swiglu(wgb_ref[0:HALF, :], wgb_ref[HALF:, :], wub_ref[0:HALF, :], wub_ref[HALF:, :], wdb_ref[...])

        @pl.when(jnp.logical_not(used))
        def _():
            y_ref[...] = jnp.zeros_like(y_ref)

    for sub in range(EXPERT_BLOCKS_PER_STEP):
        row_block(sub)


def _experts(block_e, n_used, next_e, slot, xb, w_gate, w_up, w_down):
    step_rows = EXPERT_BLOCKS_PER_STEP * EXPERT_BLOCK * SUBLANES
    nb = xb.shape[0] // step_rows
    last_step = lambda nu: (nu[0] - 1) // EXPERT_BLOCKS_PER_STEP
    return pl.pallas_call(
        _experts_kernel,
        grid_spec=pltpu.PrefetchScalarGridSpec(
            num_scalar_prefetch=4,
            grid=(nb,),
            in_specs=[
                pl.BlockSpec((step_rows, LANES), lambda i, be, nu, *_: (jnp.minimum(i, last_step(nu)), 0)),
                pl.BlockSpec(memory_space=pl.ANY),
                pl.BlockSpec(memory_space=pl.ANY),
                pl.BlockSpec(memory_space=pl.ANY),
            ],
            out_specs=pl.BlockSpec((step_rows, LANES), lambda i, *_: (i, 0)),
            scratch_shapes=[
                pltpu.VMEM((2, D_MODEL, D_EXPERT), F32),
                pltpu.VMEM((2, D_MODEL, D_EXPERT), F32),
                pltpu.VMEM((2, D_EXPERT, D_MODEL), F32),
                pltpu.VMEM((D_MODEL, D_EXPERT), BF16),
                pltpu.VMEM((D_MODEL, D_EXPERT), BF16),
                pltpu.VMEM((D_EXPERT, D_MODEL), BF16),
                pltpu.SemaphoreType.DMA((2,)),
            ],
        ),
        out_shape=jax.ShapeDtypeStruct(xb.shape, jnp.uint32),
        compiler_params=pltpu.CompilerParams(
            dimension_semantics=("arbitrary",), vmem_limit_bytes=EXPERTS_VMEM_LIMIT),
        name="experts",
    )(block_e, n_used, next_e, slot, xb, w_gate, w_up, w_down)


def _combine_kernel(dest_ref, route_ref, h_ref, g_ref, yb_hbm, o_ref, y1_ref, y2_ref, sem, *, tc):
    step = pl.program_id(0)
    n_steps = pl.num_programs(0)

    def gather(tile, slot):
        def start(i, _):
            tok = tile * tc + i
            _row_copy(yb_hbm, y1_ref.at[slot], dest_ref[2 * tok], i * SUBLANES, sem.at[slot]).start(priority=0)
            _row_copy(yb_hbm, y2_ref.at[slot], dest_ref[2 * tok + 1], i * SUBLANES, sem.at[slot]).start(priority=1)
            return 0

        lax.fori_loop(0, tc, start, 0, unroll=8)

    @pl.when(step == 0)
    def _():
        gather(0, 0)

    @pl.when(step + 1 < n_steps)
    def _():
        gather(step + 1, (step + 1) % 2)

    slot = step % 2
    pltpu.make_async_copy(yb_hbm.at[pl.ds(0, tc * SUBLANES)], y1_ref.at[slot], sem.at[slot]).wait()
    pltpu.make_async_copy(yb_hbm.at[pl.ds(0, tc * SUBLANES)], y2_ref.at[slot], sem.at[slot]).wait()

    rec = route_ref[...]
    lane = lax.broadcasted_iota(jnp.int32, (tc, LANES), 1)
    g1 = jnp.sum(jnp.where(lane == ROUTE_G, rec, 0.0), axis=-1, keepdims=True)
    g2 = jnp.sum(jnp.where(lane == ROUTE_G + 1, rec, 0.0), axis=-1, keepdims=True)
    y1_lo, y1_hi = _unpack_halves(_load_row_tiles(y1_ref.at[slot]))
    y2_lo, y2_hi = _unpack_halves(_load_row_tiles(y2_ref.at[slot]))
    h_lo = h_ref[:, 0:HALF] + (y1_lo * g1 + y2_lo * g2)
    h_hi = h_ref[:, HALF:] + (y1_hi * g1 + y2_hi * g2)
    ms = (jnp.sum(h_lo * h_lo, axis=-1, keepdims=True)
          + jnp.sum(h_hi * h_hi, axis=-1, keepdims=True)) * (1.0 / D_MODEL)
    scale = lax.rsqrt(ms + NORM_EPS)
    o_ref[:, 0:HALF] = h_lo * scale * g_ref[:, 0:HALF]
    o_ref[:, HALF:] = h_hi * scale * g_ref[:, HALF:]


def _combine(dest_flat, route, h1, g_final, yb, tc=256):
    t = h1.shape[0]
    return pl.pallas_call(
        functools.partial(_combine_kernel, tc=tc),
        grid_spec=pltpu.PrefetchScalarGridSpec(
            num_scalar_prefetch=1,
            grid=(t // tc,),
            in_specs=[
                pl.BlockSpec((tc, LANES), lambda i, d: (i, 0)),
                pl.BlockSpec((tc, D_MODEL), lambda i, d: (i, 0)),
                pl.BlockSpec((1, D_MODEL), lambda i, d: (0, 0)),
                pl.BlockSpec(memory_space=pl.ANY),
            ],
            out_specs=pl.BlockSpec((tc, D_MODEL), lambda i, d: (i, 0)),
            scratch_shapes=[
                pltpu.VMEM((2, tc * SUBLANES, LANES), jnp.uint32),
                pltpu.VMEM((2, tc * SUBLANES, LANES), jnp.uint32),
                pltpu.SemaphoreType.DMA((2,)),
            ],
        ),
        out_shape=jax.ShapeDtypeStruct((t, D_MODEL), F32),
        compiler_params=pltpu.CompilerParams(
            dimension_semantics=("arbitrary",), vmem_limit_bytes=VMEM_LIMIT),
        name="combine",
    )(dest_flat, route, h1, g_final, yb)


def _pad_cols(w, n):
    return jnp.pad(w, ((0, 0), (0, n - w.shape[1])))


def kernel(x, g_mix, w_in, w_out, da_lambda_q1, da_lambda_k1, da_lambda_q2, da_lambda_k2, da_subln_g,
           rw_mu_x, rw_mu_rkv, rw_w0, rw_w1, rw_w2, rw_a0, rw_a1, rw_a2, rw_g1, rw_g2, rw_k_k, rw_k_a,
           rw_r_k, rw_ln_w, rw_ln_b, g_ffn, moe_w_group, moe_b_group, moe_w_expert, moe_b_expert,
           moe_w_gate, moe_w_up, moe_w_down, g_final):
    b, s, d = x.shape
    t = b * s
    x2 = x.reshape(t, d)

    mu = rw_mu_x[0]
    lora = (rw_w1[0], rw_a1[0], rw_g1[0])
    keep = _pad_cols(jnp.concatenate([(1.0 - mu[i])[:, None] * w for i, w in enumerate(lora)], axis=1), LORA_PAD)
    prev = _pad_cols(jnp.concatenate([mu[i][:, None] * w for i, w in enumerate(lora)], axis=1), LORA_PAD)
    w_all = jnp.concatenate([w_in[0], keep, prev], axis=1).astype(BF16)
    proj = _inproj(x2, g_mix, w_all)

    slopes = jnp.asarray([2.0 ** (-8.0 * (i + 1) / DA_HEADS) for i in range(DA_HEADS)], F32)
    lam4 = jnp.concatenate([da_lambda_q1, da_lambda_k1, da_lambda_q2, da_lambda_k2], axis=0)
    o_da = _diffattn(proj, slopes, lam4, da_subln_g.reshape(DA_V_DIM, 1), b, s)

    w2cat = jnp.zeros((LORA_PAD, 3 * RW_WIDTH), F32)
    w2cat = w2cat.at[0:LORA_W, 0:RW_WIDTH].set(rw_w2[0])
    w2cat = w2cat.at[LORA_W:LORA_W + LORA_A, RW_WIDTH:2 * RW_WIDTH].set(rw_a2[0])
    w2cat = w2cat.at[LORA_W + LORA_A:LORA_W + LORA_A + LORA_G, 2 * RW_WIDTH:].set(rw_g2[0])
    o_rw = _rwkv(proj, rw_mu_rkv[0], rw_w0, rw_a0, w2cat.astype(BF16), rw_k_k, rw_k_a,
                 rw_r_k.reshape(1, RW_WIDTH), rw_ln_w, rw_ln_b, b, s)

    w_r = _pad_cols(jnp.concatenate([moe_w_group[0], moe_w_expert[0]], axis=1), LANES)
    wr_hi = w_r.astype(BF16)
    wr_lo = (w_r - wr_hi.astype(F32)).astype(BF16)
    b_r = _pad_cols(jnp.concatenate([moe_b_group[0], moe_b_expert[0].reshape(-1)])[None, :], LANES)
    h1, xn2, route, cnt = _outproj(x2, o_da, o_rw, w_out[0].astype(BF16), g_ffn, wr_hi, wr_lo, b_r)

    counts = cnt[0, EXPERT_LANE0:EXPERT_LANE0 + N_EXPERTS].astype(jnp.int32)
    pcounts = ((counts + EXPERT_BLOCK - 1) // EXPERT_BLOCK) * EXPERT_BLOCK
    expert_ids = jnp.arange(N_EXPERTS, dtype=jnp.int32)
    upto = expert_ids[None, :] <= expert_ids[:, None]
    pends = jnp.sum(jnp.where(upto, pcounts[None, :], 0), axis=1)
    pstarts = pends - pcounts
    e_idx = route[:, ROUTE_E:ROUTE_E + 2].astype(jnp.int32)
    rank = route[:, ROUTE_R:ROUTE_R + 2].astype(jnp.int32)
    row0 = jnp.sum(jnp.where(e_idx[..., None] == expert_ids, pstarts, 0), axis=-1)
    dest = ((row0 + rank) * SUBLANES).reshape(-1)
    n_blocks = (2 * t) // EXPERT_BLOCK + N_EXPERTS
    block_row0 = jnp.arange(n_blocks, dtype=jnp.int32) * EXPERT_BLOCK
    block_e = jnp.minimum(jnp.sum(pends[None, :] <= block_row0[:, None], axis=1), N_EXPERTS - 1).astype(jnp.int32)
    n_used = (pends[-1:] // EXPERT_BLOCK).astype(jnp.int32)
    has_rows = counts > 0
    later = (expert_ids[None, :] > expert_ids[:, None]) & has_rows[None, :]
    next_used = jnp.min(jnp.where(later, expert_ids[None, :], N_EXPERTS), axis=1)
    next_used = jnp.where(next_used == N_EXPERTS, -1, next_used).astype(jnp.int32)
    buffer_id = ((jnp.sum(jnp.where(upto & has_rows[None, :], 1, 0), axis=1) - 1) % 2).astype(jnp.int32)

    block_ids = jnp.arange(n_blocks, dtype=jnp.int32)
    last_block = pends // EXPERT_BLOCK - 1
    is_last = jnp.any(has_rows[None, :] & (block_ids[:, None] == last_block[None, :]), axis=1)
    clear_flags = (is_last | (block_ids >= n_used[0])).astype(jnp.int32)
    xb = _dispatch(dest, clear_flags, xn2, n_blocks)
    yb = _experts(block_e, n_used, next_used[block_e], buffer_id[block_e], xb,
                  moe_w_gate[0], moe_w_up[0], moe_w_down[0])
    out = _combine(dest, route, h1, g_final[None, :], yb)
    return out.reshape(b, s, d)
```

```python
import functools
import math

import jax
import jax.numpy as jnp
from jax import lax
from jax.experimental import pallas as pl
from jax.experimental.pallas import tpu as pltpu

F32 = jnp.float32
BF16 = jnp.bfloat16

D_MODEL = 2048
DA_WIDTH = 1024
RW_WIDTH = 1024
DA_HEADS = 8
DA_V_DIM = 128
DA_QK_DIM = 64
RW_N = 64
RW_PAIRS = RW_WIDTH // 128
LORA_W, LORA_A, LORA_G = 64, 64, 160
LORA_PAD = 384
IN_COLS = 3 * DA_WIDTH + 3 * RW_WIDTH
NC = IN_COLS + 2 * LORA_PAD
N_GROUPS = 4
EXPERTS_PER_GROUP = 8
N_EXPERTS = 32
D_EXPERT = 512
NORM_EPS = 1e-6
LN_X_EPS = 64e-5
LAM_INIT = 0.8 - 0.6 * math.exp(-0.3 * 0)
LANES = 128
CHUNK = 64
RW_TILE = 256
NCH = RW_TILE // CHUNK
EXPERT_BLOCK = 256
NEG = -1e30
VMEM_LIMIT = 48 * 1024 * 1024
EXPERTS_VMEM_LIMIT = 56 * 1024 * 1024


def _dot(a, b):
    return jnp.dot(a, b, preferred_element_type=F32)


def _dot_nt(a, b):
    return lax.dot_general(a, b, (((1,), (1,)), ((), ())), preferred_element_type=F32)


def _dot_tn(a, b):
    return lax.dot_general(a, b, (((0,), (0,)), ((), ())), preferred_element_type=F32)


def _sigmoid(x):
    return 1.0 / (1.0 + jnp.exp(-x))


HALF = D_MODEL // 2


def _pack_halves(xb):
    lo = lax.bitcast_convert_type(xb[:, :HALF].astype(F32), jnp.uint32) >> 16
    hi = lax.bitcast_convert_type(xb[:, HALF:].astype(F32), jnp.uint32) & jnp.uint32(0xFFFF0000)
    return hi | lo


def _unpack_halves(u):
    lo = lax.bitcast_convert_type(u << 16, F32)
    hi = lax.bitcast_convert_type(u & jnp.uint32(0xFFFF0000), F32)
    return lo, hi


SUBLANES = 8
assert HALF == SUBLANES * LANES


def _store_row_tiles(ref, u):
    n = u.shape[0]
    for k in range(SUBLANES):
        ref[pl.ds(k, n, stride=SUBLANES), :] = u[:, k * LANES:(k + 1) * LANES]


def _load_row_tiles(ref):
    n = ref.shape[0] // SUBLANES
    return jnp.concatenate([ref[pl.ds(k, n, stride=SUBLANES), :] for k in range(SUBLANES)], axis=1)


def _inproj_kernel(x_ref, g_ref, w_ref, o_ref, xn_ref):
    @pl.when(pl.program_id(1) == 0)
    def _():
        x = x_ref[...]
        ms = jnp.mean(x * x, axis=-1, keepdims=True)
        xn_ref[...] = (x * lax.rsqrt(ms + NORM_EPS) * g_ref[...]).astype(BF16)

    o_ref[...] = _dot(xn_ref[...], w_ref[...]).astype(o_ref.dtype)


def _inproj(x2, g, w_all, tm=1024, tn=2304):
    t = x2.shape[0]
    tm = min(tm, t)
    vmem_limit = 2 * (tm * D_MODEL * 4 + D_MODEL * tn * 2 + tm * tn * 2) + tm * D_MODEL * 2 + (6 << 20)
    return pl.pallas_call(
        _inproj_kernel,
        grid=(t // tm, NC // tn),
        in_specs=[
            pl.BlockSpec((tm, D_MODEL), lambda i, j: (i, 0)),
            pl.BlockSpec((1, D_MODEL), lambda i, j: (0, 0)),
            pl.BlockSpec((D_MODEL, tn), lambda i, j: (0, j)),
        ],
        out_specs=pl.BlockSpec((tm, tn), lambda i, j: (i, j)),
        out_shape=jax.ShapeDtypeStruct((t, NC), BF16),
        scratch_shapes=[pltpu.VMEM((tm, D_MODEL), BF16)],
        compiler_params=pltpu.CompilerParams(
            dimension_semantics=("parallel", "arbitrary"), vmem_limit_bytes=vmem_limit),
        name="inproj",
    )(x2, g, w_all)


DA_HEADS_PER_STEP = 8
DA_VMEM_LIMIT = 62 * 1024 * 1024


def _da_kernel(qi_tab, kj_tab, slopes_ref, lam_ref, subg_ref, q_ref, k_ref, v_ref, o_ref,
               vt_ref, qz_ref, bias_ref, m_ref, l_ref, lfin_ref, acc_ref, s_ref, p_ref, alpha_ref,
               *, tq, seq):
    tk = tq
    nq = seq // tq
    n_trips = nq * (nq + 1) // 2
    heads = range(DA_HEADS_PER_STEP)
    chains = [(hd, c) for hd in heads for c in range(2)]
    slope = [slopes_ref[pl.program_id(1) * DA_HEADS_PER_STEP + hd] for hd in heads]
    cols = lambda hd: slice(hd * LANES, (hd + 1) * LANES)

    lane = lax.broadcasted_iota(jnp.int32, (tq, LANES), 1)
    krow = lax.broadcasted_iota(jnp.int32, (tk, tq), 0)
    qcol = lax.broadcasted_iota(jnp.int32, (tk, tq), 1)
    for hd in heads:
        for blk in range(seq // LANES):
            r = slice(blk * LANES, (blk + 1) * LANES)
            vt_ref[hd, :, r] = v_ref[r, cols(hd)].astype(F32).T.astype(BF16)
        for blk in range(nq):
            r = slice(blk * tq, (blk + 1) * tq)
            q = q_ref[r, cols(hd)] * jnp.asarray(DA_QK_DIM ** -0.5, BF16)
            qz_ref[2 * hd, r, :] = jnp.where(lane < DA_QK_DIM, q, jnp.zeros_like(q))
            qz_ref[2 * hd + 1, r, :] = jnp.where(lane >= DA_QK_DIM, q, jnp.zeros_like(q))
        alibi = slope[hd] * krow.astype(F32)
        bias_ref[hd, 0] = alibi
        bias_ref[hd, 1] = jnp.where(krow <= qcol, alibi, NEG)

    m_ref[...] = jnp.full_like(m_ref, NEG)
    l_ref[...] = jnp.zeros_like(l_ref)
    lfin_ref[...] = jnp.ones_like(lfin_ref)
    acc_ref[...] = jnp.zeros_like(acc_ref)
    p_ref[...] = jnp.zeros_like(p_ref)
    alpha_ref[...] = jnp.zeros_like(alpha_ref)

    lv = lam_ref[...]
    lam = (jnp.exp(jnp.sum(lv[0:1] * lv[1:2], axis=-1, keepdims=True))
           - jnp.exp(jnp.sum(lv[2:3] * lv[3:4], axis=-1, keepdims=True)) + LAM_INIT)

    def scores_to_scratch(t):
        qs = pl.multiple_of(qi_tab[t] * tq, tq)
        ks = pl.multiple_of(kj_tab[t] * tk, tk)
        diag = (qi_tab[t] == kj_tab[t]).astype(jnp.int32)
        for hd, c in chains:
            ch = 2 * hd + c
            s_ref[ch] = (_dot_nt(k_ref[pl.ds(ks, tk), cols(hd)], qz_ref[ch, pl.ds(qs, tq), :])
                         + bias_ref[hd, diag])

    def values_from_scratch(t):
        ks = pl.multiple_of(kj_tab[t] * tk, tk)
        for hd, c in chains:
            ch = 2 * hd + c
            acc_ref[ch] = alpha_ref[ch:ch + 1, :] * acc_ref[ch] + _dot(vt_ref[hd, :, pl.ds(ks, tk)], p_ref[ch])

    def softmax(t):
        first = kj_tab[t] == 0
        diag = kj_tab[t] == qi_tab[t]
        rel = ((kj_tab[t] - qi_tab[t]) * tk).astype(F32)
        p, alpha = [], []
        for hd, c in chains:
            ch = 2 * hd + c
            off = slope[hd] * rel
            s = s_ref[ch]
            m = jnp.where(first, NEG, m_ref[ch:ch + 1, :])
            l = jnp.where(first, 0.0, l_ref[ch:ch + 1, :])
            m_new = jnp.maximum(m, jnp.max(s, axis=0, keepdims=True) + off)
            pc = jnp.exp(s - (m_new - off))
            a = jnp.exp(m - m_new)
            l_new = a * l + jnp.sum(pc, axis=0, keepdims=True)
            m_ref[ch:ch + 1, :] = m_new
            l_ref[ch:ch + 1, :] = l_new
            lfin_ref[ch:ch + 1, :] = jnp.where(diag, l_new, lfin_ref[ch:ch + 1, :])
            p.append(pc)
            alpha.append(a)
        return p, alpha

    def publish(p, alpha):
        for ch in range(len(chains)):
            alpha_ref[ch:ch + 1, :] = alpha[ch]
            p_ref[ch] = p[ch].astype(BF16)

    def finalize(t):
        qs = pl.multiple_of(qi_tab[t] * tq, tq)
        for hd in heads:
            c0, c1 = 2 * hd, 2 * hd + 1
            o = acc_ref[c0] / lfin_ref[c0:c0 + 1, :] - lam * (acc_ref[c1] / lfin_ref[c1:c1 + 1, :])
            ms = jnp.mean(o * o, axis=0, keepdims=True)
            o = o * lax.rsqrt(ms + NORM_EPS) * subg_ref[...] * (1.0 - LAM_INIT)
            o_ref[pl.ds(qs, tq), cols(hd)] = o.T.astype(o_ref.dtype)

    scores_to_scratch(0)

    def body(t, _):
        prev = jnp.maximum(t - 1, 0)
        p, alpha = softmax(t)
        values_from_scratch(prev)
        scores_to_scratch(jnp.minimum(t + 1, n_trips - 1))
        publish(p, alpha)

        @pl.when(jnp.logical_and(t > 0, qi_tab[prev] == kj_tab[prev]))
        def _():
            finalize(prev)
        return 0

    lax.fori_loop(0, n_trips, body, 0)
    values_from_scratch(n_trips - 1)
    finalize(n_trips - 1)


def _diffattn(proj, slopes, lam4, subg_col, b, s, tq=256):
    nq = s // tq
    nh = DA_HEADS_PER_STEP
    groups = DA_HEADS // nh
    width = nh * LANES
    pairs = [(qi, kj) for qi in range(nq) for kj in range(qi + 1)]
    qi_tab = jnp.asarray([p[0] for p in pairs], jnp.int32)
    kj_tab = jnp.asarray([p[1] for p in pairs], jnp.int32)
    return pl.pallas_call(
        functools.partial(_da_kernel, tq=tq, seq=s),
        grid_spec=pltpu.PrefetchScalarGridSpec(
            num_scalar_prefetch=2,
            grid=(b, groups),
            in_specs=[
                pl.BlockSpec(memory_space=pltpu.SMEM),
                pl.BlockSpec((4, DA_QK_DIM), lambda bi, g, *_: (0, 0)),
                pl.BlockSpec((DA_V_DIM, 1), lambda bi, g, *_: (0, 0)),
                pl.BlockSpec((s, width), lambda bi, g, *_: (bi, g)),
                pl.BlockSpec((s, width), lambda bi, g, *_: (bi, groups + g)),
                pl.BlockSpec((s, width), lambda bi, g, *_: (bi, 2 * groups + g)),
            ],
            out_specs=pl.BlockSpec((s, width), lambda bi, g, *_: (bi, g)),
            scratch_shapes=[
                pltpu.VMEM((nh, DA_V_DIM, s), BF16),
                pltpu.VMEM((2 * nh, s, LANES), BF16),
                pltpu.VMEM((nh, 2, tq, tq), F32),
                pltpu.VMEM((2 * nh, tq), F32),
                pltpu.VMEM((2 * nh, tq), F32),
                pltpu.VMEM((2 * nh, tq), F32),
                pltpu.VMEM((2 * nh, DA_V_DIM, tq), F32),
                pltpu.VMEM((2 * nh, tq, tq), F32),
                pltpu.VMEM((2 * nh, tq, tq), BF16),
                pltpu.VMEM((2 * nh, tq), F32),
            ],
        ),
        out_shape=jax.ShapeDtypeStruct((b * s, DA_WIDTH), BF16),
        compiler_params=pltpu.CompilerParams(
            dimension_semantics=("parallel", "arbitrary"), vmem_limit_bytes=DA_VMEM_LIMIT),
        name="diffattn",
    )(qi_tab, kj_tab, slopes, lam4, subg_col, proj, proj, proj)


def _rwkv_kernel(r_ref, k_ref, v_ref, p1_ref, p2_ref, mu_ref, w0_ref, a0_ref, w2_ref, kk_ref, ka_ref,
                 rk_ref, lnw_ref, lnb_ref, o_ref, state_ref, crkv_ref, cp2_ref):
    L, TT = CHUNK, RW_TILE
    c = pl.program_id(1)

    @pl.when(c == 0)
    def _():
        state_ref[...] = jnp.zeros_like(state_ref)
        crkv_ref[...] = jnp.zeros_like(crkv_ref)
        cp2_ref[...] = jnp.zeros_like(cp2_ref)

    first = lax.broadcasted_iota(jnp.int32, (TT, 1), 0) == 0

    def shift(x, carry):
        return jnp.where(first, carry, pltpu.roll(x, 1, 0))

    r_raw = r_ref[...].astype(F32)
    k_raw = k_ref[...].astype(F32)
    v_raw = v_ref[...].astype(F32)
    p2 = p2_ref[...].astype(F32)
    rs = shift(r_raw, crkv_ref[:, 0:RW_WIDTH])
    ks = shift(k_raw, crkv_ref[:, RW_WIDTH:2 * RW_WIDTH])
    vs = shift(v_raw, crkv_ref[:, 2 * RW_WIDTH:3 * RW_WIDTH])
    lin = p1_ref[...].astype(F32) + shift(p2, cp2_ref[...])
    crkv_ref[:, 0:RW_WIDTH] = r_raw[TT - 1:TT, :]
    crkv_ref[:, RW_WIDTH:2 * RW_WIDTH] = k_raw[TT - 1:TT, :]
    crkv_ref[:, 2 * RW_WIDTH:3 * RW_WIDTH] = v_raw[TT - 1:TT, :]
    cp2_ref[...] = p2[TT - 1:TT, :]

    r = r_raw + (rs - r_raw) * mu_ref[0:1, :]
    k = k_raw + (ks - k_raw) * mu_ref[1:2, :]
    v = v_raw + (vs - v_raw) * mu_ref[2:3, :]

    ll = lax.broadcasted_iota(jnp.int32, (TT, LORA_PAD), 1)
    z = jnp.where(ll < LORA_W, jnp.tanh(lin),
                  jnp.where(ll < LORA_W + LORA_A, lin,
                            jnp.where(ll < LORA_W + LORA_A + LORA_G, _sigmoid(lin), 0.0)))
    d = _dot(z.astype(BF16), w2_ref[...])
    logw = -_sigmoid(w0_ref[...] + d[:, 0:RW_WIDTH]) * math.exp(-0.5)
    a_sig = _sigmoid(a0_ref[...] + d[:, RW_WIDTH:2 * RW_WIDTH])
    g = d[:, 2 * RW_WIDTH:3 * RW_WIDTH]

    li = lax.broadcasted_iota(jnp.int32, (LANES, LANES), 0)
    lj = lax.broadcasted_iota(jnp.int32, (LANES, LANES), 1)
    same_head = jnp.where((li // RW_N) == (lj // RW_N), 1.0, 0.0).astype(BF16)

    def head_sum(x):
        xb = x.astype(BF16)
        return jnp.concatenate(
            [_dot(xb[:, p * LANES:(p + 1) * LANES], same_head) for p in range(RW_PAIRS)], axis=1)

    kk = k * kk_ref[...]
    kk = kk * lax.rsqrt(jnp.maximum(head_sum(kk * kk), 1e-24))
    k = k * (1.0 + (a_sig - 1.0) * ka_ref[...])
    bonus = head_sum(r * k * rk_ref[...])

    ti = lax.broadcasted_iota(jnp.int32, (TT, TT), 0)
    tj = lax.broadcasted_iota(jnp.int32, (TT, TT), 1)
    tri = jnp.where(ti >= tj, jnp.where((ti // L) == (tj // L), 1.0, 0.0), 0.0).astype(BF16)
    w_hi = logw.astype(BF16)
    w_r1 = logw - w_hi.astype(F32)
    w_mid = w_r1.astype(BF16)
    w_lo = (w_r1 - w_mid.astype(F32)).astype(BF16)
    cw = _dot(tri, w_hi) + _dot(tri, w_mid) + _dot(tri, w_lo)
    cw_last = [cw[(ci + 1) * L - 1:(ci + 1) * L, :] for ci in range(NCH)]
    e_in = jnp.exp(cw)
    e_out = jnp.exp(-cw)
    e_end = jnp.exp(jnp.concatenate([jnp.broadcast_to(cl, (L, RW_WIDTH)) for cl in cw_last], axis=0) - cw)
    w_end = [jnp.exp(cl) for cl in cw_last]

    at = -kk * jnp.exp(cw - logw)
    rt = r * e_in
    b = kk * a_sig
    bt = b * e_out
    kt = k * e_out
    bd = b * e_end
    kd = k * e_end

    lane = lax.broadcasted_iota(jnp.int32, (L, LANES), 1)
    lo = lane < RW_N

    def stack(x, ci, p):
        xs = x[ci * L:(ci + 1) * L, p * LANES:(p + 1) * LANES]
        return jnp.concatenate([jnp.where(lo, xs, 0.0), jnp.where(lo, 0.0, xs)], axis=0).astype(BF16)

    si = lax.broadcasted_iota(jnp.int32, (2 * L, 2 * L), 0)
    sj = lax.broadcasted_iota(jnp.int32, (2 * L, 2 * L), 1)
    strict = (si % L) > (sj % L)
    incl = (si % L) >= (sj % L)
    eye = jnp.where(si == sj, 1.0, 0.0)

    pairs = range(RW_PAIRS)
    cat0 = lambda *xs: jnp.concatenate(xs, axis=0)
    cat1 = lambda *xs: jnp.concatenate(xs, axis=1)
    bf = lambda x: x.astype(BF16)
    def chunk_setup(ci):
        a_s = [stack(at, ci, p) for p in pairs]
        r_s = [stack(rt, ci, p) for p in pairs]
        v_s = [stack(v, ci, p) for p in pairs]
        bk = [cat0(stack(bt, ci, p), stack(kt, ci, p)) for p in pairs]
        bkd = [cat0(stack(bd, ci, p), stack(kd, ci, p)) for p in pairs]
        aa = [_dot_nt(a_s[p], bk[p]) for p in pairs]
        rr = [_dot_nt(r_s[p], bk[p]) for p in pairs]
        a_ab = [jnp.where(strict, aa[p][:, 0:2 * L], 0.0) for p in pairs]
        a_ak = [bf(jnp.where(strict, aa[p][:, 2 * L:4 * L], 0.0)) for p in pairs]
        a_rb = [bf(jnp.where(incl, rr[p][:, 0:2 * L], 0.0)) for p in pairs]
        a_rk = [bf(jnp.where(incl, rr[p][:, 2 * L:4 * L], 0.0)) for p in pairs]
        ab = [bf(a) for a in a_ab]
        qpow = [_dot(ab[p], ab[p]) for p in pairs]
        inv = [eye + a_ab[p] for p in pairs]
        for _ in range(int(math.log2(L)) - 2):
            qb = [bf(q) for q in qpow]
            res = [_dot(qb[p], cat1(qb[p], bf(inv[p]))) for p in pairs]
            qpow = [res[p][:, 0:2 * L] for p in pairs]
            inv = [inv[p] + res[p][:, 2 * L:4 * L] for p in pairs]
        inv = [bf(inv[p] + _dot(bf(qpow[p]), bf(inv[p]))) for p in pairs]
        w_col = [jnp.broadcast_to(w_end[ci][:, p * LANES:(p + 1) * LANES], (LANES, LANES)).T for p in pairs]
        return a_s, r_s, v_s, bkd, a_ak, a_rb, a_rk, inv, w_col

    def chunk_advance(setup, h):
        a_s, r_s, v_s, bkd, a_ak, a_rb, a_rk, inv, w_col = setup
        h_b = [bf(hp) for hp in h]
        x = [_dot(cat1(a_s[p], a_ak[p]), cat0(h_b[p], v_s[p])) for p in pairs]
        u_b = [bf(_dot(inv[p], bf(x[p]))) for p in pairs]
        y2 = [_dot(cat1(r_s[p], a_rb[p], a_rk[p]), cat0(h_b[p], u_b[p], v_s[p])) for p in pairs]
        h = [w_col[p] * h[p] + _dot_tn(bkd[p], cat0(u_b[p], v_s[p])) for p in pairs]
        return h, cat1(*[y2[p][0:L] + y2[p][L:2 * L] for p in pairs])

    h = [state_ref[p] for p in pairs]
    ys = []
    setup = chunk_setup(0)
    for ci in range(NCH):
        ahead = chunk_setup(ci + 1) if ci + 1 < NCH else None
        h, y_chunk = chunk_advance(setup, h)
        ys.append(y_chunk)
        setup = ahead
    for p in pairs:
        state_ref[p] = h[p]
    y = cat0(*ys)

    mean = head_sum(y) * (1.0 / RW_N)
    yc = y - mean
    var = head_sum(yc * yc) * (1.0 / RW_N)
    y = yc * lax.rsqrt(var + LN_X_EPS) * lnw_ref[...] + lnb_ref[...]
    y = y + bonus * v
    o_ref[...] = (y * g).astype(o_ref.dtype)


def _rwkv(proj, mu_rkv, w0, a0, w2cat, k_k, k_a, r_k, ln_w, ln_b, b, s):
    tt = RW_TILE
    nc = s // tt
    row = lambda n: pl.BlockSpec((1, n), lambda bi, c: (0, 0))
    rkv_col0 = 3 * DA_WIDTH // RW_WIDTH
    lora_col0 = IN_COLS // LORA_PAD
    return pl.pallas_call(
        _rwkv_kernel,
        grid=(b, nc),
        in_specs=[
            pl.BlockSpec((tt, RW_WIDTH), lambda bi, c: (bi * nc + c, rkv_col0)),
            pl.BlockSpec((tt, RW_WIDTH), lambda bi, c: (bi * nc + c, rkv_col0 + 1)),
            pl.BlockSpec((tt, RW_WIDTH), lambda bi, c: (bi * nc + c, rkv_col0 + 2)),
            pl.BlockSpec((tt, LORA_PAD), lambda bi, c: (bi * nc + c, lora_col0)),
            pl.BlockSpec((tt, LORA_PAD), lambda bi, c: (bi * nc + c, lora_col0 + 1)),
            pl.BlockSpec((3, RW_WIDTH), lambda bi, c: (0, 0)),
            row(RW_WIDTH), row(RW_WIDTH),
            pl.BlockSpec((LORA_PAD, 3 * RW_WIDTH), lambda bi, c: (0, 0)),
            row(RW_WIDTH), row(RW_WIDTH), row(RW_WIDTH), row(RW_WIDTH), row(RW_WIDTH),
        ],
        out_specs=pl.BlockSpec((tt, RW_WIDTH), lambda bi, c: (bi * nc + c, 0)),
        out_shape=jax.ShapeDtypeStruct((b * s, RW_WIDTH), BF16),
        scratch_shapes=[
            pltpu.VMEM((RW_PAIRS, LANES, LANES), F32),
            pltpu.VMEM((1, 3 * RW_WIDTH), F32),
            pltpu.VMEM((1, LORA_PAD), F32),
        ],
        compiler_params=pltpu.CompilerParams(
            dimension_semantics=("parallel", "arbitrary"), vmem_limit_bytes=VMEM_LIMIT),
        name="rwkv",
    )(proj, proj, proj, proj, proj, mu_rkv, w0, a0, w2cat, k_k, k_a, r_k, ln_w, ln_b)


ROUTE_E, ROUTE_G, ROUTE_R = 0, 2, 4
EXPERT_LANE0 = N_GROUPS


def _outproj_kernel(x_ref, oda_ref, orw_ref, wout_ref, g_ref, wrh_ref, wrl_ref, br_ref,
                    h_ref, xn_ref, route_ref, cnt_ref, base_ref, *, tm):
    @pl.when(pl.program_id(0) == 0)
    def _():
        base_ref[...] = jnp.zeros_like(base_ref)

    mix = _dot(oda_ref[...], wout_ref[0:DA_WIDTH, :]) + _dot(orw_ref[...], wout_ref[DA_WIDTH:, :])
    h = x_ref[...] + mix
    h_ref[...] = h
    ms = jnp.mean(h * h, axis=-1, keepdims=True)
    xn = h * lax.rsqrt(ms + NORM_EPS) * g_ref[...]
    x_hi = xn.astype(BF16)
    _store_row_tiles(xn_ref, _pack_halves(x_hi))

    x_lo = (xn - x_hi.astype(F32)).astype(BF16)
    hi_both = _dot(x_hi, jnp.concatenate([wrh_ref[...], wrl_ref[...]], axis=1))
    lg = (hi_both[:, 0:LANES] + _dot(x_lo, wrh_ref[...]) + hi_both[:, LANES:]) + br_ref[...]

    lane = lax.broadcasted_iota(jnp.int32, (tm, LANES), 1)
    big = jnp.int32(1 << 20)
    gl = jnp.where(lane < N_GROUPS, lg, NEG)
    gmax = jnp.max(gl, axis=-1, keepdims=True)
    grp = jnp.min(jnp.where(gl == gmax, lane, big), axis=-1, keepdims=True)
    p_grp = 1.0 / jnp.sum(jnp.exp(gl - gmax), axis=-1, keepdims=True)
    eg = jnp.where(lane >= EXPERT_LANE0, (lane - EXPERT_LANE0) // EXPERTS_PER_GROUP, -1)
    el = jnp.where(eg == grp, lg, NEG)
    emax = jnp.max(el, axis=-1, keepdims=True)
    pe = jnp.exp(el - emax)
    probs = pe / jnp.sum(pe, axis=-1, keepdims=True)
    probs = jnp.where(eg == grp, probs, -1.0)
    p1 = jnp.max(probs, axis=-1, keepdims=True)
    i1 = jnp.min(jnp.where(probs == p1, lane, big), axis=-1, keepdims=True)
    probs2 = jnp.where(lane == i1, -1.0, probs)
    p2 = jnp.max(probs2, axis=-1, keepdims=True)
    i2 = jnp.min(jnp.where(probs2 == p2, lane, big), axis=-1, keepdims=True)
    gate1 = p_grp * p1 / (p1 + p2)
    gate2 = p_grp * p2 / (p1 + p2)

    oh1 = jnp.where(lane == i1, 1.0, 0.0)
    oh2 = jnp.where(lane == i2, 1.0, 0.0)
    ri = lax.broadcasted_iota(jnp.int32, (tm, tm), 0)
    rj = lax.broadcasted_iota(jnp.int32, (tm, tm), 1)
    lower = jnp.where(ri > rj, 1.0, 0.0).astype(BF16)
    base = base_ref[...]
    tot1 = jnp.sum(oh1, axis=0, keepdims=True)
    before = _dot(lower, jnp.concatenate([oh1, oh2], axis=1).astype(BF16))
    c1 = base + before[:, 0:LANES]
    c2 = base + tot1 + before[:, LANES:]
    rank1 = jnp.sum(oh1 * c1, axis=-1, keepdims=True)
    rank2 = jnp.sum(oh2 * c2, axis=-1, keepdims=True)
    base = base + tot1 + jnp.sum(oh2, axis=0, keepdims=True)
    base_ref[...] = base
    cnt_ref[...] = base

    e1 = (i1 - EXPERT_LANE0).astype(F32)
    e2 = (i2 - EXPERT_LANE0).astype(F32)
    rec = jnp.zeros((tm, LANES), F32)
    for ln, val in ((ROUTE_E, e1), (ROUTE_E + 1, e2), (ROUTE_G, gate1), (ROUTE_G + 1, gate2),
                    (ROUTE_R, rank1), (ROUTE_R + 1, rank2)):
        rec = jnp.where(lane == ln, val, rec)
    route_ref[...] = rec


def _outproj(x2, o_da, o_rw, w_out, g_ffn, wr_hi, wr_lo, b_r, tm=512):
    t = x2.shape[0]
    const = lambda shape: pl.BlockSpec(shape, lambda i: (0, 0))
    return pl.pallas_call(
        functools.partial(_outproj_kernel, tm=tm),
        grid=(t // tm,),
        in_specs=[
            pl.BlockSpec((tm, D_MODEL), lambda i: (i, 0)),
            pl.BlockSpec((tm, DA_WIDTH), lambda i: (i, 0)),
            pl.BlockSpec((tm, RW_WIDTH), lambda i: (i, 0)),
            const((D_MODEL, D_MODEL)),
            const((1, D_MODEL)),
            const((D_MODEL, LANES)),
            const((D_MODEL, LANES)),
            const((1, LANES)),
        ],
        out_specs=[
            pl.BlockSpec((tm, D_MODEL), lambda i: (i, 0)),
            pl.BlockSpec((tm * SUBLANES, LANES), lambda i: (i, 0)),
            pl.BlockSpec((tm, LANES), lambda i: (i, 0)),
            const((1, LANES)),
        ],
        out_shape=[
            jax.ShapeDtypeStruct((t, D_MODEL), F32),
            jax.ShapeDtypeStruct((t * SUBLANES, LANES), jnp.uint32),
            jax.ShapeDtypeStruct((t, LANES), F32),
            jax.ShapeDtypeStruct((1, LANES), F32),
        ],
        scratch_shapes=[pltpu.VMEM((1, LANES), F32)],
        compiler_params=pltpu.CompilerParams(
            dimension_semantics=("arbitrary",), vmem_limit_bytes=VMEM_LIMIT),
        name="outproj_router",
    )(x2, o_da, o_rw, w_out, g_ffn, wr_hi, wr_lo, b_r)


def _row_copy(src_ref, dst_ref, src_row8, dst_row8, sem):
    src = src_ref.at[pl.ds(pl.multiple_of(src_row8, SUBLANES), SUBLANES)]
    dst = dst_ref.at[pl.ds(pl.multiple_of(dst_row8, SUBLANES), SUBLANES)]
    return pltpu.make_async_copy(src, dst, sem)


def _dispatch_kernel(dest_ref, zflag_ref, x_ref, xb_hbm, zero_ref, sem, zsem, *, td, n_blocks):
    t0 = pl.program_id(0) * td
    block_rows = EXPERT_BLOCK * SUBLANES

    @pl.when(pl.program_id(0) == 0)
    def _():
        zero_ref[...] = jnp.zeros_like(zero_ref)

        def clear(blk):
            dst = xb_hbm.at[pl.ds(pl.multiple_of(blk * block_rows, block_rows), block_rows)]
            return pltpu.make_async_copy(zero_ref, dst, zsem)

        def start_clear(blk, _):
            @pl.when(zflag_ref[blk] != 0)
            def _():
                clear(blk).start()
            return 0

        def wait_clear(blk, _):
            @pl.when(zflag_ref[blk] != 0)
            def _():
                clear(blk).wait()
            return 0

        lax.fori_loop(0, n_blocks, start_clear, 0)
        lax.fori_loop(0, n_blocks, wait_clear, 0)

    def start(i, _):
        tok = t0 + i
        _row_copy(x_ref, xb_hbm, i * SUBLANES, dest_ref[2 * tok], sem).start(priority=0)
        _row_copy(x_ref, xb_hbm, i * SUBLANES, dest_ref[2 * tok + 1], sem).start(priority=1)
        return 0

    lax.fori_loop(0, td, start, 0, unroll=8)
    for _ in range(2):
        pltpu.make_async_copy(x_ref, xb_hbm.at[pl.ds(0, td * SUBLANES)], sem).wait()


def _dispatch(dest_flat, clear_flags, xn2, n_blocks, td=1024):
    t = xn2.shape[0] // SUBLANES
    td = min(td, t)
    block_rows = EXPERT_BLOCK * SUBLANES
    return pl.pallas_call(
        functools.partial(_dispatch_kernel, td=td, n_blocks=n_blocks),
        grid_spec=pltpu.PrefetchScalarGridSpec(
            num_scalar_prefetch=2,
            grid=(t // td,),
            in_specs=[pl.BlockSpec((td * SUBLANES, LANES), lambda i, *_: (i, 0))],
            out_specs=pl.BlockSpec(memory_space=pl.ANY),
            scratch_shapes=[
                pltpu.VMEM((block_rows, LANES), jnp.uint32),
                pltpu.SemaphoreType.DMA,
                pltpu.SemaphoreType.DMA,
            ],
        ),
        out_shape=jax.ShapeDtypeStruct((n_blocks * block_rows, LANES), jnp.uint32),
        compiler_params=pltpu.CompilerParams(dimension_semantics=("arbitrary",)),
        name="dispatch",
    )(dest_flat, clear_flags, xn2)


EXPERT_BLOCKS_PER_STEP = 4


def _experts_kernel(be_ref, nu_ref, nxt_ref, slot_ref, xb_ref, wg_hbm, wu_hbm, wd_hbm, yb_ref,
                    wgf_ref, wuf_ref, wdf_ref, wgb_ref, wub_ref, wdb_ref, sem):
    block_rows = EXPERT_BLOCK * SUBLANES

    def weight_copies(e, slot):
        return [pltpu.make_async_copy(src.at[e], dst.at[slot], sem.at[slot])
                for src, dst in ((wg_hbm, wgf_ref), (wu_hbm, wuf_ref), (wd_hbm, wdf_ref))]

    def row_block(sub):
        i = pl.program_id(0) * EXPERT_BLOCKS_PER_STEP + sub
        x_ref = xb_ref.at[pl.ds(sub * block_rows, block_rows)]
        y_ref = yb_ref.at[pl.ds(sub * block_rows, block_rows)]
        used = i < nu_ref[0]
        new_expert = jnp.logical_or(i == 0, be_ref[i] != be_ref[jnp.maximum(i - 1, 0)])

        def swiglu(wg_lo, wg_hi, wu_lo, wu_hi, wd):
            x_lo, x_hi = (half.astype(BF16) for half in _unpack_halves(_load_row_tiles(x_ref)))
            hg = _dot(x_lo, wg_lo) + _dot(x_hi, wg_hi)
            hu = _dot(x_lo, wu_lo) + _dot(x_hi, wu_hi)
            hid = hg * _sigmoid(hg) * hu
            _store_row_tiles(y_ref, _pack_halves(_dot(hid.astype(BF16), wd).astype(BF16)))

        @pl.when(jnp.logical_and(used, new_expert))
        def _():
            slot = slot_ref[i]

            @pl.when(i == 0)
            def _():
                for cp in weight_copies(be_ref[i], slot):
                    cp.start()

            for cp in weight_copies(be_ref[i], slot):
                cp.wait()

            @pl.when(nxt_ref[i] >= 0)
            def _():
                for cp in weight_copies(nxt_ref[i], 1 - slot):
                    cp.start()

            wg, wu, wd = (w[slot].astype(BF16) for w in (wgf_ref, wuf_ref, wdf_ref))
            wgb_ref[...] = wg
            wub_ref[...] = wu
            wdb_ref[...] = wd
            swiglu(wg[0:HALF, :], wg[HALF:, :], wu[0:HALF, :], wu[HALF:, :], wd)

        @pl.when(jnp.logical_and(used, jnp.logical_not(new_expert)))
        def _():
            swiglu(wgb_ref[0:HALF, :], wgb_ref[HALF:, :], wub_ref[0:HALF, :], wub_ref[HALF:, :], wdb_ref[...])

        @pl.when(jnp.logical_not(used))
        def _():
            y_ref[...] = jnp.zeros_like(y_ref)

    for sub in range(EXPERT_BLOCKS_PER_STEP):
        row_block(sub)


def _experts(block_e, n_used, next_e, slot, xb, w_gate, w_up, w_down):
    step_rows = EXPERT_BLOCKS_PER_STEP * EXPERT_BLOCK * SUBLANES
    nb = xb.shape[0] // step_rows
    last_step = lambda nu: (nu[0] - 1) // EXPERT_BLOCKS_PER_STEP
    return pl.pallas_call(
        _experts_kernel,
        grid_spec=pltpu.PrefetchScalarGridSpec(
            num_scalar_prefetch=4,
            grid=(nb,),
            in_specs=[
                pl.BlockSpec((step_rows, LANES), lambda i, be, nu, *_: (jnp.minimum(i, last_step(nu)), 0)),
                pl.BlockSpec(memory_space=pl.ANY),
                pl.BlockSpec(memory_space=pl.ANY),
                pl.BlockSpec(memory_space=pl.ANY),
            ],
            out_specs=pl.BlockSpec((step_rows, LANES), lambda i, *_: (i, 0)),
            scratch_shapes=[
                pltpu.VMEM((2, D_MODEL, D_EXPERT), F32),
                pltpu.VMEM((2, D_MODEL, D_EXPERT), F32),
                pltpu.VMEM((2, D_EXPERT, D_MODEL), F32),
                pltpu.VMEM((D_MODEL, D_EXPERT), BF16),
                pltpu.VMEM((D_MODEL, D_EXPERT), BF16),
                pltpu.VMEM((D_EXPERT, D_MODEL), BF16),
                pltpu.SemaphoreType.DMA((2,)),
            ],
        ),
        out_shape=jax.ShapeDtypeStruct(xb.shape, jnp.uint32),
        compiler_params=pltpu.CompilerParams(
            dimension_semantics=("arbitrary",), vmem_limit_bytes=EXPERTS_VMEM_LIMIT),
        name="experts",
    )(block_e, n_used, next_e, slot, xb, w_gate, w_up, w_down)


def _combine_kernel(dest_ref, route_ref, h_ref, g_ref, yb_hbm, o_ref, y1_ref, y2_ref, sem, *, tc):
    step = pl.program_id(0)
    n_steps = pl.num_programs(0)

    def gather(tile, slot):
        def start(i, _):
            tok = tile * tc + i
            _row_copy(yb_hbm, y1_ref.at[slot], dest_ref[2 * tok], i * SUBLANES, sem.at[slot]).start(priority=0)
            _row_copy(yb_hbm, y2_ref.at[slot], dest_ref[2 * tok + 1], i * SUBLANES, sem.at[slot]).start(priority=1)
            return 0

        lax.fori_loop(0, tc, start, 0, unroll=8)

    @pl.when(step == 0)
    def _():
        gather(0, 0)

    @pl.when(step + 1 < n_steps)
    def _():
        gather(step + 1, (step + 1) % 2)

    slot = step % 2
    pltpu.make_async_copy(yb_hbm.at[pl.ds(0, tc * SUBLANES)], y1_ref.at[slot], sem.at[slot]).wait()
    pltpu.make_async_copy(yb_hbm.at[pl.ds(0, tc * SUBLANES)], y2_ref.at[slot], sem.at[slot]).wait()

    rec = route_ref[...]
    lane = lax.broadcasted_iota(jnp.int32, (tc, LANES), 1)
    g1 = jnp.sum(jnp.where(lane == ROUTE_G, rec, 0.0), axis=-1, keepdims=True)
    g2 = jnp.sum(jnp.where(lane == ROUTE_G + 1, rec, 0.0), axis=-1, keepdims=True)
    y1_lo, y1_hi = _unpack_halves(_load_row_tiles(y1_ref.at[slot]))
    y2_lo, y2_hi = _unpack_halves(_load_row_tiles(y2_ref.at[slot]))
    h_lo = h_ref[:, 0:HALF] + (y1_lo * g1 + y2_lo * g2)
    h_hi = h_ref[:, HALF:] + (y1_hi * g1 + y2_hi * g2)
    ms = (jnp.sum(h_lo * h_lo, axis=-1, keepdims=True)
          + jnp.sum(h_hi * h_hi, axis=-1, keepdims=True)) * (1.0 / D_MODEL)
    scale = lax.rsqrt(ms + NORM_EPS)
    o_ref[:, 0:HALF] = h_lo * scale * g_ref[:, 0:HALF]
    o_ref[:, HALF:] = h_hi * scale * g_ref[:, HALF:]


def _combine(dest_flat, route, h1, g_final, yb, tc=256):
    t = h1.shape[0]
    return pl.pallas_call(
        functools.partial(_combine_kernel, tc=tc),
        grid_spec=pltpu.PrefetchScalarGridSpec(
            num_scalar_prefetch=1,
            grid=(t // tc,),
            in_specs=[
                pl.BlockSpec((tc, LANES), lambda i, d: (i, 0)),
                pl.BlockSpec((tc, D_MODEL), lambda i, d: (i, 0)),
                pl.BlockSpec((1, D_MODEL), lambda i, d: (0, 0)),
                pl.BlockSpec(memory_space=pl.ANY),
            ],
            out_specs=pl.BlockSpec((tc, D_MODEL), lambda i, d: (i, 0)),
            scratch_shapes=[
                pltpu.VMEM((2, tc * SUBLANES, LANES), jnp.uint32),
                pltpu.VMEM((2, tc * SUBLANES, LANES), jnp.uint32),
                pltpu.SemaphoreType.DMA((2,)),
            ],
        ),
        out_shape=jax.ShapeDtypeStruct((t, D_MODEL), F32),
        compiler_params=pltpu.CompilerParams(
            dimension_semantics=("arbitrary",), vmem_limit_bytes=VMEM_LIMIT),
        name="combine",
    )(dest_flat, route, h1, g_final, yb)


def _pad_cols(w, n):
    return jnp.pad(w, ((0, 0), (0, n - w.shape[1])))


def kernel(x, g_mix, w_in, w_out, da_lambda_q1, da_lambda_k1, da_lambda_q2, da_lambda_k2, da_subln_g,
           rw_mu_x, rw_mu_rkv, rw_w0, rw_w1, rw_w2, rw_a0, rw_a1, rw_a2, rw_g1, rw_g2, rw_k_k, rw_k_a,
           rw_r_k, rw_ln_w, rw_ln_b, g_ffn, moe_w_group, moe_b_group, moe_w_expert, moe_b_expert,
           moe_w_gate, moe_w_up, moe_w_down, g_final):
    b, s, d = x.shape
    t = b * s
    x2 = x.reshape(t, d)

    mu = rw_mu_x[0]
    lora = (rw_w1[0], rw_a1[0], rw_g1[0])
    keep = _pad_cols(jnp.concatenate([(1.0 - mu[i])[:, None] * w for i, w in enumerate(lora)], axis=1), LORA_PAD)
    prev = _pad_cols(jnp.concatenate([mu[i][:, None] * w for i, w in enumerate(lora)], axis=1), LORA_PAD)
    w_all = jnp.concatenate([w_in[0], keep, prev], axis=1).astype(BF16)
    proj = _inproj(x2, g_mix, w_all)

    slopes = jnp.asarray([2.0 ** (-8.0 * (i + 1) / DA_HEADS) for i in range(DA_HEADS)], F32)
    lam4 = jnp.concatenate([da_lambda_q1, da_lambda_k1, da_lambda_q2, da_lambda_k2], axis=0)
    o_da = _diffattn(proj, slopes, lam4, da_subln_g.reshape(DA_V_DIM, 1), b, s)

    w2cat = jnp.zeros((LORA_PAD, 3 * RW_WIDTH), F32)
    w2cat = w2cat.at[0:LORA_W, 0:RW_WIDTH].set(rw_w2[0])
    w2cat = w2cat.at[LORA_W:LORA_W + LORA_A, RW_WIDTH:2 * RW_WIDTH].set(rw_a2[0])
    w2cat = w2cat.at[LORA_W + LORA_A:LORA_W + LORA_A + LORA_G, 2 * RW_WIDTH:].set(rw_g2[0])
    o_rw = _rwkv(proj, rw_mu_rkv[0], rw_w0, rw_a0, w2cat.astype(BF16), rw_k_k, rw_k_a,
                 rw_r_k.reshape(1, RW_WIDTH), rw_ln_w, rw_ln_b, b, s)

    w_r = _pad_cols(jnp.concatenate([moe_w_group[0], moe_w_expert[0]], axis=1), LANES)
    wr_hi = w_r.astype(BF16)
    wr_lo = (w_r - wr_hi.astype(F32)).astype(BF16)
    b_r = _pad_cols(jnp.concatenate([moe_b_group[0], moe_b_expert[0].reshape(-1)])[None, :], LANES)
    h1, xn2, route, cnt = _outproj(x2, o_da, o_rw, w_out[0].astype(BF16), g_ffn, wr_hi, wr_lo, b_r)

    counts = cnt[0, EXPERT_LANE0:EXPERT_LANE0 + N_EXPERTS].astype(jnp.int32)
    pcounts = ((counts + EXPERT_BLOCK - 1) // EXPERT_BLOCK) * EXPERT_BLOCK
    expert_ids = jnp.arange(N_EXPERTS, dtype=jnp.int32)
    upto = expert_ids[None, :] <= expert_ids[:, None]
    pends = jnp.sum(jnp.where(upto, pcounts[None, :], 0), axis=1)
    pstarts = pends - pcounts
    e_idx = route[:, ROUTE_E:ROUTE_E + 2].astype(jnp.int32)
    rank = route[:, ROUTE_R:ROUTE_R + 2].astype(jnp.int32)
    row0 = jnp.sum(jnp.where(e_idx[..., None] == expert_ids, pstarts, 0), axis=-1)
    dest = ((row0 + rank) * SUBLANES).reshape(-1)
    n_blocks = (2 * t) // EXPERT_BLOCK + N_EXPERTS
    block_row0 = jnp.arange(n_blocks, dtype=jnp.int32) * EXPERT_BLOCK
    block_e = jnp.minimum(jnp.sum(pends[None, :] <= block_row0[:, None], axis=1), N_EXPERTS - 1).astype(jnp.int32)
    n_used = (pends[-1:] // EXPERT_BLOCK).astype(jnp.int32)
    has_rows = counts > 0
    later = (expert_ids[None, :] > expert_ids[:, None]) & has_rows[None, :]
    next_used = jnp.min(jnp.where(later, expert_ids[None, :], N_EXPERTS), axis=1)
    next_used = jnp.where(next_used == N_EXPERTS, -1, next_used).astype(jnp.int32)
    buffer_id = ((jnp.sum(jnp.where(upto & has_rows[None, :], 1, 0), axis=1) - 1) % 2).astype(jnp.int32)

    block_ids = jnp.arange(n_blocks, dtype=jnp.int32)
    last_block = pends // EXPERT_BLOCK - 1
    is_last = jnp.any(has_rows[None, :] & (block_ids[:, None] == last_block[None, :]), axis=1)
    clear_flags = (is_last | (block_ids >= n_used[0])).astype(jnp.int32)
    xb = _dispatch(dest, clear_flags, xn2, n_blocks)
    yb = _experts(block_e, n_used, next_used[block_e], buffer_id[block_e], xb,
                  moe_w_gate[0], moe_w_up[0], moe_w_down[0])
    out = _combine(dest, route, h1, g_final[None, :], yb)
    return out.reshape(b, s, d)
```
